```python
import math
import jax, jax.numpy as jnp
from jax import lax
import numpy as np

D_MODEL = 1024
BATCH = 16
SEQ = 2048
DEPTH = 1
DEC_BATCH = 128
DEC_SEQ = 4
PAST_LEN = 8192
PAGE_SIZE = 128

HEAD_DIM = 64
ATTN_WIDTH = D_MODEL // 2
ATTN_HEADS = ATTN_WIDTH // HEAD_DIM
SSM_WIDTH = D_MODEL - ATTN_WIDTH
SSM_GROUP_CH = 16
SSM_GROUPS = SSM_WIDTH // SSM_GROUP_CH
SSM_STATE = 64
WINDOWS = (128, 512, 2048)
DILATIONS = (1, 4, 16)
MAX_WINDOW = 2048
N_BUCKETS = 32
MAX_EXACT = 16
BUCKET_MAX_DIST = 2048
N_EXPERT_GROUPS = 4
EXPERTS_PER_GROUP = 4
EXPERT_TOP_K = 2
D_EXPERT = 512
NORM_EPS = 1e-6
DT_MIN = 1e-3
DT_MAX = 1e-1
F32 = jnp.float32

kernel_name = "hymba_dilated_s5_hmoe_step"


def rmsnorm(x, g):
    xf = x.astype(F32)
    y = xf * lax.rsqrt(jnp.mean(xf * xf, axis=-1, keepdims=True) + NORM_EPS)
    return (y * g.astype(F32)).astype(x.dtype)


def t5_bucket(dist):
    d = jnp.maximum(dist, MAX_EXACT).astype(F32)
    log_part = MAX_EXACT + (jnp.log(d / MAX_EXACT) / math.log(BUCKET_MAX_DIST / MAX_EXACT)
                            * (N_BUCKETS - MAX_EXACT)).astype(jnp.int32)
    return jnp.where(dist < MAX_EXACT, dist, jnp.minimum(log_part, N_BUCKETS - 1))


def merge_branches(ms, dens, nums):
    m_all = jnp.max(jnp.stack(ms), axis=0)
    num = jnp.zeros_like(nums[0])
    den = jnp.zeros_like(dens[0])
    for m, d, nm in zip(ms, dens, nums):
        w = jnp.exp(m - m_all)
        num = num + w[..., None] * nm
        den = den + w * d
    return num / den[..., None]


def dilated_attn_prompt(q, k, v, rel_bias):
    n_seq, s_len, n_h, d_h = q.shape
    qf = q.astype(F32) * HEAD_DIM ** -0.5
    kf, vf = k.astype(F32), v.astype(F32)
    bias_tab = rel_bias.astype(F32)
    ms, dens, nums = [], [], []
    for w, r in zip(WINDOWS, DILATIONS):
        steps = w // r
        L = s_len // r
        nb = -(-L // steps)
        Lp = nb * steps

        def to_res(a):
            a = a.reshape(n_seq, L, r, n_h, d_h)
            a = jnp.pad(a, ((0, 0), (0, Lp - L), (0, 0), (0, 0), (0, 0)))
            return a.reshape(n_seq, nb, steps, r, n_h, d_h)

        def band(a):
            prev = jnp.pad(a, ((0, 0), (1, 0), (0, 0), (0, 0), (0, 0), (0, 0)))[:, :-1]
            return jnp.concatenate([prev, a], axis=2)

        qr = to_res(qf)
        kb, vb = band(to_res(kf)), band(to_res(vf))
        ql = jnp.arange(steps)
        kl = jnp.arange(2 * steps)
        j = ql[:, None] + steps - kl[None, :]
        blk = jnp.arange(nb)
        key_ok = (blk[:, None] * steps + kl[None, :] - steps) >= 0
        mask = ((j >= 0) & (j <= steps))[None] & key_ok[:, None, :]
        bias = bias_tab[t5_bucket(jnp.clip(j, 0, steps) * r)]
        bias = jnp.transpose(bias, (2, 0, 1))
        s = jnp.einsum('bnqrhd,bnkrhd->bnrhqk', qr, kb) + bias[None, None, None]
        s = jnp.where(mask[None, :, None, None], s, -jnp.inf)
        m = jnp.max(s, axis=-1)
        p = jnp.exp(s - m[..., None])
        den = jnp.sum(p, axis=-1)
        num = jnp.einsum('bnrhqk,bnkrhd->bnqrhd', p, vb)

        def back(a):
            a = a.reshape((n_seq, Lp, r) + a.shape[4:])[:, :L]
            return a.reshape((n_seq, s_len) + a.shape[3:])

        ms.append(back(jnp.transpose(m, (0, 1, 4, 2, 3))))
        dens.append(back(jnp.transpose(den, (0, 1, 4, 2, 3))))
        nums.append(back(num))
    return merge_branches(ms, dens, nums)


def dilated_attn_decode(q, k_all, v_all, rel_bias):
    n_seq, t_len = q.shape[:2]
    w_rows = k_all.shape[1] - t_len
    qf = q.astype(F32) * HEAD_DIM ** -0.5
    kf, vf = k_all.astype(F32), v_all.astype(F32)
    bias_tab = rel_bias.astype(F32)
    ms, dens, nums = [], [], []
    for w, r in zip(WINDOWS, DILATIONS):
        steps = w // r
        j = jnp.arange(steps + 1)
        idx = w_rows + jnp.arange(t_len)[:, None] - r * j[None, :]
        valid = idx >= 0
        idx = jnp.maximum(idx, 0)
        kg, vg = kf[:, idx], vf[:, idx]
        bias = bias_tab[t5_bucket(r * j)].T
        s = jnp.einsum('nthd,ntjhd->nhtj', qf, kg) + bias[None, :, None, :]
        s = jnp.where(valid[None, None], s, -jnp.inf)
        m = jnp.max(s, axis=-1)
        p = jnp.exp(s - m[..., None])
        dens.append(jnp.transpose(jnp.sum(p, axis=-1), (0, 2, 1)))
        ms.append(jnp.transpose(m, (0, 2, 1)))
        nums.append(jnp.einsum('nhtj,ntjhd->nthd', p, vg))
    return merge_branches(ms, dens, nums)


def _complex_affine_combine(e1, e2):
    a1r, a1i, b1r, b1i = e1
    a2r, a2i, b2r, b2i = e2
    return (a2r * a1r - a2i * a1i, a2r * a1i + a2i * a1r,
            a2r * b1r - a2i * b1i + b2r, a2r * b1i + a2i * b1r + b2i)


def s5_branch(u, h0, a_re, a_im, log_dt, b_re, b_im, c_re, c_im, d_skip, glu_a, glu_b):
    n_seq, t_len, _ = u.shape
    uf = u.astype(F32).reshape(n_seq, t_len, SSM_GROUPS, SSM_GROUP_CH)
    lam_re = jnp.minimum(a_re.astype(F32), -1e-4)
    lam_im = a_im.astype(F32)
    dt = jnp.exp(log_dt.astype(F32))[:, None]
    mag = jnp.exp(lam_re * dt)
    ph = lam_im * dt
    abar_re, abar_im = mag * jnp.cos(ph), mag * jnp.sin(ph)
    nr, ni = abar_re - 1.0, abar_im
    den = lam_re * lam_re + lam_im * lam_im
    coef_re = (nr * lam_re + ni * lam_im) / den
    coef_im = (ni * lam_re - nr * lam_im) / den
    br, bi = b_re.astype(F32), b_im.astype(F32)
    bbar_re = coef_re[..., None] * br - coef_im[..., None] * bi
    bbar_im = coef_re[..., None] * bi + coef_im[..., None] * br
    bu_re = jnp.einsum('ntgc,gpc->ntgp', uf, bbar_re)
    bu_im = jnp.einsum('ntgc,gpc->ntgp', uf, bbar_im)
    if h0 is not None:
        h0_re, h0_im = h0[0].astype(F32), h0[1].astype(F32)
        bu_re = bu_re.at[:, 0].add(abar_re * h0_re - abar_im * h0_im)
        bu_im = bu_im.at[:, 0].add(abar_re * h0_im + abar_im * h0_re)
    a_r = jnp.broadcast_to(abar_re, bu_re.shape)
    a_i = jnp.broadcast_to(abar_im, bu_im.shape)
    _, _, h_re, h_im = lax.associative_scan(_complex_affine_combine, (a_r, a_i, bu_re, bu_im), axis=1)
    y = (jnp.einsum('ntgp,gcp->ntgc', h_re, c_re.astype(F32))
         - jnp.einsum('ntgp,gcp->ntgc', h_im, c_im.astype(F32))
         + d_skip.astype(F32) * uf)
    y = jax.nn.gelu(y)
    y = (jnp.einsum('ntgc,gce->ntge', y, glu_a.astype(F32))
         * jax.nn.sigmoid(jnp.einsum('ntgc,gce->ntge', y, glu_b.astype(F32))))
    return y.reshape(n_seq, t_len, SSM_WIDTH).astype(u.dtype), h_re[:, -1], h_im[:, -1]


def hier_moe(h, rg_w, rg_b, re_w, re_b, w_gate, w_up, w_down):
    n_seq, t_len, d = h.shape
    t = h.reshape(n_seq * t_len, d)
    g_prob = jax.nn.softmax((t @ rg_w + rg_b).astype(F32), axis=-1)
    g_star = jnp.argmax(g_prob, axis=-1)
    p_star = jnp.max(g_prob, axis=-1)
    g_hot = jax.nn.one_hot(g_star, N_EXPERT_GROUPS, dtype=F32)
    e_logits = (jnp.einsum('td,gde->tge', t, re_w) + re_b).astype(F32)
    e_sel = jnp.sum(e_logits * g_hot[:, :, None], axis=1)
    top_v, top_i = lax.top_k(e_sel, EXPERT_TOP_K)
    top_w = jax.nn.softmax(top_v, axis=-1) * p_star[:, None]
    e_w = jnp.sum(jax.nn.one_hot(top_i, EXPERTS_PER_GROUP, dtype=F32) * top_w[..., None], axis=1)
    gw = (g_hot[:, :, None] * e_w[:, None, :]).astype(h.dtype)
    out = jnp.zeros_like(t)
    for g in range(N_EXPERT_GROUPS):
        a = (jax.nn.silu(jnp.einsum('td,edf->tef', t, w_gate[g]))
             * jnp.einsum('td,edf->tef', t, w_up[g]))
        out = out + jnp.einsum('tef,efd->td', a * gw[:, g, :, None], w_down[g])
    return out.reshape(n_seq, t_len, d)


def hybrid_layer(x, c, past, rel_bias, ada_w, ada_b, norm1_g, w_in, ssm_a_re, ssm_a_im, ssm_log_dt,
                 ssm_b_re, ssm_b_im, ssm_c_re, ssm_c_im, ssm_d, glu_a, glu_b, attn_out_g, ssm_out_g,
                 w_out, norm2_g, router_g_w, router_g_b, router_e_w, router_e_b, w_gate, w_up, w_down):
    n_seq, t_len, _ = x.shape
    mod = jnp.einsum('nd,de->ne', jax.nn.silu(c), ada_w) + ada_b
    sh1, sc1, g1, sh2, sc2, g2 = [m[:, None, :] for m in jnp.split(mod, 6, axis=-1)]
    h = rmsnorm(x, norm1_g) * (1 + sc1) + sh1
    z = jnp.einsum('ntd,de->nte', h, w_in)
    aw = ATTN_WIDTH
    q = z[..., :aw].reshape(n_seq, t_len, ATTN_HEADS, HEAD_DIM)
    k = z[..., aw:2 * aw].reshape(n_seq, t_len, ATTN_HEADS, HEAD_DIM)
    v = z[..., 2 * aw:3 * aw].reshape(n_seq, t_len, ATTN_HEADS, HEAD_DIM)
    u = z[..., 3 * aw:]
    if past is None:
        att = dilated_attn_prompt(q, k, v, rel_bias)
        keep = min(MAX_WINDOW, t_len)
        k_new, v_new = k[:, t_len - keep:], v[:, t_len - keep:]
        h0 = None
    else:
        past_k, past_v, h0_re, h0_im = past
        k_all = jnp.concatenate([past_k.astype(k.dtype), k], axis=1)
        v_all = jnp.concatenate([past_v.astype(v.dtype), v], axis=1)
        att = dilated_attn_decode(q, k_all, v_all, rel_bias)
        k_new, v_new = k, v
        h0 = (h0_re, h0_im)
    ssm_y, hT_re, hT_im = s5_branch(u, h0, ssm_a_re, ssm_a_im, ssm_log_dt, ssm_b_re, ssm_b_im,
                                    ssm_c_re, ssm_c_im, ssm_d, glu_a, glu_b)
    att = att.reshape(n_seq, t_len, ATTN_WIDTH).astype(x.dtype)
    mixed = jnp.concatenate([rmsnorm(att, attn_out_g), rmsnorm(ssm_y, ssm_out_g)], axis=-1)
    x = x + g1 * jnp.einsum('nte,ed->ntd', mixed, w_out)
    h2 = rmsnorm(x, norm2_g) * (1 + sc2) + sh2
    x = x + g2 * hier_moe(h2, router_g_w, router_g_b, router_e_w, router_e_b, w_gate, w_up, w_down)
    return x, (k_new, v_new, hT_re, hT_im)


def setup_inputs(seed: int = 0) -> dict:
    key = jax.random.key(seed)
    ks = jax.random.split(key, 40)
    nrm = lambda i, shape, s: jax.random.normal(ks[i], shape, F32) * s
    D, G, P, Cg = D_MODEL, SSM_GROUPS, SSM_STATE, SSM_GROUP_CH
    NG, EPG, FE = N_EXPERT_GROUPS, EXPERTS_PER_GROUP, D_EXPERT
    win = min(MAX_WINDOW, PAST_LEN)
    proj_w = 3 * ATTN_WIDTH + SSM_WIDTH
    a_im0 = jnp.pi * jnp.arange(P, dtype=F32)
    return {
        "x_prompt": nrm(0, (BATCH, SEQ, D), 1.0),
        "x_sample": nrm(1, (DEC_BATCH, DEC_SEQ, D), 1.0),
        "c_prompt": nrm(2, (BATCH, D), 1.0),
        "c_sample": nrm(3, (DEC_BATCH, D), 1.0),
        "cache_k": nrm(4, (DEPTH, DEC_BATCH, win, ATTN_HEADS, HEAD_DIM), 1.0),
        "cache_v": nrm(5, (DEPTH, DEC_BATCH, win, ATTN_HEADS, HEAD_DIM), 1.0),
        "state_ssm_re": nrm(6, (DEPTH, DEC_BATCH, G, P), 1.0),
        "state_ssm_im": nrm(7, (DEPTH, DEC_BATCH, G, P), 1.0),
        "rel_bias": nrm(8, (N_BUCKETS, ATTN_HEADS), 0.5),
        "ada_w": nrm(9, (DEPTH, D, 6 * D), 0.5 * D ** -0.5),
        "ada_b": nrm(10, (DEPTH, 6 * D), 0.02),
        "norm1_g": 1.0 + nrm(11, (DEPTH, D), 0.05),
        "w_in": nrm(12, (DEPTH, D, proj_w), D ** -0.5),
        "ssm_a_re": -0.5 + nrm(13, (DEPTH, G, P), 0.01),
        "ssm_a_im": a_im0 + nrm(14, (DEPTH, G, P), 0.01),
        "ssm_log_dt": jax.random.uniform(ks[15], (DEPTH, G), F32, math.log(DT_MIN), math.log(DT_MAX)),
        "ssm_b_re": nrm(16, (DEPTH, G, P, Cg), (2 * Cg) ** -0.5),
        "ssm_b_im": nrm(17, (DEPTH, G, P, Cg), (2 * Cg) ** -0.5),
        "ssm_c_re": nrm(18, (DEPTH, G, Cg, P), (2 * P) ** -0.5),
        "ssm_c_im": nrm(19, (DEPTH, G, Cg, P), (2 * P) ** -0.5),
        "ssm_d": nrm(20, (DEPTH, G, Cg), 1.0),
        "glu_a": nrm(21, (DEPTH, G, Cg, Cg), Cg ** -0.5),
        "glu_b": nrm(22, (DEPTH, G, Cg, Cg), Cg ** -0.5),
        "attn_out_g": 1.0 + nrm(23, (DEPTH, ATTN_WIDTH), 0.05),
        "ssm_out_g": 1.0 + nrm(24, (DEPTH, SSM_WIDTH), 0.05),
        "w_out": nrm(25, (DEPTH, D, D), D ** -0.5),
        "norm2_g": 1.0 + nrm(26, (DEPTH, D), 0.05),
        "router_g_w": nrm(27, (DEPTH, D, NG), D ** -0.5),
        "router_g_b": nrm(28, (DEPTH, NG), 0.01),
        "router_e_w": nrm(29, (DEPTH, NG, D, EPG), D ** -0.5),
        "router_e_b": nrm(30, (DEPTH, NG, EPG), 0.01),
        "w_gate": nrm(31, (DEPTH, NG, EPG, D, FE), D ** -0.5),
        "w_up": nrm(32, (DEPTH, NG, EPG, D, FE), D ** -0.5),
        "w_down": nrm(33, (DEPTH, NG, EPG, FE, D), FE ** -0.5),
        "final_norm_g": 1.0 + nrm(34, (D,), 0.05),
    }


def reference(x_prompt, x_sample, c_prompt, c_sample, cache_k, cache_v, state_ssm_re, state_ssm_im,
              rel_bias, ada_w, ada_b, norm1_g, w_in, ssm_a_re, ssm_a_im, ssm_log_dt, ssm_b_re, ssm_b_im,
              ssm_c_re, ssm_c_im, ssm_d, glu_a, glu_b, attn_out_g, ssm_out_g, w_out, norm2_g,
              router_g_w, router_g_b, router_e_w, router_e_b, w_gate, w_up, w_down, final_norm_g):
    layer_params = (ada_w, ada_b, norm1_g, w_in, ssm_a_re, ssm_a_im, ssm_log_dt, ssm_b_re, ssm_b_im,
                    ssm_c_re, ssm_c_im, ssm_d, glu_a, glu_b, attn_out_g, ssm_out_g, w_out, norm2_g,
                    router_g_w, router_g_b, router_e_w, router_e_b, w_gate, w_up, w_down)
    xp, xs = x_prompt, x_sample
    kp, vp, ks_, vs_, rp, ip, rs, is_ = [], [], [], [], [], [], [], []
    for l in range(DEPTH):
        lw = tuple(w[l] for w in layer_params)
        xp, (k1, v1, r1, i1) = hybrid_layer(xp, c_prompt, None, rel_bias, *lw)
        xs, (k2, v2, r2, i2) = hybrid_layer(
            xs, c_sample, (cache_k[l], cache_v[l], state_ssm_re[l], state_ssm_im[l]), rel_bias, *lw)
        kp.append(k1); vp.append(v1); rp.append(r1); ip.append(i1)
        ks_.append(k2); vs_.append(v2); rs.append(r2); is_.append(i2)
    y_prompt = rmsnorm(xp, final_norm_g)
    y_sample = rmsnorm(xs, final_norm_g)
    return (y_prompt, y_sample, jnp.stack(kp), jnp.stack(vp), jnp.stack(ks_), jnp.stack(vs_),
            jnp.stack(rp), jnp.stack(ip), jnp.stack(rs), jnp.stack(is_))
```

```python
import functools
import math

import jax
import jax.numpy as jnp
from jax import lax
from jax.experimental import pallas as pl
from jax.experimental.pallas import tpu as pltpu

F32 = jnp.float32
BF16 = jnp.bfloat16

D_MODEL = 1024
HEAD_DIM = 64
ATTN_WIDTH = 512
ATTN_HEADS = 8
SSM_WIDTH = 512
SSM_GROUP_CH = 16
SSM_GROUPS = 32
SSM_STATE = 64
WINDOWS = (128, 512, 2048)
DILATIONS = (1, 4, 16)
WINDOW_STEPS = 128
N_BUCKETS = 32
MAX_EXACT = 16
BUCKET_MAX_DIST = 2048
N_EXPERT_GROUPS = 4
EXPERTS_PER_GROUP = 4
N_EXPERTS = N_EXPERT_GROUPS * EXPERTS_PER_GROUP
D_EXPERT = 512
NORM_EPS = 1e-6

LANES = 128
OCTET = LANES // SSM_GROUP_CH
N_OCTETS = SSM_GROUPS // OCTET
OCT_STATE = OCTET * SSM_STATE
SSM_CHUNK = 16
ROW_TILE = 512
MOE_TILE = 512
NEW_ROWS_PAD = 8
VMEM_LIMIT = 56 * 1024 * 1024

_NEG_INF = float("-inf")


def _cparams(sem):
    return pltpu.CompilerParams(dimension_semantics=sem, vmem_limit_bytes=VMEM_LIMIT)


def _rmsnorm(x, g):
    return x * lax.rsqrt(jnp.mean(x * x, axis=-1, keepdims=True) + NORM_EPS) * g


def _gelu_tanh(x):
    c = math.sqrt(2.0 / math.pi)
    return 0.5 * x * (1.0 + jnp.tanh(c * (x + 0.044715 * (x * x * x))))


def _sigmoid(x):
    return 1.0 / (1.0 + jnp.exp(-x))


def _mod_kernel(c_ref, w_ref, b_ref, o_ref):
    c = c_ref[...]
    a = (c * _sigmoid(c)).astype(BF16)
    o_ref[...] = jnp.dot(a, w_ref[...].astype(BF16), preferred_element_type=F32) + b_ref[...]


def _mod_call(c_all, ada_w, ada_b):
    rows, d = c_all.shape
    n_out = ada_w.shape[1]
    tn = 1024
    return pl.pallas_call(
        _mod_kernel,
        grid=(n_out // tn,),
        in_specs=[pl.BlockSpec((rows, d), lambda j: (0, 0)),
                  pl.BlockSpec((d, tn), lambda j: (0, j)),
                  pl.BlockSpec((1, tn), lambda j: (0, j))],
        out_specs=pl.BlockSpec((rows, tn), lambda j: (0, j)),
        out_shape=jax.ShapeDtypeStruct((rows, n_out), F32),
        compiler_params=_cparams(("arbitrary",)),
        name="adaln_mod",
    )(c_all, ada_w, ada_b.reshape(1, n_out))


def _inproj_kernel(x_ref, sh_ref, sc_ref, g_ref, w_ref, q_ref, k_ref, v_ref, u_ref):
    x = x_ref[0]
    h = _rmsnorm(x, g_ref[...]) * (1.0 + sc_ref[0]) + sh_ref[0]
    z = jnp.dot(h.astype(BF16), w_ref[...], preferred_element_type=F32)
    aw = ATTN_WIDTH
    q_ref[0] = z[:, :aw]
    k_ref[0] = z[:, aw:2 * aw]
    v_ref[0] = z[:, 2 * aw:3 * aw]
    u_ref[0] = z[:, 3 * aw:]


def _mod_spec(mod_rows, ts, chunk):
    if mod_rows == 1:
        return pl.BlockSpec((1, 1, D_MODEL), lambda n, i: (n, 0, chunk))
    return pl.BlockSpec((1, ts, D_MODEL), lambda n, i: (n, i, chunk))


def _inproj_call(x, mod, norm_g, w_in_bf):
    n, s, d = x.shape
    ts = min(ROW_TILE, s)
    proj = w_in_bf.shape[1]
    out = jax.ShapeDtypeStruct((n, s, ATTN_WIDTH), F32)
    ospec = pl.BlockSpec((1, ts, ATTN_WIDTH), lambda n_, i: (n_, i, 0))
    return pl.pallas_call(
        _inproj_kernel,
        grid=(n, s // ts),
        in_specs=[pl.BlockSpec((1, ts, d), lambda n_, i: (n_, i, 0)),
                  _mod_spec(mod.shape[1], ts, 0),
                  _mod_spec(mod.shape[1], ts, 1),
                  pl.BlockSpec((1, d), lambda n_, i: (0, 0)),
                  pl.BlockSpec((d, proj), lambda n_, i: (0, 0))],
        out_specs=[ospec, ospec, ospec, ospec],
        out_shape=[out, out, out, out],
        compiler_params=_cparams(("arbitrary", "arbitrary")),
        name="inproj",
    )(x, mod, mod, norm_g.reshape(1, d), w_in_bf)


def _t5_bucket(dist):
    d = jnp.maximum(dist, MAX_EXACT).astype(F32)
    log_part = MAX_EXACT + (jnp.log(d / MAX_EXACT) / math.log(BUCKET_MAX_DIST / MAX_EXACT)
                            * (N_BUCKETS - MAX_EXACT)).astype(jnp.int32)
    return jnp.where(dist < MAX_EXACT, dist, jnp.minimum(log_part, N_BUCKETS - 1))


def _prompt_bias_tiles(rel_bias):
    tab = rel_bias.astype(F32)
    steps = WINDOW_STEPS
    ql = jnp.arange(steps)
    kl = jnp.arange(2 * steps)
    j = ql[:, None] + steps - kl[None, :]
    mask = (j >= 0) & (j <= steps)
    tiles = []
    for r in DILATIONS:
        b = tab[_t5_bucket(jnp.clip(j, 0, steps) * r)]
        b = jnp.where(mask[..., None], b, _NEG_INF)
        tiles.append(jnp.transpose(b, (2, 0, 1)).reshape(ATTN_HEADS // 2, 2, steps, 2 * steps))
    return jnp.stack(tiles)


def _attn_prompt_kernel(q_ref, k_ref, v_ref, bias_ref, o_ref, num_ref, m_ref, den_ref):
    s_len = q_ref.shape[1]
    steps = WINDOW_STEPS
    lane = lax.broadcasted_iota(jnp.int32, (1, LANES), 1)
    first_head = lane < HEAD_DIM

    def tile(branch, q_start, k_start, n_keys):
        r = DILATIONS[branch]
        q2 = q_ref[0, pl.ds(q_start, steps, stride=r), :] * (HEAD_DIM ** -0.5)
        k2 = k_ref[0, pl.ds(k_start, n_keys, stride=r), :].astype(BF16)
        v2 = v_ref[0, pl.ds(k_start, n_keys, stride=r), :].astype(BF16)
        per_head = []
        for hh in range(2):
            sel = first_head if hh == 0 else jnp.logical_not(first_head)
            qh = jnp.where(sel, q2, 0.0).astype(BF16)
            sc = lax.dot_general(qh, k2, (((1,), (1,)), ((), ())), preferred_element_type=F32)
            sc = sc + bias_ref[branch, 0, hh, :, 2 * steps - n_keys:]
            m = jnp.max(sc, axis=1, keepdims=True)
            p = jnp.exp(sc - m)
            den = jnp.sum(p, axis=1, keepdims=True)
            o = jnp.dot(p.astype(BF16), v2, preferred_element_type=F32)
            per_head.append((o, m, den))
        (o0, m0, d0), (o1, m1, d1) = per_head
        rows = pl.ds(q_start, steps, stride=r)
        num_ref[branch, rows, :] = jnp.where(first_head, o0, o1)
        m_ref[branch, rows, :] = jnp.where(first_head, m0, m1)
        den_ref[branch, rows, :] = jnp.where(first_head, d0, d1)

    for branch, r in enumerate(DILATIONS):
        class_len = s_len // r
        n_blocks = class_len // steps

        def first_block(rho, carry, branch=branch):
            tile(branch, rho, rho, steps)
            return carry

        lax.fori_loop(0, r, first_block, 0)
        if n_blocks > 1:
            def later_block(i, carry, branch=branch, r=r, n_blocks=n_blocks):
                rho = i // (n_blocks - 1)
                b = i % (n_blocks - 1) + 1
                tile(branch, rho + r * steps * b, rho + r * steps * (b - 1), 2 * steps)
                return carry

            lax.fori_loop(0, r * (n_blocks - 1), later_block, 0)

    def merge(i, carry):
        rows = pl.ds(pl.multiple_of(i * 256, 256), 256)
        m0, m1, m2 = m_ref[0, rows, :], m_ref[1, rows, :], m_ref[2, rows, :]
        m_all = jnp.maximum(jnp.maximum(m0, m1), m2)
        w0, w1, w2 = jnp.exp(m0 - m_all), jnp.exp(m1 - m_all), jnp.exp(m2 - m_all)
        num = w0 * num_ref[0, rows, :] + w1 * num_ref[1, rows, :] + w2 * num_ref[2, rows, :]
        den = w0 * den_ref[0, rows, :] + w1 * den_ref[1, rows, :] + w2 * den_ref[2, rows, :]
        o_ref[0, rows, :] = num / den
        return carry

    lax.fori_loop(0, s_len // 256, merge, 0)


def _attn_prompt_call(q, k, v, bias_tiles):
    n, s, _ = q.shape
    pairs = ATTN_HEADS // 2
    qspec = pl.BlockSpec((1, s, LANES), lambda n_, g: (n_, 0, g))
    return pl.pallas_call(
        _attn_prompt_kernel,
        grid=(n, pairs),
        in_specs=[qspec, qspec, qspec,
                  pl.BlockSpec((3, 1, 2, WINDOW_STEPS, 2 * WINDOW_STEPS),
                               lambda n_, g: (0, g, 0, 0, 0))],
        out_specs=qspec,
        out_shape=jax.ShapeDtypeStruct((n, s, ATTN_WIDTH), F32),
        scratch_shapes=[pltpu.VMEM((3, s, LANES), F32),
                        pltpu.VMEM((3, s, LANES), F32),
                        pltpu.VMEM((3, s, LANES), F32)],
        compiler_params=_cparams(("arbitrary", "arbitrary")),
        name="attn_prompt",
    )(q, k, v, bias_tiles)


def _s5_discretise(a_re, a_im, log_dt, b_re, b_im):
    lam_re = jnp.minimum(a_re.astype(F32), -1e-4)
    lam_im = a_im.astype(F32)
    dt = jnp.exp(log_dt.astype(F32))[:, None]
    mag = jnp.exp(lam_re * dt)
    ph = lam_im * dt
    abar_re, abar_im = mag * jnp.cos(ph), mag * jnp.sin(ph)
    nr, ni = abar_re - 1.0, abar_im
    den = lam_re * lam_re + lam_im * lam_im
    coef_re = (nr * lam_re + ni * lam_im) / den
    coef_im = (ni * lam_re - nr * lam_im) / den
    br, bi = b_re.astype(F32), b_im.astype(F32)
    bbar_re = coef_re[..., None] * br - coef_im[..., None] * bi
    bbar_im = coef_re[..., None] * bi + coef_im[..., None] * br
    return lam_re * dt, ph, abar_re, abar_im, bbar_re, bbar_im


def _abar_power(log_mag, ph, n):
    nf = jnp.asarray(n, F32)[:, None, None]
    mag = jnp.exp(nf * log_mag[None])
    return mag * jnp.cos(nf * ph[None]), mag * jnp.sin(nf * ph[None])


def _s5_prompt_operators(a_re, a_im, log_dt, b_re, b_im, c_re, c_im):
    hp = lax.Precision.HIGHEST
    L = SSM_CHUNK
    log_mag, ph, _, _, bb_re, bb_im = _s5_discretise(a_re, a_im, log_dt, b_re, b_im)
    cr, ci = c_re.astype(F32), c_im.astype(F32)
    pw_re, pw_im = _abar_power(log_mag, ph, jnp.arange(L + 1))
    eye = jnp.eye(OCTET, dtype=F32)

    ab_re = pw_re[:L, :, :, None] * bb_re[None] - pw_im[:L, :, :, None] * bb_im[None]
    ab_im = pw_re[:L, :, :, None] * bb_im[None] + pw_im[:L, :, :, None] * bb_re[None]
    lag = (jnp.einsum('gop,lgpi->lgoi', cr, ab_re, precision=hp)
           - jnp.einsum('gop,lgpi->lgoi', ci, ab_im, precision=hp))
    s_idx = jnp.arange(L)[:, None]
    t_idx = jnp.arange(L)[None, :]
    causal = (t_idx >= s_idx)
    toe = lag[jnp.clip(t_idx - s_idx, 0, L - 1)]
    toe = jnp.where(causal[:, :, None, None, None], toe, 0.0)
    toe = toe.reshape(L, L, N_OCTETS, OCTET, SSM_GROUP_CH, SSM_GROUP_CH)
    t_op = jnp.einsum('stogci,gh->osgithc', toe, eye)
    t_op = t_op.reshape(N_OCTETS, L * LANES, L * LANES)

    rev = ab_re[::-1], ab_im[::-1]
    m_parts = []
    for part in rev:
        x = part.reshape(L, N_OCTETS, OCTET, SSM_STATE, SSM_GROUP_CH)
        m_parts.append(jnp.einsum('sogpi,gh->osgihp', x, eye).reshape(N_OCTETS, L * LANES, OCT_STATE))
    m_op = jnp.concatenate(m_parts, axis=-1)

    p1_re, p1_im = pw_re[1:], pw_im[1:]
    on_re = cr[None] * p1_re[:, :, None, :] - ci[None] * p1_im[:, :, None, :]
    on_im = -cr[None] * p1_im[:, :, None, :] - ci[None] * p1_re[:, :, None, :]
    p_parts = []
    for part in (on_re, on_im):
        x = part.reshape(L, N_OCTETS, OCTET, SSM_GROUP_CH, SSM_STATE)
        p_parts.append(jnp.einsum('togcp,gh->ogpthc', x, eye).reshape(N_OCTETS, OCT_STATE, L * LANES))
    p_op = jnp.concatenate(p_parts, axis=1)

    n_steps = 8
    sc_re, sc_im = _abar_power(log_mag, ph, L * (2 ** jnp.arange(n_steps)))
    sc = jnp.concatenate([sc_re.reshape(n_steps, N_OCTETS, OCT_STATE),
                          sc_im.reshape(n_steps, N_OCTETS, OCT_STATE)], axis=-1)
    sc = jnp.transpose(sc, (1, 0, 2))
    return t_op.astype(BF16), m_op.astype(BF16), p_op.astype(BF16), sc


def _octet_glu(glu):
    eye = jnp.eye(OCTET, dtype=F32)
    x = glu.astype(F32).reshape(N_OCTETS, OCTET, SSM_GROUP_CH, SSM_GROUP_CH)
    return jnp.einsum('ogce,gh->ogche', x, eye).reshape(N_OCTETS, LANES, LANES).astype(BF16)


def _ssm_prompt_kernel(u_ref, t_ref, m_ref, p_ref, sc_ref, d_ref, ga_ref, gb_ref,
                       y_ref, h_ref, uf_ref, ub_ref):
    L = SSM_CHUNK
    n_chunks = u_ref.shape[1] // L
    for s in range(L):
        blk = u_ref[0, pl.ds(s, n_chunks, stride=L), :]
        uf_ref[:, s * LANES:(s + 1) * LANES] = blk
        ub_ref[:, s * LANES:(s + 1) * LANES] = blk.astype(BF16)
    ub = ub_ref[...]

    x = jnp.dot(ub, m_ref[0], preferred_element_type=F32)
    row = lax.broadcasted_iota(jnp.int32, (n_chunks, 1), 0)
    half = OCT_STATE
    k = 1
    step = 0
    while k < n_chunks:
        a_re = sc_ref[0, step:step + 1, :half]
        a_im = sc_ref[0, step:step + 1, half:]
        sh = jnp.where(row >= k, pltpu.roll(x, k, axis=0), 0.0)
        s_re, s_im = sh[:, :half], sh[:, half:]
        x = x + jnp.concatenate([a_re * s_re - a_im * s_im, a_re * s_im + a_im * s_re], axis=1)
        k *= 2
        step += 1
    h_ref[0, 0] = x[n_chunks - 1:n_chunks, :]
    h_start = jnp.where(row >= 1, pltpu.roll(x, 1, axis=0), 0.0)

    y = (jnp.dot(ub, t_ref[0], preferred_element_type=F32)
         + jnp.dot(h_start.astype(BF16), p_ref[0], preferred_element_type=F32))
    d = d_ref[0]
    ga = ga_ref[0]
    gb = gb_ref[0]
    for t in range(L):
        lanes = slice(t * LANES, (t + 1) * LANES)
        g = _gelu_tanh(y[:, lanes] + d * uf_ref[:, lanes]).astype(BF16)
        out = (jnp.dot(g, ga, preferred_element_type=F32)
               * _sigmoid(jnp.dot(g, gb, preferred_element_type=F32)))
        y_ref[0, pl.ds(t, n_chunks, stride=L), :] = out


def _ssm_prompt_call(u, t_op, m_op, p_op, sc, d_oct, ga, gb):
    n, s, _ = u.shape
    L = SSM_CHUNK
    n_chunks = s // L
    wide = L * LANES
    wspec = lambda shape: pl.BlockSpec((1,) + shape, lambda o, n_: (o, 0, 0))
    return pl.pallas_call(
        _ssm_prompt_kernel,
        grid=(N_OCTETS, n),
        in_specs=[pl.BlockSpec((1, s, LANES), lambda o, n_: (n_, 0, o)),
                  wspec((wide, wide)), wspec((wide, 2 * OCT_STATE)), wspec((2 * OCT_STATE, wide)),
                  wspec((8, 2 * OCT_STATE)), wspec((1, LANES)),
                  wspec((LANES, LANES)), wspec((LANES, LANES))],
        out_specs=[pl.BlockSpec((1, s, LANES), lambda o, n_: (n_, 0, o)),
                   pl.BlockSpec((1, 1, 1, 2 * OCT_STATE), lambda o, n_: (n_, o, 0, 0))],
        out_shape=[jax.ShapeDtypeStruct((n, s, SSM_WIDTH), F32),
                   jax.ShapeDtypeStruct((n, N_OCTETS, 1, 2 * OCT_STATE), F32)],
        scratch_shapes=[pltpu.VMEM((n_chunks, wide), F32),
                        pltpu.VMEM((n_chunks, wide), BF16)],
        compiler_params=_cparams(("arbitrary", "arbitrary")),
        name="ssm_prompt",
    )(u, t_op, m_op, p_op, sc, d_oct, ga, gb)


def _attn_decode_kernel(qa_ref, qb_ref, kn_ref, vn_ref, k16_ref, k4_ref, k1_ref,
                        v16_ref, v4_ref, v1_ref, b16_ref, b4_ref, b1_ref, bn_ref, mult_ref,
                        ex_ref, o_ref):
    nt = (((1,), (1,)), ((), ()))
    qa = qa_ref[0]
    qb = qb_ref[0]
    t_len = o_ref.shape[1]

    s16 = lax.dot_general(k16_ref[0].astype(BF16), qa, nt, preferred_element_type=F32) + b16_ref[...]
    s4 = lax.dot_general(k4_ref[0].astype(BF16), qa, nt, preferred_element_type=F32) + b4_ref[...]
    s1 = lax.dot_general(k1_ref[0].astype(BF16), qb, nt, preferred_element_type=F32) + b1_ref[...]
    sn = lax.dot_general(kn_ref[0].astype(BF16), qb, nt, preferred_element_type=F32) + bn_ref[...]

    def colmax(a):
        return jnp.max(a, axis=0, keepdims=True)

    m = jnp.maximum(jnp.maximum(colmax(s16), colmax(s4)), jnp.maximum(colmax(s1), colmax(sn)))
    p16, p4, p1 = jnp.exp(s16 - m), jnp.exp(s4 - m), jnp.exp(s1 - m)
    pn = jnp.exp(sn - m) * mult_ref[...]
    den = (jnp.sum(p16, axis=0, keepdims=True) + jnp.sum(p4, axis=0, keepdims=True)
           + jnp.sum(p1, axis=0, keepdims=True) + jnp.sum(pn, axis=0, keepdims=True))
    inv = 1.0 / den
    ex = ex_ref[...]

    def weighted(p, v):
        pe = jnp.dot((p * inv).astype(BF16), ex, preferred_element_type=F32)
        return jnp.sum(pe * v, axis=0, keepdims=True)

    v1 = v1_ref[0]
    vn = vn_ref[0]
    acc = (weighted(p16, v16_ref[0]) + weighted(p4, v4_ref[0])
           + weighted(p1, jnp.concatenate([v1] * t_len, axis=1))
           + weighted(pn, jnp.concatenate([vn] * t_len, axis=1)))
    for t in range(t_len):
        o_ref[0, t:t + 1, :] = acc[:, t * ATTN_WIDTH:(t + 1) * ATTN_WIDTH]


def _decode_bias(rel_bias, t_len, w_rows):
    tab = rel_bias.astype(F32)
    steps = WINDOW_STEPS
    rows = jnp.arange(steps)
    cols = t_len * ATTN_HEADS
    out = []
    for r in (16, 4):
        dist = r * (steps - rows)
        b = tab[_t5_bucket(dist)]
        b = jnp.tile(b, (1, t_len))
        if w_rows - r * steps < 0:
            raise ValueError("cache shorter than the widest window")
        out.append(b)
    t = jnp.arange(t_len)
    dist1 = steps - rows[:, None] + t[None, :]
    b1 = tab[_t5_bucket(jnp.clip(dist1, 0, steps))]
    b1 = jnp.where((rows[:, None] >= t[None, :])[..., None], b1, _NEG_INF).reshape(steps, cols)
    dn = t[None, :] - t[:, None]
    bn = tab[_t5_bucket(jnp.clip(dn, 0, steps))]
    bn = jnp.where((dn >= 0)[..., None], bn, _NEG_INF).reshape(t_len, cols)
    mult = jnp.where(dn == 0, float(len(DILATIONS)), 1.0)
    mult = jnp.broadcast_to(mult[..., None], (t_len, t_len, ATTN_HEADS)).reshape(t_len, cols)
    pad = ((0, NEW_ROWS_PAD - t_len), (0, 0))
    bn = jnp.pad(bn, pad, constant_values=_NEG_INF)
    mult = jnp.pad(mult.astype(F32), pad)
    return out[0], out[1], b1, bn, mult


def _attn_decode_call(q, k_new, v_new, cache_k, cache_v, rel_bias):
    n, t_len, w = q.shape
    w_rows = cache_k.shape[1]
    steps = WINDOW_STEPS
    cols = t_len * ATTN_HEADS
    if t_len > 4 or w_rows % 16 or w_rows < 16 * steps:
        raise ValueError("unsupported decode shape")
    qs = (q * (HEAD_DIM ** -0.5)).reshape(n, t_len, ATTN_HEADS, HEAD_DIM)
    eye_h = jnp.eye(ATTN_HEADS, dtype=F32)
    eye_t = jnp.eye(t_len, dtype=F32)
    qb = jnp.einsum('nthd,hg->nthgd', qs, eye_h).reshape(n, cols, w)
    qa = jnp.einsum('ntgx,ts->ntgsx', qb.reshape(n, t_len, ATTN_HEADS, w), eye_t)
    qa = qa.reshape(n, cols, t_len * w).astype(BF16)
    qb = qb.astype(BF16)
    ex = jnp.einsum('hg,ts->thsg', eye_h, eye_t)[..., None]
    ex = jnp.broadcast_to(ex, (t_len, ATTN_HEADS, t_len, ATTN_HEADS, HEAD_DIM))
    ex = ex.reshape(cols, t_len * w).astype(BF16)
    b16, b4, b1, bn, mult = _decode_bias(rel_bias, t_len, w_rows)

    def views(c):
        c16 = c.reshape(n, w_rows // 16, 16 * w)
        c4 = c.reshape(n, w_rows // 4, 4 * w)
        return c16, c4, c

    k16, k4, k1 = views(cache_k)
    v16, v4, v1 = views(cache_v)
    wide = t_len * w
    spec16 = pl.BlockSpec((1, steps, wide), lambda i: (i, (w_rows // 16) // steps - 1, 0))
    spec4 = pl.BlockSpec((1, steps, wide), lambda i: (i, (w_rows // 4) // steps - 1, 0))
    spec1 = pl.BlockSpec((1, steps, w), lambda i: (i, w_rows // steps - 1, 0))
    new_pad = ((0, 0), (0, NEW_ROWS_PAD - t_len), (0, 0))
    k_new = jnp.pad(k_new, new_pad)
    v_new = jnp.pad(v_new, new_pad)
    new_spec = pl.BlockSpec((1, NEW_ROWS_PAD, w), lambda i: (i, 0, 0))
    full = lambda a: pl.BlockSpec(a.shape, lambda i: (0,) * a.ndim)
    return pl.pallas_call(
        _attn_decode_kernel,
        grid=(n,),
        in_specs=[pl.BlockSpec((1, cols, wide), lambda i: (i, 0, 0)),
                  pl.BlockSpec((1, cols, w), lambda i: (i, 0, 0)),
                  new_spec, new_spec, spec16, spec4, spec1, spec16, spec4, spec1,
                  full(b16), full(b4), full(b1), full(bn), full(mult), full(ex)],
        out_specs=pl.BlockSpec((1, t_len, w), lambda i: (i, 0, 0)),
        out_shape=jax.ShapeDtypeStruct((n, t_len, w), F32),
        compiler_params=_cparams(("arbitrary",)),
        name="attn_decode",
    )(qa, qb, k_new, v_new, k16, k4, k1, v16, v4, v1, b16, b4, b1, bn, mult, ex)


def _ssm_decode_kernel(u_ref, hre_ref, him_ref, are_ref, aim_ref, bre_ref, bim_ref,
                       cre_ref, cim_ref, d_ref, ga_ref, gb_ref, y_ref, ore_ref, oim_ref, *, t_len):
    h_re, h_im = hre_ref[...], him_ref[...]
    a_re, a_im = are_ref[...], aim_ref[...]
    for t in range(t_len):
        u = u_ref[t]
        ub = u.astype(BF16)
        n_re = a_re * h_re - a_im * h_im + jnp.dot(ub, bre_ref[...], preferred_element_type=F32)
        n_im = a_re * h_im + a_im * h_re + jnp.dot(ub, bim_ref[...], preferred_element_type=F32)
        h_re, h_im = n_re, n_im
        y = (jnp.dot(h_re.astype(BF16), cre_ref[...], preferred_element_type=F32)
             - jnp.dot(h_im.astype(BF16), cim_ref[...], preferred_element_type=F32)
             + d_ref[...] * u)
        g = _gelu_tanh(y).astype(BF16)
        y_ref[t] = (jnp.dot(g, ga_ref[...], preferred_element_type=F32)
                    * _sigmoid(jnp.dot(g, gb_ref[...], preferred_element_type=F32)))
    ore_ref[...] = h_re
    oim_ref[...] = h_im


def _group_blockdiag(x):
    g, a, b = x.shape
    return jnp.einsum('gab,gh->gahb', x, jnp.eye(g, dtype=x.dtype)).reshape(g * a, g * b)


def _ssm_decode_call(u_tm, h0_re, h0_im, a_re, a_im, log_dt, b_re, b_im, c_re, c_im,
                     d_skip, glu_a, glu_b):
    n = h0_re.shape[0]
    _, _, abar_re, abar_im, bb_re, bb_im = _s5_discretise(a_re, a_im, log_dt, b_re, b_im)
    state = SSM_GROUPS * SSM_STATE
    t_len = u_tm.shape[0]
    args = (u_tm, h0_re.reshape(n, state).astype(F32), h0_im.reshape(n, state).astype(F32),
            abar_re.reshape(1, state), abar_im.reshape(1, state),
            _group_blockdiag(jnp.transpose(bb_re, (0, 2, 1))).astype(BF16),
            _group_blockdiag(jnp.transpose(bb_im, (0, 2, 1))).astype(BF16),
            _group_blockdiag(jnp.transpose(c_re.astype(F32), (0, 2, 1))).astype(BF16),
            _group_blockdiag(jnp.transpose(c_im.astype(F32), (0, 2, 1))).astype(BF16),
            d_skip.astype(F32).reshape(1, SSM_WIDTH),
            _group_blockdiag(glu_a.astype(F32)).astype(BF16),
            _group_blockdiag(glu_b.astype(F32)).astype(BF16))
    full = lambda a: pl.BlockSpec(a.shape, lambda i: (0,) * a.ndim)
    out_shape = [jax.ShapeDtypeStruct(u_tm.shape, F32),
                 jax.ShapeDtypeStruct((n, state), F32), jax.ShapeDtypeStruct((n, state), F32)]
    return pl.pallas_call(
        functools.partial(_ssm_decode_kernel, t_len=t_len),
        grid=(1,),
        in_specs=[full(a) for a in args],
        out_specs=[full(o) for o in out_shape],
        out_shape=out_shape,
        compiler_params=_cparams(("arbitrary",)),
        name="ssm_decode",
    )(*args)


def _outproj_kernel(att_ref, ssm_ref, x_ref, g1_ref, sh2_ref, sc2_ref, ag_ref, sg_ref, n2_ref,
                    wo_ref, wr_ref, rb_ref, x1_ref, h2_ref, route_ref):
    mixed = jnp.concatenate([_rmsnorm(att_ref[0], ag_ref[...]), _rmsnorm(ssm_ref[0], sg_ref[...])],
                            axis=1).astype(BF16)
    x1 = x_ref[0] + g1_ref[0] * jnp.dot(mixed, wo_ref[...], preferred_element_type=F32)
    x1_ref[0] = x1
    h2 = _rmsnorm(x1, n2_ref[...]) * (1.0 + sc2_ref[0]) + sh2_ref[0]
    hi = h2.astype(BF16)
    h2_ref[0] = hi
    lo = (h2 - hi.astype(F32)).astype(BF16)
    r1 = jnp.dot(hi, wr_ref[...], preferred_element_type=F32)
    r2 = jnp.dot(lo, wr_ref[:, :LANES], preferred_element_type=F32)
    logits = r1[:, :LANES] + r1[:, LANES:] + r2 + rb_ref[...]

    lane = lax.broadcasted_iota(jnp.int32, (1, LANES), 1)
    lane_f = lane.astype(F32)
    big = float(LANES)
    ng, epg = N_EXPERT_GROUPS, EXPERTS_PER_GROUP
    lg = jnp.where(lane < ng, logits, _NEG_INF)
    gmax = jnp.max(lg, axis=1, keepdims=True)
    p_star = 1.0 / jnp.sum(jnp.exp(lg - gmax), axis=1, keepdims=True)
    g_star = jnp.min(jnp.where(lg == gmax, lane_f, big), axis=1, keepdims=True)
    in_group = ((lane >= ng) & (lane < ng + ng * epg)
                & (lax.shift_right_arithmetic(lane - ng, int(math.log2(epg))).astype(F32) == g_star))
    le = jnp.where(in_group, logits, _NEG_INF)
    v1 = jnp.max(le, axis=1, keepdims=True)
    i1 = jnp.min(jnp.where(le == v1, lane_f, big), axis=1, keepdims=True)
    le2 = jnp.where(lane_f == i1, _NEG_INF, le)
    v2 = jnp.max(le2, axis=1, keepdims=True)
    i2 = jnp.min(jnp.where(le2 == v2, lane_f, big), axis=1, keepdims=True)
    e2 = jnp.exp(v2 - v1)
    w1 = p_star / (1.0 + e2)
    w2 = p_star * e2 / (1.0 + e2)
    route = jnp.where(lane == 0, i1 - ng,
                      jnp.where(lane == 1, i2 - ng,
                                jnp.where(lane == 2, w1, jnp.where(lane == 3, w2, 0.0))))
    route_ref[0] = route


def _outproj_call(att, ssm_y, x, mod, attn_g, ssm_g, norm2_g, w_out_bf, wr, rb):
    n, s, d = x.shape
    ts = min(ROW_TILE, s)
    row = lambda width: pl.BlockSpec((1, ts, width), lambda n_, i: (n_, i, 0))
    const = lambda a: pl.BlockSpec(a.shape, lambda n_, i: (0,) * a.ndim)
    attn_g = attn_g.reshape(1, ATTN_WIDTH)
    ssm_g = ssm_g.reshape(1, SSM_WIDTH)
    norm2_g = norm2_g.reshape(1, d)
    mr = mod.shape[1]
    return pl.pallas_call(
        _outproj_kernel,
        grid=(n, s // ts),
        in_specs=[row(ATTN_WIDTH), row(SSM_WIDTH), row(d),
                  _mod_spec(mr, ts, 2), _mod_spec(mr, ts, 3), _mod_spec(mr, ts, 4),
                  const(attn_g), const(ssm_g), const(norm2_g), const(w_out_bf), const(wr), const(rb)],
        out_specs=[row(d), row(d), row(LANES)],
        out_shape=[jax.ShapeDtypeStruct((n, s, d), F32), jax.ShapeDtypeStruct((n, s, d), BF16),
                   jax.ShapeDtypeStruct((n, s, LANES), F32)],
        compiler_params=_cparams(("arbitrary", "arbitrary")),
        name="outproj_router",
    )(att, ssm_y, x, mod, mod, mod, attn_g, ssm_g, norm2_g, w_out_bf, wr, rb)


def _router_weights(router_g_w, router_g_b, router_e_w, router_e_b):
    d = router_g_w.shape[0]
    ne = N_EXPERT_GROUPS * EXPERTS_PER_GROUP
    w = jnp.concatenate([router_g_w.astype(F32),
                         jnp.transpose(router_e_w.astype(F32), (1, 0, 2)).reshape(d, ne)], axis=1)
    w = jnp.pad(w, ((0, 0), (0, LANES - w.shape[1])))
    hi = w.astype(BF16)
    lo = (w - hi.astype(F32)).astype(BF16)
    b = jnp.concatenate([router_g_b.astype(F32), router_e_b.astype(F32).reshape(ne)])
    b = jnp.pad(b, (0, LANES - b.shape[0])).reshape(1, LANES)
    return jnp.concatenate([hi, lo], axis=1), b


def _moe_kernel(te_ref, x_ref, w_ref, wg_ref, wu_ref, wd_ref, o_ref):
    del te_ref
    x = x_ref[...]
    gate = jnp.dot(x, wg_ref[0], preferred_element_type=F32)
    up = jnp.dot(x, wu_ref[0], preferred_element_type=F32)
    a = (gate * _sigmoid(gate)) * up * w_ref[...]
    o_ref[...] = jnp.dot(a.astype(BF16), wd_ref[0], preferred_element_type=F32)


def _moe_call(tile_expert, x_sorted, w_sorted, wg, wu, wd):
    n_slots, d = x_sorted.shape
    tm = MOE_TILE
    fe = wg.shape[2]
    grid_spec = pltpu.PrefetchScalarGridSpec(
        num_scalar_prefetch=1,
        grid=(n_slots // tm,),
        in_specs=[pl.BlockSpec((tm, d), lambda i, te: (i, 0)),
                  pl.BlockSpec((tm, 1), lambda i, te: (i, 0)),
                  pl.BlockSpec((1, d, fe), lambda i, te: (te[i], 0, 0)),
                  pl.BlockSpec((1, d, fe), lambda i, te: (te[i], 0, 0)),
                  pl.BlockSpec((1, fe, d), lambda i, te: (te[i], 0, 0))],
        out_specs=pl.BlockSpec((tm, d), lambda i, te: (i, 0)),
    )
    return pl.pallas_call(
        _moe_kernel,
        grid_spec=grid_spec,
        out_shape=jax.ShapeDtypeStruct((n_slots, d), F32),
        compiler_params=_cparams(("arbitrary",)),
        name="moe_experts",
    )(tile_expert, x_sorted, w_sorted, wg, wu, wd)


def _moe_dispatch(route):
    n_tok = route.shape[0]
    tm = MOE_TILE
    ids = route[:, :2].astype(jnp.int32).reshape(-1)
    wts = route[:, 2:4].reshape(-1)
    n_pairs = ids.shape[0]
    n_slots = (-(-n_pairs // tm) + N_EXPERTS) * tm
    hot = (ids[:, None] == jnp.arange(N_EXPERTS)[None, :]).astype(jnp.int32)
    csum = jnp.cumsum(hot, axis=0)
    rank = jnp.sum((csum - hot) * hot, axis=1)
    counts = csum[-1]
    padded = -(-counts // tm) * tm
    ends = jnp.cumsum(padded)
    starts = ends - padded
    pos = jnp.sum(hot * starts[None, :], axis=1) + rank
    tok_of_slot = jnp.zeros((n_slots,), jnp.int32).at[pos].set(jnp.arange(n_pairs, dtype=jnp.int32) // 2)
    w_of_slot = jnp.zeros((n_slots,), F32).at[pos].set(wts)
    tile_start = jnp.arange(n_slots // tm, dtype=jnp.int32) * tm
    tile_expert = jnp.minimum(jnp.sum((tile_start[:, None] >= ends[None, :]).astype(jnp.int32), axis=1),
                              N_EXPERTS - 1).astype(jnp.int32)
    return pos.reshape(n_tok, 2), tok_of_slot, w_of_slot.reshape(n_slots, 1), tile_expert


def _final_kernel(x1_ref, ya_ref, yb_ref, g2_ref, fg_ref, o_ref):
    x = x1_ref[0] + g2_ref[0] * (ya_ref[0] + yb_ref[0])
    o_ref[0] = _rmsnorm(x, fg_ref[...])


def _final_call(x1, ya, yb, mod, final_g):
    n, s, d = x1.shape
    ts = min(ROW_TILE, s)
    row = pl.BlockSpec((1, ts, d), lambda n_, i: (n_, i, 0))
    return pl.pallas_call(
        _final_kernel,
        grid=(n, s // ts),
        in_specs=[row, row, row, _mod_spec(mod.shape[1], ts, 5),
                  pl.BlockSpec((1, d), lambda n_, i: (0, 0))],
        out_specs=row,
        out_shape=jax.ShapeDtypeStruct((n, s, d), F32),
        compiler_params=_cparams(("arbitrary", "arbitrary")),
        name="final_norm",
    )(x1, ya, yb, mod, final_g.reshape(1, d))


def kernel(x_prompt, x_sample, c_prompt, c_sample, cache_k, cache_v, state_ssm_re, state_ssm_im,
           rel_bias, ada_w, ada_b, norm1_g, w_in, ssm_a_re, ssm_a_im, ssm_log_dt, ssm_b_re, ssm_b_im,
           ssm_c_re, ssm_c_im, ssm_d, glu_a, glu_b, attn_out_g, ssm_out_g, w_out, norm2_g,
           router_g_w, router_g_b, router_e_w, router_e_b, w_gate, w_up, w_down, final_norm_g):
    if ada_w.shape[0] != 1:
        raise ValueError("single-layer trunk expected")
    nb, s_len, d = x_prompt.shape
    nd, t_len, _ = x_sample.shape
    if s_len != max(WINDOWS):
        raise ValueError("prompt length must equal the widest window")

    mod = _mod_call(jnp.concatenate([c_prompt, c_sample], axis=0).astype(F32), ada_w[0], ada_b[0])
    mod_p = mod[:nb].reshape(nb, 1, 6 * d)
    mod_s = jnp.repeat(mod[nb:], t_len, axis=0).reshape(1, nd * t_len, 6 * d)

    w_in_bf = w_in[0].astype(BF16)
    w_out_bf = w_out[0].astype(BF16)
    xs_rows = x_sample.reshape(1, nd * t_len, d)

    qp, kp, vp, up = _inproj_call(x_prompt, mod_p, norm1_g[0], w_in_bf)
    qs, ks, vs, us = _inproj_call(xs_rows, mod_s, norm1_g[0], w_in_bf)

    att_p = _attn_prompt_call(qp, kp, vp, _prompt_bias_tiles(rel_bias))
    ks3 = ks.reshape(nd, t_len, ATTN_WIDTH)
    vs3 = vs.reshape(nd, t_len, ATTN_WIDTH)
    w_rows = cache_k.shape[2]
    att_s = _attn_decode_call(qs.reshape(nd, t_len, ATTN_WIDTH), ks3, vs3,
                              cache_k[0].reshape(nd, w_rows, ATTN_WIDTH).astype(F32),
                              cache_v[0].reshape(nd, w_rows, ATTN_WIDTH).astype(F32), rel_bias)

    s5 = (ssm_a_re[0], ssm_a_im[0], ssm_log_dt[0], ssm_b_re[0], ssm_b_im[0], ssm_c_re[0], ssm_c_im[0])
    t_op, m_op, p_op, sc = _s5_prompt_operators(*s5)
    d_oct = ssm_d[0].astype(F32).reshape(N_OCTETS, 1, LANES)
    ssm_p, hT_p = _ssm_prompt_call(up, t_op, m_op, p_op, sc, d_oct, _octet_glu(glu_a[0]), _octet_glu(glu_b[0]))
    hT_p = hT_p.reshape(nb, N_OCTETS, 2, OCTET, SSM_STATE)
    ssm_re_p = hT_p[:, :, 0].reshape(nb, SSM_GROUPS, SSM_STATE)
    ssm_im_p = hT_p[:, :, 1].reshape(nb, SSM_GROUPS, SSM_STATE)
    us_tm = jnp.transpose(us.reshape(nd, t_len, SSM_WIDTH), (1, 0, 2))
    ssm_s, hre_s, him_s = _ssm_decode_call(us_tm, state_ssm_re[0], state_ssm_im[0], *s5,
                                           ssm_d[0], glu_a[0], glu_b[0])
    ssm_s = jnp.transpose(ssm_s, (1, 0, 2))

    wr, rb = _router_weights(router_g_w[0], router_g_b[0], router_e_w[0], router_e_b[0])
    x1_p, h2_p, route_p = _outproj_call(att_p, ssm_p, x_prompt, mod_p, attn_out_g[0], ssm_out_g[0],
                                        norm2_g[0], w_out_bf, wr, rb)
    x1_s, h2_s, route_s = _outproj_call(att_s.reshape(1, nd * t_len, ATTN_WIDTH),
                                        ssm_s.reshape(1, nd * t_len, SSM_WIDTH), xs_rows, mod_s,
                                        attn_out_g[0], ssm_out_g[0], norm2_g[0], w_out_bf, wr, rb)

    n_p = nb * s_len
    h2_all = jnp.concatenate([h2_p.reshape(n_p, d), h2_s.reshape(nd * t_len, d)], axis=0)
    route = jnp.concatenate([route_p.reshape(n_p, LANES), route_s.reshape(nd * t_len, LANES)], axis=0)
    pos, tok_of_slot, w_of_slot, tile_expert = _moe_dispatch(route)
    ne = N_EXPERTS
    wg = w_gate[0].reshape(ne, d, D_EXPERT).astype(BF16)
    wu = w_up[0].reshape(ne, d, D_EXPERT).astype(BF16)
    wd = w_down[0].reshape(ne, D_EXPERT, d).astype(BF16)
    y_slots = _moe_call(tile_expert, jnp.take(h2_all, tok_of_slot, axis=0), w_of_slot, wg, wu, wd)
    ya = jnp.take(y_slots, pos[:, 0], axis=0)
    yb = jnp.take(y_slots, pos[:, 1], axis=0)

    y_p = _final_call(x1_p, ya[:n_p].reshape(nb, s_len, d), yb[:n_p].reshape(nb, s_len, d),
                      mod_p, final_norm_g)
    y_s = _final_call(x1_s, ya[n_p:].reshape(1, nd * t_len, d), yb[n_p:].reshape(1, nd * t_len, d),
                      mod_s, final_norm_g)

    heads = (ATTN_HEADS, HEAD_DIM)
    return (y_p, y_s.reshape(nd, t_len, d),
            kp.reshape((1, nb, s_len) + heads), vp.reshape((1, nb, s_len) + heads),
            ks3.reshape((1, nd, t_len) + heads), vs3.reshape((1, nd, t_len) + heads),
            ssm_re_p[None], ssm_im_p[None],
            hre_s.reshape(1, nd, SSM_GROUPS, SSM_STATE), him_s.reshape(1, nd, SSM_GROUPS, SSM_STATE))
```

```python
import functools
import math

import numpy as np

import jax
import jax.numpy as jnp
from jax import lax
from jax.experimental import pallas as pl
from jax.experimental.pallas import tpu as pltpu

F32 = jnp.float32
BF16 = jnp.bfloat16

D_MODEL = 1024
HEAD_DIM = 64
ATTN_WIDTH = 512
ATTN_HEADS = 8
SSM_WIDTH = 512
SSM_GROUP_CH = 16
SSM_GROUPS = 32
SSM_STATE = 64
WINDOWS = (128, 512, 2048)
DILATIONS = (1, 4, 16)
WINDOW_STEPS = 128
N_BUCKETS = 32
MAX_EXACT = 16
BUCKET_MAX_DIST = 2048
N_EXPERT_GROUPS = 4
EXPERTS_PER_GROUP = 4
N_EXPERTS = N_EXPERT_GROUPS * EXPERTS_PER_GROUP
D_EXPERT = 512
NORM_EPS = 1e-6

LANES = 128
Q_ROWS = 16
OCTET = LANES // SSM_GROUP_CH
N_OCTETS = SSM_GROUPS // OCTET
OCT_STATE = OCTET * SSM_STATE
SSM_CHUNK = 16
ROW_TILE = 512
MOE_TILE = 512
ATTN_UNROLL = 2
VMEM_LIMIT = 56 * 1024 * 1024

_NEG_INF = float("-inf")
_HIGHEST = lax.Precision.HIGHEST


def _cparams(sem):
    return pltpu.CompilerParams(dimension_semantics=sem, vmem_limit_bytes=VMEM_LIMIT)


def _rmsnorm(x, g):
    return x * lax.rsqrt(jnp.mean(x * x, axis=-1, keepdims=True) + NORM_EPS) * g


def _gelu_tanh(x):
    c = math.sqrt(2.0 / math.pi)
    return 0.5 * x * (1.0 + jnp.tanh(c * (x + 0.044715 * (x * x * x))))


def _sigmoid(x):
    return 1.0 / (1.0 + jnp.exp(-x))


def _mod_kernel(c_ref, w_ref, b_ref, o_ref):
    c = c_ref[...]
    a = (c * _sigmoid(c)).astype(BF16)
    o_ref[...] = jnp.dot(a, w_ref[...].astype(BF16), preferred_element_type=F32) + b_ref[...]


def _mod_call(c_all, ada_w, ada_b):
    rows, d = c_all.shape
    n_out = ada_w.shape[1]
    tn = 1024
    return pl.pallas_call(
        _mod_kernel,
        grid=(n_out // tn,),
        in_specs=[pl.BlockSpec((rows, d), lambda j: (0, 0)),
                  pl.BlockSpec((d, tn), lambda j: (0, j)),
                  pl.BlockSpec((1, tn), lambda j: (0, j))],
        out_specs=pl.BlockSpec((rows, tn), lambda j: (0, j)),
        out_shape=jax.ShapeDtypeStruct((rows, n_out), F32),
        compiler_params=_cparams(("arbitrary",)),
        name="adaln_mod",
    )(c_all, ada_w, ada_b.reshape(1, n_out))


def _mod_spec(mod, rows, ts, chunk):
    if mod.shape[1] == 1:
        tiles_per_group = (rows // mod.shape[0]) // ts
        return pl.BlockSpec((1, 1, D_MODEL), lambda i: (i // tiles_per_group, 0, chunk))
    return pl.BlockSpec((1, ts, D_MODEL), lambda i: (0, i, chunk))


def _inproj_kernel(x_ref, sh_ref, sc_ref, g_ref, w_ref, q_ref, k_ref, v_ref, u_ref):
    h = _rmsnorm(x_ref[...], g_ref[...]) * (1.0 + sc_ref[0]) + sh_ref[0]
    z = jnp.dot(h.astype(BF16), w_ref[...], preferred_element_type=F32)
    aw = ATTN_WIDTH
    q_ref[...] = z[:, :aw]
    k_ref[...] = z[:, aw:2 * aw]
    v_ref[...] = z[:, 2 * aw:3 * aw]
    u_ref[...] = z[:, 3 * aw:]


def _inproj_call(x_rows, mod, norm_g, w_in_bf):
    rows, d = x_rows.shape
    ts = min(ROW_TILE, rows)
    proj = w_in_bf.shape[1]
    out = jax.ShapeDtypeStruct((rows, ATTN_WIDTH), F32)
    ospec = pl.BlockSpec((ts, ATTN_WIDTH), lambda i: (i, 0))
    return pl.pallas_call(
        _inproj_kernel,
        grid=(rows // ts,),
        in_specs=[pl.BlockSpec((ts, d), lambda i: (i, 0)),
                  _mod_spec(mod, rows, ts, 0),
                  _mod_spec(mod, rows, ts, 1),
                  pl.BlockSpec((1, d), lambda i: (0, 0)),
                  pl.BlockSpec((d, proj), lambda i: (0, 0))],
        out_specs=[ospec, ospec, ospec, ospec],
        out_shape=[out, out, out, out],
        compiler_params=_cparams(("arbitrary",)),
        name="inproj",
    )(x_rows, mod, mod, norm_g.reshape(1, d), w_in_bf)


def _t5_bucket(dist):
    d = jnp.maximum(dist, MAX_EXACT).astype(F32)
    log_part = MAX_EXACT + (jnp.log(d / MAX_EXACT) / math.log(BUCKET_MAX_DIST / MAX_EXACT)
                            * (N_BUCKETS - MAX_EXACT)).astype(jnp.int32)
    return jnp.where(dist < MAX_EXACT, dist, jnp.minimum(log_part, N_BUCKETS - 1))


def _bias_by_distance(rel_bias, dists):
    hot = (_t5_bucket(jnp.asarray(dists, jnp.int32))[:, None]
           == jnp.arange(N_BUCKETS, dtype=jnp.int32)[None, :]).astype(F32)
    return jnp.dot(hot, rel_bias.astype(F32), precision=_HIGHEST)


def _prompt_bias_tiles(rel_bias):
    steps = WINDOW_STEPS
    period = 3 * steps
    tiles = []
    for r in DILATIONS:
        vec = _bias_by_distance(rel_bias, r * np.arange(steps + 1))
        fill = jnp.full((steps - 1, ATTN_HEADS), _NEG_INF, F32)
        w = jnp.concatenate([fill, vec[::-1], fill, fill[:1]], axis=0)
        rep = jnp.tile(w.T, (1, steps))[:, :steps * (period - 1)]
        toe = rep.reshape(ATTN_HEADS, steps, period - 1)[:, :, steps - 1:]
        tiles.append(toe.reshape(ATTN_HEADS // 2, 2, steps, 2 * steps))
    return jnp.stack(tiles)


def _attn_prompt_kernel(q_ref, k_ref, v_ref, bias_ref, o_ref, num_ref, m_ref, den_ref):
    s_len = q_ref.shape[1]
    steps = WINDOW_STEPS
    lane = lax.broadcasted_iota(jnp.int32, (1, LANES), 1)
    first_head = lane < HEAD_DIM

    def tile(branch, q_start, k_start, n_keys):
        r = DILATIONS[branch]
        q2 = q_ref[0, pl.ds(q_start, steps, stride=r), :] * (HEAD_DIM ** -0.5)
        k2 = k_ref[0, pl.ds(k_start, n_keys, stride=r), :].astype(BF16)
        v2 = v_ref[0, pl.ds(k_start, n_keys, stride=r), :].astype(BF16)
        per_head = []
        for hh in range(2):
            sel = first_head if hh == 0 else jnp.logical_not(first_head)
            qh = jnp.where(sel, q2, 0.0).astype(BF16)
            sc = lax.dot_general(qh, k2, (((1,), (1,)), ((), ())), preferred_element_type=F32)
            sc = sc + bias_ref[branch, 0, hh, :, 2 * steps - n_keys:]
            m = jnp.max(sc, axis=1, keepdims=True)
            p = jnp.exp(sc - m)
            den = jnp.sum(p, axis=1, keepdims=True)
            o = jnp.dot(p.astype(BF16), v2, preferred_element_type=F32)
            per_head.append((o, m, den))
        (o0, m0, d0), (o1, m1, d1) = per_head
        rows = pl.ds(q_start, steps, stride=r)
        num_ref[branch, rows, :] = jnp.where(first_head, o0, o1)
        m_ref[branch, rows, :] = jnp.where(first_head, m0, m1)
        den_ref[branch, rows, :] = jnp.where(first_head, d0, d1)

    for branch, r in enumerate(DILATIONS):
        class_len = s_len // r
        n_blocks = class_len // steps

        def first_block(rho, carry, branch=branch):
            tile(branch, rho, rho, steps)
            return carry

        lax.fori_loop(0, r, first_block, 0, unroll=min(ATTN_UNROLL, r))
        if n_blocks > 1:
            def later_block(i, carry, branch=branch, r=r, n_blocks=n_blocks):
                rho = i // (n_blocks - 1)
                b = i % (n_blocks - 1) + 1
                tile(branch, rho + r * steps * b, rho + r * steps * (b - 1), 2 * steps)
                return carry

            lax.fori_loop(0, r * (n_blocks - 1), later_block, 0, unroll=ATTN_UNROLL)

    def merge(i, carry):
        rows = pl.ds(pl.multiple_of(i * 256, 256), 256)
        m0, m1, m2 = m_ref[0, rows, :], m_ref[1, rows, :], m_ref[2, rows, :]
        m_all = jnp.maximum(jnp.maximum(m0, m1), m2)
        w0, w1, w2 = jnp.exp(m0 - m_all), jnp.exp(m1 - m_all), jnp.exp(m2 - m_all)
        num = w0 * num_ref[0, rows, :] + w1 * num_ref[1, rows, :] + w2 * num_ref[2, rows, :]
        den = w0 * den_ref[0, rows, :] + w1 * den_ref[1, rows, :] + w2 * den_ref[2, rows, :]
        o_ref[0, rows, :] = num / den
        return carry

    lax.fori_loop(0, s_len // 256, merge, 0)


def _attn_prompt_call(q, k, v, bias_tiles):
    n, s, _ = q.shape
    pairs = ATTN_HEADS // 2
    qspec = pl.BlockSpec((1, s, LANES), lambda n_, g: (n_, 0, g))
    return pl.pallas_call(
        _attn_prompt_kernel,
        grid=(n, pairs),
        in_specs=[qspec, qspec, qspec,
                  pl.BlockSpec((3, 1, 2, WINDOW_STEPS, 2 * WINDOW_STEPS),
                               lambda n_, g: (0, g, 0, 0, 0))],
        out_specs=qspec,
        out_shape=jax.ShapeDtypeStruct((n, s, ATTN_WIDTH), F32),
        scratch_shapes=[pltpu.VMEM((3, s, LANES), F32),
                        pltpu.VMEM((3, s, LANES), F32),
                        pltpu.VMEM((3, s, LANES), F32)],
        compiler_params=_cparams(("arbitrary", "arbitrary")),
        name="attn_prompt",
    )(q, k, v, bias_tiles)


def _s5_discretise(a_re, a_im, log_dt, b_re, b_im):
    lam_re = jnp.minimum(a_re.astype(F32), -1e-4)
    lam_im = a_im.astype(F32)
    dt = jnp.exp(log_dt.astype(F32))[:, None]
    mag = jnp.exp(lam_re * dt)
    ph = lam_im * dt
    abar_re, abar_im = mag * jnp.cos(ph), mag * jnp.sin(ph)
    nr, ni = abar_re - 1.0, abar_im
    den = lam_re * lam_re + lam_im * lam_im
    coef_re = (nr * lam_re + ni * lam_im) / den
    coef_im = (ni * lam_re - nr * lam_im) / den
    br, bi = b_re.astype(F32), b_im.astype(F32)
    bbar_re = coef_re[..., None] * br - coef_im[..., None] * bi
    bbar_im = coef_re[..., None] * bi + coef_im[..., None] * br
    return lam_re * dt, ph, abar_re, abar_im, bbar_re, bbar_im


def _abar_power(log_mag, ph, n):
    nf = jnp.asarray(n, F32)[:, None, None]
    mag = jnp.exp(nf * log_mag[None])
    return mag * jnp.cos(nf * ph[None]), mag * jnp.sin(nf * ph[None])


def _s5_prompt_operators(a_re, a_im, log_dt, b_re, b_im, c_re, c_im):
    L = SSM_CHUNK
    log_mag, ph, _, _, bb_re, bb_im = _s5_discretise(a_re, a_im, log_dt, b_re, b_im)
    cr, ci = c_re.astype(F32), c_im.astype(F32)
    pw_re, pw_im = _abar_power(log_mag, ph, np.arange(L + 1))
    eye = jnp.eye(OCTET, dtype=F32)

    ab_re = pw_re[:L, :, :, None] * bb_re[None] - pw_im[:L, :, :, None] * bb_im[None]
    ab_im = pw_re[:L, :, :, None] * bb_im[None] + pw_im[:L, :, :, None] * bb_re[None]
    lag = (jnp.einsum('gop,lgpi->lgoi', cr, ab_re, precision=_HIGHEST)
           - jnp.einsum('gop,lgpi->lgoi', ci, ab_im, precision=_HIGHEST))
    lag = lag.reshape(L, N_OCTETS, OCTET, SSM_GROUP_CH, SSM_GROUP_CH)
    bd = jnp.einsum('logci,gh->olgihc', lag, eye).reshape(N_OCTETS, L, LANES, LANES).astype(BF16)
    stack = bd[:, ::-1].reshape(N_OCTETS, L * LANES, LANES)
    shifted = jnp.concatenate([stack[:, LANES:], jnp.zeros((N_OCTETS, LANES, LANES), BF16)], axis=1)
    t_op = jnp.concatenate([shifted, stack], axis=-1)

    m_parts = []
    for part in (ab_re[::-1], ab_im[::-1]):
        x = part.reshape(L, N_OCTETS, OCTET, SSM_STATE, SSM_GROUP_CH)
        m_parts.append(jnp.einsum('sogpi,gh->osgihp', x, eye).reshape(N_OCTETS, L * LANES, OCT_STATE))
    m_op = jnp.concatenate(m_parts, axis=-1)

    p1_re, p1_im = pw_re[1:], pw_im[1:]
    on_re = cr[None] * p1_re[:, :, None, :] - ci[None] * p1_im[:, :, None, :]
    on_im = -cr[None] * p1_im[:, :, None, :] - ci[None] * p1_re[:, :, None, :]
    p_parts = []
    for part in (on_re, on_im):
        x = part.reshape(L, N_OCTETS, OCTET, SSM_GROUP_CH, SSM_STATE)
        p_parts.append(jnp.einsum('togcp,gh->ogpthc', x, eye).reshape(N_OCTETS, OCT_STATE, L * LANES))
    p_op = jnp.concatenate(p_parts, axis=1)

    n_steps = 8
    sc_re, sc_im = _abar_power(log_mag, ph, L * (2 ** np.arange(n_steps)))
    sc = jnp.concatenate([sc_re.reshape(n_steps, N_OCTETS, OCT_STATE),
                          sc_im.reshape(n_steps, N_OCTETS, OCT_STATE)], axis=-1)
    sc = jnp.transpose(sc, (1, 0, 2))
    return t_op, m_op.astype(BF16), p_op.astype(BF16), sc


def _octet_glu(glu):
    eye = jnp.eye(OCTET, dtype=F32)
    x = glu.astype(F32).reshape(N_OCTETS, OCTET, SSM_GROUP_CH, SSM_GROUP_CH)
    return jnp.einsum('ogce,gh->ogche', x, eye).reshape(N_OCTETS, LANES, LANES).astype(BF16)


def _ssm_prompt_kernel(u_ref, t_ref, m_ref, p_ref, sc_ref, d_ref, ga_ref, gb_ref,
                       y_ref, h_ref, uf_ref, ub_ref):
    L = SSM_CHUNK
    n_chunks = u_ref.shape[1] // L
    for s in range(L):
        blk = u_ref[0, pl.ds(s, n_chunks, stride=L), :]
        uf_ref[:, s * LANES:(s + 1) * LANES] = blk
        ub_ref[:, s * LANES:(s + 1) * LANES] = blk.astype(BF16)
    ub = ub_ref[...]

    x = jnp.dot(ub, m_ref[0], preferred_element_type=F32)
    row = lax.broadcasted_iota(jnp.int32, (n_chunks, 1), 0)
    half = OCT_STATE
    k = 1
    step = 0
    while k < n_chunks:
        a_re = sc_ref[0, step:step + 1, :half]
        a_im = sc_ref[0, step:step + 1, half:]
        sh = jnp.where(row >= k, pltpu.roll(x, k, axis=0), 0.0)
        s_re, s_im = sh[:, :half], sh[:, half:]
        x = x + jnp.concatenate([a_re * s_re - a_im * s_im, a_re * s_im + a_im * s_re], axis=1)
        k *= 2
        step += 1
    h_ref[0, 0] = x[n_chunks - 1:n_chunks, :]
    h_start = jnp.where(row >= 1, pltpu.roll(x, 1, axis=0), 0.0)

    hb = h_start.astype(BF16)
    d = d_ref[0]
    ga = ga_ref[0]
    gb = gb_ref[0]
    for t in range(0, L, 2):
        pair = slice(t * LANES, (t + 2) * LANES)
        y2 = (jnp.dot(ub_ref[:, :(t + 2) * LANES], t_ref[0, (L - 2 - t) * LANES:, :],
                      preferred_element_type=F32)
              + jnp.dot(hb, p_ref[0, :, pair], preferred_element_type=F32))
        for j in range(2):
            lanes = slice((t + j) * LANES, (t + j + 1) * LANES)
            g = _gelu_tanh(y2[:, j * LANES:(j + 1) * LANES] + d * uf_ref[:, lanes]).astype(BF16)
            out = (jnp.dot(g, ga, preferred_element_type=F32)
                   * _sigmoid(jnp.dot(g, gb, preferred_element_type=F32)))
            y_ref[0, pl.ds(t + j, n_chunks, stride=L), :] = out


def _ssm_prompt_call(u, t_op, m_op, p_op, sc, d_oct, ga, gb):
    n, s, _ = u.shape
    L = SSM_CHUNK
    n_chunks = s // L
    wide = L * LANES
    wspec = lambda shape: pl.BlockSpec((1,) + shape, lambda o, n_: (o, 0, 0))
    return pl.pallas_call(
        _ssm_prompt_kernel,
        grid=(N_OCTETS, n),
        in_specs=[pl.BlockSpec((1, s, LANES), lambda o, n_: (n_, 0, o)),
                  wspec((wide, 2 * LANES)), wspec((wide, 2 * OCT_STATE)), wspec((2 * OCT_STATE, wide)),
                  wspec((8, 2 * OCT_STATE)), wspec((1, LANES)),
                  wspec((LANES, LANES)), wspec((LANES, LANES))],
        out_specs=[pl.BlockSpec((1, s, LANES), lambda o, n_: (n_, 0, o)),
                   pl.BlockSpec((1, 1, 1, 2 * OCT_STATE), lambda o, n_: (n_, o, 0, 0))],
        out_shape=[jax.ShapeDtypeStruct((n, s, SSM_WIDTH), F32),
                   jax.ShapeDtypeStruct((n, N_OCTETS, 1, 2 * OCT_STATE), F32)],
        scratch_shapes=[pltpu.VMEM((n_chunks, wide), F32),
                        pltpu.VMEM((n_chunks, wide), BF16)],
        compiler_params=_cparams(("arbitrary", "arbitrary")),
        name="ssm_prompt",
    )(u, t_op, m_op, p_op, sc, d_oct, ga, gb)


def _attn_decode_kernel(q_ref, kn_ref, vn_ref, kt_ref, vt_ref, b_ref, mult_ref, o_ref):
    nt = (((1,), (1,)), ((), ()))
    mult = mult_ref[...]
    for h in range(ATTN_HEADS):
        q = q_ref[0, h]
        kt = jnp.concatenate([kt_ref[0, h].astype(BF16), kn_ref[0, h]], axis=1)
        vt = jnp.concatenate([vt_ref[0, h].astype(BF16), vn_ref[0, h]], axis=1)
        s = jnp.dot(q, kt, preferred_element_type=F32) + b_ref[h]
        m = jnp.max(s, axis=1, keepdims=True)
        p = jnp.exp(s - m) * mult
        den = jnp.sum(p, axis=1, keepdims=True)
        o = lax.dot_general(p.astype(BF16), vt, nt, preferred_element_type=F32)
        o_ref[0, h] = o / den


def _decode_tables(rel_bias, t_len, w_rows):
    t = np.arange(t_len)[:, None]
    dist = np.concatenate([w_rows + t - np.arange(w_rows)[None, :],
                           t - np.arange(LANES)[None, :]], axis=1)
    mult = np.zeros(dist.shape, np.float32)
    for w, r in zip(WINDOWS, DILATIONS):
        mult += (dist >= 0) & (dist % r == 0) & (dist <= w)
    mult = np.concatenate([mult, np.zeros((Q_ROWS - t_len, dist.shape[1]), np.float32)], axis=0)
    mult[t_len:, 0] = 1.0
    by_dist = _bias_by_distance(rel_bias, np.arange(w_rows + t_len))
    rows = []
    for ti in range(t_len):
        cache_part = by_dist[ti + 1:w_rows + ti + 1][::-1]
        new_part = by_dist[:ti + 1][::-1]
        pad = jnp.zeros((LANES - ti - 1, ATTN_HEADS), F32)
        rows.append(jnp.concatenate([cache_part, new_part, pad], axis=0))
    bias = jnp.stack(rows + [jnp.zeros_like(rows[0])] * (Q_ROWS - t_len), axis=0)
    bias = jnp.transpose(bias, (2, 0, 1))
    bias = jnp.where(jnp.asarray(mult)[None] > 0, bias, _NEG_INF)
    return bias, jnp.asarray(mult)


def _attn_decode_call(q, k_new, v_new, cache_k, cache_v, rel_bias):
    n, t_len, w = q.shape
    w_rows = cache_k.shape[1]
    if t_len > min(DILATIONS[1:]) or t_len > Q_ROWS or w_rows < max(WINDOWS):
        raise ValueError("unsupported decode shape")
    heads = (ATTN_HEADS, HEAD_DIM)

    def head_major(a, pad_to):
        a = jnp.transpose(a.reshape((n, t_len) + heads), (0, 2, 1, 3))
        return jnp.pad(a, ((0, 0), (0, 0), (0, pad_to - t_len), (0, 0)))

    qh = head_major(q * (HEAD_DIM ** -0.5), Q_ROWS).astype(BF16)
    knt = jnp.swapaxes(head_major(k_new, LANES), 2, 3).astype(BF16)
    vnt = jnp.swapaxes(head_major(v_new, LANES), 2, 3).astype(BF16)
    kt = jnp.transpose(cache_k.astype(F32), (0, 2, 3, 1))
    vt = jnp.transpose(cache_v.astype(F32), (0, 2, 3, 1))
    bias, mult = _decode_tables(rel_bias, t_len, w_rows)
    keys = w_rows + LANES
    per_seq = lambda shape: pl.BlockSpec((1,) + shape, lambda i: (i, 0, 0, 0))
    out = pl.pallas_call(
        _attn_decode_kernel,
        grid=(n,),
        in_specs=[per_seq((ATTN_HEADS, Q_ROWS, HEAD_DIM)),
                  per_seq((ATTN_HEADS, HEAD_DIM, LANES)), per_seq((ATTN_HEADS, HEAD_DIM, LANES)),
                  per_seq((ATTN_HEADS, HEAD_DIM, w_rows)), per_seq((ATTN_HEADS, HEAD_DIM, w_rows)),
                  pl.BlockSpec((ATTN_HEADS, Q_ROWS, keys), lambda i: (0, 0, 0)),
                  pl.BlockSpec((Q_ROWS, keys), lambda i: (0, 0))],
        out_specs=per_seq((ATTN_HEADS, Q_ROWS, HEAD_DIM)),
        out_shape=jax.ShapeDtypeStruct((n, ATTN_HEADS, Q_ROWS, HEAD_DIM), F32),
        compiler_params=_cparams(("arbitrary",)),
        name="attn_decode",
    )(qh, knt, vnt, kt, vt, bias, mult)
    return jnp.transpose(out[:, :, :t_len], (0, 2, 1, 3)).reshape(n, t_len, w)


def _ssm_decode_kernel(u_ref, hre_ref, him_ref, are_ref, aim_ref, bre_ref, bim_ref,
                       cre_ref, cim_ref, d_ref, ga_ref, gb_ref, y_ref, ore_ref, oim_ref, *, t_len):
    h_re, h_im = hre_ref[...], him_ref[...]
    a_re, a_im = are_ref[...], aim_ref[...]
    for t in range(t_len):
        u = u_ref[t]
        ub = u.astype(BF16)
        n_re = a_re * h_re - a_im * h_im + jnp.dot(ub, bre_ref[...], preferred_element_type=F32)
        n_im = a_re * h_im + a_im * h_re + jnp.dot(ub, bim_ref[...], preferred_element_type=F32)
        h_re, h_im = n_re, n_im
        y = (jnp.dot(h_re.astype(BF16), cre_ref[...], preferred_element_type=F32)
             - jnp.dot(h_im.astype(BF16), cim_ref[...], preferred_element_type=F32)
             + d_ref[...] * u)
        g = _gelu_tanh(y).astype(BF16)
        y_ref[t] = (jnp.dot(g, ga_ref[...], preferred_element_type=F32)
                    * _sigmoid(jnp.dot(g, gb_ref[...], preferred_element_type=F32)))
    ore_ref[...] = h_re
    oim_ref[...] = h_im


def _group_blockdiag(x):
    g, a, b = x.shape
    return jnp.einsum('gab,gh->gahb', x, jnp.eye(g, dtype=x.dtype)).reshape(g * a, g * b)


def _ssm_decode_call(u_tm, h0_re, h0_im, a_re, a_im, log_dt, b_re, b_im, c_re, c_im,
                     d_skip, glu_a, glu_b):
    n = h0_re.shape[0]
    _, _, abar_re, abar_im, bb_re, bb_im = _s5_discretise(a_re, a_im, log_dt, b_re, b_im)
    state = SSM_GROUPS * SSM_STATE
    t_len = u_tm.shape[0]
    args = (u_tm, h0_re.reshape(n, state).astype(F32), h0_im.reshape(n, state).astype(F32),
            abar_re.reshape(1, state), abar_im.reshape(1, state),
            _group_blockdiag(jnp.transpose(bb_re, (0, 2, 1))).astype(BF16),
            _group_blockdiag(jnp.transpose(bb_im, (0, 2, 1))).astype(BF16),
            _group_blockdiag(jnp.transpose(c_re.astype(F32), (0, 2, 1))).astype(BF16),
            _group_blockdiag(jnp.transpose(c_im.astype(F32), (0, 2, 1))).astype(BF16),
            d_skip.astype(F32).reshape(1, SSM_WIDTH),
            _group_blockdiag(glu_a.astype(F32)).astype(BF16),
            _group_blockdiag(glu_b.astype(F32)).astype(BF16))
    full = lambda a: pl.BlockSpec(a.shape, lambda i: (0,) * a.ndim)
    out_shape = [jax.ShapeDtypeStruct(u_tm.shape, F32),
                 jax.ShapeDtypeStruct((n, state), F32), jax.ShapeDtypeStruct((n, state), F32)]
    return pl.pallas_call(
        functools.partial(_ssm_decode_kernel, t_len=t_len),
        grid=(1,),
        in_specs=[full(a) for a in args],
        out_specs=[full(o) for o in out_shape],
        out_shape=out_shape,
        compiler_params=_cparams(("arbitrary",)),
        name="ssm_decode",
    )(*args)


def _outproj_kernel(att_ref, ssm_ref, x_ref, g1_ref, sh2_ref, sc2_ref, ag_ref, sg_ref, n2_ref,
                    wo_ref, wr_ref, rb_ref, *rest):
    x1_ref, h2_ref, route_ref = rest[-3:]
    mixed = jnp.concatenate([_rmsnorm(att_ref[...], ag_ref[...]), _rmsnorm(ssm_ref[...], sg_ref[...])],
                            axis=1).astype(BF16)
    x1 = x_ref[...] + g1_ref[0] * jnp.dot(mixed, wo_ref[...], preferred_element_type=F32)
    x1_ref[...] = x1
    h2 = _rmsnorm(x1, n2_ref[...]) * (1.0 + sc2_ref[0]) + sh2_ref[0]
    hi = h2.astype(BF16)
    h2_ref[...] = hi
    lo = (h2 - hi.astype(F32)).astype(BF16)
    r1 = jnp.dot(hi, wr_ref[...], preferred_element_type=F32)
    r2 = jnp.dot(lo, wr_ref[:, :LANES], preferred_element_type=F32)
    logits = r1[:, :LANES] + r1[:, LANES:] + r2 + rb_ref[...]

    lane = lax.broadcasted_iota(jnp.int32, (1, LANES), 1)
    lane_f = lane.astype(F32)
    big = float(LANES)
    ng, epg = N_EXPERT_GROUPS, EXPERTS_PER_GROUP
    lg = jnp.where(lane < ng, logits, _NEG_INF)
    gmax = jnp.max(lg, axis=1, keepdims=True)
    p_star = 1.0 / jnp.sum(jnp.exp(lg - gmax), axis=1, keepdims=True)
    g_star = jnp.min(jnp.where(lg == gmax, lane_f, big), axis=1, keepdims=True)
    in_group = ((lane >= ng) & (lane < ng + ng * epg)
                & (lax.shift_right_arithmetic(lane - ng, int(math.log2(epg))).astype(F32) == g_star))
    le = jnp.where(in_group, logits, _NEG_INF)
    v1 = jnp.max(le, axis=1, keepdims=True)
    i1 = jnp.min(jnp.where(le == v1, lane_f, big), axis=1, keepdims=True)
    le2 = jnp.where(lane_f == i1, _NEG_INF, le)
    v2 = jnp.max(le2, axis=1, keepdims=True)
    i2 = jnp.min(jnp.where(le2 == v2, lane_f, big), axis=1, keepdims=True)
    e2 = jnp.exp(v2 - v1)
    w1 = p_star / (1.0 + e2)
    w2 = p_star * e2 / (1.0 + e2)
    route_ref[...] = jnp.where(lane == 0, i1 - ng,
                               jnp.where(lane == 1, i2 - ng,
                                         jnp.where(lane == 2, w1, jnp.where(lane == 3, w2, 0.0))))


def _outproj_call(att, ssm_y, x_rows, mod, attn_g, ssm_g, norm2_g, w_out_bf, wr, rb,
                  total_rows, row_offset, shared=None):
    rows, d = x_rows.shape
    ts = min(ROW_TILE, rows)
    off = row_offset // ts
    row = lambda width: pl.BlockSpec((ts, width), lambda i: (i, 0))
    shared_row = lambda width: pl.BlockSpec((ts, width), lambda i: (i + off, 0))
    const = lambda a: pl.BlockSpec(a.shape, lambda i: (0,) * a.ndim)
    attn_g = attn_g.reshape(1, ATTN_WIDTH)
    ssm_g = ssm_g.reshape(1, SSM_WIDTH)
    norm2_g = norm2_g.reshape(1, d)
    in_specs = [row(ATTN_WIDTH), row(SSM_WIDTH), row(d),
                _mod_spec(mod, rows, ts, 2), _mod_spec(mod, rows, ts, 3), _mod_spec(mod, rows, ts, 4),
                const(attn_g), const(ssm_g), const(norm2_g), const(w_out_bf), const(wr), const(rb)]
    args = [att, ssm_y, x_rows, mod, mod, mod, attn_g, ssm_g, norm2_g, w_out_bf, wr, rb]
    aliases = {}
    if shared is not None:
        in_specs += [pl.BlockSpec(memory_space=pl.ANY), pl.BlockSpec(memory_space=pl.ANY)]
        aliases = {len(args): 1, len(args) + 1: 2}
        args += list(shared)
    return pl.pallas_call(
        _outproj_kernel,
        grid=(rows // ts,),
        in_specs=in_specs,
        out_specs=[row(d), shared_row(d), shared_row(LANES)],
        out_shape=[jax.ShapeDtypeStruct((rows, d), F32), jax.ShapeDtypeStruct((total_rows, d), BF16),
                   jax.ShapeDtypeStruct((total_rows, LANES), F32)],
        input_output_aliases=aliases,
        compiler_params=_cparams(("arbitrary",)),
        name="outproj_router",
    )(*args)


def _router_weights(router_g_w, router_g_b, router_e_w, router_e_b):
    d = router_g_w.shape[0]
    ne = N_EXPERT_GROUPS * EXPERTS_PER_GROUP
    w = jnp.concatenate([router_g_w.astype(F32),
                         jnp.transpose(router_e_w.astype(F32), (1, 0, 2)).reshape(d, ne)], axis=1)
    w = jnp.pad(w, ((0, 0), (0, LANES - w.shape[1])))
    hi = w.astype(BF16)
    lo = (w - hi.astype(F32)).astype(BF16)
    b = jnp.concatenate([router_g_b.astype(F32), router_e_b.astype(F32).reshape(ne)])
    b = jnp.pad(b, (0, LANES - b.shape[0])).reshape(1, LANES)
    return jnp.concatenate([hi, lo], axis=1), b


def _moe_kernel(te_ref, x_ref, wg_ref, wu_ref, wd_ref, o_ref):
    del te_ref
    x = x_ref[...]
    gate = jnp.dot(x, wg_ref[0], preferred_element_type=F32)
    up = jnp.dot(x, wu_ref[0], preferred_element_type=F32)
    a = (gate * _sigmoid(gate)) * up
    o_ref[...] = jnp.dot(a.astype(BF16), wd_ref[0], preferred_element_type=F32)


def _moe_call(tile_expert, x_sorted, wg, wu, wd):
    n_slots, d = x_sorted.shape
    tm = MOE_TILE
    fe = wg.shape[2]
    grid_spec = pltpu.PrefetchScalarGridSpec(
        num_scalar_prefetch=1,
        grid=(n_slots // tm,),
        in_specs=[pl.BlockSpec((tm, d), lambda i, te: (i, 0)),
                  pl.BlockSpec((1, d, fe), lambda i, te: (te[i], 0, 0)),
                  pl.BlockSpec((1, d, fe), lambda i, te: (te[i], 0, 0)),
                  pl.BlockSpec((1, fe, d), lambda i, te: (te[i], 0, 0))],
        out_specs=pl.BlockSpec((tm, d), lambda i, te: (i, 0)),
    )
    return pl.pallas_call(
        _moe_kernel,
        grid_spec=grid_spec,
        out_shape=jax.ShapeDtypeStruct((n_slots, d), F32),
        compiler_params=_cparams(("arbitrary",)),
        name="moe_experts",
    )(tile_expert, x_sorted, wg, wu, wd)


def _moe_dispatch(route):
    n_tok = route.shape[0]
    tm = MOE_TILE
    ids = route[:, :2].astype(jnp.int32).reshape(-1)
    n_pairs = ids.shape[0]
    n_slots = (-(-n_pairs // tm) + N_EXPERTS) * tm
    hot = (ids[:, None] == jnp.arange(N_EXPERTS)[None, :]).astype(jnp.int32)
    csum = jnp.cumsum(hot, axis=0)
    rank = jnp.sum((csum - hot) * hot, axis=1)
    counts = csum[-1]
    padded = -(-counts // tm) * tm
    ends = jnp.cumsum(padded)
    starts = ends - padded
    pos = jnp.sum(hot * starts[None, :], axis=1) + rank
    tok_of_slot = jnp.zeros((n_slots,), jnp.int32).at[pos].set(
        jnp.arange(n_pairs, dtype=jnp.int32) // 2, unique_indices=True, mode="promise_in_bounds")
    tile_start = jnp.arange(n_slots // tm, dtype=jnp.int32) * tm
    tile_expert = jnp.minimum(jnp.sum((tile_start[:, None] >= ends[None, :]).astype(jnp.int32), axis=1),
                              N_EXPERTS - 1).astype(jnp.int32)
    return pos.reshape(n_tok, 2), tok_of_slot, tile_expert


def _take_rows(x, idx):
    return x.at[idx].get(mode="promise_in_bounds")


def _final_kernel(x1_ref, ya_ref, yb_ref, route_ref, g2_ref, fg_ref, o_ref):
    wa = route_ref[:, 2:3]
    wb = route_ref[:, 3:4]
    x = x1_ref[...] + g2_ref[0] * (wa * ya_ref[...] + wb * yb_ref[...])
    o_ref[...] = _rmsnorm(x, fg_ref[...])


def _final_call(x1, ya, yb, route, mod, final_g, row_offset):
    rows, d = x1.shape
    ts = min(ROW_TILE, rows)
    off = row_offset // ts
    row = pl.BlockSpec((ts, d), lambda i: (i, 0))
    shared_row = lambda width: pl.BlockSpec((ts, width), lambda i: (i + off, 0))
    return pl.pallas_call(
        _final_kernel,
        grid=(rows // ts,),
        in_specs=[row, shared_row(d), shared_row(d), shared_row(LANES), _mod_spec(mod, rows, ts, 5),
                  pl.BlockSpec((1, d), lambda i: (0, 0))],
        out_specs=row,
        out_shape=jax.ShapeDtypeStruct((rows, d), F32),
        compiler_params=_cparams(("arbitrary",)),
        name="final_norm",
    )(x1, ya, yb, route, mod, final_g.reshape(1, d))


def kernel(x_prompt, x_sample, c_prompt, c_sample, cache_k, cache_v, state_ssm_re, state_ssm_im,
           rel_bias, ada_w, ada_b, norm1_g, w_in, ssm_a_re, ssm_a_im, ssm_log_dt, ssm_b_re, ssm_b_im,
           ssm_c_re, ssm_c_im, ssm_d, glu_a, glu_b, attn_out_g, ssm_out_g, w_out, norm2_g,
           router_g_w, router_g_b, router_e_w, router_e_b, w_gate, w_up, w_down, final_norm_g):
    if ada_w.shape[0] != 1:
        raise ValueError("single-layer trunk expected")
    nb, s_len, d = x_prompt.shape
    nd, t_len, _ = x_sample.shape
    if s_len != max(WINDOWS):
        raise ValueError("prompt length must equal the widest window")
    n_p, n_s = nb * s_len, nd * t_len
    n_tok = n_p + n_s
    if n_p % ROW_TILE or n_s % ROW_TILE:
        raise ValueError("token counts must be multiples of the row tile")

    mod = _mod_call(jnp.concatenate([c_prompt, c_sample], axis=0).astype(F32), ada_w[0], ada_b[0])
    mod_p = mod[:nb].reshape(nb, 1, 6 * d)
    mod_s = jnp.repeat(mod[nb:], t_len, axis=0).reshape(1, n_s, 6 * d)

    w_in_bf = w_in[0].astype(BF16)
    w_out_bf = w_out[0].astype(BF16)
    xp_rows = x_prompt.reshape(n_p, d)
    xs_rows = x_sample.reshape(n_s, d)

    qp, kp, vp, up = _inproj_call(xp_rows, mod_p, norm1_g[0], w_in_bf)
    qs, ks, vs, us = _inproj_call(xs_rows, mod_s, norm1_g[0], w_in_bf)

    seq = lambda a: a.reshape(nb, s_len, ATTN_WIDTH)
    att_p = _attn_prompt_call(seq(qp), seq(kp), seq(vp), _prompt_bias_tiles(rel_bias))
    dec = lambda a: a.reshape(nd, t_len, ATTN_WIDTH)
    att_s = _attn_decode_call(dec(qs), dec(ks), dec(vs), cache_k[0], cache_v[0], rel_bias)

    s5 = (ssm_a_re[0], ssm_a_im[0], ssm_log_dt[0], ssm_b_re[0], ssm_b_im[0], ssm_c_re[0], ssm_c_im[0])
    t_op, m_op, p_op, sc = _s5_prompt_operators(*s5)
    d_oct = ssm_d[0].astype(F32).reshape(N_OCTETS, 1, LANES)
    ssm_p, hT_p = _ssm_prompt_call(seq(up), t_op, m_op, p_op, sc, d_oct,
                                   _octet_glu(glu_a[0]), _octet_glu(glu_b[0]))
    hT_p = hT_p.reshape(nb, N_OCTETS, 2, OCTET, SSM_STATE)
    ssm_re_p = hT_p[:, :, 0].reshape(nb, SSM_GROUPS, SSM_STATE)
    ssm_im_p = hT_p[:, :, 1].reshape(nb, SSM_GROUPS, SSM_STATE)
    us_tm = jnp.transpose(us.reshape(nd, t_len, SSM_WIDTH), (1, 0, 2))
    ssm_s, hre_s, him_s = _ssm_decode_call(us_tm, state_ssm_re[0], state_ssm_im[0], *s5,
                                           ssm_d[0], glu_a[0], glu_b[0])
    ssm_s = jnp.transpose(ssm_s, (1, 0, 2)).reshape(n_s, SSM_WIDTH)

    wr, rb = _router_weights(router_g_w[0], router_g_b[0], router_e_w[0], router_e_b[0])
    norms = (attn_out_g[0], ssm_out_g[0], norm2_g[0], w_out_bf, wr, rb)
    x1_p, h2_all, route = _outproj_call(att_p.reshape(n_p, ATTN_WIDTH), ssm_p.reshape(n_p, SSM_WIDTH),
                                        xp_rows, mod_p, *norms, total_rows=n_tok, row_offset=0)
    x1_s, h2_all, route = _outproj_call(att_s.reshape(n_s, ATTN_WIDTH), ssm_s, xs_rows, mod_s, *norms,
                                        total_rows=n_tok, row_offset=n_p, shared=(h2_all, route))

    pos, tok_of_slot, tile_expert = _moe_dispatch(route)
    ne = N_EXPERTS
    wg = w_gate[0].reshape(ne, d, D_EXPERT).astype(BF16)
    wu = w_up[0].reshape(ne, d, D_EXPERT).astype(BF16)
    wd = w_down[0].reshape(ne, D_EXPERT, d).astype(BF16)
    y_slots = _moe_call(tile_expert, _take_rows(h2_all, tok_of_slot), wg, wu, wd)
    ya = _take_rows(y_slots, pos[:, 0])
    yb = _take_rows(y_slots, pos[:, 1])

    y_p = _final_call(x1_p, ya, yb, route, mod_p, final_norm_g, row_offset=0)
    y_s = _final_call(x1_s, ya, yb, route, mod_s, final_norm_g, row_offset=n_p)

    heads = (ATTN_HEADS, HEAD_DIM)
    return (y_p.reshape(nb, s_len, d), y_s.reshape(nd, t_len, d),
            kp.reshape((1, nb, s_len) + heads), vp.reshape((1, nb, s_len) + heads),
            ks.reshape((1, nd, t_len) + heads), vs.reshape((1, nd, t_len) + heads),
            ssm_re_p[None], ssm_im_p[None],
            hre_s.reshape(1, nd, SSM_GROUPS, SSM_STATE), him_s.reshape(1, nd, SSM_GROUPS, SSM_STATE))
```

```python
import functools
import math

import numpy as np

import jax
import jax.numpy as jnp
from jax import lax
from jax.experimental import pallas as pl
from jax.experimental.pallas import tpu as pltpu

F32 = jnp.float32
BF16 = jnp.bfloat16

D_MODEL = 1024
HEAD_DIM = 64
ATTN_WIDTH = 512
ATTN_HEADS = 8
SSM_WIDTH = 512
SSM_GROUP_CH = 16
SSM_GROUPS = 32
SSM_STATE = 64
WINDOWS = (128, 512, 2048)
DILATIONS = (1, 4, 16)
WINDOW_STEPS = 128
N_BUCKETS = 32
MAX_EXACT = 16
BUCKET_MAX_DIST = 2048
N_EXPERT_GROUPS = 4
EXPERTS_PER_GROUP = 4
N_EXPERTS = N_EXPERT_GROUPS * EXPERTS_PER_GROUP
D_EXPERT = 512
NORM_EPS = 1e-6

LANES = 128
Q_ROWS = 16
OCTET = LANES // SSM_GROUP_CH
N_OCTETS = SSM_GROUPS // OCTET
OCT_STATE = OCTET * SSM_STATE
SSM_CHUNK = 16
ROW_TILE = 512
MOE_TILE = 512
ATTN_UNROLL = 2
VMEM_LIMIT = 56 * 1024 * 1024

_NEG_INF = float("-inf")
_HIGHEST = lax.Precision.HIGHEST


def _cparams(sem):
    return pltpu.CompilerParams(dimension_semantics=sem, vmem_limit_bytes=VMEM_LIMIT)


def _rmsnorm(x, g):
    return x * lax.rsqrt(jnp.mean(x * x, axis=-1, keepdims=True) + NORM_EPS) * g


def _gelu_tanh(x):
    c = math.sqrt(2.0 / math.pi)
    return 0.5 * x * (1.0 + jnp.tanh(c * (x + 0.044715 * (x * x * x))))


def _sigmoid(x):
    return 1.0 / (1.0 + jnp.exp(-x))


def _mod_kernel(c_ref, w_ref, b_ref, o_ref):
    c = c_ref[...]
    a = (c * _sigmoid(c)).astype(BF16)
    o_ref[...] = jnp.dot(a, w_ref[...].astype(BF16), preferred_element_type=F32) + b_ref[...]


def _mod_call(c_all, ada_w, ada_b):
    rows, d = c_all.shape
    n_out = ada_w.shape[1]
    tn = 1024
    return pl.pallas_call(
        _mod_kernel,
        grid=(n_out // tn,),
        in_specs=[pl.BlockSpec((rows, d), lambda j: (0, 0)),
                  pl.BlockSpec((d, tn), lambda j: (0, j)),
                  pl.BlockSpec((1, tn), lambda j: (0, j))],
        out_specs=pl.BlockSpec((rows, tn), lambda j: (0, j)),
        out_shape=jax.ShapeDtypeStruct((rows, n_out), F32),
        compiler_params=_cparams(("arbitrary",)),
        name="adaln_mod",
    )(c_all, ada_w, ada_b.reshape(1, n_out))


def _mod_spec(mod, rows, ts, chunk):
    if mod.shape[1] == 1:
        tiles_per_group = (rows // mod.shape[0]) // ts
        return pl.BlockSpec((1, 1, D_MODEL), lambda i: (i // tiles_per_group, 0, chunk))
    return pl.BlockSpec((1, ts, D_MODEL), lambda i: (0, i, chunk))


def _inproj_kernel(x_ref, sh_ref, sc_ref, g_ref, w_ref, *rest, key_major):
    h = _rmsnorm(x_ref[...], g_ref[...]) * (1.0 + sc_ref[0]) + sh_ref[0]
    hb = h.astype(BF16)
    z = jnp.dot(hb, w_ref[...], preferred_element_type=F32)
    aw = ATTN_WIDTH
    if key_major:
        wkv_t_ref, q_ref, k_ref, v_ref, u_ref, kt_ref, vt_ref = rest
        zt = lax.dot_general(wkv_t_ref[...], hb, (((1,), (1,)), ((), ())), preferred_element_type=F32)
        kt_ref[0] = zt[:aw]
        vt_ref[0] = zt[aw:]
    else:
        q_ref, k_ref, v_ref, u_ref = rest
    q_ref[...] = z[:, :aw]
    k_ref[...] = z[:, aw:2 * aw]
    v_ref[...] = z[:, 2 * aw:3 * aw]
    u_ref[...] = z[:, 3 * aw:]


def _inproj_call(x_rows, mod, norm_g, w_in_bf, seq_len=None):
    rows, d = x_rows.shape
    ts = min(ROW_TILE, rows)
    proj = w_in_bf.shape[1]
    out = jax.ShapeDtypeStruct((rows, ATTN_WIDTH), F32)
    ospec = pl.BlockSpec((ts, ATTN_WIDTH), lambda i: (i, 0))
    in_specs = [pl.BlockSpec((ts, d), lambda i: (i, 0)),
                _mod_spec(mod, rows, ts, 0),
                _mod_spec(mod, rows, ts, 1),
                pl.BlockSpec((1, d), lambda i: (0, 0)),
                pl.BlockSpec((d, proj), lambda i: (0, 0))]
    args = [x_rows, mod, mod, norm_g.reshape(1, d), w_in_bf]
    out_specs = [ospec, ospec, ospec, ospec]
    out_shape = [out, out, out, out]
    if seq_len is not None:
        tiles = seq_len // ts
        in_specs.append(pl.BlockSpec((2 * ATTN_WIDTH, d), lambda i: (0, 0)))
        args.append(w_in_bf[:, ATTN_WIDTH:3 * ATTN_WIDTH].T)
        tspec = pl.BlockSpec((1, ATTN_WIDTH, ts), lambda i: (i // tiles, 0, i % tiles))
        out_specs += [tspec, tspec]
        out_shape += [jax.ShapeDtypeStruct((rows // seq_len, ATTN_WIDTH, seq_len), F32)] * 2
    return pl.pallas_call(
        functools.partial(_inproj_kernel, key_major=seq_len is not None),
        grid=(rows // ts,),
        in_specs=in_specs,
        out_specs=out_specs,
        out_shape=out_shape,
        compiler_params=_cparams(("arbitrary",)),
        name="inproj",
    )(*args)


def _t5_bucket(dist):
    d = jnp.maximum(dist, MAX_EXACT).astype(F32)
    log_part = MAX_EXACT + (jnp.log(d / MAX_EXACT) / math.log(BUCKET_MAX_DIST / MAX_EXACT)
                            * (N_BUCKETS - MAX_EXACT)).astype(jnp.int32)
    return jnp.where(dist < MAX_EXACT, dist, jnp.minimum(log_part, N_BUCKETS - 1))


def _bias_by_distance(rel_bias, dists):
    hot = (_t5_bucket(jnp.asarray(dists, jnp.int32))[:, None]
           == jnp.arange(N_BUCKETS, dtype=jnp.int32)[None, :]).astype(F32)
    return jnp.dot(hot, rel_bias.astype(F32), precision=_HIGHEST)


def _prompt_bias_tiles(rel_bias):
    steps = WINDOW_STEPS
    period = 3 * steps
    tiles = []
    for r in DILATIONS:
        vec = _bias_by_distance(rel_bias, r * np.arange(steps + 1))
        fill = jnp.full((steps - 1, ATTN_HEADS), _NEG_INF, F32)
        w = jnp.concatenate([fill, vec[::-1], fill, fill[:1]], axis=0)
        rep = jnp.tile(w.T, (1, steps))[:, :steps * (period - 1)]
        toe = rep.reshape(ATTN_HEADS, steps, period - 1)[:, :, steps - 1:]
        toe = toe.reshape(ATTN_HEADS // 2, 2, steps, 2 * steps)
        own_only = jnp.where(jnp.arange(2 * steps) < steps, _NEG_INF, toe)
        tiles.append(jnp.stack([toe, own_only], axis=2))
    return jnp.stack(tiles)


def _attn_prompt_kernel(q_ref, k_ref, v_ref, bias_ref, o_ref,
                        p4_ref, qh_ref, kb_ref, vb_ref, s_ref, p_ref,
                        bo_ref, bl_ref, t4o_ref, t4l_ref, on_ref, ln_ref):
    s_len = q_ref.shape[1]
    steps = WINDOW_STEPS
    n_tiles = s_len // steps
    quarter = s_len // 4
    nt = (((1,), (1,)), ((), ()))
    lane = lax.broadcasted_iota(jnp.int32, (1, LANES), 1)
    first_head = lane < HEAD_DIM
    srcs = (q_ref, k_ref, v_ref)

    zero_block = jnp.zeros((steps, LANES), BF16)
    kb_ref[0:steps, :] = zero_block
    vb_ref[0:steps, :] = zero_block
    for x in range(3):
        for sigma in range(4):
            p4_ref[x, sigma * quarter:(sigma + 1) * quarter, :] = srcs[x][0, pl.ds(sigma, quarter, stride=4), :]

    def source(branch, x, tile_idx):
        rows = slice(tile_idx * steps, (tile_idx + 1) * steps)
        if branch == 0:
            return srcs[x][0, rows, :]
        if branch == 1:
            return p4_ref[x, rows, :]
        sigma, tau = tile_idx % 4, tile_idx // 4
        return p4_ref[x, pl.ds(sigma * quarter + tau, steps, stride=4), :]

    for branch, r in enumerate(DILATIONS):
        blocks_per_class = (s_len // r) // steps
        width = 2 * steps if blocks_per_class > 1 else steps

        for t in range(n_tiles):
            rows = slice(t * steps, (t + 1) * steps)
            q2 = source(branch, 0, t) * (HEAD_DIM ** -0.5)
            qh_ref[0, rows, :] = jnp.where(first_head, q2, 0.0).astype(BF16)
            qh_ref[1, rows, :] = jnp.where(first_head, 0.0, q2).astype(BF16)
            kb_ref[steps + t * steps:steps + (t + 1) * steps, :] = source(branch, 1, t).astype(BF16)
            vb_ref[steps + t * steps:steps + (t + 1) * steps, :] = source(branch, 2, t).astype(BF16)

        def key_rows(t, width=width):
            start = t * steps if width == 2 * steps else (t + 1) * steps
            return pl.ds(pl.multiple_of(start, steps), width)

        def scores(t, carry, branch=branch, width=width, blocks_per_class=blocks_per_class,
                   key_rows=key_rows):
            rows = pl.ds(pl.multiple_of(t * steps, steps), steps)
            keys = kb_ref[key_rows(t), :]
            first = (t % blocks_per_class == 0).astype(jnp.int32)
            for hh in range(2):
                if width == 2 * steps:
                    bias = bias_ref[branch, 0, hh, pl.ds(first, 1), :, :][0]
                else:
                    bias = bias_ref[branch, 0, hh, 0, :, steps:]
                sc = lax.dot_general(qh_ref[hh, rows, :], keys, nt, preferred_element_type=F32)
                s_ref[hh, rows, 0:width] = sc + bias
            return carry

        lax.fori_loop(0, n_tiles, scores, 0, unroll=ATTN_UNROLL)

        def softmax(i, carry, width=width):
            rows = pl.ds(pl.multiple_of(i * 2 * steps, 2 * steps), 2 * steps)
            for hh in range(2):
                sc = s_ref[hh, rows, 0:width]
                m = jnp.max(sc, axis=1, keepdims=True)
                p = jnp.exp(sc - m)
                den = jnp.sum(p, axis=1, keepdims=True)
                p_ref[hh, rows, 0:width] = (p * (1.0 / den)).astype(BF16)
                lse = jnp.broadcast_to(m + jnp.log(den), (2 * steps, HEAD_DIM))
                bl_ref[rows, hh * HEAD_DIM:(hh + 1) * HEAD_DIM] = lse
            return carry

        lax.fori_loop(0, n_tiles // 2, softmax, 0, unroll=2)

        out_o = on_ref.at[0] if branch == 0 else bo_ref

        def weighted(t, carry, width=width, out_o=out_o, key_rows=key_rows):
            rows = pl.ds(pl.multiple_of(t * steps, steps), steps)
            vals = vb_ref[key_rows(t), :]
            o0 = jnp.dot(p_ref[0, rows, 0:width], vals, preferred_element_type=F32)
            o1 = jnp.dot(p_ref[1, rows, 0:width], vals, preferred_element_type=F32)
            out_o[rows, :] = jnp.where(first_head, o0, o1)
            return carry

        lax.fori_loop(0, n_tiles, weighted, 0, unroll=ATTN_UNROLL)

        if branch == 0:
            ln_ref[0] = bl_ref[...]
        else:
            for src, stage, dst in ((bo_ref, t4o_ref, on_ref), (bl_ref, t4l_ref, ln_ref)):
                if branch == 2:
                    for t in range(n_tiles):
                        sigma, tau = t % 4, t // 4
                        stage[pl.ds(sigma * quarter + tau, steps, stride=4), :] = src[t * steps:(t + 1) * steps, :]
                    src = stage
                for sigma in range(4):
                    dst[branch, pl.ds(sigma, quarter, stride=4), :] = src[sigma * quarter:(sigma + 1) * quarter, :]

    def merge(i, carry):
        rows = pl.ds(pl.multiple_of(i * 256, 256), 256)
        l0, l1, l2 = ln_ref[0, rows, :], ln_ref[1, rows, :], ln_ref[2, rows, :]
        l_all = jnp.maximum(jnp.maximum(l0, l1), l2)
        w0, w1, w2 = jnp.exp(l0 - l_all), jnp.exp(l1 - l_all), jnp.exp(l2 - l_all)
        num = w0 * on_ref[0, rows, :] + w1 * on_ref[1, rows, :] + w2 * on_ref[2, rows, :]
        o_ref[0, rows, :] = num / (w0 + w1 + w2)
        return carry

    lax.fori_loop(0, s_len // 256, merge, 0)


def _attn_prompt_call(q, k, v, bias_tiles):
    n, s, _ = q.shape
    pairs = ATTN_HEADS // 2
    steps = WINDOW_STEPS
    qspec = pl.BlockSpec((1, s, LANES), lambda n_, g: (n_, 0, g))
    return pl.pallas_call(
        _attn_prompt_kernel,
        grid=(n, pairs),
        in_specs=[qspec, qspec, qspec,
                  pl.BlockSpec((3, 1, 2, 2, steps, 2 * steps), lambda n_, g: (0, g, 0, 0, 0, 0))],
        out_specs=qspec,
        out_shape=jax.ShapeDtypeStruct((n, s, ATTN_WIDTH), F32),
        scratch_shapes=[pltpu.VMEM((3, s, LANES), F32),
                        pltpu.VMEM((2, s, LANES), BF16),
                        pltpu.VMEM((s + steps, LANES), BF16),
                        pltpu.VMEM((s + steps, LANES), BF16),
                        pltpu.VMEM((2, s, 2 * steps), F32),
                        pltpu.VMEM((2, s, 2 * steps), BF16),
                        pltpu.VMEM((s, LANES), F32), pltpu.VMEM((s, LANES), F32),
                        pltpu.VMEM((s, LANES), F32), pltpu.VMEM((s, LANES), F32),
                        pltpu.VMEM((3, s, LANES), F32), pltpu.VMEM((3, s, LANES), F32)],
        compiler_params=_cparams(("arbitrary", "arbitrary")),
        name="attn_prompt",
    )(q, k, v, bias_tiles)


def _s5_discretise(a_re, a_im, log_dt, b_re, b_im):
    lam_re = jnp.minimum(a_re.astype(F32), -1e-4)
    lam_im = a_im.astype(F32)
    dt = jnp.exp(log_dt.astype(F32))[:, None]
    mag = jnp.exp(lam_re * dt)
    ph = lam_im * dt
    abar_re, abar_im = mag * jnp.cos(ph), mag * jnp.sin(ph)
    nr, ni = abar_re - 1.0, abar_im
    den = lam_re * lam_re + lam_im * lam_im
    coef_re = (nr * lam_re + ni * lam_im) / den
    coef_im = (ni * lam_re - nr * lam_im) / den
    br, bi = b_re.astype(F32), b_im.astype(F32)
    bbar_re = coef_re[..., None] * br - coef_im[..., None] * bi
    bbar_im = coef_re[..., None] * bi + coef_im[..., None] * br
    return lam_re * dt, ph, abar_re, abar_im, bbar_re, bbar_im


def _abar_power(log_mag, ph, n):
    nf = jnp.asarray(n, F32)[:, None, None]
    mag = jnp.exp(nf * log_mag[None])
    return mag * jnp.cos(nf * ph[None]), mag * jnp.sin(nf * ph[None])


def _s5_prompt_operators(a_re, a_im, log_dt, b_re, b_im, c_re, c_im):
    L = SSM_CHUNK
    log_mag, ph, _, _, bb_re, bb_im = _s5_discretise(a_re, a_im, log_dt, b_re, b_im)
    cr, ci = c_re.astype(F32), c_im.astype(F32)
    pw_re, pw_im = _abar_power(log_mag, ph, np.arange(L + 1))
    eye = jnp.eye(OCTET, dtype=F32)

    ab_re = pw_re[:L, :, :, None] * bb_re[None] - pw_im[:L, :, :, None] * bb_im[None]
    ab_im = pw_re[:L, :, :, None] * bb_im[None] + pw_im[:L, :, :, None] * bb_re[None]
    lag = (jnp.einsum('gop,lgpi->lgoi', cr, ab_re, precision=_HIGHEST)
           - jnp.einsum('gop,lgpi->lgoi', ci, ab_im, precision=_HIGHEST))
    lag = lag.reshape(L, N_OCTETS, OCTET, SSM_GROUP_CH, SSM_GROUP_CH)
    bd = jnp.einsum('logci,gh->olgihc', lag, eye).reshape(N_OCTETS, L, LANES, LANES).astype(BF16)
    stack = bd[:, ::-1].reshape(N_OCTETS, L * LANES, LANES)
    shifted = jnp.concatenate([stack[:, LANES:], jnp.zeros((N_OCTETS, LANES, LANES), BF16)], axis=1)
    t_op = jnp.concatenate([shifted, stack], axis=-1)

    m_parts = []
    for part in (ab_re[::-1], ab_im[::-1]):
        x = part.reshape(L, N_OCTETS, OCTET, SSM_STATE, SSM_GROUP_CH)
        m_parts.append(jnp.einsum('sogpi,gh->osgihp', x, eye).reshape(N_OCTETS, L * LANES, OCT_STATE))
    m_op = jnp.concatenate(m_parts, axis=-1)

    p1_re, p1_im = pw_re[1:], pw_im[1:]
    on_re = cr[None] * p1_re[:, :, None, :] - ci[None] * p1_im[:, :, None, :]
    on_im = -cr[None] * p1_im[:, :, None, :] - ci[None] * p1_re[:, :, None, :]
    p_parts = []
    for part in (on_re, on_im):
        x = part.reshape(L, N_OCTETS, OCTET, SSM_GROUP_CH, SSM_STATE)
        p_parts.append(jnp.einsum('togcp,gh->ogpthc', x, eye).reshape(N_OCTETS, OCT_STATE, L * LANES))
    p_op = jnp.concatenate(p_parts, axis=1)

    n_steps = 8
    sc_re, sc_im = _abar_power(log_mag, ph, L * (2 ** np.arange(n_steps)))
    sc = jnp.concatenate([sc_re.reshape(n_steps, N_OCTETS, OCT_STATE),
                          sc_im.reshape(n_steps, N_OCTETS, OCT_STATE)], axis=-1)
    sc = jnp.transpose(sc, (1, 0, 2))
    return t_op, m_op.astype(BF16), p_op.astype(BF16), sc


def _octet_glu(glu):
    eye = jnp.eye(OCTET, dtype=F32)
    x = glu.astype(F32).reshape(N_OCTETS, OCTET, SSM_GROUP_CH, SSM_GROUP_CH)
    return jnp.einsum('ogce,gh->ogche', x, eye).reshape(N_OCTETS, LANES, LANES).astype(BF16)


def _ssm_prompt_kernel(u_ref, t_ref, m_ref, p_ref, sc_ref, d_ref, ga_ref, gb_ref,
                       y_ref, h_ref, uf_ref, ub_ref):
    L = SSM_CHUNK
    n_chunks = u_ref.shape[1] // L
    for s in range(L):
        blk = u_ref[0, pl.ds(s, n_chunks, stride=L), :]
        uf_ref[:, s * LANES:(s + 1) * LANES] = blk
        ub_ref[:, s * LANES:(s + 1) * LANES] = blk.astype(BF16)
    ub = ub_ref[...]

    x = jnp.dot(ub, m_ref[0], preferred_element_type=F32)
    row = lax.broadcasted_iota(jnp.int32, (n_chunks, 1), 0)
    half = OCT_STATE
    k = 1
    step = 0
    while k < n_chunks:
        a_re = sc_ref[0, step:step + 1, :half]
        a_im = sc_ref[0, step:step + 1, half:]
        sh = jnp.where(row >= k, pltpu.roll(x, k, axis=0), 0.0)
        s_re, s_im = sh[:, :half], sh[:, half:]
        x = x + jnp.concatenate([a_re * s_re - a_im * s_im, a_re * s_im + a_im * s_re], axis=1)
        k *= 2
        step += 1
    h_ref[0, 0] = x[n_chunks - 1:n_chunks, :]
    h_start = jnp.where(row >= 1, pltpu.roll(x, 1, axis=0), 0.0)

    hb = h_start.astype(BF16)
    d = d_ref[0]
    ga = ga_ref[0]
    gb = gb_ref[0]
    for t in range(0, L, 2):
        pair = slice(t * LANES, (t + 2) * LANES)
        y2 = (jnp.dot(ub_ref[:, :(t + 2) * LANES], t_ref[0, (L - 2 - t) * LANES:, :],
                      preferred_element_type=F32)
              + jnp.dot(hb, p_ref[0, :, pair], preferred_element_type=F32))
        for j in range(2):
            lanes = slice((t + j) * LANES, (t + j + 1) * LANES)
            g = _gelu_tanh(y2[:, j * LANES:(j + 1) * LANES] + d * uf_ref[:, lanes]).astype(BF16)
            out = (jnp.dot(g, ga, preferred_element_type=F32)
                   * _sigmoid(jnp.dot(g, gb, preferred_element_type=F32)))
            y_ref[0, pl.ds(t + j, n_chunks, stride=L), :] = out


def _ssm_prompt_call(u, t_op, m_op, p_op, sc, d_oct, ga, gb):
    n, s, _ = u.shape
    L = SSM_CHUNK
    n_chunks = s // L
    wide = L * LANES
    wspec = lambda shape: pl.BlockSpec((1,) + shape, lambda o, n_: (o, 0, 0))
    return pl.pallas_call(
        _ssm_prompt_kernel,
        grid=(N_OCTETS, n),
        in_specs=[pl.BlockSpec((1, s, LANES), lambda o, n_: (n_, 0, o)),
                  wspec((wide, 2 * LANES)), wspec((wide, 2 * OCT_STATE)), wspec((2 * OCT_STATE, wide)),
                  wspec((8, 2 * OCT_STATE)), wspec((1, LANES)),
                  wspec((LANES, LANES)), wspec((LANES, LANES))],
        out_specs=[pl.BlockSpec((1, s, LANES), lambda o, n_: (n_, 0, o)),
                   pl.BlockSpec((1, 1, 1, 2 * OCT_STATE), lambda o, n_: (n_, o, 0, 0))],
        out_shape=[jax.ShapeDtypeStruct((n, s, SSM_WIDTH), F32),
                   jax.ShapeDtypeStruct((n, N_OCTETS, 1, 2 * OCT_STATE), F32)],
        scratch_shapes=[pltpu.VMEM((n_chunks, wide), F32),
                        pltpu.VMEM((n_chunks, wide), BF16)],
        compiler_params=_cparams(("arbitrary", "arbitrary")),
        name="ssm_prompt",
    )(u, t_op, m_op, p_op, sc, d_oct, ga, gb)


def _attn_decode_kernel(q_ref, kn_ref, vn_ref, kt_ref, vt_ref, b_ref, mult_ref, o_ref):
    nt = (((1,), (1,)), ((), ()))
    mult = mult_ref[...]
    for h in range(ATTN_HEADS):
        q = q_ref[0, h]
        kt = jnp.concatenate([kt_ref[0, h].astype(BF16), kn_ref[0, h]], axis=1)
        vt = jnp.concatenate([vt_ref[0, h].astype(BF16), vn_ref[0, h]], axis=1)
        s = jnp.dot(q, kt, preferred_element_type=F32) + b_ref[h]
        m = jnp.max(s, axis=1, keepdims=True)
        p = jnp.exp(s - m) * mult
        den = jnp.sum(p, axis=1, keepdims=True)
        o = lax.dot_general(p.astype(BF16), vt, nt, preferred_element_type=F32)
        o_ref[0, h] = o / den


def _decode_tables(rel_bias, t_len, w_rows):
    t = np.arange(t_len)[:, None]
    dist = np.concatenate([w_rows + t - np.arange(w_rows)[None, :],
                           t - np.arange(LANES)[None, :]], axis=1)
    mult = np.zeros(dist.shape, np.float32)
    for w, r in zip(WINDOWS, DILATIONS):
        mult += (dist >= 0) & (dist % r == 0) & (dist <= w)
    mult = np.concatenate([mult, np.zeros((Q_ROWS - t_len, dist.shape[1]), np.float32)], axis=0)
    mult[t_len:, 0] = 1.0
    by_dist = _bias_by_distance(rel_bias, np.arange(w_rows + t_len))
    rows = []
    for ti in range(t_len):
        cache_part = by_dist[ti + 1:w_rows + ti + 1][::-1]
        new_part = by_dist[:ti + 1][::-1]
        pad = jnp.zeros((LANES - ti - 1, ATTN_HEADS), F32)
        rows.append(jnp.concatenate([cache_part, new_part, pad], axis=0))
    bias = jnp.stack(rows + [jnp.zeros_like(rows[0])] * (Q_ROWS - t_len), axis=0)
    bias = jnp.transpose(bias, (2, 0, 1))
    bias = jnp.where(jnp.asarray(mult)[None] > 0, bias, _NEG_INF)
    return bias, jnp.asarray(mult)


def _attn_decode_call(q, k_new, v_new, cache_k, cache_v, rel_bias):
    n, t_len, w = q.shape
    w_rows = cache_k.shape[1]
    if t_len > min(DILATIONS[1:]) or t_len > Q_ROWS or w_rows < max(WINDOWS):
        raise ValueError("unsupported decode shape")
    heads = (ATTN_HEADS, HEAD_DIM)

    def head_major(a, pad_to):
        a = jnp.transpose(a.reshape((n, t_len) + heads), (0, 2, 1, 3))
        return jnp.pad(a, ((0, 0), (0, 0), (0, pad_to - t_len), (0, 0)))

    qh = head_major(q * (HEAD_DIM ** -0.5), Q_ROWS).astype(BF16)
    knt = jnp.swapaxes(head_major(k_new, LANES), 2, 3).astype(BF16)
    vnt = jnp.swapaxes(head_major(v_new, LANES), 2, 3).astype(BF16)
    kt = jnp.transpose(cache_k.astype(F32), (0, 2, 3, 1))
    vt = jnp.transpose(cache_v.astype(F32), (0, 2, 3, 1))
    bias, mult = _decode_tables(rel_bias, t_len, w_rows)
    keys = w_rows + LANES
    per_seq = lambda shape: pl.BlockSpec((1,) + shape, lambda i: (i, 0, 0, 0))
    out = pl.pallas_call(
        _attn_decode_kernel,
        grid=(n,),
        in_specs=[per_seq((ATTN_HEADS, Q_ROWS, HEAD_DIM)),
                  per_seq((ATTN_HEADS, HEAD_DIM, LANES)), per_seq((ATTN_HEADS, HEAD_DIM, LANES)),
                  per_seq((ATTN_HEADS, HEAD_DIM, w_rows)), per_seq((ATTN_HEADS, HEAD_DIM, w_rows)),
                  pl.BlockSpec((ATTN_HEADS, Q_ROWS, keys), lambda i: (0, 0, 0)),
                  pl.BlockSpec((Q_ROWS, keys), lambda i: (0, 0))],
        out_specs=per_seq((ATTN_HEADS, Q_ROWS, HEAD_DIM)),
        out_shape=jax.ShapeDtypeStruct((n, ATTN_HEADS, Q_ROWS, HEAD_DIM), F32),
        compiler_params=_cparams(("arbitrary",)),
        name="attn_decode",
    )(qh, knt, vnt, kt, vt, bias, mult)
    return jnp.transpose(out[:, :, :t_len], (0, 2, 1, 3)).reshape(n, t_len, w)


def _ssm_decode_kernel(u_ref, hre_ref, him_ref, are_ref, aim_ref, bre_ref, bim_ref,
                       cre_ref, cim_ref, d_ref, ga_ref, gb_ref, y_ref, ore_ref, oim_ref, *, t_len):
    h_re, h_im = hre_ref[...], him_ref[...]
    a_re, a_im = are_ref[...], aim_ref[...]
    for t in range(t_len):
        u = u_ref[t]
        ub = u.astype(BF16)
        n_re = a_re * h_re - a_im * h_im + jnp.dot(ub, bre_ref[...], preferred_element_type=F32)
        n_im = a_re * h_im + a_im * h_re + jnp.dot(ub, bim_ref[...], preferred_element_type=F32)
        h_re, h_im = n_re, n_im
        y = (jnp.dot(h_re.astype(BF16), cre_ref[...], preferred_element_type=F32)
             - jnp.dot(h_im.astype(BF16), cim_ref[...], preferred_element_type=F32)
             + d_ref[...] * u)
        g = _gelu_tanh(y).astype(BF16)
        y_ref[t] = (jnp.dot(g, ga_ref[...], preferred_element_type=F32)
                    * _sigmoid(jnp.dot(g, gb_ref[...], preferred_element_type=F32)))
    ore_ref[...] = h_re
    oim_ref[...] = h_im


def _group_blockdiag(x):
    g, a, b = x.shape
    return jnp.einsum('gab,gh->gahb', x, jnp.eye(g, dtype=x.dtype)).reshape(g * a, g * b)


def _ssm_decode_call(u_tm, h0_re, h0_im, a_re, a_im, log_dt, b_re, b_im, c_re, c_im,
                     d_skip, glu_a, glu_b):
    n = h0_re.shape[0]
    _, _, abar_re, abar_im, bb_re, bb_im = _s5_discretise(a_re, a_im, log_dt, b_re, b_im)
    state = SSM_GROUPS * SSM_STATE
    t_len = u_tm.shape[0]
    args = (u_tm, h0_re.reshape(n, state).astype(F32), h0_im.reshape(n, state).astype(F32),
            abar_re.reshape(1, state), abar_im.reshape(1, state),
            _group_blockdiag(jnp.transpose(bb_re, (0, 2, 1))).astype(BF16),
            _group_blockdiag(jnp.transpose(bb_im, (0, 2, 1))).astype(BF16),
            _group_blockdiag(jnp.transpose(c_re.astype(F32), (0, 2, 1))).astype(BF16),
            _group_blockdiag(jnp.transpose(c_im.astype(F32), (0, 2, 1))).astype(BF16),
            d_skip.astype(F32).reshape(1, SSM_WIDTH),
            _group_blockdiag(glu_a.astype(F32)).astype(BF16),
            _group_blockdiag(glu_b.astype(F32)).astype(BF16))
    full = lambda a: pl.BlockSpec(a.shape, lambda i: (0,) * a.ndim)
    out_shape = [jax.ShapeDtypeStruct(u_tm.shape, F32),
                 jax.ShapeDtypeStruct((n, state), F32), jax.ShapeDtypeStruct((n, state), F32)]
    return pl.pallas_call(
        functools.partial(_ssm_decode_kernel, t_len=t_len),
        grid=(1,),
        in_specs=[full(a) for a in args],
        out_specs=[full(o) for o in out_shape],
        out_shape=out_shape,
        compiler_params=_cparams(("arbitrary",)),
        name="ssm_decode",
    )(*args)


def _outproj_kernel(att_ref, ssm_ref, x_ref, g1_ref, sh2_ref, sc2_ref, ag_ref, sg_ref, n2_ref,
                    wo_ref, wr_ref, rb_ref, *rest):
    x1_ref, h2_ref, route_ref = rest[-3:]
    mixed = jnp.concatenate([_rmsnorm(att_ref[...], ag_ref[...]), _rmsnorm(ssm_ref[...], sg_ref[...])],
                            axis=1).astype(BF16)
    x1 = x_ref[...] + g1_ref[0] * jnp.dot(mixed, wo_ref[...], preferred_element_type=F32)
    x1_ref[...] = x1
    h2 = _rmsnorm(x1, n2_ref[...]) * (1.0 + sc2_ref[0]) + sh2_ref[0]
    hi = h2.astype(BF16)
    h2_ref[...] = hi
    lo = (h2 - hi.astype(F32)).astype(BF16)
    r1 = jnp.dot(hi, wr_ref[...], preferred_element_type=F32)
    r2 = jnp.dot(lo, wr_ref[:, :LANES], preferred_element_type=F32)
    logits = r1[:, :LANES] + r1[:, LANES:] + r2 + rb_ref[...]

    lane = lax.broadcasted_iota(jnp.int32, (1, LANES), 1)
    lane_f = lane.astype(F32)
    big = float(LANES)
    ng, epg = N_EXPERT_GROUPS, EXPERTS_PER_GROUP
    lg = jnp.where(lane < ng, logits, _NEG_INF)
    gmax = jnp.max(lg, axis=1, keepdims=True)
    p_star = 1.0 / jnp.sum(jnp.exp(lg - gmax), axis=1, keepdims=True)
    g_star = jnp.min(jnp.where(lg == gmax, lane_f, big), axis=1, keepdims=True)
    in_group = ((lane >= ng) & (lane < ng + ng * epg)
                & (lax.shift_right_arithmetic(lane - ng, int(math.log2(epg))).astype(F32) == g_star))
    le = jnp.where(in_group, logits, _NEG_INF)
    v1 = jnp.max(le, axis=1, keepdims=True)
    i1 = jnp.min(jnp.where(le == v1, lane_f, big), axis=1, keepdims=True)
    le2 = jnp.where(lane_f == i1, _NEG_INF, le)
    v2 = jnp.max(le2, axis=1, keepdims=True)
    i2 = jnp.min(jnp.where(le2 == v2, lane_f, big), axis=1, keepdims=True)
    e2 = jnp.exp(v2 - v1)
    w1 = p_star / (1.0 + e2)
    w2 = p_star * e2 / (1.0 + e2)
    route_ref[...] = jnp.where(lane == 0, i1 - ng,
                               jnp.where(lane == 1, i2 - ng,
                                         jnp.where(lane == 2, w1, jnp.where(lane == 3, w2, 0.0))))


def _outproj_call(att, ssm_y, x_rows, mod, attn_g, ssm_g, norm2_g, w_out_bf, wr, rb,
                  total_rows, row_offset, shared=None):
    rows, d = x_rows.shape
    ts = min(ROW_TILE, rows)
    off = row_offset // ts
    row = lambda width: pl.BlockSpec((ts, width), lambda i: (i, 0))
    shared_row = lambda width: pl.BlockSpec((ts, width), lambda i: (i + off, 0))
    const = lambda a: pl.BlockSpec(a.shape, lambda i: (0,) * a.ndim)
    attn_g = attn_g.reshape(1, ATTN_WIDTH)
    ssm_g = ssm_g.reshape(1, SSM_WIDTH)
    norm2_g = norm2_g.reshape(1, d)
    in_specs = [row(ATTN_WIDTH), row(SSM_WIDTH), row(d),
                _mod_spec(mod, rows, ts, 2), _mod_spec(mod, rows, ts, 3), _mod_spec(mod, rows, ts, 4),
                const(attn_g), const(ssm_g), const(norm2_g), const(w_out_bf), const(wr), const(rb)]
    args = [att, ssm_y, x_rows, mod, mod, mod, attn_g, ssm_g, norm2_g, w_out_bf, wr, rb]
    aliases = {}
    if shared is not None:
        in_specs += [pl.BlockSpec(memory_space=pl.ANY), pl.BlockSpec(memory_space=pl.ANY)]
        aliases = {len(args): 1, len(args) + 1: 2}
        args += list(shared)
    return pl.pallas_call(
        _outproj_kernel,
        grid=(rows // ts,),
        in_specs=in_specs,
        out_specs=[row(d), shared_row(d), shared_row(LANES)],
        out_shape=[jax.ShapeDtypeStruct((rows, d), F32), jax.ShapeDtypeStruct((total_rows, d), BF16),
                   jax.ShapeDtypeStruct((total_rows, LANES), F32)],
        input_output_aliases=aliases,
        compiler_params=_cparams(("arbitrary",)),
        name="outproj_router",
    )(*args)


def _router_weights(router_g_w, router_g_b, router_e_w, router_e_b):
    d = router_g_w.shape[0]
    ne = N_EXPERT_GROUPS * EXPERTS_PER_GROUP
    w = jnp.concatenate([router_g_w.astype(F32),
                         jnp.transpose(router_e_w.astype(F32), (1, 0, 2)).reshape(d, ne)], axis=1)
    w = jnp.pad(w, ((0, 0), (0, LANES - w.shape[1])))
    hi = w.astype(BF16)
    lo = (w - hi.astype(F32)).astype(BF16)
    b = jnp.concatenate([router_g_b.astype(F32), router_e_b.astype(F32).reshape(ne)])
    b = jnp.pad(b, (0, LANES - b.shape[0])).reshape(1, LANES)
    return jnp.concatenate([hi, lo], axis=1), b


def _moe_kernel(te_ref, x_ref, wg_ref, wu_ref, wd_ref, o_ref):
    del te_ref
    x = x_ref[...]
    gate = jnp.dot(x, wg_ref[0], preferred_element_type=F32)
    up = jnp.dot(x, wu_ref[0], preferred_element_type=F32)
    a = (gate * _sigmoid(gate)) * up
    o_ref[...] = jnp.dot(a.astype(BF16), wd_ref[0], preferred_element_type=F32).astype(o_ref.dtype)


def _moe_call(tile_expert, x_sorted, wg, wu, wd):
    n_slots, d = x_sorted.shape
    tm = MOE_TILE
    fe = wg.shape[2]
    grid_spec = pltpu.PrefetchScalarGridSpec(
        num_scalar_prefetch=1,
        grid=(n_slots // tm,),
        in_specs=[pl.BlockSpec((tm, d), lambda i, te: (i, 0)),
                  pl.BlockSpec((1, d, fe), lambda i, te: (te[i], 0, 0)),
                  pl.BlockSpec((1, d, fe), lambda i, te: (te[i], 0, 0)),
                  pl.BlockSpec((1, fe, d), lambda i, te: (te[i], 0, 0))],
        out_specs=pl.BlockSpec((tm, d), lambda i, te: (i, 0)),
    )
    return pl.pallas_call(
        _moe_kernel,
        grid_spec=grid_spec,
        out_shape=jax.ShapeDtypeStruct((n_slots, d), BF16),
        compiler_params=_cparams(("arbitrary",)),
        name="moe_experts",
    )(tile_expert, x_sorted, wg, wu, wd)


def _moe_dispatch(route):
    n_tok = route.shape[0]
    tm = MOE_TILE
    ids = route[:, :2].astype(jnp.int32).reshape(-1)
    n_pairs = ids.shape[0]
    n_slots = (-(-n_pairs // tm) + N_EXPERTS) * tm
    hot = (ids[:, None] == jnp.arange(N_EXPERTS)[None, :]).astype(jnp.int32)
    csum = jnp.cumsum(hot, axis=0)
    rank = jnp.sum((csum - hot) * hot, axis=1)
    counts = csum[-1]
    padded = -(-counts // tm) * tm
    ends = jnp.cumsum(padded)
    starts = ends - padded
    pos = jnp.sum(hot * starts[None, :], axis=1) + rank
    tok_of_slot = jnp.zeros((n_slots,), jnp.int32).at[pos].set(
        jnp.arange(n_pairs, dtype=jnp.int32) // 2, unique_indices=True, mode="promise_in_bounds")
    tile_start = jnp.arange(n_slots // tm, dtype=jnp.int32) * tm
    tile_expert = jnp.minimum(jnp.sum((tile_start[:, None] >= ends[None, :]).astype(jnp.int32), axis=1),
                              N_EXPERTS - 1).astype(jnp.int32)
    return pos.reshape(n_tok, 2), tok_of_slot, tile_expert


def _take_rows(x, idx):
    return x.at[idx].get(mode="promise_in_bounds")


def _final_kernel(x1_ref, ya_ref, yb_ref, route_ref, g2_ref, fg_ref, o_ref):
    wa = route_ref[:, 2:3]
    wb = route_ref[:, 3:4]
    x = x1_ref[...] + g2_ref[0] * (wa * ya_ref[...].astype(F32) + wb * yb_ref[...].astype(F32))
    o_ref[...] = _rmsnorm(x, fg_ref[...])


def _final_call(x1, ya, yb, route, mod, final_g, row_offset):
    rows, d = x1.shape
    ts = min(ROW_TILE, rows)
    off = row_offset // ts
    row = pl.BlockSpec((ts, d), lambda i: (i, 0))
    shared_row = lambda width: pl.BlockSpec((ts, width), lambda i: (i + off, 0))
    return pl.pallas_call(
        _final_kernel,
        grid=(rows // ts,),
        in_specs=[row, shared_row(d), shared_row(d), shared_row(LANES), _mod_spec(mod, rows, ts, 5),
                  pl.BlockSpec((1, d), lambda i: (0, 0))],
        out_specs=row,
        out_shape=jax.ShapeDtypeStruct((rows, d), F32),
        compiler_params=_cparams(("arbitrary",)),
        name="final_norm",
    )(x1, ya, yb, route, mod, final_g.reshape(1, d))


def kernel(x_prompt, x_sample, c_prompt, c_sample, cache_k, cache_v, state_ssm_re, state_ssm_im,
           rel_bias, ada_w, ada_b, norm1_g, w_in, ssm_a_re, ssm_a_im, ssm_log_dt, ssm_b_re, ssm_b_im,
           ssm_c_re, ssm_c_im, ssm_d, glu_a, glu_b, attn_out_g, ssm_out_g, w_out, norm2_g,
           router_g_w, router_g_b, router_e_w, router_e_b, w_gate, w_up, w_down, final_norm_g):
    if ada_w.shape[0] != 1:
        raise ValueError("single-layer trunk expected")
    nb, s_len, d = x_prompt.shape
    nd, t_len, _ = x_sample.shape
    if s_len != max(WINDOWS):
        raise ValueError("prompt length must equal the widest window")
    n_p, n_s = nb * s_len, nd * t_len
    n_tok = n_p + n_s
    if n_p % ROW_TILE or n_s % ROW_TILE:
        raise ValueError("token counts must be multiples of the row tile")

    mod = _mod_call(jnp.concatenate([c_prompt, c_sample], axis=0).astype(F32), ada_w[0], ada_b[0])
    mod_p = mod[:nb].reshape(nb, 1, 6 * d)
    mod_s = jnp.repeat(mod[nb:], t_len, axis=0).reshape(1, n_s, 6 * d)

    w_in_bf = w_in[0].astype(BF16)
    w_out_bf = w_out[0].astype(BF16)
    xp_rows = x_prompt.reshape(n_p, d)
    xs_rows = x_sample.reshape(n_s, d)

    qp, kp, vp, up, kp_t, vp_t = _inproj_call(xp_rows, mod_p, norm1_g[0], w_in_bf, seq_len=s_len)
    qs, ks, vs, us = _inproj_call(xs_rows, mod_s, norm1_g[0], w_in_bf)

    seq = lambda a: a.reshape(nb, s_len, ATTN_WIDTH)
    att_p = _attn_prompt_call(seq(qp), seq(kp), seq(vp), _prompt_bias_tiles(rel_bias))
    dec = lambda a: a.reshape(nd, t_len, ATTN_WIDTH)
    att_s = _attn_decode_call(dec(qs), dec(ks), dec(vs), cache_k[0], cache_v[0], rel_bias)

    s5 = (ssm_a_re[0], ssm_a_im[0], ssm_log_dt[0], ssm_b_re[0], ssm_b_im[0], ssm_c_re[0], ssm_c_im[0])
    t_op, m_op, p_op, sc = _s5_prompt_operators(*s5)
    d_oct = ssm_d[0].astype(F32).reshape(N_OCTETS, 1, LANES)
    ssm_p, hT_p = _ssm_prompt_call(seq(up), t_op, m_op, p_op, sc, d_oct,
                                   _octet_glu(glu_a[0]), _octet_glu(glu_b[0]))
    hT_p = hT_p.reshape(nb, N_OCTETS, 2, OCTET, SSM_STATE)
    ssm_re_p = hT_p[:, :, 0].reshape(nb, SSM_GROUPS, SSM_STATE)
    ssm_im_p = hT_p[:, :, 1].reshape(nb, SSM_GROUPS, SSM_STATE)
    us_tm = jnp.transpose(us.reshape(nd, t_len, SSM_WIDTH), (1, 0, 2))
    ssm_s, hre_s, him_s = _ssm_decode_call(us_tm, state_ssm_re[0], state_ssm_im[0], *s5,
                                           ssm_d[0], glu_a[0], glu_b[0])
    ssm_s = jnp.transpose(ssm_s, (1, 0, 2)).reshape(n_s, SSM_WIDTH)

    wr, rb = _router_weights(router_g_w[0], router_g_b[0], router_e_w[0], router_e_b[0])
    norms = (attn_out_g[0], ssm_out_g[0], norm2_g[0], w_out_bf, wr, rb)
    x1_p, h2_all, route = _outproj_call(att_p.reshape(n_p, ATTN_WIDTH), ssm_p.reshape(n_p, SSM_WIDTH),
                                        xp_rows, mod_p, *norms, total_rows=n_tok, row_offset=0)
    x1_s, h2_all, route = _outproj_call(att_s.reshape(n_s, ATTN_WIDTH), ssm_s, xs_rows, mod_s, *norms,
                                        total_rows=n_tok, row_offset=n_p, shared=(h2_all, route))

    pos, tok_of_slot, tile_expert = _moe_dispatch(route)
    ne = N_EXPERTS
    wg = w_gate[0].reshape(ne, d, D_EXPERT).astype(BF16)
    wu = w_up[0].reshape(ne, d, D_EXPERT).astype(BF16)
    wd = w_down[0].reshape(ne, D_EXPERT, d).astype(BF16)
    y_slots = _moe_call(tile_expert, _take_rows(h2_all, tok_of_slot), wg, wu, wd)
    ya = _take_rows(y_slots, pos[:, 0])
    yb = _take_rows(y_slots, pos[:, 1])

    y_p = _final_call(x1_p, ya, yb, route, mod_p, final_norm_g, row_offset=0)
    y_s = _final_call(x1_s, ya, yb, route, mod_s, final_norm_g, row_offset=n_p)

    heads = (ATTN_HEADS, HEAD_DIM)
    cache_out = lambda a: jnp.transpose(a.reshape((1, nb) + heads + (s_len,)), (0, 1, 4, 2, 3))
    return (y_p.reshape(nb, s_len, d), y_s.reshape(nd, t_len, d),
            cache_out(kp_t), cache_out(vp_t),
            ks.reshape((1, nd, t_len) + heads), vs.reshape((1, nd, t_len) + heads),
            ssm_re_p[None], ssm_im_p[None],
            hre_s.reshape(1, nd, SSM_GROUPS, SSM_STATE), him_s.reshape(1, nd, SSM_GROUPS, SSM_STATE))
```

```python
import functools
import math

import numpy as np

import jax
import jax.numpy as jnp
from jax import lax
from jax.experimental import pallas as pl
from jax.experimental.pallas import tpu as pltpu

F32 = jnp.float32
BF16 = jnp.bfloat16

D_MODEL = 1024
HEAD_DIM = 64
ATTN_WIDTH = 512
ATTN_HEADS = 8
SSM_WIDTH = 512
SSM_GROUP_CH = 16
SSM_GROUPS = 32
SSM_STATE = 64
WINDOWS = (128, 512, 2048)
DILATIONS = (1, 4, 16)
WINDOW_STEPS = 128
N_BUCKETS = 32
MAX_EXACT = 16
BUCKET_MAX_DIST = 2048
N_EXPERT_GROUPS = 4
EXPERTS_PER_GROUP = 4
N_EXPERTS = N_EXPERT_GROUPS * EXPERTS_PER_GROUP
D_EXPERT = 512
NORM_EPS = 1e-6

LANES = 128
Q_ROWS = 16
OCTET = LANES // SSM_GROUP_CH
N_OCTETS = SSM_GROUPS // OCTET
OCT_STATE = OCTET * SSM_STATE
SSM_CHUNK = 16
SSM_SEQS = 2
ROW_TILE = 512
MOE_TILE = 512
MOE_TILE_DECODE = 128
ATTN_UNROLL = 2
VMEM_LIMIT = 56 * 1024 * 1024

_NEG_INF = float("-inf")
_HIGHEST = lax.Precision.HIGHEST


def _cparams(sem):
    return pltpu.CompilerParams(dimension_semantics=sem, vmem_limit_bytes=VMEM_LIMIT)


def _rmsnorm(x, g):
    return x * lax.rsqrt(jnp.mean(x * x, axis=-1, keepdims=True) + NORM_EPS) * g


def _gelu_tanh(x):
    c = math.sqrt(2.0 / math.pi)
    return 0.5 * x * (1.0 + jnp.tanh(c * (x + 0.044715 * (x * x * x))))


def _sigmoid(x):
    return 1.0 / (1.0 + jnp.exp(-x))


def _mod_kernel(c_ref, w_ref, b_ref, o_ref):
    c = c_ref[...]
    a = (c * _sigmoid(c)).astype(BF16)
    o_ref[...] = jnp.dot(a, w_ref[...].astype(BF16), preferred_element_type=F32) + b_ref[...]


def _mod_call(c_all, ada_w, ada_b):
    rows, d = c_all.shape
    n_out = ada_w.shape[1]
    tn = 1024
    return pl.pallas_call(
        _mod_kernel,
        grid=(n_out // tn,),
        in_specs=[pl.BlockSpec((rows, d), lambda j: (0, 0)),
                  pl.BlockSpec((d, tn), lambda j: (0, j)),
                  pl.BlockSpec((1, tn), lambda j: (0, j))],
        out_specs=pl.BlockSpec((rows, tn), lambda j: (0, j)),
        out_shape=jax.ShapeDtypeStruct((rows, n_out), F32),
        compiler_params=_cparams(("arbitrary",)),
        name="adaln_mod",
    )(c_all, ada_w, ada_b.reshape(1, n_out))


def _mod_spec(mod, rows, ts, chunk):
    if mod.shape[1] == 1:
        tiles_per_group = (rows // mod.shape[0]) // ts
        return pl.BlockSpec((1, 1, D_MODEL), lambda i: (i // tiles_per_group, 0, chunk))
    return pl.BlockSpec((1, ts, D_MODEL), lambda i: (0, i, chunk))


def _inproj_kernel(x_ref, sh_ref, sc_ref, g_ref, w_ref, *rest, key_major):
    h = _rmsnorm(x_ref[...], g_ref[...]) * (1.0 + sc_ref[0]) + sh_ref[0]
    hb = h.astype(BF16)
    z = jnp.dot(hb, w_ref[...], preferred_element_type=F32)
    aw = ATTN_WIDTH
    if key_major:
        wkv_t_ref, q_ref, k_ref, v_ref, u_ref, kt_ref, vt_ref = rest
        zt = lax.dot_general(wkv_t_ref[...], hb, (((1,), (1,)), ((), ())), preferred_element_type=F32)
        kt_ref[0] = zt[:aw]
        vt_ref[0] = zt[aw:]
    else:
        q_ref, k_ref, v_ref, u_ref = rest
    q_ref[...] = z[:, :aw]
    k_ref[...] = z[:, aw:2 * aw]
    v_ref[...] = z[:, 2 * aw:3 * aw]
    u_ref[...] = z[:, 3 * aw:]


def _inproj_call(x_rows, mod, norm_g, w_in_bf, seq_len=None):
    rows, d = x_rows.shape
    ts = min(ROW_TILE, rows)
    proj = w_in_bf.shape[1]
    out = jax.ShapeDtypeStruct((rows, ATTN_WIDTH), F32)
    ospec = pl.BlockSpec((ts, ATTN_WIDTH), lambda i: (i, 0))
    in_specs = [pl.BlockSpec((ts, d), lambda i: (i, 0)),
                _mod_spec(mod, rows, ts, 0),
                _mod_spec(mod, rows, ts, 1),
                pl.BlockSpec((1, d), lambda i: (0, 0)),
                pl.BlockSpec((d, proj), lambda i: (0, 0))]
    args = [x_rows, mod, mod, norm_g.reshape(1, d), w_in_bf]
    out_specs = [ospec, ospec, ospec, ospec]
    out_shape = [out, out, out, out]
    if seq_len is not None:
        tiles = seq_len // ts
        in_specs.append(pl.BlockSpec((2 * ATTN_WIDTH, d), lambda i: (0, 0)))
        args.append(w_in_bf[:, ATTN_WIDTH:3 * ATTN_WIDTH].T)
        tspec = pl.BlockSpec((1, ATTN_WIDTH, ts), lambda i: (i // tiles, 0, i % tiles))
        out_specs += [tspec, tspec]
        out_shape += [jax.ShapeDtypeStruct((rows // seq_len, ATTN_WIDTH, seq_len), F32)] * 2
    return pl.pallas_call(
        functools.partial(_inproj_kernel, key_major=seq_len is not None),
        grid=(rows // ts,),
        in_specs=in_specs,
        out_specs=out_specs,
        out_shape=out_shape,
        compiler_params=_cparams(("arbitrary",)),
        name="inproj",
    )(*args)


def _t5_bucket(dist):
    d = jnp.maximum(dist, MAX_EXACT).astype(F32)
    log_part = MAX_EXACT + (jnp.log(d / MAX_EXACT) / math.log(BUCKET_MAX_DIST / MAX_EXACT)
                            * (N_BUCKETS - MAX_EXACT)).astype(jnp.int32)
    return jnp.where(dist < MAX_EXACT, dist, jnp.minimum(log_part, N_BUCKETS - 1))


def _bias_by_distance(rel_bias, dists):
    hot = (_t5_bucket(jnp.asarray(dists, jnp.int32))[:, None]
           == jnp.arange(N_BUCKETS, dtype=jnp.int32)[None, :]).astype(F32)
    return jnp.dot(hot, rel_bias.astype(F32), precision=_HIGHEST)


def _prompt_bias_tiles(rel_bias):
    steps = WINDOW_STEPS
    period = 3 * steps
    tiles = []
    for r in DILATIONS:
        vec = _bias_by_distance(rel_bias, r * np.arange(steps + 1))
        fill = jnp.full((steps - 1, ATTN_HEADS), _NEG_INF, F32)
        w = jnp.concatenate([fill, vec[::-1], fill, fill[:1]], axis=0)
        rep = jnp.tile(w.T, (1, steps))[:, :steps * (period - 1)]
        toe = rep.reshape(ATTN_HEADS, steps, period - 1)[:, :, steps - 1:]
        toe = toe.reshape(ATTN_HEADS // 2, 2, steps, 2 * steps)
        own_only = jnp.where(jnp.arange(2 * steps) < steps, _NEG_INF, toe)
        tiles.append(jnp.stack([toe, own_only], axis=2))
    return jnp.stack(tiles)


def _attn_prompt_kernel(q_ref, k_ref, v_ref, bias_ref, o_ref,
                        p4_ref, qh_ref, kb_ref, vb_ref, s_ref, p_ref,
                        bo_ref, bl_ref, t4o_ref, t4l_ref, on_ref, ln_ref):
    s_len = q_ref.shape[1]
    steps = WINDOW_STEPS
    n_tiles = s_len // steps
    quarter = s_len // 4
    nt = (((1,), (1,)), ((), ()))
    lane = lax.broadcasted_iota(jnp.int32, (1, LANES), 1)
    first_head = lane < HEAD_DIM
    srcs = (q_ref, k_ref, v_ref)

    zero_block = jnp.zeros((steps, LANES), BF16)
    kb_ref[0:steps, :] = zero_block
    vb_ref[0:steps, :] = zero_block
    for x in range(3):
        for sigma in range(4):
            p4_ref[x, sigma * quarter:(sigma + 1) * quarter, :] = srcs[x][0, pl.ds(sigma, quarter, stride=4), :]

    def source(branch, x, tile_idx):
        rows = slice(tile_idx * steps, (tile_idx + 1) * steps)
        if branch == 0:
            return srcs[x][0, rows, :]
        if branch == 1:
            return p4_ref[x, rows, :]
        sigma, tau = tile_idx % 4, tile_idx // 4
        return p4_ref[x, pl.ds(sigma * quarter + tau, steps, stride=4), :]

    for branch, r in enumerate(DILATIONS):
        blocks_per_class = (s_len // r) // steps
        width = 2 * steps if blocks_per_class > 1 else steps

        for t in range(n_tiles):
            rows = slice(t * steps, (t + 1) * steps)
            q2 = source(branch, 0, t) * (HEAD_DIM ** -0.5)
            qh_ref[0, rows, :] = jnp.where(first_head, q2, 0.0).astype(BF16)
            qh_ref[1, rows, :] = jnp.where(first_head, 0.0, q2).astype(BF16)
            kb_ref[steps + t * steps:steps + (t + 1) * steps, :] = source(branch, 1, t).astype(BF16)
            vb_ref[steps + t * steps:steps + (t + 1) * steps, :] = source(branch, 2, t).astype(BF16)

        def key_rows(t, width=width):
            start = t * steps if width == 2 * steps else (t + 1) * steps
            return pl.ds(pl.multiple_of(start, steps), width)

        def scores(t, carry, branch=branch, width=width, blocks_per_class=blocks_per_class,
                   key_rows=key_rows):
            rows = pl.ds(pl.multiple_of(t * steps, steps), steps)
            keys = kb_ref[key_rows(t), :]
            first = jnp.where(t % blocks_per_class == 0, 1, 0)
            for hh in range(2):
                if width == 2 * steps:
                    bias = bias_ref[branch, 0, hh, pl.ds(first, 1), :, :][0]
                else:
                    bias = bias_ref[branch, 0, hh, 0, :, steps:]
                sc = lax.dot_general(qh_ref[hh, rows, :], keys, nt, preferred_element_type=F32)
                s_ref[hh, rows, 0:width] = sc + bias
            return carry

        lax.fori_loop(0, n_tiles, scores, 0, unroll=ATTN_UNROLL)

        def softmax(i, carry, width=width):
            rows = pl.ds(pl.multiple_of(i * 2 * steps, 2 * steps), 2 * steps)
            for hh in range(2):
                sc = s_ref[hh, rows, 0:width]
                m = jnp.max(sc, axis=1, keepdims=True)
                p = jnp.exp(sc - m)
                den = jnp.sum(p, axis=1, keepdims=True)
                p_ref[hh, rows, 0:width] = (p * (1.0 / den)).astype(BF16)
                lse = jnp.broadcast_to(m + jnp.log(den), (2 * steps, HEAD_DIM))
                bl_ref[rows, hh * HEAD_DIM:(hh + 1) * HEAD_DIM] = lse
            return carry

        lax.fori_loop(0, n_tiles // 2, softmax, 0, unroll=2)

        out_o = on_ref.at[0] if branch == 0 else bo_ref

        def weighted(t, carry, width=width, out_o=out_o, key_rows=key_rows):
            rows = pl.ds(pl.multiple_of(t * steps, steps), steps)
            vals = vb_ref[key_rows(t), :]
            o0 = jnp.dot(p_ref[0, rows, 0:width], vals, preferred_element_type=F32)
            o1 = jnp.dot(p_ref[1, rows, 0:width], vals, preferred_element_type=F32)
            out_o[rows, :] = jnp.where(first_head, o0, o1)
            return carry

        lax.fori_loop(0, n_tiles, weighted, 0, unroll=ATTN_UNROLL)

        if branch == 0:
            ln_ref[0] = bl_ref[...]
        else:
            for src, stage, dst in ((bo_ref, t4o_ref, on_ref), (bl_ref, t4l_ref, ln_ref)):
                if branch == 2:
                    for t in range(n_tiles):
                        sigma, tau = t % 4, t // 4
                        stage[pl.ds(sigma * quarter + tau, steps, stride=4), :] = src[t * steps:(t + 1) * steps, :]
                    src = stage
                for sigma in range(4):
                    dst[branch, pl.ds(sigma, quarter, stride=4), :] = src[sigma * quarter:(sigma + 1) * quarter, :]

    def merge(i, carry):
        rows = pl.ds(pl.multiple_of(i * 256, 256), 256)
        l0, l1, l2 = ln_ref[0, rows, :], ln_ref[1, rows, :], ln_ref[2, rows, :]
        l_all = jnp.maximum(jnp.maximum(l0, l1), l2)
        w0, w1, w2 = jnp.exp(l0 - l_all), jnp.exp(l1 - l_all), jnp.exp(l2 - l_all)
        num = w0 * on_ref[0, rows, :] + w1 * on_ref[1, rows, :] + w2 * on_ref[2, rows, :]
        o_ref[0, rows, :] = num / (w0 + w1 + w2)
        return carry

    lax.fori_loop(0, s_len // 256, merge, 0)


def _attn_prompt_call(q, k, v, bias_tiles):
    n, s, _ = q.shape
    pairs = ATTN_HEADS // 2
    steps = WINDOW_STEPS
    qspec = pl.BlockSpec((1, s, LANES), lambda n_, g: (n_, 0, g))
    return pl.pallas_call(
        _attn_prompt_kernel,
        grid=(n, pairs),
        in_specs=[qspec, qspec, qspec,
                  pl.BlockSpec((3, 1, 2, 2, steps, 2 * steps), lambda n_, g: (0, g, 0, 0, 0, 0))],
        out_specs=qspec,
        out_shape=jax.ShapeDtypeStruct((n, s, ATTN_WIDTH), F32),
        scratch_shapes=[pltpu.VMEM((3, s, LANES), F32),
                        pltpu.VMEM((2, s, LANES), BF16),
                        pltpu.VMEM((s + steps, LANES), BF16),
                        pltpu.VMEM((s + steps, LANES), BF16),
                        pltpu.VMEM((2, s, 2 * steps), F32),
                        pltpu.VMEM((2, s, 2 * steps), BF16),
                        pltpu.VMEM((s, LANES), F32), pltpu.VMEM((s, LANES), F32),
                        pltpu.VMEM((s, LANES), F32), pltpu.VMEM((s, LANES), F32),
                        pltpu.VMEM((3, s, LANES), F32), pltpu.VMEM((3, s, LANES), F32)],
        compiler_params=_cparams(("arbitrary", "arbitrary")),
        name="attn_prompt",
    )(q, k, v, bias_tiles)


def _s5_discretise(a_re, a_im, log_dt, b_re, b_im):
    lam_re = jnp.minimum(a_re.astype(F32), -1e-4)
    lam_im = a_im.astype(F32)
    dt = jnp.exp(log_dt.astype(F32))[:, None]
    mag = jnp.exp(lam_re * dt)
    ph = lam_im * dt
    abar_re, abar_im = mag * jnp.cos(ph), mag * jnp.sin(ph)
    nr, ni = abar_re - 1.0, abar_im
    den = lam_re * lam_re + lam_im * lam_im
    coef_re = (nr * lam_re + ni * lam_im) / den
    coef_im = (ni * lam_re - nr * lam_im) / den
    br, bi = b_re.astype(F32), b_im.astype(F32)
    bbar_re = coef_re[..., None] * br - coef_im[..., None] * bi
    bbar_im = coef_re[..., None] * bi + coef_im[..., None] * br
    return lam_re * dt, ph, abar_re, abar_im, bbar_re, bbar_im


def _abar_power(log_mag, ph, n):
    nf = jnp.asarray(n, F32)[:, None, None]
    mag = jnp.exp(nf * log_mag[None])
    return mag * jnp.cos(nf * ph[None]), mag * jnp.sin(nf * ph[None])


def _s5_prompt_operators(a_re, a_im, log_dt, b_re, b_im, c_re, c_im):
    L = SSM_CHUNK
    log_mag, ph, _, _, bb_re, bb_im = _s5_discretise(a_re, a_im, log_dt, b_re, b_im)
    cr, ci = c_re.astype(F32), c_im.astype(F32)
    pw_re, pw_im = _abar_power(log_mag, ph, np.arange(L + 1))
    eye = jnp.eye(OCTET, dtype=F32)

    ab_re = pw_re[:L, :, :, None] * bb_re[None] - pw_im[:L, :, :, None] * bb_im[None]
    ab_im = pw_re[:L, :, :, None] * bb_im[None] + pw_im[:L, :, :, None] * bb_re[None]
    lag = (jnp.einsum('gop,lgpi->lgoi', cr, ab_re, precision=_HIGHEST)
           - jnp.einsum('gop,lgpi->lgoi', ci, ab_im, precision=_HIGHEST))
    lag = lag.reshape(L, N_OCTETS, OCTET, SSM_GROUP_CH, SSM_GROUP_CH)
    bd = jnp.einsum('logci,gh->olgihc', lag, eye).reshape(N_OCTETS, L, LANES, LANES).astype(BF16)
    stack = bd[:, ::-1].reshape(N_OCTETS, L * LANES, LANES)
    shifted = jnp.concatenate([stack[:, LANES:], jnp.zeros((N_OCTETS, LANES, LANES), BF16)], axis=1)
    t_op = jnp.concatenate([shifted, stack], axis=-1)

    m_parts = []
    for part in (ab_re[::-1], ab_im[::-1]):
        x = part.reshape(L, N_OCTETS, OCTET, SSM_STATE, SSM_GROUP_CH)
        m_parts.append(jnp.einsum('sogpi,gh->osgihp', x, eye).reshape(N_OCTETS, L * LANES, OCT_STATE))
    m_op = jnp.concatenate(m_parts, axis=-1)

    p1_re, p1_im = pw_re[1:], pw_im[1:]
    on_re = cr[None] * p1_re[:, :, None, :] - ci[None] * p1_im[:, :, None, :]
    on_im = -cr[None] * p1_im[:, :, None, :] - ci[None] * p1_re[:, :, None, :]
    p_parts = []
    for part in (on_re, on_im):
        x = part.reshape(L, N_OCTETS, OCTET, SSM_GROUP_CH, SSM_STATE)
        p_parts.append(jnp.einsum('togcp,gh->ogpthc', x, eye).reshape(N_OCTETS, OCT_STATE, L * LANES))
    p_op = jnp.concatenate(p_parts, axis=1)

    n_steps = 8
    sc_re, sc_im = _abar_power(log_mag, ph, L * (2 ** np.arange(n_steps)))
    sc = jnp.concatenate([sc_re.reshape(n_steps, N_OCTETS, OCT_STATE),
                          sc_im.reshape(n_steps, N_OCTETS, OCT_STATE)], axis=-1)
    sc = jnp.transpose(sc, (1, 0, 2))
    return t_op, m_op.astype(BF16), p_op.astype(BF16), sc


def _octet_glu(glu):
    eye = jnp.eye(OCTET, dtype=F32)
    x = glu.astype(F32).reshape(N_OCTETS, OCTET, SSM_GROUP_CH, SSM_GROUP_CH)
    return jnp.einsum('ogce,gh->ogche', x, eye).reshape(N_OCTETS, LANES, LANES).astype(BF16)


def _ssm_prompt_kernel(u_ref, t_ref, m_ref, p_ref, sc_ref, d_ref, ga_ref, gb_ref,
                       y_ref, h_ref, uf_ref, ub_ref, st_ref):
    L = SSM_CHUNK
    n_seq, s_len = u_ref.shape[0], u_ref.shape[1]
    n_chunks = s_len // L
    quarter = s_len // 4
    rows = n_seq * n_chunks

    def staged(sq, step):
        sigma, tau = step % 4, step // 4
        return pl.ds(sq * s_len + sigma * quarter + tau, n_chunks, stride=4)

    for sq in range(n_seq):
        for sigma in range(4):
            st_ref[sq * s_len + sigma * quarter:sq * s_len + (sigma + 1) * quarter, :] = (
                u_ref[sq, pl.ds(sigma, quarter, stride=4), :])
        for step in range(L):
            blk = st_ref[staged(sq, step), :]
            uf_ref[sq * n_chunks:(sq + 1) * n_chunks, step * LANES:(step + 1) * LANES] = blk
            ub_ref[sq * n_chunks:(sq + 1) * n_chunks, step * LANES:(step + 1) * LANES] = blk.astype(BF16)
    ub = ub_ref[...]

    x = jnp.dot(ub, m_ref[0], preferred_element_type=F32)
    chunk = lax.broadcasted_iota(jnp.int32, (rows, 1), 0) % n_chunks
    half = OCT_STATE
    k = 1
    step = 0
    while k < n_chunks:
        a_re = sc_ref[0, step:step + 1, :half]
        a_im = sc_ref[0, step:step + 1, half:]
        sh = jnp.where(chunk >= k, pltpu.roll(x, k, axis=0), 0.0)
        s_re, s_im = sh[:, :half], sh[:, half:]
        x = x + jnp.concatenate([a_re * s_re - a_im * s_im, a_re * s_im + a_im * s_re], axis=1)
        k *= 2
        step += 1
    for sq in range(n_seq):
        h_ref[sq, 0] = x[(sq + 1) * n_chunks - 1:(sq + 1) * n_chunks, :]
    h_start = jnp.where(chunk >= 1, pltpu.roll(x, 1, axis=0), 0.0)

    hb = h_start.astype(BF16)
    d = d_ref[0]
    ga = ga_ref[0]
    gb = gb_ref[0]
    for t in range(0, L, 2):
        pair = slice(t * LANES, (t + 2) * LANES)
        y2 = (jnp.dot(ub_ref[:, :(t + 2) * LANES], t_ref[0, (L - 2 - t) * LANES:, :],
                      preferred_element_type=F32)
              + jnp.dot(hb, p_ref[0, :, pair], preferred_element_type=F32))
        for j in range(2):
            lanes = slice((t + j) * LANES, (t + j + 1) * LANES)
            g = _gelu_tanh(y2[:, j * LANES:(j + 1) * LANES] + d * uf_ref[:, lanes]).astype(BF16)
            out = (jnp.dot(g, ga, preferred_element_type=F32)
                   * _sigmoid(jnp.dot(g, gb, preferred_element_type=F32)))
            for sq in range(n_seq):
                st_ref[staged(sq, t + j), :] = out[sq * n_chunks:(sq + 1) * n_chunks, :]
    for sq in range(n_seq):
        for sigma in range(4):
            y_ref[sq, pl.ds(sigma, quarter, stride=4), :] = (
                st_ref[sq * s_len + sigma * quarter:sq * s_len + (sigma + 1) * quarter, :])


def _ssm_prompt_call(u, t_op, m_op, p_op, sc, d_oct, ga, gb):
    n, s, _ = u.shape
    L = SSM_CHUNK
    nq = math.gcd(SSM_SEQS, n)
    rows = nq * (s // L)
    wide = L * LANES
    wspec = lambda shape: pl.BlockSpec((1,) + shape, lambda o, n_: (o, 0, 0))
    return pl.pallas_call(
        _ssm_prompt_kernel,
        grid=(N_OCTETS, n // nq),
        in_specs=[pl.BlockSpec((nq, s, LANES), lambda o, n_: (n_, 0, o)),
                  wspec((wide, 2 * LANES)), wspec((wide, 2 * OCT_STATE)), wspec((2 * OCT_STATE, wide)),
                  wspec((8, 2 * OCT_STATE)), wspec((1, LANES)),
                  wspec((LANES, LANES)), wspec((LANES, LANES))],
        out_specs=[pl.BlockSpec((nq, s, LANES), lambda o, n_: (n_, 0, o)),
                   pl.BlockSpec((nq, 1, 1, 2 * OCT_STATE), lambda o, n_: (n_, o, 0, 0))],
        out_shape=[jax.ShapeDtypeStruct((n, s, SSM_WIDTH), F32),
                   jax.ShapeDtypeStruct((n, N_OCTETS, 1, 2 * OCT_STATE), F32)],
        scratch_shapes=[pltpu.VMEM((rows, wide), F32),
                        pltpu.VMEM((rows, wide), BF16),
                        pltpu.VMEM((nq * s, LANES), F32)],
        compiler_params=_cparams(("arbitrary", "arbitrary")),
        name="ssm_prompt",
    )(u, t_op, m_op, p_op, sc, d_oct, ga, gb)


def _attn_decode_kernel(q_ref, kn_ref, vn_ref, kt_ref, vt_ref, b_ref, mult_ref, o_ref):
    nt = (((1,), (1,)), ((), ()))
    mult = mult_ref[...]
    for h in range(ATTN_HEADS):
        q = q_ref[0, h]
        kt = jnp.concatenate([kt_ref[0, h].astype(BF16), kn_ref[0, h]], axis=1)
        vt = jnp.concatenate([vt_ref[0, h].astype(BF16), vn_ref[0, h]], axis=1)
        s = jnp.dot(q, kt, preferred_element_type=F32) + b_ref[h]
        m = jnp.max(s, axis=1, keepdims=True)
        p = jnp.exp(s - m) * mult
        den = jnp.sum(p, axis=1, keepdims=True)
        o = lax.dot_general(p.astype(BF16), vt, nt, preferred_element_type=F32)
        o_ref[0, h] = o / den


def _decode_tables(rel_bias, t_len, w_rows):
    t = np.arange(t_len)[:, None]
    dist = np.concatenate([w_rows + t - np.arange(w_rows)[None, :],
                           t - np.arange(LANES)[None, :]], axis=1)
    mult = np.zeros(dist.shape, np.float32)
    for w, r in zip(WINDOWS, DILATIONS):
        mult += (dist >= 0) & (dist % r == 0) & (dist <= w)
    mult = np.concatenate([mult, np.zeros((Q_ROWS - t_len, dist.shape[1]), np.float32)], axis=0)
    mult[t_len:, 0] = 1.0
    by_dist = _bias_by_distance(rel_bias, np.arange(w_rows + t_len))
    rows = []
    for ti in range(t_len):
        cache_part = by_dist[ti + 1:w_rows + ti + 1][::-1]
        new_part = by_dist[:ti + 1][::-1]
        pad = jnp.zeros((LANES - ti - 1, ATTN_HEADS), F32)
        rows.append(jnp.concatenate([cache_part, new_part, pad], axis=0))
    bias = jnp.stack(rows + [jnp.zeros_like(rows[0])] * (Q_ROWS - t_len), axis=0)
    bias = jnp.transpose(bias, (2, 0, 1))
    bias = jnp.where(jnp.asarray(mult)[None] > 0, bias, _NEG_INF)
    return bias, jnp.asarray(mult)


def _attn_decode_call(q, k_new, v_new, cache_k, cache_v, rel_bias):
    n, t_len, w = q.shape
    w_rows = cache_k.shape[1]
    if t_len > min(DILATIONS[1:]) or t_len > Q_ROWS or w_rows < max(WINDOWS):
        raise ValueError("unsupported decode shape")
    heads = (ATTN_HEADS, HEAD_DIM)

    def head_major(a, pad_to):
        a = jnp.transpose(a.reshape((n, t_len) + heads), (0, 2, 1, 3))
        return jnp.pad(a, ((0, 0), (0, 0), (0, pad_to - t_len), (0, 0)))

    qh = head_major(q * (HEAD_DIM ** -0.5), Q_ROWS).astype(BF16)
    knt = jnp.swapaxes(head_major(k_new, LANES), 2, 3).astype(BF16)
    vnt = jnp.swapaxes(head_major(v_new, LANES), 2, 3).astype(BF16)
    kt = jnp.transpose(cache_k.astype(F32), (0, 2, 3, 1))
    vt = jnp.transpose(cache_v.astype(F32), (0, 2, 3, 1))
    bias, mult = _decode_tables(rel_bias, t_len, w_rows)
    keys = w_rows + LANES
    per_seq = lambda shape: pl.BlockSpec((1,) + shape, lambda i: (i, 0, 0, 0))
    out = pl.pallas_call(
        _attn_decode_kernel,
        grid=(n,),
        in_specs=[per_seq((ATTN_HEADS, Q_ROWS, HEAD_DIM)),
                  per_seq((ATTN_HEADS, HEAD_DIM, LANES)), per_seq((ATTN_HEADS, HEAD_DIM, LANES)),
                  per_seq((ATTN_HEADS, HEAD_DIM, w_rows)), per_seq((ATTN_HEADS, HEAD_DIM, w_rows)),
                  pl.BlockSpec((ATTN_HEADS, Q_ROWS, keys), lambda i: (0, 0, 0)),
                  pl.BlockSpec((Q_ROWS, keys), lambda i: (0, 0))],
        out_specs=per_seq((ATTN_HEADS, Q_ROWS, HEAD_DIM)),
        out_shape=jax.ShapeDtypeStruct((n, ATTN_HEADS, Q_ROWS, HEAD_DIM), F32),
        compiler_params=_cparams(("arbitrary",)),
        name="attn_decode",
    )(qh, knt, vnt, kt, vt, bias, mult)
    return jnp.transpose(out[:, :, :t_len], (0, 2, 1, 3)).reshape(n, t_len, w)


def _ssm_decode_kernel(u_ref, hre_ref, him_ref, are_ref, aim_ref, bre_ref, bim_ref,
                       cre_ref, cim_ref, d_ref, ga_ref, gb_ref, y_ref, ore_ref, oim_ref, *, t_len):
    h_re, h_im = hre_ref[...], him_ref[...]
    a_re, a_im = are_ref[...], aim_ref[...]
    for t in range(t_len):
        u = u_ref[t]
        ub = u.astype(BF16)
        n_re = a_re * h_re - a_im * h_im + jnp.dot(ub, bre_ref[...], preferred_element_type=F32)
        n_im = a_re * h_im + a_im * h_re + jnp.dot(ub, bim_ref[...], preferred_element_type=F32)
        h_re, h_im = n_re, n_im
        y = (jnp.dot(h_re.astype(BF16), cre_ref[...], preferred_element_type=F32)
             - jnp.dot(h_im.astype(BF16), cim_ref[...], preferred_element_type=F32)
             + d_ref[...] * u)
        g = _gelu_tanh(y).astype(BF16)
        y_ref[t] = (jnp.dot(g, ga_ref[...], preferred_element_type=F32)
                    * _sigmoid(jnp.dot(g, gb_ref[...], preferred_element_type=F32)))
    ore_ref[...] = h_re
    oim_ref[...] = h_im


def _group_blockdiag(x):
    g, a, b = x.shape
    return jnp.einsum('gab,gh->gahb', x, jnp.eye(g, dtype=x.dtype)).reshape(g * a, g * b)


def _ssm_decode_call(u_tm, h0_re, h0_im, a_re, a_im, log_dt, b_re, b_im, c_re, c_im,
                     d_skip, glu_a, glu_b):
    n = h0_re.shape[0]
    _, _, abar_re, abar_im, bb_re, bb_im = _s5_discretise(a_re, a_im, log_dt, b_re, b_im)
    state = SSM_GROUPS * SSM_STATE
    t_len = u_tm.shape[0]
    args = (u_tm, h0_re.reshape(n, state).astype(F32), h0_im.reshape(n, state).astype(F32),
            abar_re.reshape(1, state), abar_im.reshape(1, state),
            _group_blockdiag(jnp.transpose(bb_re, (0, 2, 1))).astype(BF16),
            _group_blockdiag(jnp.transpose(bb_im, (0, 2, 1))).astype(BF16),
            _group_blockdiag(jnp.transpose(c_re.astype(F32), (0, 2, 1))).astype(BF16),
            _group_blockdiag(jnp.transpose(c_im.astype(F32), (0, 2, 1))).astype(BF16),
            d_skip.astype(F32).reshape(1, SSM_WIDTH),
            _group_blockdiag(glu_a.astype(F32)).astype(BF16),
            _group_blockdiag(glu_b.astype(F32)).astype(BF16))
    full = lambda a: pl.BlockSpec(a.shape, lambda i: (0,) * a.ndim)
    out_shape = [jax.ShapeDtypeStruct(u_tm.shape, F32),
                 jax.ShapeDtypeStruct((n, state), F32), jax.ShapeDtypeStruct((n, state), F32)]
    return pl.pallas_call(
        functools.partial(_ssm_decode_kernel, t_len=t_len),
        grid=(1,),
        in_specs=[full(a) for a in args],
        out_specs=[full(o) for o in out_shape],
        out_shape=out_shape,
        compiler_params=_cparams(("arbitrary",)),
        name="ssm_decode",
    )(*args)


def _outproj_kernel(att_ref, ssm_ref, x_ref, g1_ref, sh2_ref, sc2_ref, ag_ref, sg_ref, n2_ref,
                    wo_ref, wr_ref, rb_ref, x1_ref, h2_ref, route_ref):
    mixed = jnp.concatenate([_rmsnorm(att_ref[...], ag_ref[...]), _rmsnorm(ssm_ref[...], sg_ref[...])],
                            axis=1).astype(BF16)
    x1 = x_ref[...] + g1_ref[0] * jnp.dot(mixed, wo_ref[...], preferred_element_type=F32)
    x1_ref[...] = x1
    h2 = _rmsnorm(x1, n2_ref[...]) * (1.0 + sc2_ref[0]) + sh2_ref[0]
    hi = h2.astype(BF16)
    h2_ref[...] = hi
    lo = (h2 - hi.astype(F32)).astype(BF16)
    r1 = jnp.dot(hi, wr_ref[...], preferred_element_type=F32)
    r2 = jnp.dot(lo, wr_ref[:, :LANES], preferred_element_type=F32)
    logits = r1[:, :LANES] + r1[:, LANES:] + r2 + rb_ref[...]

    lane = lax.broadcasted_iota(jnp.int32, (1, LANES), 1)
    lane_f = lane.astype(F32)
    big = float(LANES)
    ng, epg = N_EXPERT_GROUPS, EXPERTS_PER_GROUP
    lg = jnp.where(lane < ng, logits, _NEG_INF)
    gmax = jnp.max(lg, axis=1, keepdims=True)
    p_star = 1.0 / jnp.sum(jnp.exp(lg - gmax), axis=1, keepdims=True)
    g_star = jnp.min(jnp.where(lg == gmax, lane_f, big), axis=1, keepdims=True)
    in_group = ((lane >= ng) & (lane < ng + ng * epg)
                & (lax.shift_right_arithmetic(lane - ng, int(math.log2(epg))).astype(F32) == g_star))
    le = jnp.where(in_group, logits, _NEG_INF)
    v1 = jnp.max(le, axis=1, keepdims=True)
    i1 = jnp.min(jnp.where(le == v1, lane_f, big), axis=1, keepdims=True)
    le2 = jnp.where(lane_f == i1, _NEG_INF, le)
    v2 = jnp.max(le2, axis=1, keepdims=True)
    i2 = jnp.min(jnp.where(le2 == v2, lane_f, big), axis=1, keepdims=True)
    e2 = jnp.exp(v2 - v1)
    w1 = p_star / (1.0 + e2)
    w2 = p_star * e2 / (1.0 + e2)
    route_ref[...] = jnp.where(lane == 0, i1 - ng,
                               jnp.where(lane == 1, i2 - ng,
                                         jnp.where(lane == 2, w1, jnp.where(lane == 3, w2, 0.0))))


def _outproj_call(att, ssm_y, x_rows, mod, attn_g, ssm_g, norm2_g, w_out_bf, wr, rb):
    rows, d = x_rows.shape
    ts = min(ROW_TILE, rows)
    row = lambda width: pl.BlockSpec((ts, width), lambda i: (i, 0))
    const = lambda a: pl.BlockSpec(a.shape, lambda i: (0,) * a.ndim)
    attn_g = attn_g.reshape(1, ATTN_WIDTH)
    ssm_g = ssm_g.reshape(1, SSM_WIDTH)
    norm2_g = norm2_g.reshape(1, d)
    return pl.pallas_call(
        _outproj_kernel,
        grid=(rows // ts,),
        in_specs=[row(ATTN_WIDTH), row(SSM_WIDTH), row(d),
                  _mod_spec(mod, rows, ts, 2), _mod_spec(mod, rows, ts, 3), _mod_spec(mod, rows, ts, 4),
                  const(attn_g), const(ssm_g), const(norm2_g), const(w_out_bf), const(wr), const(rb)],
        out_specs=[row(d), row(d), row(LANES)],
        out_shape=[jax.ShapeDtypeStruct((rows, d), F32), jax.ShapeDtypeStruct((rows, d), BF16),
                   jax.ShapeDtypeStruct((rows, LANES), F32)],
        compiler_params=_cparams(("arbitrary",)),
        name="outproj_router",
    )(att, ssm_y, x_rows, mod, mod, mod, attn_g, ssm_g, norm2_g, w_out_bf, wr, rb)


def _router_weights(router_g_w, router_g_b, router_e_w, router_e_b):
    d = router_g_w.shape[0]
    ne = N_EXPERT_GROUPS * EXPERTS_PER_GROUP
    w = jnp.concatenate([router_g_w.astype(F32),
                         jnp.transpose(router_e_w.astype(F32), (1, 0, 2)).reshape(d, ne)], axis=1)
    w = jnp.pad(w, ((0, 0), (0, LANES - w.shape[1])))
    hi = w.astype(BF16)
    lo = (w - hi.astype(F32)).astype(BF16)
    b = jnp.concatenate([router_g_b.astype(F32), router_e_b.astype(F32).reshape(ne)])
    b = jnp.pad(b, (0, LANES - b.shape[0])).reshape(1, LANES)
    return jnp.concatenate([hi, lo], axis=1), b


def _moe_kernel(te_ref, x_ref, wg_ref, wu_ref, wd_ref, o_ref):
    del te_ref
    x = x_ref[...]
    gate = jnp.dot(x, wg_ref[0], preferred_element_type=F32)
    up = jnp.dot(x, wu_ref[0], preferred_element_type=F32)
    a = (gate * _sigmoid(gate)) * up
    o_ref[...] = jnp.dot(a.astype(BF16), wd_ref[0], preferred_element_type=F32).astype(o_ref.dtype)


def _moe_call(tile_expert, x_sorted, wg, wu, wd, tm):
    n_slots, d = x_sorted.shape
    fe = wg.shape[2]
    grid_spec = pltpu.PrefetchScalarGridSpec(
        num_scalar_prefetch=1,
        grid=(n_slots // tm,),
        in_specs=[pl.BlockSpec((tm, d), lambda i, te: (i, 0)),
                  pl.BlockSpec((1, d, fe), lambda i, te: (te[i], 0, 0)),
                  pl.BlockSpec((1, d, fe), lambda i, te: (te[i], 0, 0)),
                  pl.BlockSpec((1, fe, d), lambda i, te: (te[i], 0, 0))],
        out_specs=pl.BlockSpec((tm, d), lambda i, te: (i, 0)),
    )
    return pl.pallas_call(
        _moe_kernel,
        grid_spec=grid_spec,
        out_shape=jax.ShapeDtypeStruct((n_slots, d), BF16),
        compiler_params=_cparams(("arbitrary",)),
        name="moe_experts",
    )(tile_expert, x_sorted, wg, wu, wd)


def _moe_dispatch(route, tm):
    n_tok = route.shape[0]
    ids = route[:, :2].astype(jnp.int32).reshape(-1)
    n_pairs = ids.shape[0]
    n_slots = (-(-n_pairs // tm) + N_EXPERTS) * tm
    hot = (ids[:, None] == jnp.arange(N_EXPERTS)[None, :]).astype(jnp.int32)
    csum = jnp.cumsum(hot, axis=0)
    rank = jnp.sum((csum - hot) * hot, axis=1)
    counts = csum[-1]
    padded = -(-counts // tm) * tm
    ends = jnp.cumsum(padded)
    starts = ends - padded
    pos = jnp.sum(hot * starts[None, :], axis=1) + rank
    tok_of_slot = (jnp.arange(n_slots, dtype=jnp.int32) % n_tok).at[pos].set(
        jnp.arange(n_pairs, dtype=jnp.int32) // 2, unique_indices=True, mode="promise_in_bounds")
    tile_start = jnp.arange(n_slots // tm, dtype=jnp.int32) * tm
    tile_expert = jnp.minimum(jnp.sum((tile_start[:, None] >= ends[None, :]).astype(jnp.int32), axis=1),
                              N_EXPERTS - 1).astype(jnp.int32)
    return pos.reshape(n_tok, 2), tok_of_slot, tile_expert


def _take_rows(x, idx):
    return x.at[idx].get(mode="promise_in_bounds")


def _final_kernel(x1_ref, ya_ref, yb_ref, route_ref, g2_ref, fg_ref, o_ref):
    wa = route_ref[:, 2:3]
    wb = route_ref[:, 3:4]
    x = x1_ref[...] + g2_ref[0] * (wa * ya_ref[...].astype(F32) + wb * yb_ref[...].astype(F32))
    o_ref[...] = _rmsnorm(x, fg_ref[...])


def _final_call(x1, ya, yb, route, mod, final_g):
    rows, d = x1.shape
    ts = min(ROW_TILE, rows)
    row = pl.BlockSpec((ts, d), lambda i: (i, 0))
    return pl.pallas_call(
        _final_kernel,
        grid=(rows // ts,),
        in_specs=[row, row, row, pl.BlockSpec((ts, LANES), lambda i: (i, 0)), _mod_spec(mod, rows, ts, 5),
                  pl.BlockSpec((1, d), lambda i: (0, 0))],
        out_specs=row,
        out_shape=jax.ShapeDtypeStruct((rows, d), F32),
        compiler_params=_cparams(("arbitrary",)),
        name="final_norm",
    )(x1, ya, yb, route, mod, final_g.reshape(1, d))


def kernel(x_prompt, x_sample, c_prompt, c_sample, cache_k, cache_v, state_ssm_re, state_ssm_im,
           rel_bias, ada_w, ada_b, norm1_g, w_in, ssm_a_re, ssm_a_im, ssm_log_dt, ssm_b_re, ssm_b_im,
           ssm_c_re, ssm_c_im, ssm_d, glu_a, glu_b, attn_out_g, ssm_out_g, w_out, norm2_g,
           router_g_w, router_g_b, router_e_w, router_e_b, w_gate, w_up, w_down, final_norm_g):
    if ada_w.shape[0] != 1:
        raise ValueError("single-layer trunk expected")
    nb, s_len, d = x_prompt.shape
    nd, t_len, _ = x_sample.shape
    if s_len != max(WINDOWS):
        raise ValueError("prompt length must equal the widest window")
    n_p, n_s = nb * s_len, nd * t_len
    if n_p % ROW_TILE or n_s % ROW_TILE:
        raise ValueError("token counts must be multiples of the row tile")

    mod = _mod_call(jnp.concatenate([c_prompt, c_sample], axis=0).astype(F32), ada_w[0], ada_b[0])
    mod_p = mod[:nb].reshape(nb, 1, 6 * d)
    mod_s = jnp.repeat(mod[nb:], t_len, axis=0).reshape(1, n_s, 6 * d)

    w_in_bf = w_in[0].astype(BF16)
    w_out_bf = w_out[0].astype(BF16)
    xp_rows = x_prompt.reshape(n_p, d)
    xs_rows = x_sample.reshape(n_s, d)

    qp, kp, vp, up, kp_t, vp_t = _inproj_call(xp_rows, mod_p, norm1_g[0], w_in_bf, seq_len=s_len)
    qs, ks, vs, us = _inproj_call(xs_rows, mod_s, norm1_g[0], w_in_bf)

    seq = lambda a: a.reshape(nb, s_len, ATTN_WIDTH)
    att_p = _attn_prompt_call(seq(qp), seq(kp), seq(vp), _prompt_bias_tiles(rel_bias))
    dec = lambda a: a.reshape(nd, t_len, ATTN_WIDTH)
    att_s = _attn_decode_call(dec(qs), dec(ks), dec(vs), cache_k[0], cache_v[0], rel_bias)

    s5 = (ssm_a_re[0], ssm_a_im[0], ssm_log_dt[0], ssm_b_re[0], ssm_b_im[0], ssm_c_re[0], ssm_c_im[0])
    t_op, m_op, p_op, sc = _s5_prompt_operators(*s5)
    d_oct = ssm_d[0].astype(F32).reshape(N_OCTETS, 1, LANES)
    ssm_p, hT_p = _ssm_prompt_call(seq(up), t_op, m_op, p_op, sc, d_oct,
                                   _octet_glu(glu_a[0]), _octet_glu(glu_b[0]))
    hT_p = hT_p.reshape(nb, N_OCTETS, 2, OCTET, SSM_STATE)
    ssm_re_p = hT_p[:, :, 0].reshape(nb, SSM_GROUPS, SSM_STATE)
    ssm_im_p = hT_p[:, :, 1].reshape(nb, SSM_GROUPS, SSM_STATE)
    us_tm = jnp.transpose(us.reshape(nd, t_len, SSM_WIDTH), (1, 0, 2))
    ssm_s, hre_s, him_s = _ssm_decode_call(us_tm, state_ssm_re[0], state_ssm_im[0], *s5,
                                           ssm_d[0], glu_a[0], glu_b[0])
    ssm_s = jnp.transpose(ssm_s, (1, 0, 2)).reshape(n_s, SSM_WIDTH)

    wr, rb = _router_weights(router_g_w[0], router_g_b[0], router_e_w[0], router_e_b[0])
    norms = (attn_out_g[0], ssm_out_g[0], norm2_g[0], w_out_bf, wr, rb)
    x1_p, h2_p, route_p = _outproj_call(att_p.reshape(n_p, ATTN_WIDTH), ssm_p.reshape(n_p, SSM_WIDTH),
                                        xp_rows, mod_p, *norms)
    x1_s, h2_s, route_s = _outproj_call(att_s.reshape(n_s, ATTN_WIDTH), ssm_s, xs_rows, mod_s, *norms)

    ne = N_EXPERTS
    wg = w_gate[0].reshape(ne, d, D_EXPERT).astype(BF16)
    wu = w_up[0].reshape(ne, d, D_EXPERT).astype(BF16)
    wd = w_down[0].reshape(ne, D_EXPERT, d).astype(BF16)

    def experts(x1, h2, route, mod_rows, tm):
        pos, tok_of_slot, tile_expert = _moe_dispatch(route, tm)
        y_slots = _moe_call(tile_expert, _take_rows(h2, tok_of_slot), wg, wu, wd, tm)
        return _final_call(x1, _take_rows(y_slots, pos[:, 0]), _take_rows(y_slots, pos[:, 1]),
                           route, mod_rows, final_norm_g)

    y_p = experts(x1_p, h2_p, route_p, mod_p, MOE_TILE)
    y_s = experts(x1_s, h2_s, route_s, mod_s, MOE_TILE_DECODE)

    heads = (ATTN_HEADS, HEAD_DIM)
    cache_out = lambda a: jnp.transpose(a.reshape((1, nb) + heads + (s_len,)), (0, 1, 4, 2, 3))
    return (y_p.reshape(nb, s_len, d), y_s.reshape(nd, t_len, d),
            cache_out(kp_t), cache_out(vp_t),
            ks.reshape((1, nd, t_len) + heads), vs.reshape((1, nd, t_len) + heads),
            ssm_re_p[None], ssm_im_p[None],
            hre_s.reshape(1, nd, SSM_GROUPS, SSM_STATE), him_s.reshape(1, nd, SSM_GROUPS, SSM_STATE))
```

```python
import functools
import math

import numpy as np

import jax
import jax.numpy as jnp
from jax import lax
from jax.experimental import pallas as pl
from jax.experimental.pallas import tpu as pltpu

F32 = jnp.float32
BF16 = jnp.bfloat16

D_MODEL = 1024
HEAD_DIM = 64
ATTN_WIDTH = 512
ATTN_HEADS = 8
SSM_WIDTH = 512
SSM_GROUP_CH = 16
SSM_GROUPS = 32
SSM_STATE = 64
WINDOWS = (128, 512, 2048)
DILATIONS = (1, 4, 16)
WINDOW_STEPS = 128
N_BUCKETS = 32
MAX_EXACT = 16
BUCKET_MAX_DIST = 2048
N_EXPERT_GROUPS = 4
EXPERTS_PER_GROUP = 4
N_EXPERTS = N_EXPERT_GROUPS * EXPERTS_PER_GROUP
D_EXPERT = 512
NORM_EPS = 1e-6

LANES = 128
Q_ROWS = 16
OCTET = LANES // SSM_GROUP_CH
N_OCTETS = SSM_GROUPS // OCTET
OCT_STATE = OCTET * SSM_STATE
SSM_CHUNK = 16
SSM_SEQS = 2
ROW_TILE = 512
MOE_TILE = 512
MOE_TILE_DECODE = 128
ATTN_MXU_UNROLL = 8
VMEM_LIMIT = 56 * 1024 * 1024

_NEG_INF = float("-inf")
_HIGHEST = lax.Precision.HIGHEST


def _cparams(sem):
    return pltpu.CompilerParams(dimension_semantics=sem, vmem_limit_bytes=VMEM_LIMIT)


def _rmsnorm(x, g):
    return x * lax.rsqrt(jnp.mean(x * x, axis=-1, keepdims=True) + NORM_EPS) * g


def _gelu_tanh(x):
    c = math.sqrt(2.0 / math.pi)
    return 0.5 * x * (1.0 + jnp.tanh(c * (x + 0.044715 * (x * x * x))))


def _sigmoid(x):
    return 1.0 / (1.0 + jnp.exp(-x))


def _mod_kernel(c_ref, w_ref, b_ref, o_ref):
    c = c_ref[...]
    a = (c * _sigmoid(c)).astype(BF16)
    o_ref[...] = jnp.dot(a, w_ref[...].astype(BF16), preferred_element_type=F32) + b_ref[...]


def _mod_call(c_all, ada_w, ada_b):
    rows, d = c_all.shape
    n_out = ada_w.shape[1]
    tn = 1024
    return pl.pallas_call(
        _mod_kernel,
        grid=(n_out // tn,),
        in_specs=[pl.BlockSpec((rows, d), lambda j: (0, 0)),
                  pl.BlockSpec((d, tn), lambda j: (0, j)),
                  pl.BlockSpec((1, tn), lambda j: (0, j))],
        out_specs=pl.BlockSpec((rows, tn), lambda j: (0, j)),
        out_shape=jax.ShapeDtypeStruct((rows, n_out), F32),
        compiler_params=_cparams(("arbitrary",)),
        name="adaln_mod",
    )(c_all, ada_w, ada_b.reshape(1, n_out))


def _mod_spec(mod, rows, ts, chunk):
    if mod.shape[1] == 1:
        tiles_per_group = (rows // mod.shape[0]) // ts
        return pl.BlockSpec((1, 1, D_MODEL), lambda i: (i // tiles_per_group, 0, chunk))
    return pl.BlockSpec((1, ts, D_MODEL), lambda i: (0, i, chunk))


def _inproj_kernel(x_ref, sh_ref, sc_ref, g_ref, w_ref, *rest, key_major):
    h = _rmsnorm(x_ref[...], g_ref[...]) * (1.0 + sc_ref[0]) + sh_ref[0]
    hb = h.astype(BF16)
    z = jnp.dot(hb, w_ref[...], preferred_element_type=F32)
    aw = ATTN_WIDTH
    if key_major:
        wkv_t_ref, q_ref, k_ref, v_ref, u_ref, kt_ref, vt_ref = rest
        zt = lax.dot_general(wkv_t_ref[...], hb, (((1,), (1,)), ((), ())), preferred_element_type=F32)
        kt_ref[0] = zt[:aw]
        vt_ref[0] = zt[aw:]
    else:
        q_ref, k_ref, v_ref, u_ref = rest
    q_ref[...] = z[:, :aw]
    k_ref[...] = z[:, aw:2 * aw]
    v_ref[...] = z[:, 2 * aw:3 * aw]
    u_ref[...] = z[:, 3 * aw:]


def _inproj_call(x_rows, mod, norm_g, w_in_bf, seq_len=None):
    rows, d = x_rows.shape
    ts = min(ROW_TILE, rows)
    proj = w_in_bf.shape[1]
    out = jax.ShapeDtypeStruct((rows, ATTN_WIDTH), F32)
    ospec = pl.BlockSpec((ts, ATTN_WIDTH), lambda i: (i, 0))
    in_specs = [pl.BlockSpec((ts, d), lambda i: (i, 0)),
                _mod_spec(mod, rows, ts, 0),
                _mod_spec(mod, rows, ts, 1),
                pl.BlockSpec((1, d), lambda i: (0, 0)),
                pl.BlockSpec((d, proj), lambda i: (0, 0))]
    args = [x_rows, mod, mod, norm_g.reshape(1, d), w_in_bf]
    out_specs = [ospec, ospec, ospec, ospec]
    out_shape = [out, out, out, out]
    if seq_len is not None:
        tiles = seq_len // ts
        in_specs.append(pl.BlockSpec((2 * ATTN_WIDTH, d), lambda i: (0, 0)))
        args.append(w_in_bf[:, ATTN_WIDTH:3 * ATTN_WIDTH].T)
        tspec = pl.BlockSpec((1, ATTN_WIDTH, ts), lambda i: (i // tiles, 0, i % tiles))
        out_specs += [tspec, tspec]
        out_shape += [jax.ShapeDtypeStruct((rows // seq_len, ATTN_WIDTH, seq_len), F32)] * 2
    return pl.pallas_call(
        functools.partial(_inproj_kernel, key_major=seq_len is not None),
        grid=(rows // ts,),
        in_specs=in_specs,
        out_specs=out_specs,
        out_shape=out_shape,
        compiler_params=_cparams(("arbitrary",)),
        name="inproj",
    )(*args)


def _t5_bucket(dist):
    d = jnp.maximum(dist, MAX_EXACT).astype(F32)
    log_part = MAX_EXACT + (jnp.log(d / MAX_EXACT) / math.log(BUCKET_MAX_DIST / MAX_EXACT)
                            * (N_BUCKETS - MAX_EXACT)).astype(jnp.int32)
    return jnp.where(dist < MAX_EXACT, dist, jnp.minimum(log_part, N_BUCKETS - 1))


def _bias_by_distance(rel_bias, dists):
    hot = (_t5_bucket(jnp.asarray(dists, jnp.int32))[:, None]
           == jnp.arange(N_BUCKETS, dtype=jnp.int32)[None, :]).astype(F32)
    return jnp.dot(hot, rel_bias.astype(F32), precision=_HIGHEST)


def _prompt_bias_tiles(rel_bias):
    steps = WINDOW_STEPS
    period = 3 * steps
    tiles = []
    for r in DILATIONS:
        vec = _bias_by_distance(rel_bias, r * np.arange(steps + 1))
        fill = jnp.full((steps - 1, ATTN_HEADS), _NEG_INF, F32)
        w = jnp.concatenate([fill, vec[::-1], fill, fill[:1]], axis=0)
        rep = jnp.tile(w.T, (1, steps))[:, :steps * (period - 1)]
        toe = rep.reshape(ATTN_HEADS, steps, period - 1)[:, :, steps - 1:]
        toe = toe.reshape(ATTN_HEADS // 2, 2, steps, 2 * steps)
        own_only = jnp.where(jnp.arange(2 * steps) < steps, _NEG_INF, toe)
        tiles.append(jnp.stack([toe, own_only], axis=2))
    return jnp.stack(tiles)


def _attn_prompt_kernel(q_ref, k_ref, v_ref, bias_ref, o_ref,
                        p4_ref, qh_ref, kb_ref, vb_ref, s_ref, p_ref, res_ref, stage_ref, nat_ref):
    s_len = q_ref.shape[1]
    steps = WINDOW_STEPS
    n_tiles = s_len // steps
    quarter = s_len // 4
    nt = (((1,), (1,)), ((), ()))
    lane = lax.broadcasted_iota(jnp.int32, (1, LANES), 1)
    first_head = lane < HEAD_DIM
    srcs = (q_ref, k_ref, v_ref)

    kb_ref[0:steps, :] = jnp.zeros((steps, LANES), BF16)
    vb_ref[0:steps, 0:LANES] = jnp.zeros((steps, LANES), BF16)
    vb_ref[:, LANES:] = jnp.ones((s_len + steps, LANES), BF16)
    for x in range(3):
        for sigma in range(4):
            p4_ref[x, sigma * quarter:(sigma + 1) * quarter, :] = srcs[x][0, pl.ds(sigma, quarter, stride=4), :]

    def source(branch, x, tile_idx):
        rows = slice(tile_idx * steps, (tile_idx + 1) * steps)
        if branch == 0:
            return srcs[x][0, rows, :]
        if branch == 1:
            return p4_ref[x, rows, :]
        sigma, tau = tile_idx % 4, tile_idx // 4
        return p4_ref[x, pl.ds(sigma * quarter + tau, steps, stride=4), :]

    for branch, r in enumerate(DILATIONS):
        blocks_per_class = (s_len // r) // steps
        width = 2 * steps if blocks_per_class > 1 else steps

        for t in range(n_tiles):
            rows = slice(t * steps, (t + 1) * steps)
            q2 = source(branch, 0, t) * (HEAD_DIM ** -0.5)
            qh_ref[0, rows, :] = jnp.where(first_head, q2, 0.0).astype(BF16)
            qh_ref[1, rows, :] = jnp.where(first_head, 0.0, q2).astype(BF16)
            kb_ref[steps + t * steps:steps + (t + 1) * steps, :] = source(branch, 1, t).astype(BF16)
            vb_ref[steps + t * steps:steps + (t + 1) * steps, 0:LANES] = source(branch, 2, t).astype(BF16)

        def key_rows(t, width=width):
            start = t * steps if width == 2 * steps else (t + 1) * steps
            return pl.ds(pl.multiple_of(start, steps), width)

        def scores(t, carry, branch=branch, width=width, blocks_per_class=blocks_per_class,
                   key_rows=key_rows):
            rows = pl.ds(pl.multiple_of(t * steps, steps), steps)
            keys = kb_ref[key_rows(t), :]
            first = jnp.where(t % blocks_per_class == 0, 1, 0)
            for hh in range(2):
                if width == 2 * steps:
                    bias = bias_ref[branch, 0, hh, pl.ds(first, 1), :, :][0]
                else:
                    bias = bias_ref[branch, 0, hh, 0, :, steps:]
                sc = lax.dot_general(qh_ref[hh, rows, :], keys, nt, preferred_element_type=F32)
                s_ref[hh, rows, 0:width] = sc + bias
            return carry

        lax.fori_loop(0, n_tiles, scores, 0, unroll=ATTN_MXU_UNROLL)

        def softmax(i, carry, width=width):
            rows = pl.ds(pl.multiple_of(i * 2 * steps, 2 * steps), 2 * steps)
            for hh in range(2):
                sc = s_ref[hh, rows, 0:width]
                m = jnp.max(sc, axis=1, keepdims=True)
                p_ref[hh, rows, 0:width] = jnp.exp(sc - m).astype(BF16)
                res_ref[1, rows, hh * HEAD_DIM:(hh + 1) * HEAD_DIM] = jnp.broadcast_to(m, (2 * steps, HEAD_DIM))
            return carry

        lax.fori_loop(0, n_tiles // 2, softmax, 0, unroll=2)

        def weighted(t, carry, width=width, key_rows=key_rows):
            rows = pl.ds(pl.multiple_of(t * steps, steps), steps)
            vals = vb_ref[key_rows(t), :]
            r0 = jnp.dot(p_ref[0, rows, 0:width], vals, preferred_element_type=F32)
            r1 = jnp.dot(p_ref[1, rows, 0:width], vals, preferred_element_type=F32)
            res_ref[0, rows, :] = jnp.where(first_head, r0[:, :LANES], r1[:, :LANES])
            res_ref[2, rows, :] = jnp.where(first_head, r0[:, LANES:], r1[:, LANES:])
            return carry

        lax.fori_loop(0, n_tiles, weighted, 0, unroll=ATTN_MXU_UNROLL)

        for kind in range(3):
            if branch == 0:
                nat_ref[0, kind] = res_ref[kind]
                continue
            src = res_ref
            if branch == 2:
                for t in range(n_tiles):
                    sigma, tau = t % 4, t // 4
                    stage_ref[kind, pl.ds(sigma * quarter + tau, steps, stride=4), :] = (
                        res_ref[kind, t * steps:(t + 1) * steps, :])
                src = stage_ref
            for sigma in range(4):
                nat_ref[branch, kind, pl.ds(sigma, quarter, stride=4), :] = (
                    src[kind, sigma * quarter:(sigma + 1) * quarter, :])

    def merge(i, carry):
        rows = pl.ds(pl.multiple_of(i * 256, 256), 256)
        m0, m1, m2 = nat_ref[0, 1, rows, :], nat_ref[1, 1, rows, :], nat_ref[2, 1, rows, :]
        m_all = jnp.maximum(jnp.maximum(m0, m1), m2)
        w0, w1, w2 = jnp.exp(m0 - m_all), jnp.exp(m1 - m_all), jnp.exp(m2 - m_all)
        num = w0 * nat_ref[0, 0, rows, :] + w1 * nat_ref[1, 0, rows, :] + w2 * nat_ref[2, 0, rows, :]
        den = w0 * nat_ref[0, 2, rows, :] + w1 * nat_ref[1, 2, rows, :] + w2 * nat_ref[2, 2, rows, :]
        o_ref[0, rows, :] = num / den
        return carry

    lax.fori_loop(0, s_len // 256, merge, 0)


def _attn_prompt_call(q, k, v, bias_tiles):
    n, s, _ = q.shape
    pairs = ATTN_HEADS // 2
    steps = WINDOW_STEPS
    qspec = pl.BlockSpec((1, s, LANES), lambda n_, g: (n_, 0, g))
    return pl.pallas_call(
        _attn_prompt_kernel,
        grid=(n, pairs),
        in_specs=[qspec, qspec, qspec,
                  pl.BlockSpec((3, 1, 2, 2, steps, 2 * steps), lambda n_, g: (0, g, 0, 0, 0, 0))],
        out_specs=qspec,
        out_shape=jax.ShapeDtypeStruct((n, s, ATTN_WIDTH), F32),
        scratch_shapes=[pltpu.VMEM((3, s, LANES), F32),
                        pltpu.VMEM((2, s, LANES), BF16),
                        pltpu.VMEM((s + steps, LANES), BF16),
                        pltpu.VMEM((s + steps, 2 * LANES), BF16),
                        pltpu.VMEM((2, s, 2 * steps), F32),
                        pltpu.VMEM((2, s, 2 * steps), BF16),
                        pltpu.VMEM((3, s, LANES), F32), pltpu.VMEM((3, s, LANES), F32),
                        pltpu.VMEM((3, 3, s, LANES), F32)],
        compiler_params=_cparams(("arbitrary", "arbitrary")),
        name="attn_prompt",
    )(q, k, v, bias_tiles)


def _s5_discretise(a_re, a_im, log_dt, b_re, b_im):
    lam_re = jnp.minimum(a_re.astype(F32), -1e-4)
    lam_im = a_im.astype(F32)
    dt = jnp.exp(log_dt.astype(F32))[:, None]
    mag = jnp.exp(lam_re * dt)
    ph = lam_im * dt
    abar_re, abar_im = mag * jnp.cos(ph), mag * jnp.sin(ph)
    nr, ni = abar_re - 1.0, abar_im
    den = lam_re * lam_re + lam_im * lam_im
    coef_re = (nr * lam_re + ni * lam_im) / den
    coef_im = (ni * lam_re - nr * lam_im) / den
    br, bi = b_re.astype(F32), b_im.astype(F32)
    bbar_re = coef_re[..., None] * br - coef_im[..., None] * bi
    bbar_im = coef_re[..., None] * bi + coef_im[..., None] * br
    return lam_re * dt, ph, abar_re, abar_im, bbar_re, bbar_im


def _abar_power(log_mag, ph, n):
    nf = jnp.asarray(n, F32)[:, None, None]
    mag = jnp.exp(nf * log_mag[None])
    return mag * jnp.cos(nf * ph[None]), mag * jnp.sin(nf * ph[None])


def _s5_prompt_operators(a_re, a_im, log_dt, b_re, b_im, c_re, c_im):
    L = SSM_CHUNK
    log_mag, ph, _, _, bb_re, bb_im = _s5_discretise(a_re, a_im, log_dt, b_re, b_im)
    cr, ci = c_re.astype(F32), c_im.astype(F32)
    pw_re, pw_im = _abar_power(log_mag, ph, np.arange(L + 1))
    eye = jnp.eye(OCTET, dtype=F32)

    ab_re = pw_re[:L, :, :, None] * bb_re[None] - pw_im[:L, :, :, None] * bb_im[None]
    ab_im = pw_re[:L, :, :, None] * bb_im[None] + pw_im[:L, :, :, None] * bb_re[None]
    lag = (jnp.einsum('gop,lgpi->lgoi', cr, ab_re, precision=_HIGHEST)
           - jnp.einsum('gop,lgpi->lgoi', ci, ab_im, precision=_HIGHEST))
    lag = lag.reshape(L, N_OCTETS, OCTET, SSM_GROUP_CH, SSM_GROUP_CH)
    bd = jnp.einsum('logci,gh->olgihc', lag, eye).reshape(N_OCTETS, L, LANES, LANES).astype(BF16)
    stack = bd[:, ::-1].reshape(N_OCTETS, L * LANES, LANES)
    shifted = jnp.concatenate([stack[:, LANES:], jnp.zeros((N_OCTETS, LANES, LANES), BF16)], axis=1)
    t_op = jnp.concatenate([shifted, stack], axis=-1)

    m_parts = []
    for part in (ab_re[::-1], ab_im[::-1]):
        x = part.reshape(L, N_OCTETS, OCTET, SSM_STATE, SSM_GROUP_CH)
        m_parts.append(jnp.einsum('sogpi,gh->osgihp', x, eye).reshape(N_OCTETS, L * LANES, OCT_STATE))
    m_op = jnp.concatenate(m_parts, axis=-1)

    p1_re, p1_im = pw_re[1:], pw_im[1:]
    on_re = cr[None] * p1_re[:, :, None, :] - ci[None] * p1_im[:, :, None, :]
    on_im = -cr[None] * p1_im[:, :, None, :] - ci[None] * p1_re[:, :, None, :]
    p_parts = []
    for part in (on_re, on_im):
        x = part.reshape(L, N_OCTETS, OCTET, SSM_GROUP_CH, SSM_STATE)
        p_parts.append(jnp.einsum('togcp,gh->ogpthc', x, eye).reshape(N_OCTETS, OCT_STATE, L * LANES))
    p_op = jnp.concatenate(p_parts, axis=1)

    n_steps = 8
    sc_re, sc_im = _abar_power(log_mag, ph, L * (2 ** np.arange(n_steps)))
    sc = jnp.concatenate([sc_re.reshape(n_steps, N_OCTETS, OCT_STATE),
                          sc_im.reshape(n_steps, N_OCTETS, OCT_STATE)], axis=-1)
    sc = jnp.transpose(sc, (1, 0, 2))
    return t_op, m_op.astype(BF16), p_op.astype(BF16), sc


def _octet_glu(glu):
    eye = jnp.eye(OCTET, dtype=F32)
    x = glu.astype(F32).reshape(N_OCTETS, OCTET, SSM_GROUP_CH, SSM_GROUP_CH)
    return jnp.einsum('ogce,gh->ogche', x, eye).reshape(N_OCTETS, LANES, LANES).astype(BF16)


def _ssm_prompt_kernel(u_ref, t_ref, m_ref, p_ref, sc_ref, d_ref, ga_ref, gb_ref,
                       y_ref, h_ref, uf_ref, ub_ref, st_ref):
    L = SSM_CHUNK
    n_seq, s_len = u_ref.shape[0], u_ref.shape[1]
    n_chunks = s_len // L
    quarter = s_len // 4
    rows = n_seq * n_chunks

    def staged(sq, step):
        sigma, tau = step % 4, step // 4
        return pl.ds(sq * s_len + sigma * quarter + tau, n_chunks, stride=4)

    for sq in range(n_seq):
        for sigma in range(4):
            st_ref[sq * s_len + sigma * quarter:sq * s_len + (sigma + 1) * quarter, :] = (
                u_ref[sq, pl.ds(sigma, quarter, stride=4), :])
        for step in range(L):
            blk = st_ref[staged(sq, step), :]
            uf_ref[sq * n_chunks:(sq + 1) * n_chunks, step * LANES:(step + 1) * LANES] = blk
            ub_ref[sq * n_chunks:(sq + 1) * n_chunks, step * LANES:(step + 1) * LANES] = blk.astype(BF16)
    ub = ub_ref[...]

    x = jnp.dot(ub, m_ref[0], preferred_element_type=F32)
    chunk = lax.broadcasted_iota(jnp.int32, (rows, 1), 0) % n_chunks
    half = OCT_STATE
    k = 1
    step = 0
    while k < n_chunks:
        a_re = sc_ref[0, step:step + 1, :half]
        a_im = sc_ref[0, step:step + 1, half:]
        sh = jnp.where(chunk >= k, pltpu.roll(x, k, axis=0), 0.0)
        s_re, s_im = sh[:, :half], sh[:, half:]
        x = x + jnp.concatenate([a_re * s_re - a_im * s_im, a_re * s_im + a_im * s_re], axis=1)
        k *= 2
        step += 1
    for sq in range(n_seq):
        h_ref[sq, 0] = x[(sq + 1) * n_chunks - 1:(sq + 1) * n_chunks, :]
    h_start = jnp.where(chunk >= 1, pltpu.roll(x, 1, axis=0), 0.0)

    hb = h_start.astype(BF16)
    d = d_ref[0]
    ga = ga_ref[0]
    gb = gb_ref[0]
    for t in range(0, L, 2):
        pair = slice(t * LANES, (t + 2) * LANES)
        y2 = (jnp.dot(ub_ref[:, :(t + 2) * LANES], t_ref[0, (L - 2 - t) * LANES:, :],
                      preferred_element_type=F32)
              + jnp.dot(hb, p_ref[0, :, pair], preferred_element_type=F32))
        for j in range(2):
            lanes = slice((t + j) * LANES, (t + j + 1) * LANES)
            g = _gelu_tanh(y2[:, j * LANES:(j + 1) * LANES] + d * uf_ref[:, lanes]).astype(BF16)
            out = (jnp.dot(g, ga, preferred_element_type=F32)
                   * _sigmoid(jnp.dot(g, gb, preferred_element_type=F32)))
            for sq in range(n_seq):
                st_ref[staged(sq, t + j), :] = out[sq * n_chunks:(sq + 1) * n_chunks, :]
    for sq in range(n_seq):
        for sigma in range(4):
            y_ref[sq, pl.ds(sigma, quarter, stride=4), :] = (
                st_ref[sq * s_len + sigma * quarter:sq * s_len + (sigma + 1) * quarter, :])


def _ssm_prompt_call(u, t_op, m_op, p_op, sc, d_oct, ga, gb):
    n, s, _ = u.shape
    L = SSM_CHUNK
    nq = math.gcd(SSM_SEQS, n)
    rows = nq * (s // L)
    wide = L * LANES
    wspec = lambda shape: pl.BlockSpec((1,) + shape, lambda o, n_: (o, 0, 0))
    return pl.pallas_call(
        _ssm_prompt_kernel,
        grid=(N_OCTETS, n // nq),
        in_specs=[pl.BlockSpec((nq, s, LANES), lambda o, n_: (n_, 0, o)),
                  wspec((wide, 2 * LANES)), wspec((wide, 2 * OCT_STATE)), wspec((2 * OCT_STATE, wide)),
                  wspec((8, 2 * OCT_STATE)), wspec((1, LANES)),
                  wspec((LANES, LANES)), wspec((LANES, LANES))],
        out_specs=[pl.BlockSpec((nq, s, LANES), lambda o, n_: (n_, 0, o)),
                   pl.BlockSpec((nq, 1, 1, 2 * OCT_STATE), lambda o, n_: (n_, o, 0, 0))],
        out_shape=[jax.ShapeDtypeStruct((n, s, SSM_WIDTH), F32),
                   jax.ShapeDtypeStruct((n, N_OCTETS, 1, 2 * OCT_STATE), F32)],
        scratch_shapes=[pltpu.VMEM((rows, wide), F32),
                        pltpu.VMEM((rows, wide), BF16),
                        pltpu.VMEM((nq * s, LANES), F32)],
        compiler_params=_cparams(("arbitrary", "arbitrary")),
        name="ssm_prompt",
    )(u, t_op, m_op, p_op, sc, d_oct, ga, gb)


def _attn_decode_kernel(q_ref, kn_ref, vn_ref, kt_ref, vt_ref, b_ref, mult_ref, o_ref):
    nt = (((1,), (1,)), ((), ()))
    mult = mult_ref[...]
    for h in range(ATTN_HEADS):
        q = q_ref[0, h]
        kt = jnp.concatenate([kt_ref[0, h].astype(BF16), kn_ref[0, h]], axis=1)
        vt = jnp.concatenate([vt_ref[0, h].astype(BF16), vn_ref[0, h]], axis=1)
        s = jnp.dot(q, kt, preferred_element_type=F32) + b_ref[h]
        m = jnp.max(s, axis=1, keepdims=True)
        p = jnp.exp(s - m) * mult
        den = jnp.sum(p, axis=1, keepdims=True)
        o = lax.dot_general(p.astype(BF16), vt, nt, preferred_element_type=F32)
        o_ref[0, h] = o / den


def _decode_tables(rel_bias, t_len, w_rows):
    t = np.arange(t_len)[:, None]
    dist = np.concatenate([w_rows + t - np.arange(w_rows)[None, :],
                           t - np.arange(LANES)[None, :]], axis=1)
    mult = np.zeros(dist.shape, np.float32)
    for w, r in zip(WINDOWS, DILATIONS):
        mult += (dist >= 0) & (dist % r == 0) & (dist <= w)
    mult = np.concatenate([mult, np.zeros((Q_ROWS - t_len, dist.shape[1]), np.float32)], axis=0)
    mult[t_len:, 0] = 1.0
    by_dist = _bias_by_distance(rel_bias, np.arange(w_rows + t_len))
    rows = []
    for ti in range(t_len):
        cache_part = by_dist[ti + 1:w_rows + ti + 1][::-1]
        new_part = by_dist[:ti + 1][::-1]
        pad = jnp.zeros((LANES - ti - 1, ATTN_HEADS), F32)
        rows.append(jnp.concatenate([cache_part, new_part, pad], axis=0))
    bias = jnp.stack(rows + [jnp.zeros_like(rows[0])] * (Q_ROWS - t_len), axis=0)
    bias = jnp.transpose(bias, (2, 0, 1))
    bias = jnp.where(jnp.asarray(mult)[None] > 0, bias, _NEG_INF)
    return bias, jnp.asarray(mult)


def _attn_decode_call(q, k_new, v_new, cache_k, cache_v, rel_bias):
    n, t_len, w = q.shape
    w_rows = cache_k.shape[1]
    if t_len > min(DILATIONS[1:]) or t_len > Q_ROWS or w_rows < max(WINDOWS):
        raise ValueError("unsupported decode shape")
    heads = (ATTN_HEADS, HEAD_DIM)

    def head_major(a, pad_to):
        a = jnp.transpose(a.reshape((n, t_len) + heads), (0, 2, 1, 3))
        return jnp.pad(a, ((0, 0), (0, 0), (0, pad_to - t_len), (0, 0)))

    qh = head_major(q * (HEAD_DIM ** -0.5), Q_ROWS).astype(BF16)
    knt = jnp.swapaxes(head_major(k_new, LANES), 2, 3).astype(BF16)
    vnt = jnp.swapaxes(head_major(v_new, LANES), 2, 3).astype(BF16)
    kt = jnp.transpose(cache_k.astype(F32), (0, 2, 3, 1))
    vt = jnp.transpose(cache_v.astype(F32), (0, 2, 3, 1))
    bias, mult = _decode_tables(rel_bias, t_len, w_rows)
    keys = w_rows + LANES
    per_seq = lambda shape: pl.BlockSpec((1,) + shape, lambda i: (i, 0, 0, 0))
    out = pl.pallas_call(
        _attn_decode_kernel,
        grid=(n,),
        in_specs=[per_seq((ATTN_HEADS, Q_ROWS, HEAD_DIM)),
                  per_seq((ATTN_HEADS, HEAD_DIM, LANES)), per_seq((ATTN_HEADS, HEAD_DIM, LANES)),
                  per_seq((ATTN_HEADS, HEAD_DIM, w_rows)), per_seq((ATTN_HEADS, HEAD_DIM, w_rows)),
                  pl.BlockSpec((ATTN_HEADS, Q_ROWS, keys), lambda i: (0, 0, 0)),
                  pl.BlockSpec((Q_ROWS, keys), lambda i: (0, 0))],
        out_specs=per_seq((ATTN_HEADS, Q_ROWS, HEAD_DIM)),
        out_shape=jax.ShapeDtypeStruct((n, ATTN_HEADS, Q_ROWS, HEAD_DIM), F32),
        compiler_params=_cparams(("arbitrary",)),
        name="attn_decode",
    )(qh, knt, vnt, kt, vt, bias, mult)
    return jnp.transpose(out[:, :, :t_len], (0, 2, 1, 3)).reshape(n, t_len, w)


def _ssm_decode_kernel(u_ref, hre_ref, him_ref, are_ref, aim_ref, bre_ref, bim_ref,
                       cre_ref, cim_ref, d_ref, ga_ref, gb_ref, y_ref, ore_ref, oim_ref, *, t_len):
    h_re, h_im = hre_ref[...], him_ref[...]
    a_re, a_im = are_ref[...], aim_ref[...]
    for t in range(t_len):
        u = u_ref[t]
        ub = u.astype(BF16)
        n_re = a_re * h_re - a_im * h_im + jnp.dot(ub, bre_ref[...], preferred_element_type=F32)
        n_im = a_re * h_im + a_im * h_re + jnp.dot(ub, bim_ref[...], preferred_element_type=F32)
        h_re, h_im = n_re, n_im
        y = (jnp.dot(h_re.astype(BF16), cre_ref[...], preferred_element_type=F32)
             - jnp.dot(h_im.astype(BF16), cim_ref[...], preferred_element_type=F32)
             + d_ref[...] * u)
        g = _gelu_tanh(y).astype(BF16)
        y_ref[t] = (jnp.dot(g, ga_ref[...], preferred_element_type=F32)
                    * _sigmoid(jnp.dot(g, gb_ref[...], preferred_element_type=F32)))
    ore_ref[...] = h_re
    oim_ref[...] = h_im


def _group_blockdiag(x):
    g, a, b = x.shape
    return jnp.einsum('gab,gh->gahb', x, jnp.eye(g, dtype=x.dtype)).reshape(g * a, g * b)


def _ssm_decode_call(u_tm, h0_re, h0_im, a_re, a_im, log_dt, b_re, b_im, c_re, c_im,
                     d_skip, glu_a, glu_b):
    n = h0_re.shape[0]
    _, _, abar_re, abar_im, bb_re, bb_im = _s5_discretise(a_re, a_im, log_dt, b_re, b_im)
    state = SSM_GROUPS * SSM_STATE
    t_len = u_tm.shape[0]
    args = (u_tm, h0_re.reshape(n, state).astype(F32), h0_im.reshape(n, state).astype(F32),
            abar_re.reshape(1, state), abar_im.reshape(1, state),
            _group_blockdiag(jnp.transpose(bb_re, (0, 2, 1))).astype(BF16),
            _group_blockdiag(jnp.transpose(bb_im, (0, 2, 1))).astype(BF16),
            _group_blockdiag(jnp.transpose(c_re.astype(F32), (0, 2, 1))).astype(BF16),
            _group_blockdiag(jnp.transpose(c_im.astype(F32), (0, 2, 1))).astype(BF16),
            d_skip.astype(F32).reshape(1, SSM_WIDTH),
            _group_blockdiag(glu_a.astype(F32)).astype(BF16),
            _group_blockdiag(glu_b.astype(F32)).astype(BF16))
    full = lambda a: pl.BlockSpec(a.shape, lambda i: (0,) * a.ndim)
    out_shape = [jax.ShapeDtypeStruct(u_tm.shape, F32),
                 jax.ShapeDtypeStruct((n, state), F32), jax.ShapeDtypeStruct((n, state), F32)]
    return pl.pallas_call(
        functools.partial(_ssm_decode_kernel, t_len=t_len),
        grid=(1,),
        in_specs=[full(a) for a in args],
        out_specs=[full(o) for o in out_shape],
        out_shape=out_shape,
        compiler_params=_cparams(("arbitrary",)),
        name="ssm_decode",
    )(*args)


def _outproj_kernel(att_ref, ssm_ref, x_ref, g1_ref, sh2_ref, sc2_ref, ag_ref, sg_ref, n2_ref,
                    wo_ref, wr_ref, rb_ref, x1_ref, h2_ref, route_ref):
    mixed = jnp.concatenate([_rmsnorm(att_ref[...], ag_ref[...]), _rmsnorm(ssm_ref[...], sg_ref[...])],
                            axis=1).astype(BF16)
    x1 = x_ref[...] + g1_ref[0] * jnp.dot(mixed, wo_ref[...], preferred_element_type=F32)
    x1_ref[...] = x1
    h2 = _rmsnorm(x1, n2_ref[...]) * (1.0 + sc2_ref[0]) + sh2_ref[0]
    hi = h2.astype(BF16)
    h2_ref[...] = hi
    lo = (h2 - hi.astype(F32)).astype(BF16)
    r1 = jnp.dot(hi, wr_ref[...], preferred_element_type=F32)
    r2 = jnp.dot(lo, wr_ref[:, :LANES], preferred_element_type=F32)
    logits = r1[:, :LANES] + r1[:, LANES:] + r2 + rb_ref[...]

    lane = lax.broadcasted_iota(jnp.int32, (1, LANES), 1)
    lane_f = lane.astype(F32)
    big = float(LANES)
    ng, epg = N_EXPERT_GROUPS, EXPERTS_PER_GROUP
    lg = jnp.where(lane < ng, logits, _NEG_INF)
    gmax = jnp.max(lg, axis=1, keepdims=True)
    p_star = 1.0 / jnp.sum(jnp.exp(lg - gmax), axis=1, keepdims=True)
    g_star = jnp.min(jnp.where(lg == gmax, lane_f, big), axis=1, keepdims=True)
    in_group = ((lane >= ng) & (lane < ng + ng * epg)
                & (lax.shift_right_arithmetic(lane - ng, int(math.log2(epg))).astype(F32) == g_star))
    le = jnp.where(in_group, logits, _NEG_INF)
    v1 = jnp.max(le, axis=1, keepdims=True)
    i1 = jnp.min(jnp.where(le == v1, lane_f, big), axis=1, keepdims=True)
    le2 = jnp.where(lane_f == i1, _NEG_INF, le)
    v2 = jnp.max(le2, axis=1, keepdims=True)
    i2 = jnp.min(jnp.where(le2 == v2, lane_f, big), axis=1, keepdims=True)
    e2 = jnp.exp(v2 - v1)
    w1 = p_star / (1.0 + e2)
    w2 = p_star * e2 / (1.0 + e2)
    route_ref[...] = jnp.where(lane == 0, i1 - ng,
                               jnp.where(lane == 1, i2 - ng,
                                         jnp.where(lane == 2, w1, jnp.where(lane == 3, w2, 0.0))))


def _outproj_call(att, ssm_y, x_rows, mod, attn_g, ssm_g, norm2_g, w_out_bf, wr, rb):
    rows, d = x_rows.shape
    ts = min(ROW_TILE, rows)
    row = lambda width: pl.BlockSpec((ts, width), lambda i: (i, 0))
    const = lambda a: pl.BlockSpec(a.shape, lambda i: (0,) * a.ndim)
    attn_g = attn_g.reshape(1, ATTN_WIDTH)
    ssm_g = ssm_g.reshape(1, SSM_WIDTH)
    norm2_g = norm2_g.reshape(1, d)
    return pl.pallas_call(
        _outproj_kernel,
        grid=(rows // ts,),
        in_specs=[row(ATTN_WIDTH), row(SSM_WIDTH), row(d),
                  _mod_spec(mod, rows, ts, 2), _mod_spec(mod, rows, ts, 3), _mod_spec(mod, rows, ts, 4),
                  const(attn_g), const(ssm_g), const(norm2_g), const(w_out_bf), const(wr), const(rb)],
        out_specs=[row(d), row(d), row(LANES)],
        out_shape=[jax.ShapeDtypeStruct((rows, d), F32), jax.ShapeDtypeStruct((rows, d), BF16),
                   jax.ShapeDtypeStruct((rows, LANES), F32)],
        compiler_params=_cparams(("arbitrary",)),
        name="outproj_router",
    )(att, ssm_y, x_rows, mod, mod, mod, attn_g, ssm_g, norm2_g, w_out_bf, wr, rb)


def _router_weights(router_g_w, router_g_b, router_e_w, router_e_b):
    d = router_g_w.shape[0]
    ne = N_EXPERT_GROUPS * EXPERTS_PER_GROUP
    w = jnp.concatenate([router_g_w.astype(F32),
                         jnp.transpose(router_e_w.astype(F32), (1, 0, 2)).reshape(d, ne)], axis=1)
    w = jnp.pad(w, ((0, 0), (0, LANES - w.shape[1])))
    hi = w.astype(BF16)
    lo = (w - hi.astype(F32)).astype(BF16)
    b = jnp.concatenate([router_g_b.astype(F32), router_e_b.astype(F32).reshape(ne)])
    b = jnp.pad(b, (0, LANES - b.shape[0])).reshape(1, LANES)
    return jnp.concatenate([hi, lo], axis=1), b


def _moe_kernel(te_ref, x_ref, wg_ref, wu_ref, wd_ref, o_ref):
    del te_ref
    x = x_ref[...]
    gate = jnp.dot(x, wg_ref[0], preferred_element_type=F32)
    up = jnp.dot(x, wu_ref[0], preferred_element_type=F32)
    a = (gate * _sigmoid(gate)) * up
    o_ref[...] = jnp.dot(a.astype(BF16), wd_ref[0], preferred_element_type=F32).astype(o_ref.dtype)


def _moe_call(tile_expert, x_sorted, wg, wu, wd, tm):
    n_slots, d = x_sorted.shape
    fe = wg.shape[2]
    grid_spec = pltpu.PrefetchScalarGridSpec(
        num_scalar_prefetch=1,
        grid=(n_slots // tm,),
        in_specs=[pl.BlockSpec((tm, d), lambda i, te: (i, 0)),
                  pl.BlockSpec((1, d, fe), lambda i, te: (te[i], 0, 0)),
                  pl.BlockSpec((1, d, fe), lambda i, te: (te[i], 0, 0)),
                  pl.BlockSpec((1, fe, d), lambda i, te: (te[i], 0, 0))],
        out_specs=pl.BlockSpec((tm, d), lambda i, te: (i, 0)),
    )
    return pl.pallas_call(
        _moe_kernel,
        grid_spec=grid_spec,
        out_shape=jax.ShapeDtypeStruct((n_slots, d), BF16),
        compiler_params=_cparams(("arbitrary",)),
        name="moe_experts",
    )(tile_expert, x_sorted, wg, wu, wd)


def _moe_dispatch(route, tm):
    n_tok = route.shape[0]
    ids = route[:, :2].astype(jnp.int32).reshape(-1)
    n_pairs = ids.shape[0]
    n_slots = (-(-n_pairs // tm) + N_EXPERTS) * tm
    hot = (ids[:, None] == jnp.arange(N_EXPERTS)[None, :]).astype(jnp.int32)
    csum = jnp.cumsum(hot, axis=0)
    rank = jnp.sum((csum - hot) * hot, axis=1)
    counts = csum[-1]
    padded = -(-counts // tm) * tm
    ends = jnp.cumsum(padded)
    starts = ends - padded
    pos = jnp.sum(hot * starts[None, :], axis=1) + rank
    tok_of_slot = (jnp.arange(n_slots, dtype=jnp.int32) % n_tok).at[pos].set(
        jnp.arange(n_pairs, dtype=jnp.int32) // 2, unique_indices=True, mode="promise_in_bounds")
    tile_start = jnp.arange(n_slots // tm, dtype=jnp.int32) * tm
    tile_expert = jnp.minimum(jnp.sum((tile_start[:, None] >= ends[None, :]).astype(jnp.int32), axis=1),
                              N_EXPERTS - 1).astype(jnp.int32)
    return pos.reshape(n_tok, 2), tok_of_slot, tile_expert


def _take_rows(x, idx):
    return x.at[idx].get(mode="promise_in_bounds")


def _final_kernel(x1_ref, ya_ref, yb_ref, route_ref, g2_ref, fg_ref, o_ref):
    wa = route_ref[:, 2:3]
    wb = route_ref[:, 3:4]
    x = x1_ref[...] + g2_ref[0] * (wa * ya_ref[...].astype(F32) + wb * yb_ref[...].astype(F32))
    o_ref[...] = _rmsnorm(x, fg_ref[...])


def _final_call(x1, ya, yb, route, mod, final_g):
    rows, d = x1.shape
    ts = min(ROW_TILE, rows)
    row = pl.BlockSpec((ts, d), lambda i: (i, 0))
    return pl.pallas_call(
        _final_kernel,
        grid=(rows // ts,),
        in_specs=[row, row, row, pl.BlockSpec((ts, LANES), lambda i: (i, 0)), _mod_spec(mod, rows, ts, 5),
                  pl.BlockSpec((1, d), lambda i: (0, 0))],
        out_specs=row,
        out_shape=jax.ShapeDtypeStruct((rows, d), F32),
        compiler_params=_cparams(("arbitrary",)),
        name="final_norm",
    )(x1, ya, yb, route, mod, final_g.reshape(1, d))


def kernel(x_prompt, x_sample, c_prompt, c_sample, cache_k, cache_v, state_ssm_re, state_ssm_im,
           rel_bias, ada_w, ada_b, norm1_g, w_in, ssm_a_re, ssm_a_im, ssm_log_dt, ssm_b_re, ssm_b_im,
           ssm_c_re, ssm_c_im, ssm_d, glu_a, glu_b, attn_out_g, ssm_out_g, w_out, norm2_g,
           router_g_w, router_g_b, router_e_w, router_e_b, w_gate, w_up, w_down, final_norm_g):
    if ada_w.shape[0] != 1:
        raise ValueError("single-layer trunk expected")
    nb, s_len, d = x_prompt.shape
    nd, t_len, _ = x_sample.shape
    if s_len != max(WINDOWS):
        raise ValueError("prompt length must equal the widest window")
    n_p, n_s = nb * s_len, nd * t_len
    if n_p % ROW_TILE or n_s % ROW_TILE:
        raise ValueError("token counts must be multiples of the row tile")

    mod = _mod_call(jnp.concatenate([c_prompt, c_sample], axis=0).astype(F32), ada_w[0], ada_b[0])
    mod_p = mod[:nb].reshape(nb, 1, 6 * d)
    mod_s = jnp.repeat(mod[nb:], t_len, axis=0).reshape(1, n_s, 6 * d)

    w_in_bf = w_in[0].astype(BF16)
    w_out_bf = w_out[0].astype(BF16)
    xp_rows = x_prompt.reshape(n_p, d)
    xs_rows = x_sample.reshape(n_s, d)

    qp, kp, vp, up, kp_t, vp_t = _inproj_call(xp_rows, mod_p, norm1_g[0], w_in_bf, seq_len=s_len)
    qs, ks, vs, us = _inproj_call(xs_rows, mod_s, norm1_g[0], w_in_bf)

    seq = lambda a: a.reshape(nb, s_len, ATTN_WIDTH)
    att_p = _attn_prompt_call(seq(qp), seq(kp), seq(vp), _prompt_bias_tiles(rel_bias))
    dec = lambda a: a.reshape(nd, t_len, ATTN_WIDTH)
    att_s = _attn_decode_call(dec(qs), dec(ks), dec(vs), cache_k[0], cache_v[0], rel_bias)

    s5 = (ssm_a_re[0], ssm_a_im[0], ssm_log_dt[0], ssm_b_re[0], ssm_b_im[0], ssm_c_re[0], ssm_c_im[0])
    t_op, m_op, p_op, sc = _s5_prompt_operators(*s5)
    d_oct = ssm_d[0].astype(F32).reshape(N_OCTETS, 1, LANES)
    ssm_p, hT_p = _ssm_prompt_call(seq(up), t_op, m_op, p_op, sc, d_oct,
                                   _octet_glu(glu_a[0]), _octet_glu(glu_b[0]))
    hT_p = hT_p.reshape(nb, N_OCTETS, 2, OCTET, SSM_STATE)
    ssm_re_p = hT_p[:, :, 0].reshape(nb, SSM_GROUPS, SSM_STATE)
    ssm_im_p = hT_p[:, :, 1].reshape(nb, SSM_GROUPS, SSM_STATE)
    us_tm = jnp.transpose(us.reshape(nd, t_len, SSM_WIDTH), (1, 0, 2))
    ssm_s, hre_s, him_s = _ssm_decode_call(us_tm, state_ssm_re[0], state_ssm_im[0], *s5,
                                           ssm_d[0], glu_a[0], glu_b[0])
    ssm_s = jnp.transpose(ssm_s, (1, 0, 2)).reshape(n_s, SSM_WIDTH)

    wr, rb = _router_weights(router_g_w[0], router_g_b[0], router_e_w[0], router_e_b[0])
    norms = (attn_out_g[0], ssm_out_g[0], norm2_g[0], w_out_bf, wr, rb)
    x1_p, h2_p, route_p = _outproj_call(att_p.reshape(n_p, ATTN_WIDTH), ssm_p.reshape(n_p, SSM_WIDTH),
                                        xp_rows, mod_p, *norms)
    x1_s, h2_s, route_s = _outproj_call(att_s.reshape(n_s, ATTN_WIDTH), ssm_s, xs_rows, mod_s, *norms)

    ne = N_EXPERTS
    wg = w_gate[0].reshape(ne, d, D_EXPERT).astype(BF16)
    wu = w_up[0].reshape(ne, d, D_EXPERT).astype(BF16)
    wd = w_down[0].reshape(ne, D_EXPERT, d).astype(BF16)

    def experts(x1, h2, route, mod_rows, tm):
        pos, tok_of_slot, tile_expert = _moe_dispatch(route, tm)
        y_slots = _moe_call(tile_expert, _take_rows(h2, tok_of_slot), wg, wu, wd, tm)
        return _final_call(x1, _take_rows(y_slots, pos[:, 0]), _take_rows(y_slots, pos[:, 1]),
                           route, mod_rows, final_norm_g)

    y_p = experts(x1_p, h2_p, route_p, mod_p, MOE_TILE)
    y_s = experts(x1_s, h2_s, route_s, mod_s, MOE_TILE_DECODE)

    heads = (ATTN_HEADS, HEAD_DIM)
    cache_out = lambda a: jnp.transpose(a.reshape((1, nb) + heads + (s_len,)), (0, 1, 4, 2, 3))
    return (y_p.reshape(nb, s_len, d), y_s.reshape(nd, t_len, d),
            cache_out(kp_t), cache_out(vp_t),
            ks.reshape((1, nd, t_len) + heads), vs.reshape((1, nd, t_len) + heads),
            ssm_re_p[None], ssm_im_p[None],
            hre_s.reshape(1, nd, SSM_GROUPS, SSM_STATE), him_s.reshape(1, nd, SSM_GROUPS, SSM_STATE))
```

```python
import functools
import math

import numpy as np

import jax
import jax.numpy as jnp
from jax import lax
from jax.experimental import pallas as pl
from jax.experimental.pallas import tpu as pltpu

F32 = jnp.float32
BF16 = jnp.bfloat16

D_MODEL = 1024
HEAD_DIM = 64
ATTN_WIDTH = 512
ATTN_HEADS = 8
SSM_WIDTH = 512
SSM_GROUP_CH = 16
SSM_GROUPS = 32
SSM_STATE = 64
WINDOWS = (128, 512, 2048)
DILATIONS = (1, 4, 16)
WINDOW_STEPS = 128
N_BUCKETS = 32
MAX_EXACT = 16
BUCKET_MAX_DIST = 2048
N_EXPERT_GROUPS = 4
EXPERTS_PER_GROUP = 4
N_EXPERTS = N_EXPERT_GROUPS * EXPERTS_PER_GROUP
D_EXPERT = 512
NORM_EPS = 1e-6

LANES = 128
Q_ROWS = 16
OCTET = LANES // SSM_GROUP_CH
N_OCTETS = SSM_GROUPS // OCTET
OCT_STATE = OCTET * SSM_STATE
SSM_CHUNK = 16
SSM_SEQS = 2
ROW_TILE = 512
MOE_TILE = 512
MOE_TILE_DECODE = 128
ATTN_MXU_UNROLL = 8
VMEM_LIMIT = 56 * 1024 * 1024

_NEG_INF = float("-inf")
_HIGHEST = lax.Precision.HIGHEST


def _cparams(sem):
    return pltpu.CompilerParams(dimension_semantics=sem, vmem_limit_bytes=VMEM_LIMIT)


def _rmsnorm(x, g):
    return x * lax.rsqrt(jnp.mean(x * x, axis=-1, keepdims=True) + NORM_EPS) * g


def _gelu_tanh(x):
    c = math.sqrt(2.0 / math.pi)
    return 0.5 * x * (1.0 + jnp.tanh(c * (x + 0.044715 * (x * x * x))))


def _sigmoid(x):
    return 1.0 / (1.0 + jnp.exp(-x))


def _mod_kernel(c_ref, w_ref, b_ref, o_ref):
    c = c_ref[...]
    a = (c * _sigmoid(c)).astype(BF16)
    o_ref[...] = jnp.dot(a, w_ref[...].astype(BF16), preferred_element_type=F32) + b_ref[...]


def _mod_call(c_all, ada_w, ada_b):
    rows, d = c_all.shape
    n_out = ada_w.shape[1]
    tn = 1024
    return pl.pallas_call(
        _mod_kernel,
        grid=(n_out // tn,),
        in_specs=[pl.BlockSpec((rows, d), lambda j: (0, 0)),
                  pl.BlockSpec((d, tn), lambda j: (0, j)),
                  pl.BlockSpec((1, tn), lambda j: (0, j))],
        out_specs=pl.BlockSpec((rows, tn), lambda j: (0, j)),
        out_shape=jax.ShapeDtypeStruct((rows, n_out), F32),
        compiler_params=_cparams(("arbitrary",)),
        name="adaln_mod",
    )(c_all, ada_w, ada_b.reshape(1, n_out))


def _mod_spec(mod, rows, ts, chunk):
    if mod.shape[1] == 1:
        tiles_per_group = (rows // mod.shape[0]) // ts
        return pl.BlockSpec((1, 1, D_MODEL), lambda i: (i // tiles_per_group, 0, chunk))
    return pl.BlockSpec((1, ts, D_MODEL), lambda i: (0, i, chunk))


def _inproj_kernel(x_ref, sh_ref, sc_ref, g_ref, w_ref, *rest, key_major):
    h = _rmsnorm(x_ref[...], g_ref[...]) * (1.0 + sc_ref[0]) + sh_ref[0]
    hb = h.astype(BF16)
    z = jnp.dot(hb, w_ref[...], preferred_element_type=F32)
    aw = ATTN_WIDTH
    if key_major:
        q_ref, k_ref, v_ref, u_ref, kt_ref, vt_ref = rest
        kt_ref[0] = z[:, aw:2 * aw].T
        vt_ref[0] = z[:, 2 * aw:3 * aw].T
    else:
        q_ref, k_ref, v_ref, u_ref = rest
    q_ref[...] = z[:, :aw]
    k_ref[...] = z[:, aw:2 * aw]
    v_ref[...] = z[:, 2 * aw:3 * aw]
    u_ref[...] = z[:, 3 * aw:]


def _inproj_call(x_rows, mod, norm_g, w_in_bf, seq_len=None):
    rows, d = x_rows.shape
    ts = min(ROW_TILE, rows)
    proj = w_in_bf.shape[1]
    out = jax.ShapeDtypeStruct((rows, ATTN_WIDTH), F32)
    ospec = pl.BlockSpec((ts, ATTN_WIDTH), lambda i: (i, 0))
    in_specs = [pl.BlockSpec((ts, d), lambda i: (i, 0)),
                _mod_spec(mod, rows, ts, 0),
                _mod_spec(mod, rows, ts, 1),
                pl.BlockSpec((1, d), lambda i: (0, 0)),
                pl.BlockSpec((d, proj), lambda i: (0, 0))]
    args = [x_rows, mod, mod, norm_g.reshape(1, d), w_in_bf]
    out_specs = [ospec, ospec, ospec, ospec]
    out_shape = [out, out, out, out]
    if seq_len is not None:
        tiles = seq_len // ts
        tspec = pl.BlockSpec((1, ATTN_WIDTH, ts), lambda i: (i // tiles, 0, i % tiles))
        out_specs += [tspec, tspec]
        out_shape += [jax.ShapeDtypeStruct((rows // seq_len, ATTN_WIDTH, seq_len), F32)] * 2
    return pl.pallas_call(
        functools.partial(_inproj_kernel, key_major=seq_len is not None),
        grid=(rows // ts,),
        in_specs=in_specs,
        out_specs=out_specs,
        out_shape=out_shape,
        compiler_params=_cparams(("arbitrary",)),
        name="inproj",
    )(*args)


def _t5_bucket(dist):
    d = jnp.maximum(dist, MAX_EXACT).astype(F32)
    log_part = MAX_EXACT + (jnp.log(d / MAX_EXACT) / math.log(BUCKET_MAX_DIST / MAX_EXACT)
                            * (N_BUCKETS - MAX_EXACT)).astype(jnp.int32)
    return jnp.where(dist < MAX_EXACT, dist, jnp.minimum(log_part, N_BUCKETS - 1))


def _bias_by_distance(rel_bias, dists):
    hot = (_t5_bucket(jnp.asarray(dists, jnp.int32))[:, None]
           == jnp.arange(N_BUCKETS, dtype=jnp.int32)[None, :]).astype(F32)
    return jnp.dot(hot, rel_bias.astype(F32), precision=_HIGHEST)


def _prompt_bias_tiles(rel_bias):
    steps = WINDOW_STEPS
    period = 3 * steps
    tiles = []
    for r in DILATIONS:
        vec = _bias_by_distance(rel_bias, r * np.arange(steps + 1))
        fill = jnp.full((steps - 1, ATTN_HEADS), _NEG_INF, F32)
        w = jnp.concatenate([fill, vec[::-1], fill, fill[:1]], axis=0)
        rep = jnp.tile(w.T, (1, steps))[:, :steps * (period - 1)]
        toe = rep.reshape(ATTN_HEADS, steps, period - 1)[:, :, steps - 1:]
        toe = toe.reshape(ATTN_HEADS // 2, 2, steps, 2 * steps)
        own_only = jnp.where(jnp.arange(2 * steps) < steps, _NEG_INF, toe)
        tiles.append(jnp.stack([toe, own_only], axis=2))
    return jnp.stack(tiles)


def _attn_prompt_kernel(q_ref, k_ref, v_ref, bias_ref, o_ref,
                        p4_ref, qh_ref, kb_ref, vb_ref, s_ref, p_ref, res_ref, stage_ref, nat_ref):
    s_len = q_ref.shape[1]
    steps = WINDOW_STEPS
    n_tiles = s_len // steps
    quarter = s_len // 4
    nt = (((1,), (1,)), ((), ()))
    lane = lax.broadcasted_iota(jnp.int32, (1, LANES), 1)
    first_head = lane < HEAD_DIM
    srcs = (q_ref, k_ref, v_ref)

    kb_ref[0:steps, :] = jnp.zeros((steps, LANES), BF16)
    vb_ref[0:steps, 0:LANES] = jnp.zeros((steps, LANES), BF16)
    vb_ref[:, LANES:] = jnp.ones((s_len + steps, LANES), BF16)
    for x in range(3):
        for sigma in range(4):
            p4_ref[x, sigma * quarter:(sigma + 1) * quarter, :] = srcs[x][0, pl.ds(sigma, quarter, stride=4), :]

    def source(branch, x, tile_idx):
        rows = slice(tile_idx * steps, (tile_idx + 1) * steps)
        if branch == 0:
            return srcs[x][0, rows, :]
        if branch == 1:
            return p4_ref[x, rows, :]
        sigma, tau = tile_idx % 4, tile_idx // 4
        return p4_ref[x, pl.ds(sigma * quarter + tau, steps, stride=4), :]

    for branch, r in enumerate(DILATIONS):
        blocks_per_class = (s_len // r) // steps
        width = 2 * steps if blocks_per_class > 1 else steps

        for t in range(n_tiles):
            rows = slice(t * steps, (t + 1) * steps)
            q2 = source(branch, 0, t) * (HEAD_DIM ** -0.5)
            qh_ref[0, rows, :] = jnp.where(first_head, q2, 0.0).astype(BF16)
            qh_ref[1, rows, :] = jnp.where(first_head, 0.0, q2).astype(BF16)
            kb_ref[steps + t * steps:steps + (t + 1) * steps, :] = source(branch, 1, t).astype(BF16)
            vb_ref[steps + t * steps:steps + (t + 1) * steps, 0:LANES] = source(branch, 2, t).astype(BF16)

        def key_rows(t, width=width):
            start = t * steps if width == 2 * steps else (t + 1) * steps
            return pl.ds(pl.multiple_of(start, steps), width)

        def scores(t, carry, branch=branch, width=width, blocks_per_class=blocks_per_class,
                   key_rows=key_rows):
            rows = pl.ds(pl.multiple_of(t * steps, steps), steps)
            keys = kb_ref[key_rows(t), :]
            first = jnp.where(t % blocks_per_class == 0, 1, 0)
            for hh in range(2):
                if width == 2 * steps:
                    bias = bias_ref[branch, 0, hh, pl.ds(first, 1), :, :][0]
                else:
                    bias = bias_ref[branch, 0, hh, 0, :, steps:]
                sc = lax.dot_general(qh_ref[hh, rows, :], keys, nt, preferred_element_type=F32)
                s_ref[hh, rows, 0:width] = sc + bias
            return carry

        lax.fori_loop(0, n_tiles, scores, 0, unroll=ATTN_MXU_UNROLL)

        def softmax(i, carry, width=width):
            rows = pl.ds(pl.multiple_of(i * 2 * steps, 2 * steps), 2 * steps)
            for hh in range(2):
                sc = s_ref[hh, rows, 0:width]
                m = jnp.max(sc, axis=1, keepdims=True)
                p_ref[hh, rows, 0:width] = jnp.exp(sc - m).astype(BF16)
                res_ref[1, rows, hh * HEAD_DIM:(hh + 1) * HEAD_DIM] = jnp.broadcast_to(m, (2 * steps, HEAD_DIM))
            return carry

        lax.fori_loop(0, n_tiles // 2, softmax, 0, unroll=2)

        def weighted(t, carry, width=width, key_rows=key_rows):
            rows = pl.ds(pl.multiple_of(t * steps, steps), steps)
            vals = vb_ref[key_rows(t), :]
            r0 = jnp.dot(p_ref[0, rows, 0:width], vals, preferred_element_type=F32)
            r1 = jnp.dot(p_ref[1, rows, 0:width], vals, preferred_element_type=F32)
            res_ref[0, rows, :] = jnp.where(first_head, r0[:, :LANES], r1[:, :LANES])
            res_ref[2, rows, :] = jnp.where(first_head, r0[:, LANES:], r1[:, LANES:])
            return carry

        lax.fori_loop(0, n_tiles, weighted, 0, unroll=ATTN_MXU_UNROLL)

        for kind in range(3):
            if branch == 0:
                nat_ref[0, kind] = res_ref[kind]
                continue
            src = res_ref
            if branch == 2:
                for t in range(n_tiles):
                    sigma, tau = t % 4, t // 4
                    stage_ref[kind, pl.ds(sigma * quarter + tau, steps, stride=4), :] = (
                        res_ref[kind, t * steps:(t + 1) * steps, :])
                src = stage_ref
            for sigma in range(4):
                nat_ref[branch, kind, pl.ds(sigma, quarter, stride=4), :] = (
                    src[kind, sigma * quarter:(sigma + 1) * quarter, :])

    def merge(i, carry):
        rows = pl.ds(pl.multiple_of(i * 256, 256), 256)
        m0, m1, m2 = nat_ref[0, 1, rows, :], nat_ref[1, 1, rows, :], nat_ref[2, 1, rows, :]
        m_all = jnp.maximum(jnp.maximum(m0, m1), m2)
        w0, w1, w2 = jnp.exp(m0 - m_all), jnp.exp(m1 - m_all), jnp.exp(m2 - m_all)
        num = w0 * nat_ref[0, 0, rows, :] + w1 * nat_ref[1, 0, rows, :] + w2 * nat_ref[2, 0, rows, :]
        den = w0 * nat_ref[0, 2, rows, :] + w1 * nat_ref[1, 2, rows, :] + w2 * nat_ref[2, 2, rows, :]
        o_ref[0, rows, :] = num / den
        return carry

    lax.fori_loop(0, s_len // 256, merge, 0)


def _attn_prompt_call(q, k, v, bias_tiles):
    n, s, _ = q.shape
    pairs = ATTN_HEADS // 2
    steps = WINDOW_STEPS
    qspec = pl.BlockSpec((1, s, LANES), lambda n_, g: (n_, 0, g))
    return pl.pallas_call(
        _attn_prompt_kernel,
        grid=(n, pairs),
        in_specs=[qspec, qspec, qspec,
                  pl.BlockSpec((3, 1, 2, 2, steps, 2 * steps), lambda n_, g: (0, g, 0, 0, 0, 0))],
        out_specs=qspec,
        out_shape=jax.ShapeDtypeStruct((n, s, ATTN_WIDTH), F32),
        scratch_shapes=[pltpu.VMEM((3, s, LANES), F32),
                        pltpu.VMEM((2, s, LANES), BF16),
                        pltpu.VMEM((s + steps, LANES), BF16),
                        pltpu.VMEM((s + steps, 2 * LANES), BF16),
                        pltpu.VMEM((2, s, 2 * steps), F32),
                        pltpu.VMEM((2, s, 2 * steps), BF16),
                        pltpu.VMEM((3, s, LANES), F32), pltpu.VMEM((3, s, LANES), F32),
                        pltpu.VMEM((3, 3, s, LANES), F32)],
        compiler_params=_cparams(("arbitrary", "arbitrary")),
        name="attn_prompt",
    )(q, k, v, bias_tiles)


def _s5_discretise(a_re, a_im, log_dt, b_re, b_im):
    lam_re = jnp.minimum(a_re.astype(F32), -1e-4)
    lam_im = a_im.astype(F32)
    dt = jnp.exp(log_dt.astype(F32))[:, None]
    mag = jnp.exp(lam_re * dt)
    ph = lam_im * dt
    abar_re, abar_im = mag * jnp.cos(ph), mag * jnp.sin(ph)
    nr, ni = abar_re - 1.0, abar_im
    den = lam_re * lam_re + lam_im * lam_im
    coef_re = (nr * lam_re + ni * lam_im) / den
    coef_im = (ni * lam_re - nr * lam_im) / den
    br, bi = b_re.astype(F32), b_im.astype(F32)
    bbar_re = coef_re[..., None] * br - coef_im[..., None] * bi
    bbar_im = coef_re[..., None] * bi + coef_im[..., None] * br
    return lam_re * dt, ph, abar_re, abar_im, bbar_re, bbar_im


def _abar_power(log_mag, ph, n):
    nf = jnp.asarray(n, F32)[:, None, None]
    mag = jnp.exp(nf * log_mag[None])
    return mag * jnp.cos(nf * ph[None]), mag * jnp.sin(nf * ph[None])


def _s5_prompt_operators(a_re, a_im, log_dt, b_re, b_im, c_re, c_im):
    L = SSM_CHUNK
    log_mag, ph, _, _, bb_re, bb_im = _s5_discretise(a_re, a_im, log_dt, b_re, b_im)
    cr, ci = c_re.astype(F32), c_im.astype(F32)
    pw_re, pw_im = _abar_power(log_mag, ph, np.arange(L + 1))
    eye = jnp.eye(OCTET, dtype=F32)

    ab_re = pw_re[:L, :, :, None] * bb_re[None] - pw_im[:L, :, :, None] * bb_im[None]
    ab_im = pw_re[:L, :, :, None] * bb_im[None] + pw_im[:L, :, :, None] * bb_re[None]
    lag = (jnp.einsum('gop,lgpi->lgoi', cr, ab_re, precision=_HIGHEST)
           - jnp.einsum('gop,lgpi->lgoi', ci, ab_im, precision=_HIGHEST))
    lag = lag.reshape(L, N_OCTETS, OCTET, SSM_GROUP_CH, SSM_GROUP_CH)
    bd = jnp.einsum('logci,gh->olgihc', lag, eye).reshape(N_OCTETS, L, LANES, LANES).astype(BF16)
    stack = bd[:, ::-1].reshape(N_OCTETS, L * LANES, LANES)
    shifted = jnp.concatenate([stack[:, LANES:], jnp.zeros((N_OCTETS, LANES, LANES), BF16)], axis=1)
    t_op = jnp.concatenate([shifted, stack], axis=-1)

    m_parts = []
    for part in (ab_re[::-1], ab_im[::-1]):
        x = part.reshape(L, N_OCTETS, OCTET, SSM_STATE, SSM_GROUP_CH)
        m_parts.append(jnp.einsum('sogpi,gh->osgihp', x, eye).reshape(N_OCTETS, L * LANES, OCT_STATE))
    m_op = jnp.concatenate(m_parts, axis=-1)

    p1_re, p1_im = pw_re[1:], pw_im[1:]
    on_re = cr[None] * p1_re[:, :, None, :] - ci[None] * p1_im[:, :, None, :]
    on_im = -cr[None] * p1_im[:, :, None, :] - ci[None] * p1_re[:, :, None, :]
    p_parts = []
    for part in (on_re, on_im):
        x = part.reshape(L, N_OCTETS, OCTET, SSM_GROUP_CH, SSM_STATE)
        p_parts.append(jnp.einsum('togcp,gh->ogpthc', x, eye).reshape(N_OCTETS, OCT_STATE, L * LANES))
    p_op = jnp.concatenate(p_parts, axis=1)

    n_steps = 8
    sc_re, sc_im = _abar_power(log_mag, ph, L * (2 ** np.arange(n_steps)))
    sc = jnp.concatenate([sc_re.reshape(n_steps, N_OCTETS, OCT_STATE),
                          sc_im.reshape(n_steps, N_OCTETS, OCT_STATE)], axis=-1)
    sc = jnp.transpose(sc, (1, 0, 2))
    return t_op, m_op.astype(BF16), p_op.astype(BF16), sc


def _octet_glu(glu):
    eye = jnp.eye(OCTET, dtype=F32)
    x = glu.astype(F32).reshape(N_OCTETS, OCTET, SSM_GROUP_CH, SSM_GROUP_CH)
    return jnp.einsum('ogce,gh->ogche', x, eye).reshape(N_OCTETS, LANES, LANES).astype(BF16)


def _ssm_prompt_kernel(u_ref, t_ref, m_ref, p_ref, sc_ref, d_ref, ga_ref, gb_ref,
                       y_ref, h_ref, uf_ref, ub_ref, st_ref):
    L = SSM_CHUNK
    n_seq, s_len = u_ref.shape[0], u_ref.shape[1]
    n_chunks = s_len // L
    quarter = s_len // 4
    rows = n_seq * n_chunks

    def staged(sq, step):
        sigma, tau = step % 4, step // 4
        return pl.ds(sq * s_len + sigma * quarter + tau, n_chunks, stride=4)

    for sq in range(n_seq):
        for sigma in range(4):
            st_ref[sq * s_len + sigma * quarter:sq * s_len + (sigma + 1) * quarter, :] = (
                u_ref[sq, pl.ds(sigma, quarter, stride=4), :])
        for step in range(L):
            blk = st_ref[staged(sq, step), :]
            uf_ref[sq * n_chunks:(sq + 1) * n_chunks, step * LANES:(step + 1) * LANES] = blk
            ub_ref[sq * n_chunks:(sq + 1) * n_chunks, step * LANES:(step + 1) * LANES] = blk.astype(BF16)
    ub = ub_ref[...]

    x = jnp.dot(ub, m_ref[0], preferred_element_type=F32)
    chunk = lax.broadcasted_iota(jnp.int32, (rows, 1), 0) % n_chunks
    half = OCT_STATE
    k = 1
    step = 0
    while k < n_chunks:
        a_re = sc_ref[0, step:step + 1, :half]
        a_im = sc_ref[0, step:step + 1, half:]
        sh = jnp.where(chunk >= k, pltpu.roll(x, k, axis=0), 0.0)
        s_re, s_im = sh[:, :half], sh[:, half:]
        x = x + jnp.concatenate([a_re * s_re - a_im * s_im, a_re * s_im + a_im * s_re], axis=1)
        k *= 2
        step += 1
    for sq in range(n_seq):
        h_ref[sq, 0] = x[(sq + 1) * n_chunks - 1:(sq + 1) * n_chunks, :]
    h_start = jnp.where(chunk >= 1, pltpu.roll(x, 1, axis=0), 0.0)

    hb = h_start.astype(BF16)
    d = d_ref[0]
    ga = ga_ref[0]
    gb = gb_ref[0]
    for t in range(0, L, 2):
        pair = slice(t * LANES, (t + 2) * LANES)
        y2 = (jnp.dot(ub_ref[:, :(t + 2) * LANES], t_ref[0, (L - 2 - t) * LANES:, :],
                      preferred_element_type=F32)
              + jnp.dot(hb, p_ref[0, :, pair], preferred_element_type=F32))
        for j in range(2):
            lanes = slice((t + j) * LANES, (t + j + 1) * LANES)
            g = _gelu_tanh(y2[:, j * LANES:(j + 1) * LANES] + d * uf_ref[:, lanes]).astype(BF16)
            out = (jnp.dot(g, ga, preferred_element_type=F32)
                   * _sigmoid(jnp.dot(g, gb, preferred_element_type=F32)))
            for sq in range(n_seq):
                st_ref[staged(sq, t + j), :] = out[sq * n_chunks:(sq + 1) * n_chunks, :]
    for sq in range(n_seq):
        for sigma in range(4):
            y_ref[sq, pl.ds(sigma, quarter, stride=4), :] = (
                st_ref[sq * s_len + sigma * quarter:sq * s_len + (sigma + 1) * quarter, :])


def _ssm_prompt_call(u, t_op, m_op, p_op, sc, d_oct, ga, gb):
    n, s, _ = u.shape
    L = SSM_CHUNK
    nq = math.gcd(SSM_SEQS, n)
    rows = nq * (s // L)
    wide = L * LANES
    wspec = lambda shape: pl.BlockSpec((1,) + shape, lambda o, n_: (o, 0, 0))
    return pl.pallas_call(
        _ssm_prompt_kernel,
        grid=(N_OCTETS, n // nq),
        in_specs=[pl.BlockSpec((nq, s, LANES), lambda o, n_: (n_, 0, o)),
                  wspec((wide, 2 * LANES)), wspec((wide, 2 * OCT_STATE)), wspec((2 * OCT_STATE, wide)),
                  wspec((8, 2 * OCT_STATE)), wspec((1, LANES)),
                  wspec((LANES, LANES)), wspec((LANES, LANES))],
        out_specs=[pl.BlockSpec((nq, s, LANES), lambda o, n_: (n_, 0, o)),
                   pl.BlockSpec((nq, 1, 1, 2 * OCT_STATE), lambda o, n_: (n_, o, 0, 0))],
        out_shape=[jax.ShapeDtypeStruct((n, s, SSM_WIDTH), F32),
                   jax.ShapeDtypeStruct((n, N_OCTETS, 1, 2 * OCT_STATE), F32)],
        scratch_shapes=[pltpu.VMEM((rows, wide), F32),
                        pltpu.VMEM((rows, wide), BF16),
                        pltpu.VMEM((nq * s, LANES), F32)],
        compiler_params=_cparams(("arbitrary", "arbitrary")),
        name="ssm_prompt",
    )(u, t_op, m_op, p_op, sc, d_oct, ga, gb)


def _attn_decode_kernel(q_ref, kn_ref, vn_ref, kt_ref, vt_ref, b_ref, mult_ref, o_ref):
    nt = (((1,), (1,)), ((), ()))
    mult = mult_ref[...]
    for h in range(ATTN_HEADS):
        q = q_ref[0, h]
        kt = jnp.concatenate([kt_ref[0, h].astype(BF16), kn_ref[0, h]], axis=1)
        vt = jnp.concatenate([vt_ref[0, h].astype(BF16), vn_ref[0, h]], axis=1)
        s = jnp.dot(q, kt, preferred_element_type=F32) + b_ref[h]
        m = jnp.max(s, axis=1, keepdims=True)
        p = jnp.exp(s - m) * mult
        den = jnp.sum(p, axis=1, keepdims=True)
        o = lax.dot_general(p.astype(BF16), vt, nt, preferred_element_type=F32)
        o_ref[0, h] = o / den


def _decode_tables(rel_bias, t_len, w_rows):
    t = np.arange(t_len)[:, None]
    dist = np.concatenate([w_rows + t - np.arange(w_rows)[None, :],
                           t - np.arange(LANES)[None, :]], axis=1)
    mult = np.zeros(dist.shape, np.float32)
    for w, r in zip(WINDOWS, DILATIONS):
        mult += (dist >= 0) & (dist % r == 0) & (dist <= w)
    mult = np.concatenate([mult, np.zeros((Q_ROWS - t_len, dist.shape[1]), np.float32)], axis=0)
    mult[t_len:, 0] = 1.0
    by_dist = _bias_by_distance(rel_bias, np.arange(w_rows + t_len))
    rows = []
    for ti in range(t_len):
        cache_part = by_dist[ti + 1:w_rows + ti + 1][::-1]
        new_part = by_dist[:ti + 1][::-1]
        pad = jnp.zeros((LANES - ti - 1, ATTN_HEADS), F32)
        rows.append(jnp.concatenate([cache_part, new_part, pad], axis=0))
    bias = jnp.stack(rows + [jnp.zeros_like(rows[0])] * (Q_ROWS - t_len), axis=0)
    bias = jnp.transpose(bias, (2, 0, 1))
    bias = jnp.where(jnp.asarray(mult)[None] > 0, bias, _NEG_INF)
    return bias, jnp.asarray(mult)


def _attn_decode_call(q, k_new, v_new, cache_k, cache_v, rel_bias):
    n, t_len, w = q.shape
    w_rows = cache_k.shape[1]
    if t_len > min(DILATIONS[1:]) or t_len > Q_ROWS or w_rows < max(WINDOWS):
        raise ValueError("unsupported decode shape")
    heads = (ATTN_HEADS, HEAD_DIM)

    def head_major(a, pad_to):
        a = jnp.transpose(a.reshape((n, t_len) + heads), (0, 2, 1, 3))
        return jnp.pad(a, ((0, 0), (0, 0), (0, pad_to - t_len), (0, 0)))

    qh = head_major(q * (HEAD_DIM ** -0.5), Q_ROWS).astype(BF16)
    knt = jnp.swapaxes(head_major(k_new, LANES), 2, 3).astype(BF16)
    vnt = jnp.swapaxes(head_major(v_new, LANES), 2, 3).astype(BF16)
    kt = jnp.transpose(cache_k.astype(F32), (0, 2, 3, 1))
    vt = jnp.transpose(cache_v.astype(F32), (0, 2, 3, 1))
    bias, mult = _decode_tables(rel_bias, t_len, w_rows)
    keys = w_rows + LANES
    per_seq = lambda shape: pl.BlockSpec((1,) + shape, lambda i: (i, 0, 0, 0))
    out = pl.pallas_call(
        _attn_decode_kernel,
        grid=(n,),
        in_specs=[per_seq((ATTN_HEADS, Q_ROWS, HEAD_DIM)),
                  per_seq((ATTN_HEADS, HEAD_DIM, LANES)), per_seq((ATTN_HEADS, HEAD_DIM, LANES)),
                  per_seq((ATTN_HEADS, HEAD_DIM, w_rows)), per_seq((ATTN_HEADS, HEAD_DIM, w_rows)),
                  pl.BlockSpec((ATTN_HEADS, Q_ROWS, keys), lambda i: (0, 0, 0)),
                  pl.BlockSpec((Q_ROWS, keys), lambda i: (0, 0))],
        out_specs=per_seq((ATTN_HEADS, Q_ROWS, HEAD_DIM)),
        out_shape=jax.ShapeDtypeStruct((n, ATTN_HEADS, Q_ROWS, HEAD_DIM), F32),
        compiler_params=_cparams(("arbitrary",)),
        name="attn_decode",
    )(qh, knt, vnt, kt, vt, bias, mult)
    return jnp.transpose(out[:, :, :t_len], (0, 2, 1, 3)).reshape(n, t_len, w)


def _ssm_decode_kernel(u_ref, hre_ref, him_ref, are_ref, aim_ref, bre_ref, bim_ref,
                       cre_ref, cim_ref, d_ref, ga_ref, gb_ref, y_ref, ore_ref, oim_ref, *, t_len):
    h_re, h_im = hre_ref[...], him_ref[...]
    a_re, a_im = are_ref[...], aim_ref[...]
    for t in range(t_len):
        u = u_ref[t]
        ub = u.astype(BF16)
        n_re = a_re * h_re - a_im * h_im + jnp.dot(ub, bre_ref[...], preferred_element_type=F32)
        n_im = a_re * h_im + a_im * h_re + jnp.dot(ub, bim_ref[...], preferred_element_type=F32)
        h_re, h_im = n_re, n_im
        y = (jnp.dot(h_re.astype(BF16), cre_ref[...], preferred_element_type=F32)
             - jnp.dot(h_im.astype(BF16), cim_ref[...], preferred_element_type=F32)
             + d_ref[...] * u)
        g = _gelu_tanh(y).astype(BF16)
        y_ref[t] = (jnp.dot(g, ga_ref[...], preferred_element_type=F32)
                    * _sigmoid(jnp.dot(g, gb_ref[...], preferred_element_type=F32)))
    ore_ref[...] = h_re
    oim_ref[...] = h_im


def _group_blockdiag(x):
    g, a, b = x.shape
    return jnp.einsum('gab,gh->gahb', x, jnp.eye(g, dtype=x.dtype)).reshape(g * a, g * b)


def _ssm_decode_call(u_tm, h0_re, h0_im, a_re, a_im, log_dt, b_re, b_im, c_re, c_im,
                     d_skip, glu_a, glu_b):
    n = h0_re.shape[0]
    _, _, abar_re, abar_im, bb_re, bb_im = _s5_discretise(a_re, a_im, log_dt, b_re, b_im)
    state = SSM_GROUPS * SSM_STATE
    t_len = u_tm.shape[0]
    args = (u_tm, h0_re.reshape(n, state).astype(F32), h0_im.reshape(n, state).astype(F32),
            abar_re.reshape(1, state), abar_im.reshape(1, state),
            _group_blockdiag(jnp.transpose(bb_re, (0, 2, 1))).astype(BF16),
            _group_blockdiag(jnp.transpose(bb_im, (0, 2, 1))).astype(BF16),
            _group_blockdiag(jnp.transpose(c_re.astype(F32), (0, 2, 1))).astype(BF16),
            _group_blockdiag(jnp.transpose(c_im.astype(F32), (0, 2, 1))).astype(BF16),
            d_skip.astype(F32).reshape(1, SSM_WIDTH),
            _group_blockdiag(glu_a.astype(F32)).astype(BF16),
            _group_blockdiag(glu_b.astype(F32)).astype(BF16))
    full = lambda a: pl.BlockSpec(a.shape, lambda i: (0,) * a.ndim)
    out_shape = [jax.ShapeDtypeStruct(u_tm.shape, F32),
                 jax.ShapeDtypeStruct((n, state), F32), jax.ShapeDtypeStruct((n, state), F32)]
    return pl.pallas_call(
        functools.partial(_ssm_decode_kernel, t_len=t_len),
        grid=(1,),
        in_specs=[full(a) for a in args],
        out_specs=[full(o) for o in out_shape],
        out_shape=out_shape,
        compiler_params=_cparams(("arbitrary",)),
        name="ssm_decode",
    )(*args)


def _outproj_kernel(att_ref, ssm_ref, x_ref, g1_ref, sh2_ref, sc2_ref, ag_ref, sg_ref, n2_ref,
                    wo_ref, wr_ref, rb_ref, x1_ref, h2_ref, route_ref):
    mixed = jnp.concatenate([_rmsnorm(att_ref[...], ag_ref[...]), _rmsnorm(ssm_ref[...], sg_ref[...])],
                            axis=1).astype(BF16)
    x1 = x_ref[...] + g1_ref[0] * jnp.dot(mixed, wo_ref[...], preferred_element_type=F32)
    x1_ref[...] = x1
    h2 = _rmsnorm(x1, n2_ref[...]) * (1.0 + sc2_ref[0]) + sh2_ref[0]
    hi = h2.astype(BF16)
    h2_ref[...] = hi
    lo = (h2 - hi.astype(F32)).astype(BF16)
    r1 = jnp.dot(hi, wr_ref[...], preferred_element_type=F32)
    r2 = jnp.dot(lo, wr_ref[:, :LANES], preferred_element_type=F32)
    logits = r1[:, :LANES] + r1[:, LANES:] + r2 + rb_ref[...]

    lane = lax.broadcasted_iota(jnp.int32, (1, LANES), 1)
    lane_f = lane.astype(F32)
    big = float(LANES)
    ng, epg = N_EXPERT_GROUPS, EXPERTS_PER_GROUP
    lg = jnp.where(lane < ng, logits, _NEG_INF)
    gmax = jnp.max(lg, axis=1, keepdims=True)
    p_star = 1.0 / jnp.sum(jnp.exp(lg - gmax), axis=1, keepdims=True)
    g_star = jnp.min(jnp.where(lg == gmax, lane_f, big), axis=1, keepdims=True)
    in_group = ((lane >= ng) & (lane < ng + ng * epg)
                & (lax.shift_right_arithmetic(lane - ng, int(math.log2(epg))).astype(F32) == g_star))
    le = jnp.where(in_group, logits, _NEG_INF)
    v1 = jnp.max(le, axis=1, keepdims=True)
    i1 = jnp.min(jnp.where(le == v1, lane_f, big), axis=1, keepdims=True)
    le2 = jnp.where(lane_f == i1, _NEG_INF, le)
    v2 = jnp.max(le2, axis=1, keepdims=True)
    i2 = jnp.min(jnp.where(le2 == v2, lane_f, big), axis=1, keepdims=True)
    e2 = jnp.exp(v2 - v1)
    w1 = p_star / (1.0 + e2)
    w2 = p_star * e2 / (1.0 + e2)
    route_ref[...] = jnp.where(lane == 0, i1 - ng,
                               jnp.where(lane == 1, i2 - ng,
                                         jnp.where(lane == 2, w1, jnp.where(lane == 3, w2, 0.0))))


def _outproj_call(att, ssm_y, x_rows, mod, attn_g, ssm_g, norm2_g, w_out_bf, wr, rb):
    rows, d = x_rows.shape
    ts = min(ROW_TILE, rows)
    row = lambda width: pl.BlockSpec((ts, width), lambda i: (i, 0))
    const = lambda a: pl.BlockSpec(a.shape, lambda i: (0,) * a.ndim)
    attn_g = attn_g.reshape(1, ATTN_WIDTH)
    ssm_g = ssm_g.reshape(1, SSM_WIDTH)
    norm2_g = norm2_g.reshape(1, d)
    return pl.pallas_call(
        _outproj_kernel,
        grid=(rows // ts,),
        in_specs=[row(ATTN_WIDTH), row(SSM_WIDTH), row(d),
                  _mod_spec(mod, rows, ts, 2), _mod_spec(mod, rows, ts, 3), _mod_spec(mod, rows, ts, 4),
                  const(attn_g), const(ssm_g), const(norm2_g), const(w_out_bf), const(wr), const(rb)],
        out_specs=[row(d), row(d), row(LANES)],
        out_shape=[jax.ShapeDtypeStruct((rows, d), F32), jax.ShapeDtypeStruct((rows, d), BF16),
                   jax.ShapeDtypeStruct((rows, LANES), F32)],
        compiler_params=_cparams(("arbitrary",)),
        name="outproj_router",
    )(att, ssm_y, x_rows, mod, mod, mod, attn_g, ssm_g, norm2_g, w_out_bf, wr, rb)


def _router_weights(router_g_w, router_g_b, router_e_w, router_e_b):
    d = router_g_w.shape[0]
    ne = N_EXPERT_GROUPS * EXPERTS_PER_GROUP
    w = jnp.concatenate([router_g_w.astype(F32),
                         jnp.transpose(router_e_w.astype(F32), (1, 0, 2)).reshape(d, ne)], axis=1)
    w = jnp.pad(w, ((0, 0), (0, LANES - w.shape[1])))
    hi = w.astype(BF16)
    lo = (w - hi.astype(F32)).astype(BF16)
    b = jnp.concatenate([router_g_b.astype(F32), router_e_b.astype(F32).reshape(ne)])
    b = jnp.pad(b, (0, LANES - b.shape[0])).reshape(1, LANES)
    return jnp.concatenate([hi, lo], axis=1), b


def _moe_kernel(te_ref, x_ref, wg_ref, wu_ref, wd_ref, o_ref, wgb_ref, wub_ref, wdb_ref):
    i = pl.program_id(0)
    changed = jnp.logical_or(i == 0, te_ref[i] != te_ref[jnp.maximum(i - 1, 0)])

    @pl.when(changed)
    def _():
        wgb_ref[...] = wg_ref[0].astype(BF16)
        wub_ref[...] = wu_ref[0].astype(BF16)
        wdb_ref[...] = wd_ref[0].astype(BF16)

    x = x_ref[...]
    gate = jnp.dot(x, wgb_ref[...], preferred_element_type=F32)
    up = jnp.dot(x, wub_ref[...], preferred_element_type=F32)
    a = (gate * _sigmoid(gate)) * up
    o_ref[...] = jnp.dot(a.astype(BF16), wdb_ref[...], preferred_element_type=F32).astype(o_ref.dtype)


def _moe_call(tile_expert, x_sorted, wg, wu, wd, tm):
    n_slots, d = x_sorted.shape
    fe = wg.shape[2]
    grid_spec = pltpu.PrefetchScalarGridSpec(
        num_scalar_prefetch=1,
        grid=(n_slots // tm,),
        in_specs=[pl.BlockSpec((tm, d), lambda i, te: (i, 0)),
                  pl.BlockSpec((1, d, fe), lambda i, te: (te[i], 0, 0)),
                  pl.BlockSpec((1, d, fe), lambda i, te: (te[i], 0, 0)),
                  pl.BlockSpec((1, fe, d), lambda i, te: (te[i], 0, 0))],
        out_specs=pl.BlockSpec((tm, d), lambda i, te: (i, 0)),
        scratch_shapes=[pltpu.VMEM((d, fe), BF16), pltpu.VMEM((d, fe), BF16), pltpu.VMEM((fe, d), BF16)],
    )
    return pl.pallas_call(
        _moe_kernel,
        grid_spec=grid_spec,
        out_shape=jax.ShapeDtypeStruct((n_slots, d), BF16),
        compiler_params=_cparams(("arbitrary",)),
        name="moe_experts",
    )(tile_expert, x_sorted, wg, wu, wd)


def _moe_dispatch(route, tm):
    n_tok = route.shape[0]
    ids = route[:, :2].astype(jnp.int32).reshape(-1)
    n_pairs = ids.shape[0]
    n_slots = (-(-n_pairs // tm) + N_EXPERTS) * tm
    hot = (ids[:, None] == jnp.arange(N_EXPERTS)[None, :]).astype(jnp.int32)
    csum = jnp.cumsum(hot, axis=0)
    rank = jnp.sum((csum - hot) * hot, axis=1)
    counts = csum[-1]
    padded = -(-counts // tm) * tm
    ends = jnp.cumsum(padded)
    starts = ends - padded
    pos = jnp.sum(hot * starts[None, :], axis=1) + rank
    tok_of_slot = (jnp.arange(n_slots, dtype=jnp.int32) % n_tok).at[pos].set(
        jnp.arange(n_pairs, dtype=jnp.int32) // 2, unique_indices=True, mode="promise_in_bounds")
    tile_start = jnp.arange(n_slots // tm, dtype=jnp.int32) * tm
    tile_expert = jnp.minimum(jnp.sum((tile_start[:, None] >= ends[None, :]).astype(jnp.int32), axis=1),
                              N_EXPERTS - 1).astype(jnp.int32)
    return pos.reshape(n_tok, 2), tok_of_slot, tile_expert


def _take_rows(x, idx):
    return x.at[idx].get(mode="promise_in_bounds")


def _final_kernel(x1_ref, ya_ref, yb_ref, route_ref, g2_ref, fg_ref, o_ref):
    wa = route_ref[:, 2:3]
    wb = route_ref[:, 3:4]
    x = x1_ref[...] + g2_ref[0] * (wa * ya_ref[...].astype(F32) + wb * yb_ref[...].astype(F32))
    o_ref[...] = _rmsnorm(x, fg_ref[...])


def _final_call(x1, ya, yb, route, mod, final_g):
    rows, d = x1.shape
    ts = min(ROW_TILE, rows)
    row = pl.BlockSpec((ts, d), lambda i: (i, 0))
    return pl.pallas_call(
        _final_kernel,
        grid=(rows // ts,),
        in_specs=[row, row, row, pl.BlockSpec((ts, LANES), lambda i: (i, 0)), _mod_spec(mod, rows, ts, 5),
                  pl.BlockSpec((1, d), lambda i: (0, 0))],
        out_specs=row,
        out_shape=jax.ShapeDtypeStruct((rows, d), F32),
        compiler_params=_cparams(("arbitrary",)),
        name="final_norm",
    )(x1, ya, yb, route, mod, final_g.reshape(1, d))


def kernel(x_prompt, x_sample, c_prompt, c_sample, cache_k, cache_v, state_ssm_re, state_ssm_im,
           rel_bias, ada_w, ada_b, norm1_g, w_in, ssm_a_re, ssm_a_im, ssm_log_dt, ssm_b_re, ssm_b_im,
           ssm_c_re, ssm_c_im, ssm_d, glu_a, glu_b, attn_out_g, ssm_out_g, w_out, norm2_g,
           router_g_w, router_g_b, router_e_w, router_e_b, w_gate, w_up, w_down, final_norm_g):
    if ada_w.shape[0] != 1:
        raise ValueError("single-layer trunk expected")
    nb, s_len, d = x_prompt.shape
    nd, t_len, _ = x_sample.shape
    if s_len != max(WINDOWS):
        raise ValueError("prompt length must equal the widest window")
    n_p, n_s = nb * s_len, nd * t_len
    if n_p % ROW_TILE or n_s % ROW_TILE:
        raise ValueError("token counts must be multiples of the row tile")

    mod = _mod_call(jnp.concatenate([c_prompt, c_sample], axis=0).astype(F32), ada_w[0], ada_b[0])
    mod_p = mod[:nb].reshape(nb, 1, 6 * d)
    mod_s = jnp.repeat(mod[nb:], t_len, axis=0).reshape(1, n_s, 6 * d)

    w_in_bf = w_in[0].astype(BF16)
    w_out_bf = w_out[0].astype(BF16)
    xp_rows = x_prompt.reshape(n_p, d)
    xs_rows = x_sample.reshape(n_s, d)

    qp, kp, vp, up, kp_t, vp_t = _inproj_call(xp_rows, mod_p, norm1_g[0], w_in_bf, seq_len=s_len)
    qs, ks, vs, us = _inproj_call(xs_rows, mod_s, norm1_g[0], w_in_bf)

    seq = lambda a: a.reshape(nb, s_len, ATTN_WIDTH)
    att_p = _attn_prompt_call(seq(qp), seq(kp), seq(vp), _prompt_bias_tiles(rel_bias))
    dec = lambda a: a.reshape(nd, t_len, ATTN_WIDTH)
    att_s = _attn_decode_call(dec(qs), dec(ks), dec(vs), cache_k[0], cache_v[0], rel_bias)

    s5 = (ssm_a_re[0], ssm_a_im[0], ssm_log_dt[0], ssm_b_re[0], ssm_b_im[0], ssm_c_re[0], ssm_c_im[0])
    t_op, m_op, p_op, sc = _s5_prompt_operators(*s5)
    d_oct = ssm_d[0].astype(F32).reshape(N_OCTETS, 1, LANES)
    ssm_p, hT_p = _ssm_prompt_call(seq(up), t_op, m_op, p_op, sc, d_oct,
                                   _octet_glu(glu_a[0]), _octet_glu(glu_b[0]))
    hT_p = hT_p.reshape(nb, N_OCTETS, 2, OCTET, SSM_STATE)
    ssm_re_p = hT_p[:, :, 0].reshape(nb, SSM_GROUPS, SSM_STATE)
    ssm_im_p = hT_p[:, :, 1].reshape(nb, SSM_GROUPS, SSM_STATE)
    us_tm = jnp.transpose(us.reshape(nd, t_len, SSM_WIDTH), (1, 0, 2))
    ssm_s, hre_s, him_s = _ssm_decode_call(us_tm, state_ssm_re[0], state_ssm_im[0], *s5,
                                           ssm_d[0], glu_a[0], glu_b[0])
    ssm_s = jnp.transpose(ssm_s, (1, 0, 2)).reshape(n_s, SSM_WIDTH)

    wr, rb = _router_weights(router_g_w[0], router_g_b[0], router_e_w[0], router_e_b[0])
    norms = (attn_out_g[0], ssm_out_g[0], norm2_g[0], w_out_bf, wr, rb)
    x1_p, h2_p, route_p = _outproj_call(att_p.reshape(n_p, ATTN_WIDTH), ssm_p.reshape(n_p, SSM_WIDTH),
                                        xp_rows, mod_p, *norms)
    x1_s, h2_s, route_s = _outproj_call(att_s.reshape(n_s, ATTN_WIDTH), ssm_s, xs_rows, mod_s, *norms)

    ne = N_EXPERTS
    wg = w_gate[0].reshape(ne, d, D_EXPERT)
    wu = w_up[0].reshape(ne, d, D_EXPERT)
    wd = w_down[0].reshape(ne, D_EXPERT, d)

    def experts(x1, h2, route, mod_rows, tm):
        pos, tok_of_slot, tile_expert = _moe_dispatch(route, tm)
        y_slots = _moe_call(tile_expert, _take_rows(h2, tok_of_slot), wg, wu, wd, tm)
        return _final_call(x1, _take_rows(y_slots, pos[:, 0]), _take_rows(y_slots, pos[:, 1]),
                           route, mod_rows, final_norm_g)

    y_p = experts(x1_p, h2_p, route_p, mod_p, MOE_TILE)
    y_s = experts(x1_s, h2_s, route_s, mod_s, MOE_TILE_DECODE)

    heads = (ATTN_HEADS, HEAD_DIM)
    cache_out = lambda a: jnp.transpose(a.reshape((1, nb) + heads + (s_len,)), (0, 1, 4, 2, 3))
    return (y_p.reshape(nb, s_len, d), y_s.reshape(nd, t_len, d),
            cache_out(kp_t), cache_out(vp_t),
            ks.reshape((1, nd, t_len) + heads), vs.reshape((1, nd, t_len) + heads),
            ssm_re_p[None], ssm_im_p[None],
            hre_s.reshape(1, nd, SSM_GROUPS, SSM_STATE), him_s.reshape(1, nd, SSM_GROUPS, SSM_STATE))
```

```python
import functools
import math

import numpy as np

import jax
import jax.numpy as jnp
from jax import lax
from jax.experimental import pallas as pl
from jax.experimental.pallas import tpu as pltpu

F32 = jnp.float32
BF16 = jnp.bfloat16

D_MODEL = 1024
HEAD_DIM = 64
ATTN_WIDTH = 512
ATTN_HEADS = 8
SSM_WIDTH = 512
SSM_GROUP_CH = 16
SSM_GROUPS = 32
SSM_STATE = 64
WINDOWS = (128, 512, 2048)
DILATIONS = (1, 4, 16)
WINDOW_STEPS = 128
N_BUCKETS = 32
MAX_EXACT = 16
BUCKET_MAX_DIST = 2048
N_EXPERT_GROUPS = 4
EXPERTS_PER_GROUP = 4
N_EXPERTS = N_EXPERT_GROUPS * EXPERTS_PER_GROUP
D_EXPERT = 512
NORM_EPS = 1e-6

LANES = 128
Q_ROWS = 16
OCTET = LANES // SSM_GROUP_CH
N_OCTETS = SSM_GROUPS // OCTET
OCT_STATE = OCTET * SSM_STATE
SSM_CHUNK = 16
SSM_SEQS = 2
ROW_TILE = 512
MOE_TILE = 512
MOE_TILE_DECODE = 128
ATTN_GROUP = 4
ATTN_MXU_UNROLL = 8
VMEM_LIMIT = 56 * 1024 * 1024

_NEG_INF = float("-inf")
_HIGHEST = lax.Precision.HIGHEST


def _cparams(sem):
    return pltpu.CompilerParams(dimension_semantics=sem, vmem_limit_bytes=VMEM_LIMIT)


def _rmsnorm(x, g):
    return x * lax.rsqrt(jnp.mean(x * x, axis=-1, keepdims=True) + NORM_EPS) * g


def _gelu_tanh(x):
    c = math.sqrt(2.0 / math.pi)
    return 0.5 * x * (1.0 + jnp.tanh(c * (x + 0.044715 * (x * x * x))))


def _sigmoid(x):
    return 1.0 / (1.0 + jnp.exp(-x))


def _mod_kernel(c_ref, w_ref, b_ref, o_ref):
    c = c_ref[...]
    a = (c * _sigmoid(c)).astype(BF16)
    o_ref[...] = jnp.dot(a, w_ref[...].astype(BF16), preferred_element_type=F32) + b_ref[...]


def _mod_call(c_all, ada_w, ada_b):
    rows, d = c_all.shape
    n_out = ada_w.shape[1]
    tn = 1024
    return pl.pallas_call(
        _mod_kernel,
        grid=(n_out // tn,),
        in_specs=[pl.BlockSpec((rows, d), lambda j: (0, 0)),
                  pl.BlockSpec((d, tn), lambda j: (0, j)),
                  pl.BlockSpec((1, tn), lambda j: (0, j))],
        out_specs=pl.BlockSpec((rows, tn), lambda j: (0, j)),
        out_shape=jax.ShapeDtypeStruct((rows, n_out), F32),
        compiler_params=_cparams(("arbitrary",)),
        name="adaln_mod",
    )(c_all, ada_w, ada_b.reshape(1, n_out))


def _mod_spec(mod, rows, ts, chunk):
    if mod.shape[1] == 1:
        tiles_per_group = (rows // mod.shape[0]) // ts
        return pl.BlockSpec((1, 1, D_MODEL), lambda i: (i // tiles_per_group, 0, chunk))
    return pl.BlockSpec((1, ts, D_MODEL), lambda i: (0, i, chunk))


def _inproj_kernel(x_ref, sh_ref, sc_ref, g_ref, w_ref, *rest, key_major):
    h = _rmsnorm(x_ref[...], g_ref[...]) * (1.0 + sc_ref[0]) + sh_ref[0]
    hb = h.astype(BF16)
    z = jnp.dot(hb, w_ref[...], preferred_element_type=F32)
    aw = ATTN_WIDTH
    if key_major:
        q_ref, k_ref, v_ref, u_ref, kt_ref, vt_ref = rest
        kt_ref[0] = z[:, aw:2 * aw].T
        vt_ref[0] = z[:, 2 * aw:3 * aw].T
    else:
        q_ref, k_ref, v_ref, u_ref = rest
    q_ref[...] = z[:, :aw]
    k_ref[...] = z[:, aw:2 * aw]
    v_ref[...] = z[:, 2 * aw:3 * aw]
    u_ref[...] = z[:, 3 * aw:]


def _inproj_call(x_rows, mod, norm_g, w_in_bf, seq_len=None):
    rows, d = x_rows.shape
    ts = min(ROW_TILE, rows)
    proj = w_in_bf.shape[1]
    out = jax.ShapeDtypeStruct((rows, ATTN_WIDTH), F32)
    ospec = pl.BlockSpec((ts, ATTN_WIDTH), lambda i: (i, 0))
    in_specs = [pl.BlockSpec((ts, d), lambda i: (i, 0)),
                _mod_spec(mod, rows, ts, 0),
                _mod_spec(mod, rows, ts, 1),
                pl.BlockSpec((1, d), lambda i: (0, 0)),
                pl.BlockSpec((d, proj), lambda i: (0, 0))]
    args = [x_rows, mod, mod, norm_g.reshape(1, d), w_in_bf]
    out_specs = [ospec, ospec, ospec, ospec]
    out_shape = [out, out, out, out]
    if seq_len is not None:
        tiles = seq_len // ts
        tspec = pl.BlockSpec((1, ATTN_WIDTH, ts), lambda i: (i // tiles, 0, i % tiles))
        out_specs += [tspec, tspec]
        out_shape += [jax.ShapeDtypeStruct((rows // seq_len, ATTN_WIDTH, seq_len), F32)] * 2
    return pl.pallas_call(
        functools.partial(_inproj_kernel, key_major=seq_len is not None),
        grid=(rows // ts,),
        in_specs=in_specs,
        out_specs=out_specs,
        out_shape=out_shape,
        compiler_params=_cparams(("arbitrary",)),
        name="inproj",
    )(*args)


def _t5_bucket(dist):
    d = jnp.maximum(dist, MAX_EXACT).astype(F32)
    log_part = MAX_EXACT + (jnp.log(d / MAX_EXACT) / math.log(BUCKET_MAX_DIST / MAX_EXACT)
                            * (N_BUCKETS - MAX_EXACT)).astype(jnp.int32)
    return jnp.where(dist < MAX_EXACT, dist, jnp.minimum(log_part, N_BUCKETS - 1))


def _bias_by_distance(rel_bias, dists):
    hot = (_t5_bucket(jnp.asarray(dists, jnp.int32))[:, None]
           == jnp.arange(N_BUCKETS, dtype=jnp.int32)[None, :]).astype(F32)
    return jnp.dot(hot, rel_bias.astype(F32), precision=_HIGHEST)


def _prompt_bias_tiles(rel_bias):
    steps = WINDOW_STEPS
    period = 3 * steps
    tiles = []
    for r in DILATIONS:
        vec = _bias_by_distance(rel_bias, r * np.arange(steps + 1))
        fill = jnp.full((steps - 1, ATTN_HEADS), _NEG_INF, F32)
        w = jnp.concatenate([fill, vec[::-1], fill, fill[:1]], axis=0)
        rep = jnp.tile(w.T, (1, steps))[:, :steps * (period - 1)]
        toe = rep.reshape(ATTN_HEADS, steps, period - 1)[:, :, steps - 1:]
        toe = toe.reshape(ATTN_HEADS // 2, 2, steps, 2 * steps)
        own_only = jnp.where(jnp.arange(2 * steps) < steps, _NEG_INF, toe)
        tiles.append(jnp.stack([toe, own_only], axis=2))
    return jnp.stack(tiles)


def _attn_prompt_kernel(q_ref, k_ref, v_ref, bias_ref, o_ref,
                        p4_ref, qh_ref, kb_ref, vb_ref, s_ref, p_ref, res_ref, stage_ref, nat_ref):
    s_len = q_ref.shape[1]
    steps = WINDOW_STEPS
    n_tiles = s_len // steps
    quarter = s_len // 4
    nt = (((1,), (1,)), ((), ()))
    lane = lax.broadcasted_iota(jnp.int32, (1, LANES), 1)
    first_head = lane < HEAD_DIM
    srcs = (q_ref, k_ref, v_ref)

    kb_ref[0:steps, :] = jnp.zeros((steps, LANES), BF16)
    vb_ref[0:steps, 0:LANES] = jnp.zeros((steps, LANES), BF16)
    vb_ref[:, LANES:] = jnp.ones((s_len + steps, LANES), BF16)
    for x in range(3):
        for sigma in range(4):
            p4_ref[x, sigma * quarter:(sigma + 1) * quarter, :] = srcs[x][0, pl.ds(sigma, quarter, stride=4), :]

    def source(branch, x, tile_idx):
        rows = slice(tile_idx * steps, (tile_idx + 1) * steps)
        if branch == 0:
            return srcs[x][0, rows, :]
        if branch == 1:
            return p4_ref[x, rows, :]
        sigma, tau = tile_idx % 4, tile_idx // 4
        return p4_ref[x, pl.ds(sigma * quarter + tau, steps, stride=4), :]

    for branch, r in enumerate(DILATIONS):
        blocks_per_class = (s_len // r) // steps
        width = 2 * steps if blocks_per_class > 1 else steps

        for t in range(n_tiles):
            rows = slice(t * steps, (t + 1) * steps)
            q2 = source(branch, 0, t) * (HEAD_DIM ** -0.5)
            qh_ref[0, rows, :] = jnp.where(first_head, q2, 0.0).astype(BF16)
            qh_ref[1, rows, :] = jnp.where(first_head, 0.0, q2).astype(BF16)
            kb_ref[steps + t * steps:steps + (t + 1) * steps, :] = source(branch, 1, t).astype(BF16)
            vb_ref[steps + t * steps:steps + (t + 1) * steps, 0:LANES] = source(branch, 2, t).astype(BF16)

        def aligned(start):
            return start if isinstance(start, int) else pl.multiple_of(start, steps)

        def tile_rows(t):
            return pl.ds(aligned(t * steps), steps)

        def key_rows(t, width=width):
            start = t * steps if width == 2 * steps else (t + 1) * steps
            return pl.ds(aligned(start), width)

        def scores(t, branch=branch, width=width, blocks_per_class=blocks_per_class, key_rows=key_rows):
            rows = tile_rows(t)
            keys = kb_ref[key_rows(t), :]
            first = jnp.where(t % blocks_per_class == 0, 1, 0)
            for hh in range(2):
                if width == 2 * steps:
                    bias = bias_ref[branch, 0, hh, pl.ds(first, 1), :, :][0]
                else:
                    bias = bias_ref[branch, 0, hh, 0, :, steps:]
                sc = lax.dot_general(qh_ref[hh, rows, :], keys, nt, preferred_element_type=F32)
                s_ref[hh, rows, 0:width] = sc + bias

        def softmax(t, width=width):
            rows = tile_rows(t)
            for hh in range(2):
                sc = s_ref[hh, rows, 0:width]
                m = jnp.max(sc, axis=1, keepdims=True)
                p_ref[hh, rows, 0:width] = jnp.exp(sc - m).astype(BF16)
                res_ref[1, rows, hh * HEAD_DIM:(hh + 1) * HEAD_DIM] = jnp.broadcast_to(m, (steps, HEAD_DIM))

        def weighted(t, width=width, key_rows=key_rows):
            rows = tile_rows(t)
            vals = vb_ref[key_rows(t), :]
            r0 = jnp.dot(p_ref[0, rows, 0:width], vals, preferred_element_type=F32)
            r1 = jnp.dot(p_ref[1, rows, 0:width], vals, preferred_element_type=F32)
            res_ref[0, rows, :] = jnp.where(first_head, r0[:, :LANES], r1[:, :LANES])
            res_ref[2, rows, :] = jnp.where(first_head, r0[:, LANES:], r1[:, LANES:])

        def stage(g_scores, g_softmax, scores=scores, softmax=softmax):
            for fn, g in ((softmax, g_softmax), (scores, g_scores)):
                if g is not None:
                    for j in range(ATTN_GROUP):
                        fn(g * ATTN_GROUP + j)

        n_groups = n_tiles // ATTN_GROUP
        stage(0, None)

        def steady(g, carry, stage=stage):
            stage(g, g - 1)
            return carry

        lax.fori_loop(1, n_groups, steady, 0)
        stage(None, n_groups - 1)

        def weighted_pass(t, carry, weighted=weighted):
            weighted(t)
            return carry

        lax.fori_loop(0, n_tiles, weighted_pass, 0, unroll=ATTN_MXU_UNROLL)

        for kind in range(3):
            if branch == 0:
                nat_ref[0, kind] = res_ref[kind]
                continue
            src = res_ref
            if branch == 2:
                for t in range(n_tiles):
                    sigma, tau = t % 4, t // 4
                    stage_ref[kind, pl.ds(sigma * quarter + tau, steps, stride=4), :] = (
                        res_ref[kind, t * steps:(t + 1) * steps, :])
                src = stage_ref
            for sigma in range(4):
                nat_ref[branch, kind, pl.ds(sigma, quarter, stride=4), :] = (
                    src[kind, sigma * quarter:(sigma + 1) * quarter, :])

    def merge(i, carry):
        rows = pl.ds(pl.multiple_of(i * 256, 256), 256)
        m0, m1, m2 = nat_ref[0, 1, rows, :], nat_ref[1, 1, rows, :], nat_ref[2, 1, rows, :]
        m_all = jnp.maximum(jnp.maximum(m0, m1), m2)
        w0, w1, w2 = jnp.exp(m0 - m_all), jnp.exp(m1 - m_all), jnp.exp(m2 - m_all)
        num = w0 * nat_ref[0, 0, rows, :] + w1 * nat_ref[1, 0, rows, :] + w2 * nat_ref[2, 0, rows, :]
        den = w0 * nat_ref[0, 2, rows, :] + w1 * nat_ref[1, 2, rows, :] + w2 * nat_ref[2, 2, rows, :]
        o_ref[0, rows, :] = num / den
        return carry

    lax.fori_loop(0, s_len // 256, merge, 0)


def _attn_prompt_call(q, k, v, bias_tiles):
    n, s, _ = q.shape
    pairs = ATTN_HEADS // 2
    steps = WINDOW_STEPS
    qspec = pl.BlockSpec((1, s, LANES), lambda n_, g: (n_, 0, g))
    return pl.pallas_call(
        _attn_prompt_kernel,
        grid=(n, pairs),
        in_specs=[qspec, qspec, qspec,
                  pl.BlockSpec((3, 1, 2, 2, steps, 2 * steps), lambda n_, g: (0, g, 0, 0, 0, 0))],
        out_specs=qspec,
        out_shape=jax.ShapeDtypeStruct((n, s, ATTN_WIDTH), F32),
        scratch_shapes=[pltpu.VMEM((3, s, LANES), F32),
                        pltpu.VMEM((2, s, LANES), BF16),
                        pltpu.VMEM((s + steps, LANES), BF16),
                        pltpu.VMEM((s + steps, 2 * LANES), BF16),
                        pltpu.VMEM((2, s, 2 * steps), F32),
                        pltpu.VMEM((2, s, 2 * steps), BF16),
                        pltpu.VMEM((3, s, LANES), F32), pltpu.VMEM((3, s, LANES), F32),
                        pltpu.VMEM((3, 3, s, LANES), F32)],
        compiler_params=_cparams(("arbitrary", "arbitrary")),
        name="attn_prompt",
    )(q, k, v, bias_tiles)


def _s5_discretise(a_re, a_im, log_dt, b_re, b_im):
    lam_re = jnp.minimum(a_re.astype(F32), -1e-4)
    lam_im = a_im.astype(F32)
    dt = jnp.exp(log_dt.astype(F32))[:, None]
    mag = jnp.exp(lam_re * dt)
    ph = lam_im * dt
    abar_re, abar_im = mag * jnp.cos(ph), mag * jnp.sin(ph)
    nr, ni = abar_re - 1.0, abar_im
    den = lam_re * lam_re + lam_im * lam_im
    coef_re = (nr * lam_re + ni * lam_im) / den
    coef_im = (ni * lam_re - nr * lam_im) / den
    br, bi = b_re.astype(F32), b_im.astype(F32)
    bbar_re = coef_re[..., None] * br - coef_im[..., None] * bi
    bbar_im = coef_re[..., None] * bi + coef_im[..., None] * br
    return lam_re * dt, ph, abar_re, abar_im, bbar_re, bbar_im


def _abar_power(log_mag, ph, n):
    nf = jnp.asarray(n, F32)[:, None, None]
    mag = jnp.exp(nf * log_mag[None])
    return mag * jnp.cos(nf * ph[None]), mag * jnp.sin(nf * ph[None])


def _s5_prompt_operators(a_re, a_im, log_dt, b_re, b_im, c_re, c_im):
    L = SSM_CHUNK
    log_mag, ph, _, _, bb_re, bb_im = _s5_discretise(a_re, a_im, log_dt, b_re, b_im)
    cr, ci = c_re.astype(F32), c_im.astype(F32)
    pw_re, pw_im = _abar_power(log_mag, ph, np.arange(L + 1))
    eye = jnp.eye(OCTET, dtype=F32)

    ab_re = pw_re[:L, :, :, None] * bb_re[None] - pw_im[:L, :, :, None] * bb_im[None]
    ab_im = pw_re[:L, :, :, None] * bb_im[None] + pw_im[:L, :, :, None] * bb_re[None]
    lag = (jnp.einsum('gop,lgpi->lgoi', cr, ab_re, precision=_HIGHEST)
           - jnp.einsum('gop,lgpi->lgoi', ci, ab_im, precision=_HIGHEST))
    lag = lag.reshape(L, N_OCTETS, OCTET, SSM_GROUP_CH, SSM_GROUP_CH)
    bd = jnp.einsum('logci,gh->olgihc', lag, eye).reshape(N_OCTETS, L, LANES, LANES).astype(BF16)
    stack = bd[:, ::-1].reshape(N_OCTETS, L * LANES, LANES)
    shifted = jnp.concatenate([stack[:, LANES:], jnp.zeros((N_OCTETS, LANES, LANES), BF16)], axis=1)
    t_op = jnp.concatenate([shifted, stack], axis=-1)

    m_parts = []
    for part in (ab_re[::-1], ab_im[::-1]):
        x = part.reshape(L, N_OCTETS, OCTET, SSM_STATE, SSM_GROUP_CH)
        m_parts.append(jnp.einsum('sogpi,gh->osgihp', x, eye).reshape(N_OCTETS, L * LANES, OCT_STATE))
    m_op = jnp.concatenate(m_parts, axis=-1)

    p1_re, p1_im = pw_re[1:], pw_im[1:]
    on_re = cr[None] * p1_re[:, :, None, :] - ci[None] * p1_im[:, :, None, :]
    on_im = -cr[None] * p1_im[:, :, None, :] - ci[None] * p1_re[:, :, None, :]
    p_parts = []
    for part in (on_re, on_im):
        x = part.reshape(L, N_OCTETS, OCTET, SSM_GROUP_CH, SSM_STATE)
        p_parts.append(jnp.einsum('togcp,gh->ogpthc', x, eye).reshape(N_OCTETS, OCT_STATE, L * LANES))
    p_op = jnp.concatenate(p_parts, axis=1)

    n_steps = 8
    sc_re, sc_im = _abar_power(log_mag, ph, L * (2 ** np.arange(n_steps)))
    sc = jnp.concatenate([sc_re.reshape(n_steps, N_OCTETS, OCT_STATE),
                          sc_im.reshape(n_steps, N_OCTETS, OCT_STATE)], axis=-1)
    sc = jnp.transpose(sc, (1, 0, 2))
    return t_op, m_op.astype(BF16), p_op.astype(BF16), sc


def _octet_glu(glu):
    eye = jnp.eye(OCTET, dtype=F32)
    x = glu.astype(F32).reshape(N_OCTETS, OCTET, SSM_GROUP_CH, SSM_GROUP_CH)
    return jnp.einsum('ogce,gh->ogche', x, eye).reshape(N_OCTETS, LANES, LANES).astype(BF16)


def _ssm_prompt_kernel(u_ref, t_ref, m_ref, p_ref, sc_ref, d_ref, ga_ref, gb_ref,
                       y_ref, h_ref, uf_ref, ub_ref, st_ref):
    L = SSM_CHUNK
    n_seq, s_len = u_ref.shape[0], u_ref.shape[1]
    n_chunks = s_len // L
    quarter = s_len // 4
    rows = n_seq * n_chunks

    def staged(sq, step):
        sigma, tau = step % 4, step // 4
        return pl.ds(sq * s_len + sigma * quarter + tau, n_chunks, stride=4)

    for sq in range(n_seq):
        for sigma in range(4):
            st_ref[sq * s_len + sigma * quarter:sq * s_len + (sigma + 1) * quarter, :] = (
                u_ref[sq, pl.ds(sigma, quarter, stride=4), :])
        for step in range(L):
            blk = st_ref[staged(sq, step), :]
            uf_ref[sq * n_chunks:(sq + 1) * n_chunks, step * LANES:(step + 1) * LANES] = blk
            ub_ref[sq * n_chunks:(sq + 1) * n_chunks, step * LANES:(step + 1) * LANES] = blk.astype(BF16)
    ub = ub_ref[...]

    x = jnp.dot(ub, m_ref[0], preferred_element_type=F32)
    chunk = lax.broadcasted_iota(jnp.int32, (rows, 1), 0) % n_chunks
    half = OCT_STATE
    k = 1
    step = 0
    while k < n_chunks:
        a_re = sc_ref[0, step:step + 1, :half]
        a_im = sc_ref[0, step:step + 1, half:]
        sh = jnp.where(chunk >= k, pltpu.roll(x, k, axis=0), 0.0)
        s_re, s_im = sh[:, :half], sh[:, half:]
        x = x + jnp.concatenate([a_re * s_re - a_im * s_im, a_re * s_im + a_im * s_re], axis=1)
        k *= 2
        step += 1
    for sq in range(n_seq):
        h_ref[sq, 0] = x[(sq + 1) * n_chunks - 1:(sq + 1) * n_chunks, :]
    h_start = jnp.where(chunk >= 1, pltpu.roll(x, 1, axis=0), 0.0)

    hb = h_start.astype(BF16)
    d = d_ref[0]
    ga = ga_ref[0]
    gb = gb_ref[0]
    for t in range(0, L, 2):
        pair = slice(t * LANES, (t + 2) * LANES)
        y2 = (jnp.dot(ub_ref[:, :(t + 2) * LANES], t_ref[0, (L - 2 - t) * LANES:, :],
                      preferred_element_type=F32)
              + jnp.dot(hb, p_ref[0, :, pair], preferred_element_type=F32))
        for j in range(2):
            lanes = slice((t + j) * LANES, (t + j + 1) * LANES)
            g = _gelu_tanh(y2[:, j * LANES:(j + 1) * LANES] + d * uf_ref[:, lanes]).astype(BF16)
            out = (jnp.dot(g, ga, preferred_element_type=F32)
                   * _sigmoid(jnp.dot(g, gb, preferred_element_type=F32)))
            for sq in range(n_seq):
                st_ref[staged(sq, t + j), :] = out[sq * n_chunks:(sq + 1) * n_chunks, :]
    for sq in range(n_seq):
        for sigma in range(4):
            y_ref[sq, pl.ds(sigma, quarter, stride=4), :] = (
                st_ref[sq * s_len + sigma * quarter:sq * s_len + (sigma + 1) * quarter, :])


def _ssm_prompt_call(u, t_op, m_op, p_op, sc, d_oct, ga, gb):
    n, s, _ = u.shape
    L = SSM_CHUNK
    nq = math.gcd(SSM_SEQS, n)
    rows = nq * (s // L)
    wide = L * LANES
    wspec = lambda shape: pl.BlockSpec((1,) + shape, lambda o, n_: (o, 0, 0))
    return pl.pallas_call(
        _ssm_prompt_kernel,
        grid=(N_OCTETS, n // nq),
        in_specs=[pl.BlockSpec((nq, s, LANES), lambda o, n_: (n_, 0, o)),
                  wspec((wide, 2 * LANES)), wspec((wide, 2 * OCT_STATE)), wspec((2 * OCT_STATE, wide)),
                  wspec((8, 2 * OCT_STATE)), wspec((1, LANES)),
                  wspec((LANES, LANES)), wspec((LANES, LANES))],
        out_specs=[pl.BlockSpec((nq, s, LANES), lambda o, n_: (n_, 0, o)),
                   pl.BlockSpec((nq, 1, 1, 2 * OCT_STATE), lambda o, n_: (n_, o, 0, 0))],
        out_shape=[jax.ShapeDtypeStruct((n, s, SSM_WIDTH), F32),
                   jax.ShapeDtypeStruct((n, N_OCTETS, 1, 2 * OCT_STATE), F32)],
        scratch_shapes=[pltpu.VMEM((rows, wide), F32),
                        pltpu.VMEM((rows, wide), BF16),
                        pltpu.VMEM((nq * s, LANES), F32)],
        compiler_params=_cparams(("arbitrary", "arbitrary")),
        name="ssm_prompt",
    )(u, t_op, m_op, p_op, sc, d_oct, ga, gb)


def _attn_decode_kernel(q_ref, kn_ref, vn_ref, kt_ref, vt_ref, b_ref, mult_ref, o_ref):
    nt = (((1,), (1,)), ((), ()))
    mult = mult_ref[...]
    for h in range(ATTN_HEADS):
        q = q_ref[0, h]
        kt = jnp.concatenate([kt_ref[0, h].astype(BF16), kn_ref[0, h]], axis=1)
        vt = jnp.concatenate([vt_ref[0, h].astype(BF16), vn_ref[0, h]], axis=1)
        s = jnp.dot(q, kt, preferred_element_type=F32) + b_ref[h]
        m = jnp.max(s, axis=1, keepdims=True)
        p = jnp.exp(s - m) * mult
        den = jnp.sum(p, axis=1, keepdims=True)
        o = lax.dot_general(p.astype(BF16), vt, nt, preferred_element_type=F32)
        o_ref[0, h] = o / den


def _decode_tables(rel_bias, t_len, w_rows):
    t = np.arange(t_len)[:, None]
    dist = np.concatenate([w_rows + t - np.arange(w_rows)[None, :],
                           t - np.arange(LANES)[None, :]], axis=1)
    mult = np.zeros(dist.shape, np.float32)
    for w, r in zip(WINDOWS, DILATIONS):
        mult += (dist >= 0) & (dist % r == 0) & (dist <= w)
    mult = np.concatenate([mult, np.zeros((Q_ROWS - t_len, dist.shape[1]), np.float32)], axis=0)
    mult[t_len:, 0] = 1.0
    by_dist = _bias_by_distance(rel_bias, np.arange(w_rows + t_len))
    rows = []
    for ti in range(t_len):
        cache_part = by_dist[ti + 1:w_rows + ti + 1][::-1]
        new_part = by_dist[:ti + 1][::-1]
        pad = jnp.zeros((LANES - ti - 1, ATTN_HEADS), F32)
        rows.append(jnp.concatenate([cache_part, new_part, pad], axis=0))
    bias = jnp.stack(rows + [jnp.zeros_like(rows[0])] * (Q_ROWS - t_len), axis=0)
    bias = jnp.transpose(bias, (2, 0, 1))
    bias = jnp.where(jnp.asarray(mult)[None] > 0, bias, _NEG_INF)
    return bias, jnp.asarray(mult)


def _attn_decode_call(q, k_new, v_new, cache_k, cache_v, rel_bias):
    n, t_len, w = q.shape
    w_rows = cache_k.shape[1]
    if t_len > min(DILATIONS[1:]) or t_len > Q_ROWS or w_rows < max(WINDOWS):
        raise ValueError("unsupported decode shape")
    heads = (ATTN_HEADS, HEAD_DIM)

    def head_major(a, pad_to):
        a = jnp.transpose(a.reshape((n, t_len) + heads), (0, 2, 1, 3))
        return jnp.pad(a, ((0, 0), (0, 0), (0, pad_to - t_len), (0, 0)))

    qh = head_major(q * (HEAD_DIM ** -0.5), Q_ROWS).astype(BF16)
    knt = jnp.swapaxes(head_major(k_new, LANES), 2, 3).astype(BF16)
    vnt = jnp.swapaxes(head_major(v_new, LANES), 2, 3).astype(BF16)
    kt = jnp.transpose(cache_k.astype(F32), (0, 2, 3, 1))
    vt = jnp.transpose(cache_v.astype(F32), (0, 2, 3, 1))
    bias, mult = _decode_tables(rel_bias, t_len, w_rows)
    keys = w_rows + LANES
    per_seq = lambda shape: pl.BlockSpec((1,) + shape, lambda i: (i, 0, 0, 0))
    out = pl.pallas_call(
        _attn_decode_kernel,
        grid=(n,),
        in_specs=[per_seq((ATTN_HEADS, Q_ROWS, HEAD_DIM)),
                  per_seq((ATTN_HEADS, HEAD_DIM, LANES)), per_seq((ATTN_HEADS, HEAD_DIM, LANES)),
                  per_seq((ATTN_HEADS, HEAD_DIM, w_rows)), per_seq((ATTN_HEADS, HEAD_DIM, w_rows)),
                  pl.BlockSpec((ATTN_HEADS, Q_ROWS, keys), lambda i: (0, 0, 0)),
                  pl.BlockSpec((Q_ROWS, keys), lambda i: (0, 0))],
        out_specs=per_seq((ATTN_HEADS, Q_ROWS, HEAD_DIM)),
        out_shape=jax.ShapeDtypeStruct((n, ATTN_HEADS, Q_ROWS, HEAD_DIM), F32),
        compiler_params=_cparams(("arbitrary",)),
        name="attn_decode",
    )(qh, knt, vnt, kt, vt, bias, mult)
    return jnp.transpose(out[:, :, :t_len], (0, 2, 1, 3)).reshape(n, t_len, w)


def _ssm_decode_kernel(u_ref, hre_ref, him_ref, are_ref, aim_ref, bre_ref, bim_ref,
                       cre_ref, cim_ref, d_ref, ga_ref, gb_ref, y_ref, ore_ref, oim_ref, *, t_len):
    h_re, h_im = hre_ref[...], him_ref[...]
    a_re, a_im = are_ref[...], aim_ref[...]
    for t in range(t_len):
        u = u_ref[t]
        ub = u.astype(BF16)
        n_re = a_re * h_re - a_im * h_im + jnp.dot(ub, bre_ref[...], preferred_element_type=F32)
        n_im = a_re * h_im + a_im * h_re + jnp.dot(ub, bim_ref[...], preferred_element_type=F32)
        h_re, h_im = n_re, n_im
        y = (jnp.dot(h_re.astype(BF16), cre_ref[...], preferred_element_type=F32)
             - jnp.dot(h_im.astype(BF16), cim_ref[...], preferred_element_type=F32)
             + d_ref[...] * u)
        g = _gelu_tanh(y).astype(BF16)
        y_ref[t] = (jnp.dot(g, ga_ref[...], preferred_element_type=F32)
                    * _sigmoid(jnp.dot(g, gb_ref[...], preferred_element_type=F32)))
    ore_ref[...] = h_re
    oim_ref[...] = h_im


def _group_blockdiag(x):
    g, a, b = x.shape
    return jnp.einsum('gab,gh->gahb', x, jnp.eye(g, dtype=x.dtype)).reshape(g * a, g * b)


def _ssm_decode_call(u_tm, h0_re, h0_im, a_re, a_im, log_dt, b_re, b_im, c_re, c_im,
                     d_skip, glu_a, glu_b):
    n = h0_re.shape[0]
    _, _, abar_re, abar_im, bb_re, bb_im = _s5_discretise(a_re, a_im, log_dt, b_re, b_im)
    state = SSM_GROUPS * SSM_STATE
    t_len = u_tm.shape[0]
    args = (u_tm, h0_re.reshape(n, state).astype(F32), h0_im.reshape(n, state).astype(F32),
            abar_re.reshape(1, state), abar_im.reshape(1, state),
            _group_blockdiag(jnp.transpose(bb_re, (0, 2, 1))).astype(BF16),
            _group_blockdiag(jnp.transpose(bb_im, (0, 2, 1))).astype(BF16),
            _group_blockdiag(jnp.transpose(c_re.astype(F32), (0, 2, 1))).astype(BF16),
            _group_blockdiag(jnp.transpose(c_im.astype(F32), (0, 2, 1))).astype(BF16),
            d_skip.astype(F32).reshape(1, SSM_WIDTH),
            _group_blockdiag(glu_a.astype(F32)).astype(BF16),
            _group_blockdiag(glu_b.astype(F32)).astype(BF16))
    full = lambda a: pl.BlockSpec(a.shape, lambda i: (0,) * a.ndim)
    out_shape = [jax.ShapeDtypeStruct(u_tm.shape, F32),
                 jax.ShapeDtypeStruct((n, state), F32), jax.ShapeDtypeStruct((n, state), F32)]
    return pl.pallas_call(
        functools.partial(_ssm_decode_kernel, t_len=t_len),
        grid=(1,),
        in_specs=[full(a) for a in args],
        out_specs=[full(o) for o in out_shape],
        out_shape=out_shape,
        compiler_params=_cparams(("arbitrary",)),
        name="ssm_decode",
    )(*args)


def _outproj_kernel(att_ref, ssm_ref, x_ref, g1_ref, sh2_ref, sc2_ref, ag_ref, sg_ref, n2_ref,
                    wo_ref, wr_ref, rb_ref, x1_ref, h2_ref, route_ref):
    mixed = jnp.concatenate([_rmsnorm(att_ref[...], ag_ref[...]), _rmsnorm(ssm_ref[...], sg_ref[...])],
                            axis=1).astype(BF16)
    x1 = x_ref[...] + g1_ref[0] * jnp.dot(mixed, wo_ref[...], preferred_element_type=F32)
    x1_ref[...] = x1
    h2 = _rmsnorm(x1, n2_ref[...]) * (1.0 + sc2_ref[0]) + sh2_ref[0]
    hi = h2.astype(BF16)
    h2_ref[...] = hi
    lo = (h2 - hi.astype(F32)).astype(BF16)
    r1 = jnp.dot(hi, wr_ref[...], preferred_element_type=F32)
    r2 = jnp.dot(lo, wr_ref[:, :LANES], preferred_element_type=F32)
    logits = r1[:, :LANES] + r1[:, LANES:] + r2 + rb_ref[...]

    lane = lax.broadcasted_iota(jnp.int32, (1, LANES), 1)
    lane_f = lane.astype(F32)
    big = float(LANES)
    ng, epg = N_EXPERT_GROUPS, EXPERTS_PER_GROUP
    lg = jnp.where(lane < ng, logits, _NEG_INF)
    gmax = jnp.max(lg, axis=1, keepdims=True)
    p_star = 1.0 / jnp.sum(jnp.exp(lg - gmax), axis=1, keepdims=True)
    g_star = jnp.min(jnp.where(lg == gmax, lane_f, big), axis=1, keepdims=True)
    in_group = ((lane >= ng) & (lane < ng + ng * epg)
                & (lax.shift_right_arithmetic(lane - ng, int(math.log2(epg))).astype(F32) == g_star))
    le = jnp.where(in_group, logits, _NEG_INF)
    v1 = jnp.max(le, axis=1, keepdims=True)
    i1 = jnp.min(jnp.where(le == v1, lane_f, big), axis=1, keepdims=True)
    le2 = jnp.where(lane_f == i1, _NEG_INF, le)
    v2 = jnp.max(le2, axis=1, keepdims=True)
    i2 = jnp.min(jnp.where(le2 == v2, lane_f, big), axis=1, keepdims=True)
    e2 = jnp.exp(v2 - v1)
    w1 = p_star / (1.0 + e2)
    w2 = p_star * e2 / (1.0 + e2)
    route_ref[...] = jnp.where(lane == 0, i1 - ng,
                               jnp.where(lane == 1, i2 - ng,
                                         jnp.where(lane == 2, w1, jnp.where(lane == 3, w2, 0.0))))


def _outproj_call(att, ssm_y, x_rows, mod, attn_g, ssm_g, norm2_g, w_out_bf, wr, rb):
    rows, d = x_rows.shape
    ts = min(ROW_TILE, rows)
    row = lambda width: pl.BlockSpec((ts, width), lambda i: (i, 0))
    const = lambda a: pl.BlockSpec(a.shape, lambda i: (0,) * a.ndim)
    attn_g = attn_g.reshape(1, ATTN_WIDTH)
    ssm_g = ssm_g.reshape(1, SSM_WIDTH)
    norm2_g = norm2_g.reshape(1, d)
    return pl.pallas_call(
        _outproj_kernel,
        grid=(rows // ts,),
        in_specs=[row(ATTN_WIDTH), row(SSM_WIDTH), row(d),
                  _mod_spec(mod, rows, ts, 2), _mod_spec(mod, rows, ts, 3), _mod_spec(mod, rows, ts, 4),
                  const(attn_g), const(ssm_g), const(norm2_g), const(w_out_bf), const(wr), const(rb)],
        out_specs=[row(d), row(d), row(LANES)],
        out_shape=[jax.ShapeDtypeStruct((rows, d), F32), jax.ShapeDtypeStruct((rows, d), BF16),
                   jax.ShapeDtypeStruct((rows, LANES), F32)],
        compiler_params=_cparams(("arbitrary",)),
        name="outproj_router",
    )(att, ssm_y, x_rows, mod, mod, mod, attn_g, ssm_g, norm2_g, w_out_bf, wr, rb)


def _router_weights(router_g_w, router_g_b, router_e_w, router_e_b):
    d = router_g_w.shape[0]
    ne = N_EXPERT_GROUPS * EXPERTS_PER_GROUP
    w = jnp.concatenate([router_g_w.astype(F32),
                         jnp.transpose(router_e_w.astype(F32), (1, 0, 2)).reshape(d, ne)], axis=1)
    w = jnp.pad(w, ((0, 0), (0, LANES - w.shape[1])))
    hi = w.astype(BF16)
    lo = (w - hi.astype(F32)).astype(BF16)
    b = jnp.concatenate([router_g_b.astype(F32), router_e_b.astype(F32).reshape(ne)])
    b = jnp.pad(b, (0, LANES - b.shape[0])).reshape(1, LANES)
    return jnp.concatenate([hi, lo], axis=1), b


def _moe_kernel(te_ref, x_ref, wg_ref, wu_ref, wd_ref, o_ref, wgb_ref, wub_ref, wdb_ref):
    i = pl.program_id(0)
    changed = jnp.logical_or(i == 0, te_ref[i] != te_ref[jnp.maximum(i - 1, 0)])

    @pl.when(changed)
    def _():
        wgb_ref[...] = wg_ref[0].astype(BF16)
        wub_ref[...] = wu_ref[0].astype(BF16)
        wdb_ref[...] = wd_ref[0].astype(BF16)

    x = x_ref[...]
    gate = jnp.dot(x, wgb_ref[...], preferred_element_type=F32)
    up = jnp.dot(x, wub_ref[...], preferred_element_type=F32)
    a = (gate * _sigmoid(gate)) * up
    o_ref[...] = jnp.dot(a.astype(BF16), wdb_ref[...], preferred_element_type=F32).astype(o_ref.dtype)


def _moe_call(tile_expert, x_sorted, wg, wu, wd, tm):
    n_slots, d = x_sorted.shape
    fe = wg.shape[2]
    grid_spec = pltpu.PrefetchScalarGridSpec(
        num_scalar_prefetch=1,
        grid=(n_slots // tm,),
        in_specs=[pl.BlockSpec((tm, d), lambda i, te: (i, 0)),
                  pl.BlockSpec((1, d, fe), lambda i, te: (te[i], 0, 0)),
                  pl.BlockSpec((1, d, fe), lambda i, te: (te[i], 0, 0)),
                  pl.BlockSpec((1, fe, d), lambda i, te: (te[i], 0, 0))],
        out_specs=pl.BlockSpec((tm, d), lambda i, te: (i, 0)),
        scratch_shapes=[pltpu.VMEM((d, fe), BF16), pltpu.VMEM((d, fe), BF16), pltpu.VMEM((fe, d), BF16)],
    )
    return pl.pallas_call(
        _moe_kernel,
        grid_spec=grid_spec,
        out_shape=jax.ShapeDtypeStruct((n_slots, d), BF16),
        compiler_params=_cparams(("arbitrary",)),
        name="moe_experts",
    )(tile_expert, x_sorted, wg, wu, wd)


def _moe_dispatch(route, tm):
    n_tok = route.shape[0]
    ids = route[:, :2].astype(jnp.int32).reshape(-1)
    n_pairs = ids.shape[0]
    n_slots = (-(-n_pairs // tm) + N_EXPERTS) * tm
    hot = (ids[:, None] == jnp.arange(N_EXPERTS)[None, :]).astype(jnp.int32)
    csum = jnp.cumsum(hot, axis=0)
    rank = jnp.sum((csum - hot) * hot, axis=1)
    counts = csum[-1]
    padded = -(-counts // tm) * tm
    ends = jnp.cumsum(padded)
    starts = ends - padded
    pos = jnp.sum(hot * starts[None, :], axis=1) + rank
    tok_of_slot = (jnp.arange(n_slots, dtype=jnp.int32) % n_tok).at[pos].set(
        jnp.arange(n_pairs, dtype=jnp.int32) // 2, unique_indices=True, mode="promise_in_bounds")
    tile_start = jnp.arange(n_slots // tm, dtype=jnp.int32) * tm
    tile_expert = jnp.minimum(jnp.sum((tile_start[:, None] >= ends[None, :]).astype(jnp.int32), axis=1),
                              N_EXPERTS - 1).astype(jnp.int32)
    return pos.reshape(n_tok, 2), tok_of_slot, tile_expert


def _take_rows(x, idx):
    return x.at[idx].get(mode="promise_in_bounds")


def _final_kernel(x1_ref, ya_ref, yb_ref, route_ref, g2_ref, fg_ref, o_ref):
    wa = route_ref[:, 2:3]
    wb = route_ref[:, 3:4]
    x = x1_ref[...] + g2_ref[0] * (wa * ya_ref[...].astype(F32) + wb * yb_ref[...].astype(F32))
    o_ref[...] = _rmsnorm(x, fg_ref[...])


def _final_call(x1, ya, yb, route, mod, final_g):
    rows, d = x1.shape
    ts = min(ROW_TILE, rows)
    row = pl.BlockSpec((ts, d), lambda i: (i, 0))
    return pl.pallas_call(
        _final_kernel,
        grid=(rows // ts,),
        in_specs=[row, row, row, pl.BlockSpec((ts, LANES), lambda i: (i, 0)), _mod_spec(mod, rows, ts, 5),
                  pl.BlockSpec((1, d), lambda i: (0, 0))],
        out_specs=row,
        out_shape=jax.ShapeDtypeStruct((rows, d), F32),
        compiler_params=_cparams(("arbitrary",)),
        name="final_norm",
    )(x1, ya, yb, route, mod, final_g.reshape(1, d))


def kernel(x_prompt, x_sample, c_prompt, c_sample, cache_k, cache_v, state_ssm_re, state_ssm_im,
           rel_bias, ada_w, ada_b, norm1_g, w_in, ssm_a_re, ssm_a_im, ssm_log_dt, ssm_b_re, ssm_b_im,
           ssm_c_re, ssm_c_im, ssm_d, glu_a, glu_b, attn_out_g, ssm_out_g, w_out, norm2_g,
           router_g_w, router_g_b, router_e_w, router_e_b, w_gate, w_up, w_down, final_norm_g):
    if ada_w.shape[0] != 1:
        raise ValueError("single-layer trunk expected")
    nb, s_len, d = x_prompt.shape
    nd, t_len, _ = x_sample.shape
    if s_len != max(WINDOWS):
        raise ValueError("prompt length must equal the widest window")
    n_p, n_s = nb * s_len, nd * t_len
    if n_p % ROW_TILE or n_s % ROW_TILE:
        raise ValueError("token counts must be multiples of the row tile")

    mod = _mod_call(jnp.concatenate([c_prompt, c_sample], axis=0).astype(F32), ada_w[0], ada_b[0])
    mod_p = mod[:nb].reshape(nb, 1, 6 * d)
    mod_s = jnp.repeat(mod[nb:], t_len, axis=0).reshape(1, n_s, 6 * d)

    w_in_bf = w_in[0].astype(BF16)
    w_out_bf = w_out[0].astype(BF16)
    xp_rows = x_prompt.reshape(n_p, d)
    xs_rows = x_sample.reshape(n_s, d)

    qp, kp, vp, up, kp_t, vp_t = _inproj_call(xp_rows, mod_p, norm1_g[0], w_in_bf, seq_len=s_len)
    qs, ks, vs, us = _inproj_call(xs_rows, mod_s, norm1_g[0], w_in_bf)

    seq = lambda a: a.reshape(nb, s_len, ATTN_WIDTH)
    att_p = _attn_prompt_call(seq(qp), seq(kp), seq(vp), _prompt_bias_tiles(rel_bias))
    dec = lambda a: a.reshape(nd, t_len, ATTN_WIDTH)
    att_s = _attn_decode_call(dec(qs), dec(ks), dec(vs), cache_k[0], cache_v[0], rel_bias)

    s5 = (ssm_a_re[0], ssm_a_im[0], ssm_log_dt[0], ssm_b_re[0], ssm_b_im[0], ssm_c_re[0], ssm_c_im[0])
    t_op, m_op, p_op, sc = _s5_prompt_operators(*s5)
    d_oct = ssm_d[0].astype(F32).reshape(N_OCTETS, 1, LANES)
    ssm_p, hT_p = _ssm_prompt_call(seq(up), t_op, m_op, p_op, sc, d_oct,
                                   _octet_glu(glu_a[0]), _octet_glu(glu_b[0]))
    hT_p = hT_p.reshape(nb, N_OCTETS, 2, OCTET, SSM_STATE)
    ssm_re_p = hT_p[:, :, 0].reshape(nb, SSM_GROUPS, SSM_STATE)
    ssm_im_p = hT_p[:, :, 1].reshape(nb, SSM_GROUPS, SSM_STATE)
    us_tm = jnp.transpose(us.reshape(nd, t_len, SSM_WIDTH), (1, 0, 2))
    ssm_s, hre_s, him_s = _ssm_decode_call(us_tm, state_ssm_re[0], state_ssm_im[0], *s5,
                                           ssm_d[0], glu_a[0], glu_b[0])
    ssm_s = jnp.transpose(ssm_s, (1, 0, 2)).reshape(n_s, SSM_WIDTH)

    wr, rb = _router_weights(router_g_w[0], router_g_b[0], router_e_w[0], router_e_b[0])
    norms = (attn_out_g[0], ssm_out_g[0], norm2_g[0], w_out_bf, wr, rb)
    x1_p, h2_p, route_p = _outproj_call(att_p.reshape(n_p, ATTN_WIDTH), ssm_p.reshape(n_p, SSM_WIDTH),
                                        xp_rows, mod_p, *norms)
    x1_s, h2_s, route_s = _outproj_call(att_s.reshape(n_s, ATTN_WIDTH), ssm_s, xs_rows, mod_s, *norms)

    ne = N_EXPERTS
    wg = w_gate[0].reshape(ne, d, D_EXPERT)
    wu = w_up[0].reshape(ne, d, D_EXPERT)
    wd = w_down[0].reshape(ne, D_EXPERT, d)

    def experts(x1, h2, route, mod_rows, tm):
        pos, tok_of_slot, tile_expert = _moe_dispatch(route, tm)
        y_slots = _moe_call(tile_expert, _take_rows(h2, tok_of_slot), wg, wu, wd, tm)
        return _final_call(x1, _take_rows(y_slots, pos[:, 0]), _take_rows(y_slots, pos[:, 1]),
                           route, mod_rows, final_norm_g)

    y_p = experts(x1_p, h2_p, route_p, mod_p, MOE_TILE)
    y_s = experts(x1_s, h2_s, route_s, mod_s, MOE_TILE_DECODE)

    heads = (ATTN_HEADS, HEAD_DIM)
    cache_out = lambda a: jnp.transpose(a.reshape((1, nb) + heads + (s_len,)), (0, 1, 4, 2, 3))
    return (y_p.reshape(nb, s_len, d), y_s.reshape(nd, t_len, d),
            cache_out(kp_t), cache_out(vp_t),
            ks.reshape((1, nd, t_len) + heads), vs.reshape((1, nd, t_len) + heads),
            ssm_re_p[None], ssm_im_p[None],
            hre_s.reshape(1, nd, SSM_GROUPS, SSM_STATE), him_s.reshape(1, nd, SSM_GROUPS, SSM_STATE))
```

```python
import functools
import math

import numpy as np

import jax
import jax.numpy as jnp
from jax import lax
from jax.experimental import pallas as pl
from jax.experimental.pallas import tpu as pltpu

F32 = jnp.float32
BF16 = jnp.bfloat16

D_MODEL = 1024
HEAD_DIM = 64
ATTN_WIDTH = 512
ATTN_HEADS = 8
SSM_WIDTH = 512
SSM_GROUP_CH = 16
SSM_GROUPS = 32
SSM_STATE = 64
WINDOWS = (128, 512, 2048)
DILATIONS = (1, 4, 16)
WINDOW_STEPS = 128
N_BUCKETS = 32
MAX_EXACT = 16
BUCKET_MAX_DIST = 2048
N_EXPERT_GROUPS = 4
EXPERTS_PER_GROUP = 4
N_EXPERTS = N_EXPERT_GROUPS * EXPERTS_PER_GROUP
D_EXPERT = 512
NORM_EPS = 1e-6

LANES = 128
Q_ROWS = 16
OCTET = LANES // SSM_GROUP_CH
N_OCTETS = SSM_GROUPS // OCTET
OCT_STATE = OCTET * SSM_STATE
SSM_CHUNK = 16
SSM_SEQS = 2
ROW_TILE = 512
WIDE_ROW_TILE = 1024
MOE_TILE = 512
MOE_TILE_DECODE = 128
ATTN_GROUP = 4
ATTN_MXU_UNROLL = 8
VMEM_LIMIT = 56 * 1024 * 1024

_NEG_INF = float("-inf")
_HIGHEST = lax.Precision.HIGHEST


def _cparams(sem):
    return pltpu.CompilerParams(dimension_semantics=sem, vmem_limit_bytes=VMEM_LIMIT)


def _rmsnorm(x, g):
    return x * lax.rsqrt(jnp.mean(x * x, axis=-1, keepdims=True) + NORM_EPS) * g


def _gelu_tanh(x):
    c = math.sqrt(2.0 / math.pi)
    return 0.5 * x * (1.0 + jnp.tanh(c * (x + 0.044715 * (x * x * x))))


def _sigmoid(x):
    return 1.0 / (1.0 + jnp.exp(-x))


def _mod_kernel(c_ref, w_ref, b_ref, o_ref):
    c = c_ref[...]
    a = (c * _sigmoid(c)).astype(BF16)
    o_ref[...] = jnp.dot(a, w_ref[...].astype(BF16), preferred_element_type=F32) + b_ref[...]


def _mod_call(c_all, ada_w, ada_b):
    rows, d = c_all.shape
    n_out = ada_w.shape[1]
    tn = 1024
    return pl.pallas_call(
        _mod_kernel,
        grid=(n_out // tn,),
        in_specs=[pl.BlockSpec((rows, d), lambda j: (0, 0)),
                  pl.BlockSpec((d, tn), lambda j: (0, j)),
                  pl.BlockSpec((1, tn), lambda j: (0, j))],
        out_specs=pl.BlockSpec((rows, tn), lambda j: (0, j)),
        out_shape=jax.ShapeDtypeStruct((rows, n_out), F32),
        compiler_params=_cparams(("arbitrary",)),
        name="adaln_mod",
    )(c_all, ada_w, ada_b.reshape(1, n_out))


def _mod_spec(mod, rows, ts, chunk):
    if mod.shape[1] == 1:
        tiles_per_group = (rows // mod.shape[0]) // ts
        return pl.BlockSpec((1, 1, D_MODEL), lambda i: (i // tiles_per_group, 0, chunk))
    return pl.BlockSpec((1, ts, D_MODEL), lambda i: (0, i, chunk))


def _inproj_kernel(x_ref, sh_ref, sc_ref, g_ref, w_ref, *rest, key_major):
    h = _rmsnorm(x_ref[...], g_ref[...]) * (1.0 + sc_ref[0]) + sh_ref[0]
    hb = h.astype(BF16)
    z = jnp.dot(hb, w_ref[...], preferred_element_type=F32)
    aw = ATTN_WIDTH
    if key_major:
        q_ref, k_ref, v_ref, u_ref, kt_ref, vt_ref = rest
        kt_ref[0] = z[:, aw:2 * aw].T
        vt_ref[0] = z[:, 2 * aw:3 * aw].T
    else:
        q_ref, k_ref, v_ref, u_ref = rest
    q_ref[...] = z[:, :aw]
    k_ref[...] = z[:, aw:2 * aw]
    v_ref[...] = z[:, 2 * aw:3 * aw]
    u_ref[...] = z[:, 3 * aw:]


def _inproj_call(x_rows, mod, norm_g, w_in_bf, seq_len=None):
    rows, d = x_rows.shape
    ts = min(ROW_TILE, rows)
    proj = w_in_bf.shape[1]
    out = jax.ShapeDtypeStruct((rows, ATTN_WIDTH), F32)
    ospec = pl.BlockSpec((ts, ATTN_WIDTH), lambda i: (i, 0))
    in_specs = [pl.BlockSpec((ts, d), lambda i: (i, 0)),
                _mod_spec(mod, rows, ts, 0),
                _mod_spec(mod, rows, ts, 1),
                pl.BlockSpec((1, d), lambda i: (0, 0)),
                pl.BlockSpec((d, proj), lambda i: (0, 0))]
    args = [x_rows, mod, mod, norm_g.reshape(1, d), w_in_bf]
    out_specs = [ospec, ospec, ospec, ospec]
    out_shape = [out, out, out, out]
    if seq_len is not None:
        tiles = seq_len // ts
        tspec = pl.BlockSpec((1, ATTN_WIDTH, ts), lambda i: (i // tiles, 0, i % tiles))
        out_specs += [tspec, tspec]
        out_shape += [jax.ShapeDtypeStruct((rows // seq_len, ATTN_WIDTH, seq_len), F32)] * 2
    return pl.pallas_call(
        functools.partial(_inproj_kernel, key_major=seq_len is not None),
        grid=(rows // ts,),
        in_specs=in_specs,
        out_specs=out_specs,
        out_shape=out_shape,
        compiler_params=_cparams(("arbitrary",)),
        name="inproj",
    )(*args)


def _t5_bucket(dist):
    d = jnp.maximum(dist, MAX_EXACT).astype(F32)
    log_part = MAX_EXACT + (jnp.log(d / MAX_EXACT) / math.log(BUCKET_MAX_DIST / MAX_EXACT)
                            * (N_BUCKETS - MAX_EXACT)).astype(jnp.int32)
    return jnp.where(dist < MAX_EXACT, dist, jnp.minimum(log_part, N_BUCKETS - 1))


def _bias_by_distance(rel_bias, dists):
    hot = (_t5_bucket(jnp.asarray(dists, jnp.int32))[:, None]
           == jnp.arange(N_BUCKETS, dtype=jnp.int32)[None, :]).astype(F32)
    return jnp.dot(hot, rel_bias.astype(F32), precision=_HIGHEST)


def _prompt_bias_tiles(rel_bias):
    steps = WINDOW_STEPS
    period = 3 * steps
    tiles = []
    for r in DILATIONS:
        vec = _bias_by_distance(rel_bias, r * np.arange(steps + 1))
        fill = jnp.full((steps - 1, ATTN_HEADS), _NEG_INF, F32)
        w = jnp.concatenate([fill, vec[::-1], fill, fill[:1]], axis=0)
        rep = jnp.tile(w.T, (1, steps))[:, :steps * (period - 1)]
        toe = rep.reshape(ATTN_HEADS, steps, period - 1)[:, :, steps - 1:]
        toe = toe.reshape(ATTN_HEADS // 2, 2, steps, 2 * steps)
        own_only = jnp.where(jnp.arange(2 * steps) < steps, _NEG_INF, toe)
        tiles.append(jnp.stack([toe, own_only], axis=2))
    return jnp.stack(tiles)


def _attn_prompt_kernel(q_ref, k_ref, v_ref, bias_ref, o_ref,
                        p4_ref, qh_ref, kb_ref, vb_ref, s_ref, p_ref, res_ref, stage_ref, nat_ref):
    s_len = q_ref.shape[1]
    steps = WINDOW_STEPS
    n_tiles = s_len // steps
    quarter = s_len // 4
    nt = (((1,), (1,)), ((), ()))
    lane = lax.broadcasted_iota(jnp.int32, (1, LANES), 1)
    first_head = lane < HEAD_DIM
    srcs = (q_ref, k_ref, v_ref)

    kb_ref[0:steps, :] = jnp.zeros((steps, LANES), BF16)
    vb_ref[0:steps, 0:LANES] = jnp.zeros((steps, LANES), BF16)
    vb_ref[:, LANES:] = jnp.ones((s_len + steps, LANES), BF16)
    for x in range(3):
        for sigma in range(4):
            p4_ref[x, sigma * quarter:(sigma + 1) * quarter, :] = srcs[x][0, pl.ds(sigma, quarter, stride=4), :]

    def source(branch, x, tile_idx):
        rows = slice(tile_idx * steps, (tile_idx + 1) * steps)
        if branch == 0:
            return srcs[x][0, rows, :]
        if branch == 1:
            return p4_ref[x, rows, :]
        sigma, tau = tile_idx % 4, tile_idx // 4
        return p4_ref[x, pl.ds(sigma * quarter + tau, steps, stride=4), :]

    for branch, r in enumerate(DILATIONS):
        blocks_per_class = (s_len // r) // steps
        width = 2 * steps if blocks_per_class > 1 else steps

        for t in range(n_tiles):
            rows = slice(t * steps, (t + 1) * steps)
            q2 = source(branch, 0, t) * (HEAD_DIM ** -0.5)
            qh_ref[0, rows, :] = jnp.where(first_head, q2, 0.0).astype(BF16)
            qh_ref[1, rows, :] = jnp.where(first_head, 0.0, q2).astype(BF16)
            kb_ref[steps + t * steps:steps + (t + 1) * steps, :] = source(branch, 1, t).astype(BF16)
            vb_ref[steps + t * steps:steps + (t + 1) * steps, 0:LANES] = source(branch, 2, t).astype(BF16)

        def aligned(start):
            return start if isinstance(start, int) else pl.multiple_of(start, steps)

        def tile_rows(t):
            return pl.ds(aligned(t * steps), steps)

        def key_rows(t, width=width):
            start = t * steps if width == 2 * steps else (t + 1) * steps
            return pl.ds(aligned(start), width)

        def scores(t, branch=branch, width=width, blocks_per_class=blocks_per_class, key_rows=key_rows):
            rows = tile_rows(t)
            keys = kb_ref[key_rows(t), :]
            first = jnp.where(t % blocks_per_class == 0, 1, 0)
            for hh in range(2):
                if width == 2 * steps:
                    bias = bias_ref[branch, 0, hh, pl.ds(first, 1), :, :][0]
                else:
                    bias = bias_ref[branch, 0, hh, 0, :, steps:]
                sc = lax.dot_general(qh_ref[hh, rows, :], keys, nt, preferred_element_type=F32)
                s_ref[hh, rows, 0:width] = sc + bias

        def softmax(t, width=width):
            rows = tile_rows(t)
            for hh in range(2):
                sc = s_ref[hh, rows, 0:width]
                m = jnp.max(sc, axis=1, keepdims=True)
                p_ref[hh, rows, 0:width] = jnp.exp(sc - m).astype(BF16)
                res_ref[1, rows, hh * HEAD_DIM:(hh + 1) * HEAD_DIM] = jnp.broadcast_to(m, (steps, HEAD_DIM))

        def weighted(t, width=width, key_rows=key_rows):
            rows = tile_rows(t)
            vals = vb_ref[key_rows(t), :]
            r0 = jnp.dot(p_ref[0, rows, 0:width], vals, preferred_element_type=F32)
            r1 = jnp.dot(p_ref[1, rows, 0:width], vals, preferred_element_type=F32)
            res_ref[0, rows, :] = jnp.where(first_head, r0[:, :LANES], r1[:, :LANES])
            res_ref[2, rows, :] = jnp.where(first_head, r0[:, LANES:], r1[:, LANES:])

        def stage(g_scores, g_softmax, scores=scores, softmax=softmax):
            for fn, g in ((softmax, g_softmax), (scores, g_scores)):
                if g is not None:
                    for j in range(ATTN_GROUP):
                        fn(g * ATTN_GROUP + j)

        n_groups = n_tiles // ATTN_GROUP
        stage(0, None)

        def steady(g, carry, stage=stage):
            stage(g, g - 1)
            return carry

        lax.fori_loop(1, n_groups, steady, 0)
        stage(None, n_groups - 1)

        def weighted_pass(t, carry, weighted=weighted):
            weighted(t)
            return carry

        lax.fori_loop(0, n_tiles, weighted_pass, 0, unroll=ATTN_MXU_UNROLL)

        for kind in range(3):
            if branch == 0:
                nat_ref[0, kind] = res_ref[kind]
                continue
            src = res_ref
            if branch == 2:
                for t in range(n_tiles):
                    sigma, tau = t % 4, t // 4
                    stage_ref[kind, pl.ds(sigma * quarter + tau, steps, stride=4), :] = (
                        res_ref[kind, t * steps:(t + 1) * steps, :])
                src = stage_ref
            for sigma in range(4):
                nat_ref[branch, kind, pl.ds(sigma, quarter, stride=4), :] = (
                    src[kind, sigma * quarter:(sigma + 1) * quarter, :])

    def merge(i, carry):
        rows = pl.ds(pl.multiple_of(i * 256, 256), 256)
        m0, m1, m2 = nat_ref[0, 1, rows, :], nat_ref[1, 1, rows, :], nat_ref[2, 1, rows, :]
        m_all = jnp.maximum(jnp.maximum(m0, m1), m2)
        w0, w1, w2 = jnp.exp(m0 - m_all), jnp.exp(m1 - m_all), jnp.exp(m2 - m_all)
        num = w0 * nat_ref[0, 0, rows, :] + w1 * nat_ref[1, 0, rows, :] + w2 * nat_ref[2, 0, rows, :]
        den = w0 * nat_ref[0, 2, rows, :] + w1 * nat_ref[1, 2, rows, :] + w2 * nat_ref[2, 2, rows, :]
        o_ref[0, rows, :] = num / den
        return carry

    lax.fori_loop(0, s_len // 256, merge, 0)


def _attn_prompt_call(q, k, v, bias_tiles):
    n, s, _ = q.shape
    pairs = ATTN_HEADS // 2
    steps = WINDOW_STEPS
    qspec = pl.BlockSpec((1, s, LANES), lambda g, n_: (n_, 0, g))
    return pl.pallas_call(
        _attn_prompt_kernel,
        grid=(pairs, n),
        in_specs=[qspec, qspec, qspec,
                  pl.BlockSpec((3, 1, 2, 2, steps, 2 * steps), lambda g, n_: (0, g, 0, 0, 0, 0))],
        out_specs=qspec,
        out_shape=jax.ShapeDtypeStruct((n, s, ATTN_WIDTH), F32),
        scratch_shapes=[pltpu.VMEM((3, s, LANES), F32),
                        pltpu.VMEM((2, s, LANES), BF16),
                        pltpu.VMEM((s + steps, LANES), BF16),
                        pltpu.VMEM((s + steps, 2 * LANES), BF16),
                        pltpu.VMEM((2, s, 2 * steps), F32),
                        pltpu.VMEM((2, s, 2 * steps), BF16),
                        pltpu.VMEM((3, s, LANES), F32), pltpu.VMEM((3, s, LANES), F32),
                        pltpu.VMEM((3, 3, s, LANES), F32)],
        compiler_params=_cparams(("arbitrary", "arbitrary")),
        name="attn_prompt",
    )(q, k, v, bias_tiles)


def _s5_discretise(a_re, a_im, log_dt, b_re, b_im):
    lam_re = jnp.minimum(a_re.astype(F32), -1e-4)
    lam_im = a_im.astype(F32)
    dt = jnp.exp(log_dt.astype(F32))[:, None]
    mag = jnp.exp(lam_re * dt)
    ph = lam_im * dt
    abar_re, abar_im = mag * jnp.cos(ph), mag * jnp.sin(ph)
    nr, ni = abar_re - 1.0, abar_im
    den = lam_re * lam_re + lam_im * lam_im
    coef_re = (nr * lam_re + ni * lam_im) / den
    coef_im = (ni * lam_re - nr * lam_im) / den
    br, bi = b_re.astype(F32), b_im.astype(F32)
    bbar_re = coef_re[..., None] * br - coef_im[..., None] * bi
    bbar_im = coef_re[..., None] * bi + coef_im[..., None] * br
    return lam_re * dt, ph, abar_re, abar_im, bbar_re, bbar_im


def _abar_power(log_mag, ph, n):
    nf = jnp.asarray(n, F32)[:, None, None]
    mag = jnp.exp(nf * log_mag[None])
    return mag * jnp.cos(nf * ph[None]), mag * jnp.sin(nf * ph[None])


def _s5_prompt_operators(a_re, a_im, log_dt, b_re, b_im, c_re, c_im):
    L = SSM_CHUNK
    log_mag, ph, _, _, bb_re, bb_im = _s5_discretise(a_re, a_im, log_dt, b_re, b_im)
    cr, ci = c_re.astype(F32), c_im.astype(F32)
    pw_re, pw_im = _abar_power(log_mag, ph, np.arange(L + 1))
    eye = jnp.eye(OCTET, dtype=F32)

    ab_re = pw_re[:L, :, :, None] * bb_re[None] - pw_im[:L, :, :, None] * bb_im[None]
    ab_im = pw_re[:L, :, :, None] * bb_im[None] + pw_im[:L, :, :, None] * bb_re[None]
    lag = (jnp.einsum('gop,lgpi->lgoi', cr, ab_re, precision=_HIGHEST)
           - jnp.einsum('gop,lgpi->lgoi', ci, ab_im, precision=_HIGHEST))
    lag = lag.reshape(L, N_OCTETS, OCTET, SSM_GROUP_CH, SSM_GROUP_CH)
    bd = jnp.einsum('logci,gh->olgihc', lag, eye).reshape(N_OCTETS, L, LANES, LANES).astype(BF16)
    stack = bd[:, ::-1].reshape(N_OCTETS, L * LANES, LANES)
    shifted = jnp.concatenate([stack[:, LANES:], jnp.zeros((N_OCTETS, LANES, LANES), BF16)], axis=1)
    t_op = jnp.concatenate([shifted, stack], axis=-1)

    m_parts = []
    for part in (ab_re[::-1], ab_im[::-1]):
        x = part.reshape(L, N_OCTETS, OCTET, SSM_STATE, SSM_GROUP_CH)
        m_parts.append(jnp.einsum('sogpi,gh->osgihp', x, eye).reshape(N_OCTETS, L * LANES, OCT_STATE))
    m_op = jnp.concatenate(m_parts, axis=-1)

    p1_re, p1_im = pw_re[1:], pw_im[1:]
    on_re = cr[None] * p1_re[:, :, None, :] - ci[None] * p1_im[:, :, None, :]
    on_im = -cr[None] * p1_im[:, :, None, :] - ci[None] * p1_re[:, :, None, :]
    p_parts = []
    for part in (on_re, on_im):
        x = part.reshape(L, N_OCTETS, OCTET, SSM_GROUP_CH, SSM_STATE)
        p_parts.append(jnp.einsum('togcp,gh->ogpthc', x, eye).reshape(N_OCTETS, OCT_STATE, L * LANES))
    p_op = jnp.concatenate(p_parts, axis=1)

    n_steps = 8
    sc_re, sc_im = _abar_power(log_mag, ph, L * (2 ** np.arange(n_steps)))
    sc = jnp.concatenate([sc_re.reshape(n_steps, N_OCTETS, OCT_STATE),
                          sc_im.reshape(n_steps, N_OCTETS, OCT_STATE)], axis=-1)
    sc = jnp.transpose(sc, (1, 0, 2))
    return t_op, m_op.astype(BF16), p_op.astype(BF16), sc


def _octet_glu(glu):
    eye = jnp.eye(OCTET, dtype=F32)
    x = glu.astype(F32).reshape(N_OCTETS, OCTET, SSM_GROUP_CH, SSM_GROUP_CH)
    return jnp.einsum('ogce,gh->ogche', x, eye).reshape(N_OCTETS, LANES, LANES).astype(BF16)


def _ssm_prompt_kernel(u_ref, t_ref, m_ref, p_ref, sc_ref, d_ref, ga_ref, gb_ref,
                       y_ref, h_ref, uf_ref, ub_ref, st_ref):
    L = SSM_CHUNK
    n_seq, s_len = u_ref.shape[0], u_ref.shape[1]
    n_chunks = s_len // L
    quarter = s_len // 4
    rows = n_seq * n_chunks

    def staged(sq, step):
        sigma, tau = step % 4, step // 4
        return pl.ds(sq * s_len + sigma * quarter + tau, n_chunks, stride=4)

    for sq in range(n_seq):
        for sigma in range(4):
            st_ref[sq * s_len + sigma * quarter:sq * s_len + (sigma + 1) * quarter, :] = (
                u_ref[sq, pl.ds(sigma, quarter, stride=4), :])
        for step in range(L):
            blk = st_ref[staged(sq, step), :]
            uf_ref[sq * n_chunks:(sq + 1) * n_chunks, step * LANES:(step + 1) * LANES] = blk
            ub_ref[sq * n_chunks:(sq + 1) * n_chunks, step * LANES:(step + 1) * LANES] = blk.astype(BF16)
    ub = ub_ref[...]

    x = jnp.dot(ub, m_ref[0], preferred_element_type=F32)
    chunk = lax.broadcasted_iota(jnp.int32, (rows, 1), 0) % n_chunks
    half = OCT_STATE
    k = 1
    step = 0
    while k < n_chunks:
        a_re = sc_ref[0, step:step + 1, :half]
        a_im = sc_ref[0, step:step + 1, half:]
        sh = jnp.where(chunk >= k, pltpu.roll(x, k, axis=0), 0.0)
        s_re, s_im = sh[:, :half], sh[:, half:]
        x = x + jnp.concatenate([a_re * s_re - a_im * s_im, a_re * s_im + a_im * s_re], axis=1)
        k *= 2
        step += 1
    for sq in range(n_seq):
        h_ref[sq, 0] = x[(sq + 1) * n_chunks - 1:(sq + 1) * n_chunks, :]
    h_start = jnp.where(chunk >= 1, pltpu.roll(x, 1, axis=0), 0.0)

    hb = h_start.astype(BF16)
    d = d_ref[0]
    ga = ga_ref[0]
    gb = gb_ref[0]
    for t in range(0, L, 2):
        pair = slice(t * LANES, (t + 2) * LANES)
        y2 = (jnp.dot(ub_ref[:, :(t + 2) * LANES], t_ref[0, (L - 2 - t) * LANES:, :],
                      preferred_element_type=F32)
              + jnp.dot(hb, p_ref[0, :, pair], preferred_element_type=F32))
        for j in range(2):
            lanes = slice((t + j) * LANES, (t + j + 1) * LANES)
            g = _gelu_tanh(y2[:, j * LANES:(j + 1) * LANES] + d * uf_ref[:, lanes]).astype(BF16)
            out = (jnp.dot(g, ga, preferred_element_type=F32)
                   * _sigmoid(jnp.dot(g, gb, preferred_element_type=F32)))
            for sq in range(n_seq):
                st_ref[staged(sq, t + j), :] = out[sq * n_chunks:(sq + 1) * n_chunks, :]
    for sq in range(n_seq):
        for sigma in range(4):
            y_ref[sq, pl.ds(sigma, quarter, stride=4), :] = (
                st_ref[sq * s_len + sigma * quarter:sq * s_len + (sigma + 1) * quarter, :])


def _ssm_prompt_call(u, t_op, m_op, p_op, sc, d_oct, ga, gb):
    n, s, _ = u.shape
    L = SSM_CHUNK
    nq = math.gcd(SSM_SEQS, n)
    rows = nq * (s // L)
    wide = L * LANES
    wspec = lambda shape: pl.BlockSpec((1,) + shape, lambda o, n_: (o, 0, 0))
    return pl.pallas_call(
        _ssm_prompt_kernel,
        grid=(N_OCTETS, n // nq),
        in_specs=[pl.BlockSpec((nq, s, LANES), lambda o, n_: (n_, 0, o)),
                  wspec((wide, 2 * LANES)), wspec((wide, 2 * OCT_STATE)), wspec((2 * OCT_STATE, wide)),
                  wspec((8, 2 * OCT_STATE)), wspec((1, LANES)),
                  wspec((LANES, LANES)), wspec((LANES, LANES))],
        out_specs=[pl.BlockSpec((nq, s, LANES), lambda o, n_: (n_, 0, o)),
                   pl.BlockSpec((nq, 1, 1, 2 * OCT_STATE), lambda o, n_: (n_, o, 0, 0))],
        out_shape=[jax.ShapeDtypeStruct((n, s, SSM_WIDTH), F32),
                   jax.ShapeDtypeStruct((n, N_OCTETS, 1, 2 * OCT_STATE), F32)],
        scratch_shapes=[pltpu.VMEM((rows, wide), F32),
                        pltpu.VMEM((rows, wide), BF16),
                        pltpu.VMEM((nq * s, LANES), F32)],
        compiler_params=_cparams(("arbitrary", "arbitrary")),
        name="ssm_prompt",
    )(u, t_op, m_op, p_op, sc, d_oct, ga, gb)


def _attn_decode_kernel(q_ref, kn_ref, vn_ref, kt_ref, vt_ref, b_ref, mult_ref, o_ref):
    nt = (((1,), (1,)), ((), ()))
    mult = mult_ref[...]
    for h in range(ATTN_HEADS):
        q = q_ref[0, h]
        kt = jnp.concatenate([kt_ref[0, h].astype(BF16), kn_ref[0, h]], axis=1)
        vt = jnp.concatenate([vt_ref[0, h].astype(BF16), vn_ref[0, h]], axis=1)
        s = jnp.dot(q, kt, preferred_element_type=F32) + b_ref[h]
        m = jnp.max(s, axis=1, keepdims=True)
        p = jnp.exp(s - m) * mult
        den = jnp.sum(p, axis=1, keepdims=True)
        o = lax.dot_general(p.astype(BF16), vt, nt, preferred_element_type=F32)
        o_ref[0, h] = o / den


def _decode_tables(rel_bias, t_len, w_rows):
    t = np.arange(t_len)[:, None]
    dist = np.concatenate([w_rows + t - np.arange(w_rows)[None, :],
                           t - np.arange(LANES)[None, :]], axis=1)
    mult = np.zeros(dist.shape, np.float32)
    for w, r in zip(WINDOWS, DILATIONS):
        mult += (dist >= 0) & (dist % r == 0) & (dist <= w)
    mult = np.concatenate([mult, np.zeros((Q_ROWS - t_len, dist.shape[1]), np.float32)], axis=0)
    mult[t_len:, 0] = 1.0
    by_dist = _bias_by_distance(rel_bias, np.arange(w_rows + t_len))
    rows = []
    for ti in range(t_len):
        cache_part = by_dist[ti + 1:w_rows + ti + 1][::-1]
        new_part = by_dist[:ti + 1][::-1]
        pad = jnp.zeros((LANES - ti - 1, ATTN_HEADS), F32)
        rows.append(jnp.concatenate([cache_part, new_part, pad], axis=0))
    bias = jnp.stack(rows + [jnp.zeros_like(rows[0])] * (Q_ROWS - t_len), axis=0)
    bias = jnp.transpose(bias, (2, 0, 1))
    bias = jnp.where(jnp.asarray(mult)[None] > 0, bias, _NEG_INF)
    return bias, jnp.asarray(mult)


def _attn_decode_call(q, k_new, v_new, cache_k, cache_v, rel_bias):
    n, t_len, w = q.shape
    w_rows = cache_k.shape[1]
    if t_len > min(DILATIONS[1:]) or t_len > Q_ROWS or w_rows < max(WINDOWS):
        raise ValueError("unsupported decode shape")
    heads = (ATTN_HEADS, HEAD_DIM)

    def head_major(a, pad_to):
        a = jnp.transpose(a.reshape((n, t_len) + heads), (0, 2, 1, 3))
        return jnp.pad(a, ((0, 0), (0, 0), (0, pad_to - t_len), (0, 0)))

    qh = head_major(q * (HEAD_DIM ** -0.5), Q_ROWS).astype(BF16)
    knt = jnp.swapaxes(head_major(k_new, LANES), 2, 3).astype(BF16)
    vnt = jnp.swapaxes(head_major(v_new, LANES), 2, 3).astype(BF16)
    kt = jnp.transpose(cache_k.astype(F32), (0, 2, 3, 1))
    vt = jnp.transpose(cache_v.astype(F32), (0, 2, 3, 1))
    bias, mult = _decode_tables(rel_bias, t_len, w_rows)
    keys = w_rows + LANES
    per_seq = lambda shape: pl.BlockSpec((1,) + shape, lambda i: (i, 0, 0, 0))
    out = pl.pallas_call(
        _attn_decode_kernel,
        grid=(n,),
        in_specs=[per_seq((ATTN_HEADS, Q_ROWS, HEAD_DIM)),
                  per_seq((ATTN_HEADS, HEAD_DIM, LANES)), per_seq((ATTN_HEADS, HEAD_DIM, LANES)),
                  per_seq((ATTN_HEADS, HEAD_DIM, w_rows)), per_seq((ATTN_HEADS, HEAD_DIM, w_rows)),
                  pl.BlockSpec((ATTN_HEADS, Q_ROWS, keys), lambda i: (0, 0, 0)),
                  pl.BlockSpec((Q_ROWS, keys), lambda i: (0, 0))],
        out_specs=per_seq((ATTN_HEADS, Q_ROWS, HEAD_DIM)),
        out_shape=jax.ShapeDtypeStruct((n, ATTN_HEADS, Q_ROWS, HEAD_DIM), F32),
        compiler_params=_cparams(("arbitrary",)),
        name="attn_decode",
    )(qh, knt, vnt, kt, vt, bias, mult)
    return jnp.transpose(out[:, :, :t_len], (0, 2, 1, 3)).reshape(n, t_len, w)


def _ssm_decode_kernel(u_ref, hre_ref, him_ref, are_ref, aim_ref, bre_ref, bim_ref,
                       cre_ref, cim_ref, d_ref, ga_ref, gb_ref, y_ref, ore_ref, oim_ref, *, t_len):
    h_re, h_im = hre_ref[...], him_ref[...]
    a_re, a_im = are_ref[...], aim_ref[...]
    for t in range(t_len):
        u = u_ref[t]
        ub = u.astype(BF16)
        n_re = a_re * h_re - a_im * h_im + jnp.dot(ub, bre_ref[...], preferred_element_type=F32)
        n_im = a_re * h_im + a_im * h_re + jnp.dot(ub, bim_ref[...], preferred_element_type=F32)
        h_re, h_im = n_re, n_im
        y = (jnp.dot(h_re.astype(BF16), cre_ref[...], preferred_element_type=F32)
             - jnp.dot(h_im.astype(BF16), cim_ref[...], preferred_element_type=F32)
             + d_ref[...] * u)
        g = _gelu_tanh(y).astype(BF16)
        y_ref[t] = (jnp.dot(g, ga_ref[...], preferred_element_type=F32)
                    * _sigmoid(jnp.dot(g, gb_ref[...], preferred_element_type=F32)))
    ore_ref[...] = h_re
    oim_ref[...] = h_im


def _group_blockdiag(x):
    g, a, b = x.shape
    return jnp.einsum('gab,gh->gahb', x, jnp.eye(g, dtype=x.dtype)).reshape(g * a, g * b)


def _ssm_decode_call(u_tm, h0_re, h0_im, a_re, a_im, log_dt, b_re, b_im, c_re, c_im,
                     d_skip, glu_a, glu_b):
    n = h0_re.shape[0]
    _, _, abar_re, abar_im, bb_re, bb_im = _s5_discretise(a_re, a_im, log_dt, b_re, b_im)
    state = SSM_GROUPS * SSM_STATE
    t_len = u_tm.shape[0]
    args = (u_tm, h0_re.reshape(n, state).astype(F32), h0_im.reshape(n, state).astype(F32),
            abar_re.reshape(1, state), abar_im.reshape(1, state),
            _group_blockdiag(jnp.transpose(bb_re, (0, 2, 1))).astype(BF16),
            _group_blockdiag(jnp.transpose(bb_im, (0, 2, 1))).astype(BF16),
            _group_blockdiag(jnp.transpose(c_re.astype(F32), (0, 2, 1))).astype(BF16),
            _group_blockdiag(jnp.transpose(c_im.astype(F32), (0, 2, 1))).astype(BF16),
            d_skip.astype(F32).reshape(1, SSM_WIDTH),
            _group_blockdiag(glu_a.astype(F32)).astype(BF16),
            _group_blockdiag(glu_b.astype(F32)).astype(BF16))
    full = lambda a: pl.BlockSpec(a.shape, lambda i: (0,) * a.ndim)
    out_shape = [jax.ShapeDtypeStruct(u_tm.shape, F32),
                 jax.ShapeDtypeStruct((n, state), F32), jax.ShapeDtypeStruct((n, state), F32)]
    return pl.pallas_call(
        functools.partial(_ssm_decode_kernel, t_len=t_len),
        grid=(1,),
        in_specs=[full(a) for a in args],
        out_specs=[full(o) for o in out_shape],
        out_shape=out_shape,
        compiler_params=_cparams(("arbitrary",)),
        name="ssm_decode",
    )(*args)


def _outproj_kernel(att_ref, ssm_ref, x_ref, g1_ref, sh2_ref, sc2_ref, ag_ref, sg_ref, n2_ref,
                    wo_ref, wr_ref, rb_ref, x1_ref, h2_ref, route_ref):
    mixed = jnp.concatenate([_rmsnorm(att_ref[...], ag_ref[...]), _rmsnorm(ssm_ref[...], sg_ref[...])],
                            axis=1).astype(BF16)
    x1 = x_ref[...] + g1_ref[0] * jnp.dot(mixed, wo_ref[...], preferred_element_type=F32)
    x1_ref[...] = x1
    h2 = _rmsnorm(x1, n2_ref[...]) * (1.0 + sc2_ref[0]) + sh2_ref[0]
    hi = h2.astype(BF16)
    h2_ref[...] = hi
    lo = (h2 - hi.astype(F32)).astype(BF16)
    r1 = jnp.dot(hi, wr_ref[...], preferred_element_type=F32)
    r2 = jnp.dot(lo, wr_ref[:, :LANES], preferred_element_type=F32)
    logits = r1[:, :LANES] + r1[:, LANES:] + r2 + rb_ref[...]

    lane = lax.broadcasted_iota(jnp.int32, (1, LANES), 1)
    lane_f = lane.astype(F32)
    big = float(LANES)
    ng, epg = N_EXPERT_GROUPS, EXPERTS_PER_GROUP
    lg = jnp.where(lane < ng, logits, _NEG_INF)
    gmax = jnp.max(lg, axis=1, keepdims=True)
    p_star = 1.0 / jnp.sum(jnp.exp(lg - gmax), axis=1, keepdims=True)
    g_star = jnp.min(jnp.where(lg == gmax, lane_f, big), axis=1, keepdims=True)
    in_group = ((lane >= ng) & (lane < ng + ng * epg)
                & (lax.shift_right_arithmetic(lane - ng, int(math.log2(epg))).astype(F32) == g_star))
    le = jnp.where(in_group, logits, _NEG_INF)
    v1 = jnp.max(le, axis=1, keepdims=True)
    i1 = jnp.min(jnp.where(le == v1, lane_f, big), axis=1, keepdims=True)
    le2 = jnp.where(lane_f == i1, _NEG_INF, le)
    v2 = jnp.max(le2, axis=1, keepdims=True)
    i2 = jnp.min(jnp.where(le2 == v2, lane_f, big), axis=1, keepdims=True)
    e2 = jnp.exp(v2 - v1)
    w1 = p_star / (1.0 + e2)
    w2 = p_star * e2 / (1.0 + e2)
    route_ref[...] = jnp.where(lane == 0, i1 - ng,
                               jnp.where(lane == 1, i2 - ng,
                                         jnp.where(lane == 2, w1, jnp.where(lane == 3, w2, 0.0))))


def _outproj_call(att, ssm_y, x_rows, mod, attn_g, ssm_g, norm2_g, w_out_bf, wr, rb):
    rows, d = x_rows.shape
    ts = min(WIDE_ROW_TILE, rows)
    row = lambda width: pl.BlockSpec((ts, width), lambda i: (i, 0))
    const = lambda a: pl.BlockSpec(a.shape, lambda i: (0,) * a.ndim)
    attn_g = attn_g.reshape(1, ATTN_WIDTH)
    ssm_g = ssm_g.reshape(1, SSM_WIDTH)
    norm2_g = norm2_g.reshape(1, d)
    return pl.pallas_call(
        _outproj_kernel,
        grid=(rows // ts,),
        in_specs=[row(ATTN_WIDTH), row(SSM_WIDTH), row(d),
                  _mod_spec(mod, rows, ts, 2), _mod_spec(mod, rows, ts, 3), _mod_spec(mod, rows, ts, 4),
                  const(attn_g), const(ssm_g), const(norm2_g), const(w_out_bf), const(wr), const(rb)],
        out_specs=[row(d), row(d), row(LANES)],
        out_shape=[jax.ShapeDtypeStruct((rows, d), F32), jax.ShapeDtypeStruct((rows, d), BF16),
                   jax.ShapeDtypeStruct((rows, LANES), F32)],
        compiler_params=_cparams(("arbitrary",)),
        name="outproj_router",
    )(att, ssm_y, x_rows, mod, mod, mod, attn_g, ssm_g, norm2_g, w_out_bf, wr, rb)


def _router_weights(router_g_w, router_g_b, router_e_w, router_e_b):
    d = router_g_w.shape[0]
    ne = N_EXPERT_GROUPS * EXPERTS_PER_GROUP
    w = jnp.concatenate([router_g_w.astype(F32),
                         jnp.transpose(router_e_w.astype(F32), (1, 0, 2)).reshape(d, ne)], axis=1)
    w = jnp.pad(w, ((0, 0), (0, LANES - w.shape[1])))
    hi = w.astype(BF16)
    lo = (w - hi.astype(F32)).astype(BF16)
    b = jnp.concatenate([router_g_b.astype(F32), router_e_b.astype(F32).reshape(ne)])
    b = jnp.pad(b, (0, LANES - b.shape[0])).reshape(1, LANES)
    return jnp.concatenate([hi, lo], axis=1), b


def _moe_kernel(te_ref, x_ref, wg_ref, wu_ref, wd_ref, o_ref, wgb_ref, wub_ref, wdb_ref):
    i = pl.program_id(0)
    changed = jnp.logical_or(i == 0, te_ref[i] != te_ref[jnp.maximum(i - 1, 0)])

    @pl.when(changed)
    def _():
        wgb_ref[...] = wg_ref[0].astype(BF16)
        wub_ref[...] = wu_ref[0].astype(BF16)
        wdb_ref[...] = wd_ref[0].astype(BF16)

    x = x_ref[...]
    gate = jnp.dot(x, wgb_ref[...], preferred_element_type=F32)
    up = jnp.dot(x, wub_ref[...], preferred_element_type=F32)
    a = (gate * _sigmoid(gate)) * up
    o_ref[...] = jnp.dot(a.astype(BF16), wdb_ref[...], preferred_element_type=F32).astype(o_ref.dtype)


def _moe_call(tile_expert, x_sorted, wg, wu, wd, tm):
    n_slots, d = x_sorted.shape
    fe = wg.shape[2]
    grid_spec = pltpu.PrefetchScalarGridSpec(
        num_scalar_prefetch=1,
        grid=(n_slots // tm,),
        in_specs=[pl.BlockSpec((tm, d), lambda i, te: (i, 0)),
                  pl.BlockSpec((1, d, fe), lambda i, te: (te[i], 0, 0)),
                  pl.BlockSpec((1, d, fe), lambda i, te: (te[i], 0, 0)),
                  pl.BlockSpec((1, fe, d), lambda i, te: (te[i], 0, 0))],
        out_specs=pl.BlockSpec((tm, d), lambda i, te: (i, 0)),
        scratch_shapes=[pltpu.VMEM((d, fe), BF16), pltpu.VMEM((d, fe), BF16), pltpu.VMEM((fe, d), BF16)],
    )
    return pl.pallas_call(
        _moe_kernel,
        grid_spec=grid_spec,
        out_shape=jax.ShapeDtypeStruct((n_slots, d), BF16),
        compiler_params=_cparams(("arbitrary",)),
        name="moe_experts",
    )(tile_expert, x_sorted, wg, wu, wd)


def _moe_dispatch(route, tm):
    n_tok = route.shape[0]
    ids = route[:, :2].astype(jnp.int32).reshape(-1)
    n_pairs = ids.shape[0]
    n_slots = (-(-n_pairs // tm) + N_EXPERTS) * tm
    hot = (ids[:, None] == jnp.arange(N_EXPERTS)[None, :]).astype(jnp.int32)
    csum = jnp.cumsum(hot, axis=0)
    rank = jnp.sum((csum - hot) * hot, axis=1)
    counts = csum[-1]
    padded = -(-counts // tm) * tm
    ends = jnp.cumsum(padded)
    starts = ends - padded
    pos = jnp.sum(hot * starts[None, :], axis=1) + rank
    tok_of_slot = (jnp.arange(n_slots, dtype=jnp.int32) % n_tok).at[pos].set(
        jnp.arange(n_pairs, dtype=jnp.int32) // 2, unique_indices=True, mode="promise_in_bounds")
    tile_start = jnp.arange(n_slots // tm, dtype=jnp.int32) * tm
    tile_expert = jnp.minimum(jnp.sum((tile_start[:, None] >= ends[None, :]).astype(jnp.int32), axis=1),
                              N_EXPERTS - 1).astype(jnp.int32)
    return pos.reshape(n_tok, 2), tok_of_slot, tile_expert


def _take_rows(x, idx):
    return x.at[idx].get(mode="promise_in_bounds")


def _final_kernel(x1_ref, ya_ref, yb_ref, route_ref, g2_ref, fg_ref, o_ref):
    wa = route_ref[:, 2:3]
    wb = route_ref[:, 3:4]
    x = x1_ref[...] + g2_ref[0] * (wa * ya_ref[...].astype(F32) + wb * yb_ref[...].astype(F32))
    o_ref[...] = _rmsnorm(x, fg_ref[...])


def _final_call(x1, ya, yb, route, mod, final_g):
    rows, d = x1.shape
    ts = min(WIDE_ROW_TILE, rows)
    row = pl.BlockSpec((ts, d), lambda i: (i, 0))
    return pl.pallas_call(
        _final_kernel,
        grid=(rows // ts,),
        in_specs=[row, row, row, pl.BlockSpec((ts, LANES), lambda i: (i, 0)), _mod_spec(mod, rows, ts, 5),
                  pl.BlockSpec((1, d), lambda i: (0, 0))],
        out_specs=row,
        out_shape=jax.ShapeDtypeStruct((rows, d), F32),
        compiler_params=_cparams(("arbitrary",)),
        name="final_norm",
    )(x1, ya, yb, route, mod, final_g.reshape(1, d))


def kernel(x_prompt, x_sample, c_prompt, c_sample, cache_k, cache_v, state_ssm_re, state_ssm_im,
           rel_bias, ada_w, ada_b, norm1_g, w_in, ssm_a_re, ssm_a_im, ssm_log_dt, ssm_b_re, ssm_b_im,
           ssm_c_re, ssm_c_im, ssm_d, glu_a, glu_b, attn_out_g, ssm_out_g, w_out, norm2_g,
           router_g_w, router_g_b, router_e_w, router_e_b, w_gate, w_up, w_down, final_norm_g):
    if ada_w.shape[0] != 1:
        raise ValueError("single-layer trunk expected")
    nb, s_len, d = x_prompt.shape
    nd, t_len, _ = x_sample.shape
    if s_len != max(WINDOWS):
        raise ValueError("prompt length must equal the widest window")
    n_p, n_s = nb * s_len, nd * t_len
    if n_p % ROW_TILE or n_s % ROW_TILE:
        raise ValueError("token counts must be multiples of the row tile")

    mod = _mod_call(jnp.concatenate([c_prompt, c_sample], axis=0).astype(F32), ada_w[0], ada_b[0])
    mod_p = mod[:nb].reshape(nb, 1, 6 * d)
    mod_s = jnp.repeat(mod[nb:], t_len, axis=0).reshape(1, n_s, 6 * d)

    w_in_bf = w_in[0].astype(BF16)
    w_out_bf = w_out[0].astype(BF16)
    xp_rows = x_prompt.reshape(n_p, d)
    xs_rows = x_sample.reshape(n_s, d)

    qp, kp, vp, up, kp_t, vp_t = _inproj_call(xp_rows, mod_p, norm1_g[0], w_in_bf, seq_len=s_len)
    qs, ks, vs, us = _inproj_call(xs_rows, mod_s, norm1_g[0], w_in_bf)

    seq = lambda a: a.reshape(nb, s_len, ATTN_WIDTH)
    att_p = _attn_prompt_call(seq(qp), seq(kp), seq(vp), _prompt_bias_tiles(rel_bias))
    dec = lambda a: a.reshape(nd, t_len, ATTN_WIDTH)
    att_s = _attn_decode_call(dec(qs), dec(ks), dec(vs), cache_k[0], cache_v[0], rel_bias)

    s5 = (ssm_a_re[0], ssm_a_im[0], ssm_log_dt[0], ssm_b_re[0], ssm_b_im[0], ssm_c_re[0], ssm_c_im[0])
    t_op, m_op, p_op, sc = _s5_prompt_operators(*s5)
    d_oct = ssm_d[0].astype(F32).reshape(N_OCTETS, 1, LANES)
    ssm_p, hT_p = _ssm_prompt_call(seq(up), t_op, m_op, p_op, sc, d_oct,
                                   _octet_glu(glu_a[0]), _octet_glu(glu_b[0]))
    hT_p = hT_p.reshape(nb, N_OCTETS, 2, OCTET, SSM_STATE)
    ssm_re_p = hT_p[:, :, 0].reshape(nb, SSM_GROUPS, SSM_STATE)
    ssm_im_p = hT_p[:, :, 1].reshape(nb, SSM_GROUPS, SSM_STATE)
    us_tm = jnp.transpose(us.reshape(nd, t_len, SSM_WIDTH), (1, 0, 2))
    ssm_s, hre_s, him_s = _ssm_decode_call(us_tm, state_ssm_re[0], state_ssm_im[0], *s5,
                                           ssm_d[0], glu_a[0], glu_b[0])
    ssm_s = jnp.transpose(ssm_s, (1, 0, 2)).reshape(n_s, SSM_WIDTH)

    wr, rb = _router_weights(router_g_w[0], router_g_b[0], router_e_w[0], router_e_b[0])
    norms = (attn_out_g[0], ssm_out_g[0], norm2_g[0], w_out_bf, wr, rb)
    x1_p, h2_p, route_p = _outproj_call(att_p.reshape(n_p, ATTN_WIDTH), ssm_p.reshape(n_p, SSM_WIDTH),
                                        xp_rows, mod_p, *norms)
    x1_s, h2_s, route_s = _outproj_call(att_s.reshape(n_s, ATTN_WIDTH), ssm_s, xs_rows, mod_s, *norms)

    ne = N_EXPERTS
    wg = w_gate[0].reshape(ne, d, D_EXPERT)
    wu = w_up[0].reshape(ne, d, D_EXPERT)
    wd = w_down[0].reshape(ne, D_EXPERT, d)

    def experts(x1, h2, route, mod_rows, tm):
        pos, tok_of_slot, tile_expert = _moe_dispatch(route, tm)
        y_slots = _moe_call(tile_expert, _take_rows(h2, tok_of_slot), wg, wu, wd, tm)
        return _final_call(x1, _take_rows(y_slots, pos[:, 0]), _take_rows(y_slots, pos[:, 1]),
                           route, mod_rows, final_norm_g)

    y_p = experts(x1_p, h2_p, route_p, mod_p, MOE_TILE)
    y_s = experts(x1_s, h2_s, route_s, mod_s, MOE_TILE_DECODE)

    heads = (ATTN_HEADS, HEAD_DIM)
    cache_out = lambda a: jnp.transpose(a.reshape((1, nb) + heads + (s_len,)), (0, 1, 4, 2, 3))
    return (y_p.reshape(nb, s_len, d), y_s.reshape(nd, t_len, d),
            cache_out(kp_t), cache_out(vp_t),
            ks.reshape((1, nd, t_len) + heads), vs.reshape((1, nd, t_len) + heads),
            ssm_re_p[None], ssm_im_p[None],
            hre_s.reshape(1, nd, SSM_GROUPS, SSM_STATE), him_s.reshape(1, nd, SSM_GROUPS, SSM_STATE))
```

```python
import functools
import math

import numpy as np

import jax
import jax.numpy as jnp
from jax import lax
from jax.experimental import pallas as pl
from jax.experimental.pallas import tpu as pltpu

F32 = jnp.float32
BF16 = jnp.bfloat16

D_MODEL = 1024
HEAD_DIM = 64
ATTN_WIDTH = 512
ATTN_HEADS = 8
SSM_WIDTH = 512
SSM_GROUP_CH = 16
SSM_GROUPS = 32
SSM_STATE = 64
WINDOWS = (128, 512, 2048)
DILATIONS = (1, 4, 16)
WINDOW_STEPS = 128
N_BUCKETS = 32
MAX_EXACT = 16
BUCKET_MAX_DIST = 2048
N_EXPERT_GROUPS = 4
EXPERTS_PER_GROUP = 4
N_EXPERTS = N_EXPERT_GROUPS * EXPERTS_PER_GROUP
D_EXPERT = 512
NORM_EPS = 1e-6

LANES = 128
Q_ROWS = 16
OCTET = LANES // SSM_GROUP_CH
N_OCTETS = SSM_GROUPS // OCTET
OCT_STATE = OCTET * SSM_STATE
SSM_CHUNK = 16
SSM_SEQS = 2
ROW_TILE = 512
WIDE_ROW_TILE = 1024
MOE_TILE = 512
MOE_TILE_DECODE = 128
ROUTE_ROWS = 8
ATTN_GROUP = 4
ATTN_MXU_UNROLL = 8
VMEM_LIMIT = 56 * 1024 * 1024

_NEG_INF = float("-inf")
_HIGHEST = lax.Precision.HIGHEST


def _cparams(sem):
    return pltpu.CompilerParams(dimension_semantics=sem, vmem_limit_bytes=VMEM_LIMIT)


def _rmsnorm(x, g):
    return x * lax.rsqrt(jnp.mean(x * x, axis=-1, keepdims=True) + NORM_EPS) * g


def _gelu_tanh(x):
    c = math.sqrt(2.0 / math.pi)
    return 0.5 * x * (1.0 + jnp.tanh(c * (x + 0.044715 * (x * x * x))))


def _sigmoid(x):
    return 1.0 / (1.0 + jnp.exp(-x))


def _mod_kernel(c_ref, w_ref, b_ref, o_ref):
    c = c_ref[...]
    a = (c * _sigmoid(c)).astype(BF16)
    o_ref[...] = jnp.dot(a, w_ref[...].astype(BF16), preferred_element_type=F32) + b_ref[...]


def _mod_call(c_all, ada_w, ada_b):
    rows, d = c_all.shape
    n_out = ada_w.shape[1]
    tn = 1024
    return pl.pallas_call(
        _mod_kernel,
        grid=(n_out // tn,),
        in_specs=[pl.BlockSpec((rows, d), lambda j: (0, 0)),
                  pl.BlockSpec((d, tn), lambda j: (0, j)),
                  pl.BlockSpec((1, tn), lambda j: (0, j))],
        out_specs=pl.BlockSpec((rows, tn), lambda j: (0, j)),
        out_shape=jax.ShapeDtypeStruct((rows, n_out), F32),
        compiler_params=_cparams(("arbitrary",)),
        name="adaln_mod",
    )(c_all, ada_w, ada_b.reshape(1, n_out))


def _mod_spec(mod, rows, ts, chunk):
    if mod.shape[1] == 1:
        tiles_per_group = (rows // mod.shape[0]) // ts
        return pl.BlockSpec((1, 1, D_MODEL), lambda i: (i // tiles_per_group, 0, chunk))
    return pl.BlockSpec((1, ts, D_MODEL), lambda i: (0, i, chunk))


def _inproj_kernel(x_ref, sh_ref, sc_ref, g_ref, w_ref, *rest, key_major):
    h = _rmsnorm(x_ref[...], g_ref[...]) * (1.0 + sc_ref[0]) + sh_ref[0]
    hb = h.astype(BF16)
    z = jnp.dot(hb, w_ref[...], preferred_element_type=F32)
    aw = ATTN_WIDTH
    if key_major:
        q_ref, k_ref, v_ref, u_ref, kt_ref, vt_ref = rest
        kt_ref[0] = z[:, aw:2 * aw].T
        vt_ref[0] = z[:, 2 * aw:3 * aw].T
    else:
        q_ref, k_ref, v_ref, u_ref = rest
    q_ref[...] = z[:, :aw]
    k_ref[...] = z[:, aw:2 * aw]
    v_ref[...] = z[:, 2 * aw:3 * aw]
    u_ref[...] = z[:, 3 * aw:]


def _inproj_call(x_rows, mod, norm_g, w_in_bf, seq_len=None):
    rows, d = x_rows.shape
    ts = min(ROW_TILE, rows)
    proj = w_in_bf.shape[1]
    out = jax.ShapeDtypeStruct((rows, ATTN_WIDTH), F32)
    ospec = pl.BlockSpec((ts, ATTN_WIDTH), lambda i: (i, 0))
    in_specs = [pl.BlockSpec((ts, d), lambda i: (i, 0)),
                _mod_spec(mod, rows, ts, 0),
                _mod_spec(mod, rows, ts, 1),
                pl.BlockSpec((1, d), lambda i: (0, 0)),
                pl.BlockSpec((d, proj), lambda i: (0, 0))]
    args = [x_rows, mod, mod, norm_g.reshape(1, d), w_in_bf]
    out_specs = [ospec, ospec, ospec, ospec]
    out_shape = [out, out, out, out]
    if seq_len is not None:
        tiles = seq_len // ts
        tspec = pl.BlockSpec((1, ATTN_WIDTH, ts), lambda i: (i // tiles, 0, i % tiles))
        out_specs += [tspec, tspec]
        out_shape += [jax.ShapeDtypeStruct((rows // seq_len, ATTN_WIDTH, seq_len), F32)] * 2
    return pl.pallas_call(
        functools.partial(_inproj_kernel, key_major=seq_len is not None),
        grid=(rows // ts,),
        in_specs=in_specs,
        out_specs=out_specs,
        out_shape=out_shape,
        compiler_params=_cparams(("arbitrary",)),
        name="inproj",
    )(*args)


def _t5_bucket(dist):
    d = jnp.maximum(dist, MAX_EXACT).astype(F32)
    log_part = MAX_EXACT + (jnp.log(d / MAX_EXACT) / math.log(BUCKET_MAX_DIST / MAX_EXACT)
                            * (N_BUCKETS - MAX_EXACT)).astype(jnp.int32)
    return jnp.where(dist < MAX_EXACT, dist, jnp.minimum(log_part, N_BUCKETS - 1))


def _bias_by_distance(rel_bias, dists):
    hot = (_t5_bucket(jnp.asarray(dists, jnp.int32))[:, None]
           == jnp.arange(N_BUCKETS, dtype=jnp.int32)[None, :]).astype(F32)
    return jnp.dot(hot, rel_bias.astype(F32), precision=_HIGHEST)


def _prompt_bias_tiles(rel_bias):
    steps = WINDOW_STEPS
    period = 3 * steps
    tiles = []
    for r in DILATIONS:
        vec = _bias_by_distance(rel_bias, r * np.arange(steps + 1))
        fill = jnp.full((steps - 1, ATTN_HEADS), _NEG_INF, F32)
        w = jnp.concatenate([fill, vec[::-1], fill, fill[:1]], axis=0)
        rep = jnp.tile(w.T, (1, steps))[:, :steps * (period - 1)]
        toe = rep.reshape(ATTN_HEADS, steps, period - 1)[:, :, steps - 1:]
        toe = toe.reshape(ATTN_HEADS // 2, 2, steps, 2 * steps)
        own_only = jnp.where(jnp.arange(2 * steps) < steps, _NEG_INF, toe)
        tiles.append(jnp.stack([toe, own_only], axis=2))
    return jnp.stack(tiles)


def _attn_prompt_kernel(q_ref, k_ref, v_ref, bias_ref, o_ref,
                        p4_ref, qh_ref, kb_ref, vb_ref, s_ref, p_ref, res_ref, stage_ref, nat_ref):
    s_len = q_ref.shape[1]
    steps = WINDOW_STEPS
    n_tiles = s_len // steps
    quarter = s_len // 4
    nt = (((1,), (1,)), ((), ()))
    lane = lax.broadcasted_iota(jnp.int32, (1, LANES), 1)
    first_head = lane < HEAD_DIM
    srcs = (q_ref, k_ref, v_ref)

    kb_ref[0:steps, :] = jnp.zeros((steps, LANES), BF16)
    vb_ref[0:steps, 0:LANES] = jnp.zeros((steps, LANES), BF16)
    vb_ref[:, LANES:] = jnp.ones((s_len + steps, LANES), BF16)
    for x in range(3):
        for sigma in range(4):
            p4_ref[x, sigma * quarter:(sigma + 1) * quarter, :] = srcs[x][0, pl.ds(sigma, quarter, stride=4), :]

    def source(branch, x, tile_idx):
        rows = slice(tile_idx * steps, (tile_idx + 1) * steps)
        if branch == 0:
            return srcs[x][0, rows, :]
        if branch == 1:
            return p4_ref[x, rows, :]
        sigma, tau = tile_idx % 4, tile_idx // 4
        return p4_ref[x, pl.ds(sigma * quarter + tau, steps, stride=4), :]

    for branch, r in enumerate(DILATIONS):
        blocks_per_class = (s_len // r) // steps
        width = 2 * steps if blocks_per_class > 1 else steps

        for t in range(n_tiles):
            rows = slice(t * steps, (t + 1) * steps)
            q2 = source(branch, 0, t) * (HEAD_DIM ** -0.5)
            qh_ref[0, rows, :] = jnp.where(first_head, q2, 0.0).astype(BF16)
            qh_ref[1, rows, :] = jnp.where(first_head, 0.0, q2).astype(BF16)
            kb_ref[steps + t * steps:steps + (t + 1) * steps, :] = source(branch, 1, t).astype(BF16)
            vb_ref[steps + t * steps:steps + (t + 1) * steps, 0:LANES] = source(branch, 2, t).astype(BF16)

        def aligned(start):
            return start if isinstance(start, int) else pl.multiple_of(start, steps)

        def tile_rows(t):
            return pl.ds(aligned(t * steps), steps)

        def key_rows(t, width=width):
            start = t * steps if width == 2 * steps else (t + 1) * steps
            return pl.ds(aligned(start), width)

        def scores(t, branch=branch, width=width, blocks_per_class=blocks_per_class, key_rows=key_rows):
            rows = tile_rows(t)
            keys = kb_ref[key_rows(t), :]
            first = jnp.where(t % blocks_per_class == 0, 1, 0)
            for hh in range(2):
                if width == 2 * steps:
                    bias = bias_ref[branch, 0, hh, pl.ds(first, 1), :, :][0]
                else:
                    bias = bias_ref[branch, 0, hh, 0, :, steps:]
                sc = lax.dot_general(qh_ref[hh, rows, :], keys, nt, preferred_element_type=F32)
                s_ref[hh, rows, 0:width] = sc + bias

        def softmax(t, width=width):
            rows = tile_rows(t)
            for hh in range(2):
                sc = s_ref[hh, rows, 0:width]
                m = jnp.max(sc, axis=1, keepdims=True)
                p_ref[hh, rows, 0:width] = jnp.exp(sc - m).astype(BF16)
                res_ref[1, rows, hh * HEAD_DIM:(hh + 1) * HEAD_DIM] = jnp.broadcast_to(m, (steps, HEAD_DIM))

        def weighted(t, width=width, key_rows=key_rows):
            rows = tile_rows(t)
            vals = vb_ref[key_rows(t), :]
            r0 = jnp.dot(p_ref[0, rows, 0:width], vals, preferred_element_type=F32)
            r1 = jnp.dot(p_ref[1, rows, 0:width], vals, preferred_element_type=F32)
            res_ref[0, rows, :] = jnp.where(first_head, r0[:, :LANES], r1[:, :LANES])
            res_ref[2, rows, :] = jnp.where(first_head, r0[:, LANES:], r1[:, LANES:])

        def stage(g_scores, g_softmax, scores=scores, softmax=softmax):
            for fn, g in ((softmax, g_softmax), (scores, g_scores)):
                if g is not None:
                    for j in range(ATTN_GROUP):
                        fn(g * ATTN_GROUP + j)

        n_groups = n_tiles // ATTN_GROUP
        stage(0, None)

        def steady(g, carry, stage=stage):
            stage(g, g - 1)
            return carry

        lax.fori_loop(1, n_groups, steady, 0)
        stage(None, n_groups - 1)

        def weighted_pass(t, carry, weighted=weighted):
            weighted(t)
            return carry

        lax.fori_loop(0, n_tiles, weighted_pass, 0, unroll=ATTN_MXU_UNROLL)

        for kind in range(3):
            if branch == 0:
                nat_ref[0, kind] = res_ref[kind]
                continue
            src = res_ref
            if branch == 2:
                for t in range(n_tiles):
                    sigma, tau = t % 4, t // 4
                    stage_ref[kind, pl.ds(sigma * quarter + tau, steps, stride=4), :] = (
                        res_ref[kind, t * steps:(t + 1) * steps, :])
                src = stage_ref
            for sigma in range(4):
                nat_ref[branch, kind, pl.ds(sigma, quarter, stride=4), :] = (
                    src[kind, sigma * quarter:(sigma + 1) * quarter, :])

    def merge(i, carry):
        rows = pl.ds(pl.multiple_of(i * 256, 256), 256)
        m0, m1, m2 = nat_ref[0, 1, rows, :], nat_ref[1, 1, rows, :], nat_ref[2, 1, rows, :]
        m_all = jnp.maximum(jnp.maximum(m0, m1), m2)
        w0, w1, w2 = jnp.exp(m0 - m_all), jnp.exp(m1 - m_all), jnp.exp(m2 - m_all)
        num = w0 * nat_ref[0, 0, rows, :] + w1 * nat_ref[1, 0, rows, :] + w2 * nat_ref[2, 0, rows, :]
        den = w0 * nat_ref[0, 2, rows, :] + w1 * nat_ref[1, 2, rows, :] + w2 * nat_ref[2, 2, rows, :]
        o_ref[0, rows, :] = num / den
        return carry

    lax.fori_loop(0, s_len // 256, merge, 0)


def _attn_prompt_call(q, k, v, bias_tiles):
    n, s, _ = q.shape
    pairs = ATTN_HEADS // 2
    steps = WINDOW_STEPS
    qspec = pl.BlockSpec((1, s, LANES), lambda g, n_: (n_, 0, g))
    return pl.pallas_call(
        _attn_prompt_kernel,
        grid=(pairs, n),
        in_specs=[qspec, qspec, qspec,
                  pl.BlockSpec((3, 1, 2, 2, steps, 2 * steps), lambda g, n_: (0, g, 0, 0, 0, 0))],
        out_specs=qspec,
        out_shape=jax.ShapeDtypeStruct((n, s, ATTN_WIDTH), F32),
        scratch_shapes=[pltpu.VMEM((3, s, LANES), F32),
                        pltpu.VMEM((2, s, LANES), BF16),
                        pltpu.VMEM((s + steps, LANES), BF16),
                        pltpu.VMEM((s + steps, 2 * LANES), BF16),
                        pltpu.VMEM((2, s, 2 * steps), F32),
                        pltpu.VMEM((2, s, 2 * steps), BF16),
                        pltpu.VMEM((3, s, LANES), F32), pltpu.VMEM((3, s, LANES), F32),
                        pltpu.VMEM((3, 3, s, LANES), F32)],
        compiler_params=_cparams(("arbitrary", "arbitrary")),
        name="attn_prompt",
    )(q, k, v, bias_tiles)


def _s5_discretise(a_re, a_im, log_dt, b_re, b_im):
    lam_re = jnp.minimum(a_re.astype(F32), -1e-4)
    lam_im = a_im.astype(F32)
    dt = jnp.exp(log_dt.astype(F32))[:, None]
    mag = jnp.exp(lam_re * dt)
    ph = lam_im * dt
    abar_re, abar_im = mag * jnp.cos(ph), mag * jnp.sin(ph)
    nr, ni = abar_re - 1.0, abar_im
    den = lam_re * lam_re + lam_im * lam_im
    coef_re = (nr * lam_re + ni * lam_im) / den
    coef_im = (ni * lam_re - nr * lam_im) / den
    br, bi = b_re.astype(F32), b_im.astype(F32)
    bbar_re = coef_re[..., None] * br - coef_im[..., None] * bi
    bbar_im = coef_re[..., None] * bi + coef_im[..., None] * br
    return lam_re * dt, ph, abar_re, abar_im, bbar_re, bbar_im


def _abar_power(log_mag, ph, n):
    nf = jnp.asarray(n, F32)[:, None, None]
    mag = jnp.exp(nf * log_mag[None])
    return mag * jnp.cos(nf * ph[None]), mag * jnp.sin(nf * ph[None])


def _s5_prompt_operators(a_re, a_im, log_dt, b_re, b_im, c_re, c_im):
    L = SSM_CHUNK
    log_mag, ph, _, _, bb_re, bb_im = _s5_discretise(a_re, a_im, log_dt, b_re, b_im)
    cr, ci = c_re.astype(F32), c_im.astype(F32)
    pw_re, pw_im = _abar_power(log_mag, ph, np.arange(L + 1))
    eye = jnp.eye(OCTET, dtype=F32)

    ab_re = pw_re[:L, :, :, None] * bb_re[None] - pw_im[:L, :, :, None] * bb_im[None]
    ab_im = pw_re[:L, :, :, None] * bb_im[None] + pw_im[:L, :, :, None] * bb_re[None]
    lag = (jnp.einsum('gop,lgpi->lgoi', cr, ab_re, precision=_HIGHEST)
           - jnp.einsum('gop,lgpi->lgoi', ci, ab_im, precision=_HIGHEST))
    lag = lag.reshape(L, N_OCTETS, OCTET, SSM_GROUP_CH, SSM_GROUP_CH)
    bd = jnp.einsum('logci,gh->olgihc', lag, eye).reshape(N_OCTETS, L, LANES, LANES).astype(BF16)
    stack = bd[:, ::-1].reshape(N_OCTETS, L * LANES, LANES)
    shifted = jnp.concatenate([stack[:, LANES:], jnp.zeros((N_OCTETS, LANES, LANES), BF16)], axis=1)
    t_op = jnp.concatenate([shifted, stack], axis=-1)

    group_of_lane = jnp.arange(LANES) // SSM_GROUP_CH
    parts = []
    for part in (ab_re[::-1], ab_im[::-1]):
        x = part.reshape(L, N_OCTETS, OCTET, SSM_STATE, SSM_GROUP_CH)
        parts.append(jnp.transpose(x, (1, 0, 3, 2, 4)).reshape(N_OCTETS, L, SSM_STATE, LANES))
    x = jnp.stack(parts, axis=2)
    own = (jnp.arange(OCTET)[:, None, None] == group_of_lane[None, None, :])
    mt_op = jnp.where(own, x[:, :, :, None], 0.0)
    mt_op = mt_op.reshape(N_OCTETS, L, 2 * OCT_STATE, LANES)

    p1_re, p1_im = pw_re[1:], pw_im[1:]
    on_re = cr[None] * p1_re[:, :, None, :] - ci[None] * p1_im[:, :, None, :]
    on_im = -cr[None] * p1_im[:, :, None, :] - ci[None] * p1_re[:, :, None, :]
    parts = []
    for part in (on_re, on_im):
        y = part.reshape(L, N_OCTETS, OCTET, SSM_GROUP_CH, SSM_STATE)
        parts.append(jnp.transpose(y, (1, 0, 3, 2, 4)).reshape(N_OCTETS, L, SSM_GROUP_CH, OCT_STATE))
    y = jnp.concatenate(parts, axis=-1)
    group_of_state = (jnp.arange(2 * OCT_STATE) % OCT_STATE) // SSM_STATE
    own = (jnp.arange(OCTET)[:, None, None] == group_of_state[None, None, :])
    pt_op = jnp.where(own, y[:, :, None], 0.0)
    pt_op = pt_op.reshape(N_OCTETS, L * LANES, 2 * OCT_STATE)

    n_steps = 8
    sc_re, sc_im = _abar_power(log_mag, ph, L * (2 ** np.arange(n_steps)))
    sc = jnp.concatenate([sc_re.reshape(n_steps, N_OCTETS, OCT_STATE),
                          sc_im.reshape(n_steps, N_OCTETS, OCT_STATE)], axis=-1)
    sc = jnp.transpose(sc, (1, 0, 2))
    return t_op, mt_op.astype(BF16), pt_op.astype(BF16), sc


def _octet_glu(glu):
    eye = jnp.eye(OCTET, dtype=F32)
    x = glu.astype(F32).reshape(N_OCTETS, OCTET, SSM_GROUP_CH, SSM_GROUP_CH)
    return jnp.einsum('ogce,gh->ogche', x, eye).reshape(N_OCTETS, LANES, LANES).astype(BF16)


def _ssm_prompt_kernel(u_ref, t_ref, mb_ref, pt_ref, sc_ref, d_ref, ga_ref, gb_ref,
                       y_ref, h_ref, uf_ref, ub_ref, st_ref, mt_ref):
    L = SSM_CHUNK
    nt = (((1,), (1,)), ((), ()))

    @pl.when(pl.program_id(1) == 0)
    def _():
        for s in range(L):
            mt_ref[:, s * LANES:(s + 1) * LANES] = mb_ref[0, s]

    n_seq, s_len = u_ref.shape[0], u_ref.shape[1]
    n_chunks = s_len // L
    quarter = s_len // 4
    rows = n_seq * n_chunks

    def staged(sq, step):
        sigma, tau = step % 4, step // 4
        return pl.ds(sq * s_len + sigma * quarter + tau, n_chunks, stride=4)

    for sq in range(n_seq):
        for sigma in range(4):
            st_ref[sq * s_len + sigma * quarter:sq * s_len + (sigma + 1) * quarter, :] = (
                u_ref[sq, pl.ds(sigma, quarter, stride=4), :])
        for step in range(L):
            blk = st_ref[staged(sq, step), :]
            uf_ref[sq * n_chunks:(sq + 1) * n_chunks, step * LANES:(step + 1) * LANES] = blk
            ub_ref[sq * n_chunks:(sq + 1) * n_chunks, step * LANES:(step + 1) * LANES] = blk.astype(BF16)
    ub = ub_ref[...]

    x = lax.dot_general(ub, mt_ref[...], nt, preferred_element_type=F32)
    chunk = lax.broadcasted_iota(jnp.int32, (rows, 1), 0) % n_chunks
    half = OCT_STATE
    k = 1
    step = 0
    while k < n_chunks:
        a_re = sc_ref[0, step:step + 1, :half]
        a_im = sc_ref[0, step:step + 1, half:]
        sh = jnp.where(chunk >= k, pltpu.roll(x, k, axis=0), 0.0)
        s_re, s_im = sh[:, :half], sh[:, half:]
        x = x + jnp.concatenate([a_re * s_re - a_im * s_im, a_re * s_im + a_im * s_re], axis=1)
        k *= 2
        step += 1
    for sq in range(n_seq):
        h_ref[sq, 0] = x[(sq + 1) * n_chunks - 1:(sq + 1) * n_chunks, :]
    h_start = jnp.where(chunk >= 1, pltpu.roll(x, 1, axis=0), 0.0)

    hb = h_start.astype(BF16)
    d = d_ref[0]
    ga = ga_ref[0]
    gb = gb_ref[0]
    for t in range(0, L, 2):
        pair = slice(t * LANES, (t + 2) * LANES)
        y2 = (jnp.dot(ub_ref[:, :(t + 2) * LANES], t_ref[0, (L - 2 - t) * LANES:, :],
                      preferred_element_type=F32)
              + lax.dot_general(hb, pt_ref[0, pair, :], nt, preferred_element_type=F32))
        for j in range(2):
            lanes = slice((t + j) * LANES, (t + j + 1) * LANES)
            g = _gelu_tanh(y2[:, j * LANES:(j + 1) * LANES] + d * uf_ref[:, lanes]).astype(BF16)
            out = (jnp.dot(g, ga, preferred_element_type=F32)
                   * _sigmoid(jnp.dot(g, gb, preferred_element_type=F32)))
            for sq in range(n_seq):
                st_ref[staged(sq, t + j), :] = out[sq * n_chunks:(sq + 1) * n_chunks, :]
    for sq in range(n_seq):
        for sigma in range(4):
            y_ref[sq, pl.ds(sigma, quarter, stride=4), :] = (
                st_ref[sq * s_len + sigma * quarter:sq * s_len + (sigma + 1) * quarter, :])


def _ssm_prompt_call(u, t_op, m_op, p_op, sc, d_oct, ga, gb):
    n, s, _ = u.shape
    L = SSM_CHUNK
    nq = math.gcd(SSM_SEQS, n)
    rows = nq * (s // L)
    wide = L * LANES
    wspec = lambda shape: pl.BlockSpec((1,) + shape, lambda o, n_: (o, 0, 0))
    return pl.pallas_call(
        _ssm_prompt_kernel,
        grid=(N_OCTETS, n // nq),
        in_specs=[pl.BlockSpec((nq, s, LANES), lambda o, n_: (n_, 0, o)),
                  wspec((wide, 2 * LANES)),
                  pl.BlockSpec((1, L, 2 * OCT_STATE, LANES), lambda o, n_: (o, 0, 0, 0)),
                  wspec((wide, 2 * OCT_STATE)),
                  wspec((8, 2 * OCT_STATE)), wspec((1, LANES)),
                  wspec((LANES, LANES)), wspec((LANES, LANES))],
        out_specs=[pl.BlockSpec((nq, s, LANES), lambda o, n_: (n_, 0, o)),
                   pl.BlockSpec((nq, 1, 1, 2 * OCT_STATE), lambda o, n_: (n_, o, 0, 0))],
        out_shape=[jax.ShapeDtypeStruct((n, s, SSM_WIDTH), F32),
                   jax.ShapeDtypeStruct((n, N_OCTETS, 1, 2 * OCT_STATE), F32)],
        scratch_shapes=[pltpu.VMEM((rows, wide), F32),
                        pltpu.VMEM((rows, wide), BF16),
                        pltpu.VMEM((nq * s, LANES), F32),
                        pltpu.VMEM((2 * OCT_STATE, wide), BF16)],
        compiler_params=_cparams(("arbitrary", "arbitrary")),
        name="ssm_prompt",
    )(u, t_op, m_op, p_op, sc, d_oct, ga, gb)


def _attn_decode_kernel(q_ref, kn_ref, vn_ref, kt_ref, vt_ref, b_ref, mult_ref, o_ref):
    nt = (((1,), (1,)), ((), ()))
    mult = mult_ref[...]
    for h in range(ATTN_HEADS):
        q = q_ref[0, h]
        kt = jnp.concatenate([kt_ref[0, h].astype(BF16), kn_ref[0, h]], axis=1)
        vt = jnp.concatenate([vt_ref[0, h].astype(BF16), vn_ref[0, h]], axis=1)
        s = jnp.dot(q, kt, preferred_element_type=F32) + b_ref[h]
        m = jnp.max(s, axis=1, keepdims=True)
        p = jnp.exp(s - m) * mult
        den = jnp.sum(p, axis=1, keepdims=True)
        o = lax.dot_general(p.astype(BF16), vt, nt, preferred_element_type=F32)
        o_ref[0, h] = o / den


def _decode_tables(rel_bias, t_len, w_rows):
    t = np.arange(t_len)[:, None]
    dist = np.concatenate([w_rows + t - np.arange(w_rows)[None, :],
                           t - np.arange(LANES)[None, :]], axis=1)
    mult = np.zeros(dist.shape, np.float32)
    for w, r in zip(WINDOWS, DILATIONS):
        mult += (dist >= 0) & (dist % r == 0) & (dist <= w)
    mult = np.concatenate([mult, np.zeros((Q_ROWS - t_len, dist.shape[1]), np.float32)], axis=0)
    mult[t_len:, 0] = 1.0
    by_dist = _bias_by_distance(rel_bias, np.arange(w_rows + t_len))
    rows = []
    for ti in range(t_len):
        cache_part = by_dist[ti + 1:w_rows + ti + 1][::-1]
        new_part = by_dist[:ti + 1][::-1]
        pad = jnp.zeros((LANES - ti - 1, ATTN_HEADS), F32)
        rows.append(jnp.concatenate([cache_part, new_part, pad], axis=0))
    bias = jnp.stack(rows + [jnp.zeros_like(rows[0])] * (Q_ROWS - t_len), axis=0)
    bias = jnp.transpose(bias, (2, 0, 1))
    bias = jnp.where(jnp.asarray(mult)[None] > 0, bias, _NEG_INF)
    return bias, jnp.asarray(mult)


def _attn_decode_call(q, k_new, v_new, cache_k, cache_v, rel_bias):
    n, t_len, w = q.shape
    w_rows = cache_k.shape[1]
    if t_len > min(DILATIONS[1:]) or t_len > Q_ROWS or w_rows < max(WINDOWS):
        raise ValueError("unsupported decode shape")
    heads = (ATTN_HEADS, HEAD_DIM)

    def head_major(a, pad_to):
        a = jnp.transpose(a.reshape((n, t_len) + heads), (0, 2, 1, 3))
        return jnp.pad(a, ((0, 0), (0, 0), (0, pad_to - t_len), (0, 0)))

    qh = head_major(q * (HEAD_DIM ** -0.5), Q_ROWS).astype(BF16)
    knt = jnp.swapaxes(head_major(k_new, LANES), 2, 3).astype(BF16)
    vnt = jnp.swapaxes(head_major(v_new, LANES), 2, 3).astype(BF16)
    kt = jnp.transpose(cache_k.astype(F32), (0, 2, 3, 1))
    vt = jnp.transpose(cache_v.astype(F32), (0, 2, 3, 1))
    bias, mult = _decode_tables(rel_bias, t_len, w_rows)
    keys = w_rows + LANES
    per_seq = lambda shape: pl.BlockSpec((1,) + shape, lambda i: (i, 0, 0, 0))
    out = pl.pallas_call(
        _attn_decode_kernel,
        grid=(n,),
        in_specs=[per_seq((ATTN_HEADS, Q_ROWS, HEAD_DIM)),
                  per_seq((ATTN_HEADS, HEAD_DIM, LANES)), per_seq((ATTN_HEADS, HEAD_DIM, LANES)),
                  per_seq((ATTN_HEADS, HEAD_DIM, w_rows)), per_seq((ATTN_HEADS, HEAD_DIM, w_rows)),
                  pl.BlockSpec((ATTN_HEADS, Q_ROWS, keys), lambda i: (0, 0, 0)),
                  pl.BlockSpec((Q_ROWS, keys), lambda i: (0, 0))],
        out_specs=per_seq((ATTN_HEADS, Q_ROWS, HEAD_DIM)),
        out_shape=jax.ShapeDtypeStruct((n, ATTN_HEADS, Q_ROWS, HEAD_DIM), F32),
        compiler_params=_cparams(("arbitrary",)),
        name="attn_decode",
    )(qh, knt, vnt, kt, vt, bias, mult)
    return jnp.transpose(out[:, :, :t_len], (0, 2, 1, 3)).reshape(n, t_len, w)


def _ssm_decode_kernel(u_ref, hre_ref, him_ref, are_ref, aim_ref, bre_ref, bim_ref,
                       cre_ref, cim_ref, d_ref, ga_ref, gb_ref, y_ref, ore_ref, oim_ref, *, t_len):
    h_re, h_im = hre_ref[...], him_ref[...]
    a_re, a_im = are_ref[...], aim_ref[...]
    for t in range(t_len):
        u = u_ref[t]
        ub = u.astype(BF16)
        n_re = a_re * h_re - a_im * h_im + jnp.dot(ub, bre_ref[...], preferred_element_type=F32)
        n_im = a_re * h_im + a_im * h_re + jnp.dot(ub, bim_ref[...], preferred_element_type=F32)
        h_re, h_im = n_re, n_im
        y = (jnp.dot(h_re.astype(BF16), cre_ref[...], preferred_element_type=F32)
             - jnp.dot(h_im.astype(BF16), cim_ref[...], preferred_element_type=F32)
             + d_ref[...] * u)
        g = _gelu_tanh(y).astype(BF16)
        y_ref[t] = (jnp.dot(g, ga_ref[...], preferred_element_type=F32)
                    * _sigmoid(jnp.dot(g, gb_ref[...], preferred_element_type=F32)))
    ore_ref[...] = h_re
    oim_ref[...] = h_im


def _group_blockdiag(x):
    g, a, b = x.shape
    return jnp.einsum('gab,gh->gahb', x, jnp.eye(g, dtype=x.dtype)).reshape(g * a, g * b)


def _ssm_decode_call(u_tm, h0_re, h0_im, a_re, a_im, log_dt, b_re, b_im, c_re, c_im,
                     d_skip, glu_a, glu_b):
    n = h0_re.shape[0]
    _, _, abar_re, abar_im, bb_re, bb_im = _s5_discretise(a_re, a_im, log_dt, b_re, b_im)
    state = SSM_GROUPS * SSM_STATE
    t_len = u_tm.shape[0]
    args = (u_tm, h0_re.reshape(n, state).astype(F32), h0_im.reshape(n, state).astype(F32),
            abar_re.reshape(1, state), abar_im.reshape(1, state),
            _group_blockdiag(jnp.transpose(bb_re, (0, 2, 1))).astype(BF16),
            _group_blockdiag(jnp.transpose(bb_im, (0, 2, 1))).astype(BF16),
            _group_blockdiag(jnp.transpose(c_re.astype(F32), (0, 2, 1))).astype(BF16),
            _group_blockdiag(jnp.transpose(c_im.astype(F32), (0, 2, 1))).astype(BF16),
            d_skip.astype(F32).reshape(1, SSM_WIDTH),
            _group_blockdiag(glu_a.astype(F32)).astype(BF16),
            _group_blockdiag(glu_b.astype(F32)).astype(BF16))
    full = lambda a: pl.BlockSpec(a.shape, lambda i: (0,) * a.ndim)
    out_shape = [jax.ShapeDtypeStruct(u_tm.shape, F32),
                 jax.ShapeDtypeStruct((n, state), F32), jax.ShapeDtypeStruct((n, state), F32)]
    return pl.pallas_call(
        functools.partial(_ssm_decode_kernel, t_len=t_len),
        grid=(1,),
        in_specs=[full(a) for a in args],
        out_specs=[full(o) for o in out_shape],
        out_shape=out_shape,
        compiler_params=_cparams(("arbitrary",)),
        name="ssm_decode",
    )(*args)


def _outproj_kernel(att_ref, ssm_ref, x_ref, g1_ref, sh2_ref, sc2_ref, ag_ref, sg_ref, n2_ref,
                    wo_ref, wr_ref, rb_ref, x1_ref, h2_ref, route_ref, ids_ref):
    mixed = jnp.concatenate([_rmsnorm(att_ref[...], ag_ref[...]), _rmsnorm(ssm_ref[...], sg_ref[...])],
                            axis=1).astype(BF16)
    x1 = x_ref[...] + g1_ref[0] * jnp.dot(mixed, wo_ref[...], preferred_element_type=F32)
    x1_ref[...] = x1
    h2 = _rmsnorm(x1, n2_ref[...]) * (1.0 + sc2_ref[0]) + sh2_ref[0]
    hi = h2.astype(BF16)
    h2_ref[...] = hi
    lo = (h2 - hi.astype(F32)).astype(BF16)
    r1 = jnp.dot(hi, wr_ref[...], preferred_element_type=F32)
    r2 = jnp.dot(lo, wr_ref[:, :LANES], preferred_element_type=F32)
    logits = r1[:, :LANES] + r1[:, LANES:] + r2 + rb_ref[...]

    lane = lax.broadcasted_iota(jnp.int32, (1, LANES), 1)
    lane_f = lane.astype(F32)
    big = float(LANES)
    ng, epg = N_EXPERT_GROUPS, EXPERTS_PER_GROUP
    lg = jnp.where(lane < ng, logits, _NEG_INF)
    gmax = jnp.max(lg, axis=1, keepdims=True)
    p_star = 1.0 / jnp.sum(jnp.exp(lg - gmax), axis=1, keepdims=True)
    g_star = jnp.min(jnp.where(lg == gmax, lane_f, big), axis=1, keepdims=True)
    in_group = ((lane >= ng) & (lane < ng + ng * epg)
                & (lax.shift_right_arithmetic(lane - ng, int(math.log2(epg))).astype(F32) == g_star))
    le = jnp.where(in_group, logits, _NEG_INF)
    v1 = jnp.max(le, axis=1, keepdims=True)
    i1 = jnp.min(jnp.where(le == v1, lane_f, big), axis=1, keepdims=True)
    le2 = jnp.where(lane_f == i1, _NEG_INF, le)
    v2 = jnp.max(le2, axis=1, keepdims=True)
    i2 = jnp.min(jnp.where(le2 == v2, lane_f, big), axis=1, keepdims=True)
    e2 = jnp.exp(v2 - v1)
    w1 = p_star / (1.0 + e2)
    w2 = p_star * e2 / (1.0 + e2)
    route = jnp.where(lane == 0, i1 - ng,
                      jnp.where(lane == 1, i2 - ng,
                                jnp.where(lane == 2, w1, jnp.where(lane == 3, w2, 0.0))))
    route_ref[...] = route
    ids_ref[...] = route.T[:ids_ref.shape[0], :]


def _outproj_call(att, ssm_y, x_rows, mod, attn_g, ssm_g, norm2_g, w_out_bf, wr, rb):
    rows, d = x_rows.shape
    ts = min(WIDE_ROW_TILE, rows)
    row = lambda width: pl.BlockSpec((ts, width), lambda i: (i, 0))
    const = lambda a: pl.BlockSpec(a.shape, lambda i: (0,) * a.ndim)
    attn_g = attn_g.reshape(1, ATTN_WIDTH)
    ssm_g = ssm_g.reshape(1, SSM_WIDTH)
    norm2_g = norm2_g.reshape(1, d)
    return pl.pallas_call(
        _outproj_kernel,
        grid=(rows // ts,),
        in_specs=[row(ATTN_WIDTH), row(SSM_WIDTH), row(d),
                  _mod_spec(mod, rows, ts, 2), _mod_spec(mod, rows, ts, 3), _mod_spec(mod, rows, ts, 4),
                  const(attn_g), const(ssm_g), const(norm2_g), const(w_out_bf), const(wr), const(rb)],
        out_specs=[row(d), row(d), row(LANES), pl.BlockSpec((ROUTE_ROWS, ts), lambda i: (0, i))],
        out_shape=[jax.ShapeDtypeStruct((rows, d), F32), jax.ShapeDtypeStruct((rows, d), BF16),
                   jax.ShapeDtypeStruct((rows, LANES), F32),
                   jax.ShapeDtypeStruct((ROUTE_ROWS, rows), F32)],
        compiler_params=_cparams(("arbitrary",)),
        name="outproj_router",
    )(att, ssm_y, x_rows, mod, mod, mod, attn_g, ssm_g, norm2_g, w_out_bf, wr, rb)


def _router_weights(router_g_w, router_g_b, router_e_w, router_e_b):
    d = router_g_w.shape[0]
    ne = N_EXPERT_GROUPS * EXPERTS_PER_GROUP
    w = jnp.concatenate([router_g_w.astype(F32),
                         jnp.transpose(router_e_w.astype(F32), (1, 0, 2)).reshape(d, ne)], axis=1)
    w = jnp.pad(w, ((0, 0), (0, LANES - w.shape[1])))
    hi = w.astype(BF16)
    lo = (w - hi.astype(F32)).astype(BF16)
    b = jnp.concatenate([router_g_b.astype(F32), router_e_b.astype(F32).reshape(ne)])
    b = jnp.pad(b, (0, LANES - b.shape[0])).reshape(1, LANES)
    return jnp.concatenate([hi, lo], axis=1), b


def _moe_kernel(te_ref, x_ref, wg_ref, wu_ref, wd_ref, o_ref, wgb_ref, wub_ref, wdb_ref):
    i = pl.program_id(0)
    changed = jnp.logical_or(i == 0, te_ref[i] != te_ref[jnp.maximum(i - 1, 0)])

    @pl.when(changed)
    def _():
        wgb_ref[...] = wg_ref[0].astype(BF16)
        wub_ref[...] = wu_ref[0].astype(BF16)
        wdb_ref[...] = wd_ref[0].astype(BF16)

    x = x_ref[...]
    gate = jnp.dot(x, wgb_ref[...], preferred_element_type=F32)
    up = jnp.dot(x, wub_ref[...], preferred_element_type=F32)
    a = (gate * _sigmoid(gate)) * up
    o_ref[...] = jnp.dot(a.astype(BF16), wdb_ref[...], preferred_element_type=F32).astype(o_ref.dtype)


def _moe_call(tile_expert, x_sorted, wg, wu, wd, tm):
    n_slots, d = x_sorted.shape
    fe = wg.shape[2]
    grid_spec = pltpu.PrefetchScalarGridSpec(
        num_scalar_prefetch=1,
        grid=(n_slots // tm,),
        in_specs=[pl.BlockSpec((tm, d), lambda i, te: (i, 0)),
                  pl.BlockSpec((1, d, fe), lambda i, te: (te[i], 0, 0)),
                  pl.BlockSpec((1, d, fe), lambda i, te: (te[i], 0, 0)),
                  pl.BlockSpec((1, fe, d), lambda i, te: (te[i], 0, 0))],
        out_specs=pl.BlockSpec((tm, d), lambda i, te: (i, 0)),
        scratch_shapes=[pltpu.VMEM((d, fe), BF16), pltpu.VMEM((d, fe), BF16), pltpu.VMEM((fe, d), BF16)],
    )
    return pl.pallas_call(
        _moe_kernel,
        grid_spec=grid_spec,
        out_shape=jax.ShapeDtypeStruct((n_slots, d), BF16),
        compiler_params=_cparams(("arbitrary",)),
        name="moe_experts",
    )(tile_expert, x_sorted, wg, wu, wd)


def _moe_dispatch(ids_rows, tm):
    n_tok = ids_rows.shape[1]
    ids = jnp.concatenate([ids_rows[0], ids_rows[1]]).astype(jnp.int32)
    n_pairs = ids.shape[0]
    n_slots = (-(-n_pairs // tm) + N_EXPERTS) * tm
    hot = (ids[:, None] == jnp.arange(N_EXPERTS)[None, :]).astype(jnp.int32)
    csum = jnp.cumsum(hot, axis=0)
    rank = jnp.sum((csum - hot) * hot, axis=1)
    counts = csum[-1]
    padded = -(-counts // tm) * tm
    ends = jnp.cumsum(padded)
    starts = ends - padded
    pos = jnp.sum(hot * starts[None, :], axis=1) + rank
    tok_of_slot = (jnp.arange(n_slots, dtype=jnp.int32) % n_tok).at[pos].set(
        jnp.arange(n_pairs, dtype=jnp.int32) % n_tok, unique_indices=True, mode="promise_in_bounds")
    tile_start = jnp.arange(n_slots // tm, dtype=jnp.int32) * tm
    tile_expert = jnp.minimum(jnp.sum((tile_start[:, None] >= ends[None, :]).astype(jnp.int32), axis=1),
                              N_EXPERTS - 1).astype(jnp.int32)
    return pos.reshape(2, n_tok), tok_of_slot, tile_expert


def _take_rows(x, idx):
    return x.at[idx].get(mode="promise_in_bounds")


def _final_kernel(x1_ref, ya_ref, yb_ref, route_ref, g2_ref, fg_ref, o_ref):
    wa = route_ref[:, 2:3]
    wb = route_ref[:, 3:4]
    x = x1_ref[...] + g2_ref[0] * (wa * ya_ref[...].astype(F32) + wb * yb_ref[...].astype(F32))
    o_ref[...] = _rmsnorm(x, fg_ref[...])


def _final_call(x1, ya, yb, route, mod, final_g):
    rows, d = x1.shape
    ts = min(WIDE_ROW_TILE, rows)
    row = pl.BlockSpec((ts, d), lambda i: (i, 0))
    return pl.pallas_call(
        _final_kernel,
        grid=(rows // ts,),
        in_specs=[row, row, row, pl.BlockSpec((ts, LANES), lambda i: (i, 0)), _mod_spec(mod, rows, ts, 5),
                  pl.BlockSpec((1, d), lambda i: (0, 0))],
        out_specs=row,
        out_shape=jax.ShapeDtypeStruct((rows, d), F32),
        compiler_params=_cparams(("arbitrary",)),
        name="final_norm",
    )(x1, ya, yb, route, mod, final_g.reshape(1, d))


def kernel(x_prompt, x_sample, c_prompt, c_sample, cache_k, cache_v, state_ssm_re, state_ssm_im,
           rel_bias, ada_w, ada_b, norm1_g, w_in, ssm_a_re, ssm_a_im, ssm_log_dt, ssm_b_re, ssm_b_im,
           ssm_c_re, ssm_c_im, ssm_d, glu_a, glu_b, attn_out_g, ssm_out_g, w_out, norm2_g,
           router_g_w, router_g_b, router_e_w, router_e_b, w_gate, w_up, w_down, final_norm_g):
    if ada_w.shape[0] != 1:
        raise ValueError("single-layer trunk expected")
    nb, s_len, d = x_prompt.shape
    nd, t_len, _ = x_sample.shape
    if s_len != max(WINDOWS):
        raise ValueError("prompt length must equal the widest window")
    n_p, n_s = nb * s_len, nd * t_len
    if n_p % ROW_TILE or n_s % ROW_TILE:
        raise ValueError("token counts must be multiples of the row tile")

    mod = _mod_call(jnp.concatenate([c_prompt, c_sample], axis=0).astype(F32), ada_w[0], ada_b[0])
    mod_p = mod[:nb].reshape(nb, 1, 6 * d)
    mod_s = jnp.repeat(mod[nb:], t_len, axis=0).reshape(1, n_s, 6 * d)

    w_in_bf = w_in[0].astype(BF16)
    w_out_bf = w_out[0].astype(BF16)
    xp_rows = x_prompt.reshape(n_p, d)
    xs_rows = x_sample.reshape(n_s, d)

    qp, kp, vp, up, kp_t, vp_t = _inproj_call(xp_rows, mod_p, norm1_g[0], w_in_bf, seq_len=s_len)
    qs, ks, vs, us = _inproj_call(xs_rows, mod_s, norm1_g[0], w_in_bf)

    seq = lambda a: a.reshape(nb, s_len, ATTN_WIDTH)
    att_p = _attn_prompt_call(seq(qp), seq(kp), seq(vp), _prompt_bias_tiles(rel_bias))
    dec = lambda a: a.reshape(nd, t_len, ATTN_WIDTH)
    att_s = _attn_decode_call(dec(qs), dec(ks), dec(vs), cache_k[0], cache_v[0], rel_bias)

    s5 = (ssm_a_re[0], ssm_a_im[0], ssm_log_dt[0], ssm_b_re[0], ssm_b_im[0], ssm_c_re[0], ssm_c_im[0])
    t_op, m_op, p_op, sc = _s5_prompt_operators(*s5)
    d_oct = ssm_d[0].astype(F32).reshape(N_OCTETS, 1, LANES)
    ssm_p, hT_p = _ssm_prompt_call(seq(up), t_op, m_op, p_op, sc, d_oct,
                                   _octet_glu(glu_a[0]), _octet_glu(glu_b[0]))
    hT_p = hT_p.reshape(nb, N_OCTETS, 2, OCTET, SSM_STATE)
    ssm_re_p = hT_p[:, :, 0].reshape(nb, SSM_GROUPS, SSM_STATE)
    ssm_im_p = hT_p[:, :, 1].reshape(nb, SSM_GROUPS, SSM_STATE)
    us_tm = jnp.transpose(us.reshape(nd, t_len, SSM_WIDTH), (1, 0, 2))
    ssm_s, hre_s, him_s = _ssm_decode_call(us_tm, state_ssm_re[0], state_ssm_im[0], *s5,
                                           ssm_d[0], glu_a[0], glu_b[0])
    ssm_s = jnp.transpose(ssm_s, (1, 0, 2)).reshape(n_s, SSM_WIDTH)

    wr, rb = _router_weights(router_g_w[0], router_g_b[0], router_e_w[0], router_e_b[0])
    norms = (attn_out_g[0], ssm_out_g[0], norm2_g[0], w_out_bf, wr, rb)
    x1_p, h2_p, route_p, ids_p = _outproj_call(att_p.reshape(n_p, ATTN_WIDTH), ssm_p.reshape(n_p, SSM_WIDTH),
                                        xp_rows, mod_p, *norms)
    x1_s, h2_s, route_s, ids_s = _outproj_call(att_s.reshape(n_s, ATTN_WIDTH), ssm_s, xs_rows, mod_s, *norms)

    ne = N_EXPERTS
    wg = w_gate[0].reshape(ne, d, D_EXPERT)
    wu = w_up[0].reshape(ne, d, D_EXPERT)
    wd = w_down[0].reshape(ne, D_EXPERT, d)

    def experts(x1, h2, route, ids_rows, mod_rows, tm):
        pos, tok_of_slot, tile_expert = _moe_dispatch(ids_rows, tm)
        y_slots = _moe_call(tile_expert, _take_rows(h2, tok_of_slot), wg, wu, wd, tm)
        return _final_call(x1, _take_rows(y_slots, pos[0]), _take_rows(y_slots, pos[1]),
                           route, mod_rows, final_norm_g)

    y_p = experts(x1_p, h2_p, route_p, ids_p, mod_p, MOE_TILE)
    y_s = experts(x1_s, h2_s, route_s, ids_s, mod_s, MOE_TILE_DECODE)

    heads = (ATTN_HEADS, HEAD_DIM)
    cache_out = lambda a: jnp.transpose(a.reshape((1, nb) + heads + (s_len,)), (0, 1, 4, 2, 3))
    return (y_p.reshape(nb, s_len, d), y_s.reshape(nd, t_len, d),
            cache_out(kp_t), cache_out(vp_t),
            ks.reshape((1, nd, t_len) + heads), vs.reshape((1, nd, t_len) + heads),
            ssm_re_p[None], ssm_im_p[None],
            hre_s.reshape(1, nd, SSM_GROUPS, SSM_STATE), him_s.reshape(1, nd, SSM_GROUPS, SSM_STATE))
```

```python
import functools
import math

import numpy as np

import jax
import jax.numpy as jnp
from jax import lax
from jax.experimental import pallas as pl
from jax.experimental.pallas import tpu as pltpu

F32 = jnp.float32
BF16 = jnp.bfloat16

D_MODEL = 1024
HEAD_DIM = 64
ATTN_WIDTH = 512
ATTN_HEADS = 8
SSM_WIDTH = 512
SSM_GROUP_CH = 16
SSM_GROUPS = 32
SSM_STATE = 64
WINDOWS = (128, 512, 2048)
DILATIONS = (1, 4, 16)
WINDOW_STEPS = 128
N_BUCKETS = 32
MAX_EXACT = 16
BUCKET_MAX_DIST = 2048
N_EXPERT_GROUPS = 4
EXPERTS_PER_GROUP = 4
N_EXPERTS = N_EXPERT_GROUPS * EXPERTS_PER_GROUP
D_EXPERT = 512
NORM_EPS = 1e-6

LANES = 128
Q_ROWS = 16
OCTET = LANES // SSM_GROUP_CH
N_OCTETS = SSM_GROUPS // OCTET
OCT_STATE = OCTET * SSM_STATE
SSM_CHUNK = 16
SSM_SEQS = 2
ROW_TILE = 512
WIDE_ROW_TILE = 1024
MOE_TILE = 512
MOE_TILE_DECODE = 128
ROUTE_ROWS = 8
ATTN_GROUP = 4
ATTN_MXU_UNROLL = 8
VMEM_LIMIT = 56 * 1024 * 1024

_NEG_INF = float("-inf")
_HIGHEST = lax.Precision.HIGHEST


def _cparams(sem):
    return pltpu.CompilerParams(dimension_semantics=sem, vmem_limit_bytes=VMEM_LIMIT)


def _rmsnorm(x, g):
    return x * lax.rsqrt(jnp.mean(x * x, axis=-1, keepdims=True) + NORM_EPS) * g


def _gelu_tanh(x):
    c = math.sqrt(2.0 / math.pi)
    return 0.5 * x * (1.0 + jnp.tanh(c * (x + 0.044715 * (x * x * x))))


def _sigmoid(x):
    return 1.0 / (1.0 + jnp.exp(-x))


def _mod_kernel(c_ref, w_ref, b_ref, o_ref):
    c = c_ref[...]
    a = (c * _sigmoid(c)).astype(BF16)
    o_ref[...] = jnp.dot(a, w_ref[...].astype(BF16), preferred_element_type=F32) + b_ref[...]


def _mod_call(c_all, ada_w, ada_b):
    rows, d = c_all.shape
    n_out = ada_w.shape[1]
    tn = 1024
    return pl.pallas_call(
        _mod_kernel,
        grid=(n_out // tn,),
        in_specs=[pl.BlockSpec((rows, d), lambda j: (0, 0)),
                  pl.BlockSpec((d, tn), lambda j: (0, j)),
                  pl.BlockSpec((1, tn), lambda j: (0, j))],
        out_specs=pl.BlockSpec((rows, tn), lambda j: (0, j)),
        out_shape=jax.ShapeDtypeStruct((rows, n_out), F32),
        compiler_params=_cparams(("arbitrary",)),
        name="adaln_mod",
    )(c_all, ada_w, ada_b.reshape(1, n_out))


def _mod_spec(mod, rows, ts, chunk):
    if mod.shape[1] == 1:
        tiles_per_group = (rows // mod.shape[0]) // ts
        return pl.BlockSpec((1, 1, D_MODEL), lambda i: (i // tiles_per_group, 0, chunk))
    return pl.BlockSpec((1, ts, D_MODEL), lambda i: (0, i, chunk))


def _inproj_kernel(x_ref, sh_ref, sc_ref, g_ref, w_ref, *rest, key_major):
    h = _rmsnorm(x_ref[...], g_ref[...]) * (1.0 + sc_ref[0]) + sh_ref[0]
    hb = h.astype(BF16)
    z = jnp.dot(hb, w_ref[...], preferred_element_type=F32)
    aw = ATTN_WIDTH
    if key_major:
        q_ref, k_ref, v_ref, u_ref, kt_ref, vt_ref = rest
        kt_ref[0] = z[:, aw:2 * aw].T
        vt_ref[0] = z[:, 2 * aw:3 * aw].T
    else:
        q_ref, k_ref, v_ref, u_ref = rest
    q_ref[...] = z[:, :aw]
    k_ref[...] = z[:, aw:2 * aw]
    v_ref[...] = z[:, 2 * aw:3 * aw]
    u_ref[...] = z[:, 3 * aw:]


def _inproj_call(x_rows, mod, norm_g, w_in_bf, seq_len=None):
    rows, d = x_rows.shape
    ts = min(ROW_TILE, rows)
    proj = w_in_bf.shape[1]
    out = jax.ShapeDtypeStruct((rows, ATTN_WIDTH), F32)
    ospec = pl.BlockSpec((ts, ATTN_WIDTH), lambda i: (i, 0))
    in_specs = [pl.BlockSpec((ts, d), lambda i: (i, 0)),
                _mod_spec(mod, rows, ts, 0),
                _mod_spec(mod, rows, ts, 1),
                pl.BlockSpec((1, d), lambda i: (0, 0)),
                pl.BlockSpec((d, proj), lambda i: (0, 0))]
    args = [x_rows, mod, mod, norm_g.reshape(1, d), w_in_bf]
    out_specs = [ospec, ospec, ospec, ospec]
    out_shape = [out, out, out, out]
    if seq_len is not None:
        tiles = seq_len // ts
        tspec = pl.BlockSpec((1, ATTN_WIDTH, ts), lambda i: (i // tiles, 0, i % tiles))
        out_specs += [tspec, tspec]
        out_shape += [jax.ShapeDtypeStruct((rows // seq_len, ATTN_WIDTH, seq_len), F32)] * 2
    return pl.pallas_call(
        functools.partial(_inproj_kernel, key_major=seq_len is not None),
        grid=(rows // ts,),
        in_specs=in_specs,
        out_specs=out_specs,
        out_shape=out_shape,
        compiler_params=_cparams(("arbitrary",)),
        name="inproj",
    )(*args)


def _t5_bucket(dist):
    d = jnp.maximum(dist, MAX_EXACT).astype(F32)
    log_part = MAX_EXACT + (jnp.log(d / MAX_EXACT) / math.log(BUCKET_MAX_DIST / MAX_EXACT)
                            * (N_BUCKETS - MAX_EXACT)).astype(jnp.int32)
    return jnp.where(dist < MAX_EXACT, dist, jnp.minimum(log_part, N_BUCKETS - 1))


def _bias_by_distance(rel_bias, dists):
    hot = (_t5_bucket(jnp.asarray(dists, jnp.int32))[:, None]
           == jnp.arange(N_BUCKETS, dtype=jnp.int32)[None, :]).astype(F32)
    return jnp.dot(hot, rel_bias.astype(F32), precision=_HIGHEST)


def _prompt_bias_tiles(rel_bias):
    steps = WINDOW_STEPS
    period = 3 * steps
    tiles = []
    for r in DILATIONS:
        vec = _bias_by_distance(rel_bias, r * np.arange(steps + 1))
        fill = jnp.full((steps - 1, ATTN_HEADS), _NEG_INF, F32)
        w = jnp.concatenate([fill, vec[::-1], fill, fill[:1]], axis=0)
        rep = jnp.tile(w.T, (1, steps))[:, :steps * (period - 1)]
        toe = rep.reshape(ATTN_HEADS, steps, period - 1)[:, :, steps - 1:]
        toe = toe.reshape(ATTN_HEADS // 2, 2, steps, 2 * steps)
        own_only = jnp.where(jnp.arange(2 * steps) < steps, _NEG_INF, toe)
        tiles.append(jnp.stack([toe, own_only], axis=2))
    return jnp.stack(tiles)


def _attn_prompt_kernel(q_ref, k_ref, v_ref, bias_ref, o_ref,
                        p4_ref, qh_ref, kb_ref, vb_ref, s_ref, p_ref, res_ref, stage_ref, nat_ref):
    s_len = q_ref.shape[1]
    steps = WINDOW_STEPS
    n_tiles = s_len // steps
    quarter = s_len // 4
    nt = (((1,), (1,)), ((), ()))
    lane = lax.broadcasted_iota(jnp.int32, (1, LANES), 1)
    first_head = lane < HEAD_DIM
    srcs = (q_ref, k_ref, v_ref)

    kb_ref[0:steps, :] = jnp.zeros((steps, LANES), BF16)
    vb_ref[0:steps, 0:LANES] = jnp.zeros((steps, LANES), BF16)
    vb_ref[:, LANES:] = jnp.ones((s_len + steps, LANES), BF16)
    for x in range(3):
        for sigma in range(4):
            p4_ref[x, sigma * quarter:(sigma + 1) * quarter, :] = srcs[x][0, pl.ds(sigma, quarter, stride=4), :]

    def source(branch, x, tile_idx):
        rows = slice(tile_idx * steps, (tile_idx + 1) * steps)
        if branch == 0:
            return srcs[x][0, rows, :]
        if branch == 1:
            return p4_ref[x, rows, :]
        sigma, tau = tile_idx % 4, tile_idx // 4
        return p4_ref[x, pl.ds(sigma * quarter + tau, steps, stride=4), :]

    for branch, r in enumerate(DILATIONS):
        blocks_per_class = (s_len // r) // steps
        width = 2 * steps if blocks_per_class > 1 else steps

        for t in range(n_tiles):
            rows = slice(t * steps, (t + 1) * steps)
            q2 = source(branch, 0, t) * (HEAD_DIM ** -0.5)
            qh_ref[0, rows, :] = jnp.where(first_head, q2, 0.0).astype(BF16)
            qh_ref[1, rows, :] = jnp.where(first_head, 0.0, q2).astype(BF16)
            kb_ref[steps + t * steps:steps + (t + 1) * steps, :] = source(branch, 1, t).astype(BF16)
            vb_ref[steps + t * steps:steps + (t + 1) * steps, 0:LANES] = source(branch, 2, t).astype(BF16)

        def aligned(start):
            return start if isinstance(start, int) else pl.multiple_of(start, steps)

        def tile_rows(t):
            return pl.ds(aligned(t * steps), steps)

        def key_rows(t, width=width):
            start = t * steps if width == 2 * steps else (t + 1) * steps
            return pl.ds(aligned(start), width)

        def scores(t, branch=branch, width=width, blocks_per_class=blocks_per_class, key_rows=key_rows):
            rows = tile_rows(t)
            keys = kb_ref[key_rows(t), :]
            first = jnp.where(t % blocks_per_class == 0, 1, 0)
            for hh in range(2):
                if width == 2 * steps:
                    bias = bias_ref[branch, 0, hh, pl.ds(first, 1), :, :][0]
                else:
                    bias = bias_ref[branch, 0, hh, 0, :, steps:]
                sc = lax.dot_general(qh_ref[hh, rows, :], keys, nt, preferred_element_type=F32)
                s_ref[hh, rows, 0:width] = sc + bias

        def softmax(t, width=width):
            rows = tile_rows(t)
            for hh in range(2):
                sc = s_ref[hh, rows, 0:width]
                m = jnp.max(sc, axis=1, keepdims=True)
                p_ref[hh, rows, 0:width] = jnp.exp(sc - m).astype(BF16)
                res_ref[1, rows, hh * HEAD_DIM:(hh + 1) * HEAD_DIM] = jnp.broadcast_to(m, (steps, HEAD_DIM))

        def weighted(t, width=width, key_rows=key_rows):
            rows = tile_rows(t)
            vals = vb_ref[key_rows(t), :]
            r0 = jnp.dot(p_ref[0, rows, 0:width], vals, preferred_element_type=F32)
            r1 = jnp.dot(p_ref[1, rows, 0:width], vals, preferred_element_type=F32)
            res_ref[0, rows, :] = jnp.where(first_head, r0[:, :LANES], r1[:, :LANES])
            res_ref[2, rows, :] = jnp.where(first_head, r0[:, LANES:], r1[:, LANES:])

        def stage(g_scores, g_softmax, scores=scores, softmax=softmax):
            for fn, g in ((softmax, g_softmax), (scores, g_scores)):
                if g is not None:
                    for j in range(ATTN_GROUP):
                        fn(g * ATTN_GROUP + j)

        n_groups = n_tiles // ATTN_GROUP
        stage(0, None)

        def steady(g, carry, stage=stage):
            stage(g, g - 1)
            return carry

        lax.fori_loop(1, n_groups, steady, 0)
        stage(None, n_groups - 1)

        def weighted_pass(t, carry, weighted=weighted):
            weighted(t)
            return carry

        lax.fori_loop(0, n_tiles, weighted_pass, 0, unroll=ATTN_MXU_UNROLL)

        for kind in range(3):
            if branch == 0:
                nat_ref[0, kind] = res_ref[kind]
                continue
            src = res_ref
            if branch == 2:
                for t in range(n_tiles):
                    sigma, tau = t % 4, t // 4
                    stage_ref[kind, pl.ds(sigma * quarter + tau, steps, stride=4), :] = (
                        res_ref[kind, t * steps:(t + 1) * steps, :])
                src = stage_ref
            for sigma in range(4):
                nat_ref[branch, kind, pl.ds(sigma, quarter, stride=4), :] = (
                    src[kind, sigma * quarter:(sigma + 1) * quarter, :])

    def merge(i, carry):
        rows = pl.ds(pl.multiple_of(i * 256, 256), 256)
        m0, m1, m2 = nat_ref[0, 1, rows, :], nat_ref[1, 1, rows, :], nat_ref[2, 1, rows, :]
        m_all = jnp.maximum(jnp.maximum(m0, m1), m2)
        w0, w1, w2 = jnp.exp(m0 - m_all), jnp.exp(m1 - m_all), jnp.exp(m2 - m_all)
        num = w0 * nat_ref[0, 0, rows, :] + w1 * nat_ref[1, 0, rows, :] + w2 * nat_ref[2, 0, rows, :]
        den = w0 * nat_ref[0, 2, rows, :] + w1 * nat_ref[1, 2, rows, :] + w2 * nat_ref[2, 2, rows, :]
        o_ref[0, rows, :] = num / den
        return carry

    lax.fori_loop(0, s_len // 256, merge, 0)


def _attn_prompt_call(q, k, v, bias_tiles):
    n, s, _ = q.shape
    pairs = ATTN_HEADS // 2
    steps = WINDOW_STEPS
    qspec = pl.BlockSpec((1, s, LANES), lambda g, n_: (n_, 0, g))
    return pl.pallas_call(
        _attn_prompt_kernel,
        grid=(pairs, n),
        in_specs=[qspec, qspec, qspec,
                  pl.BlockSpec((3, 1, 2, 2, steps, 2 * steps), lambda g, n_: (0, g, 0, 0, 0, 0))],
        out_specs=qspec,
        out_shape=jax.ShapeDtypeStruct((n, s, ATTN_WIDTH), F32),
        scratch_shapes=[pltpu.VMEM((3, s, LANES), F32),
                        pltpu.VMEM((2, s, LANES), BF16),
                        pltpu.VMEM((s + steps, LANES), BF16),
                        pltpu.VMEM((s + steps, 2 * LANES), BF16),
                        pltpu.VMEM((2, s, 2 * steps), F32),
                        pltpu.VMEM((2, s, 2 * steps), BF16),
                        pltpu.VMEM((3, s, LANES), F32), pltpu.VMEM((3, s, LANES), F32),
                        pltpu.VMEM((3, 3, s, LANES), F32)],
        compiler_params=_cparams(("arbitrary", "arbitrary")),
        name="attn_prompt",
    )(q, k, v, bias_tiles)


def _s5_discretise(a_re, a_im, log_dt, b_re, b_im):
    lam_re = jnp.minimum(a_re.astype(F32), -1e-4)
    lam_im = a_im.astype(F32)
    dt = jnp.exp(log_dt.astype(F32))[:, None]
    mag = jnp.exp(lam_re * dt)
    ph = lam_im * dt
    abar_re, abar_im = mag * jnp.cos(ph), mag * jnp.sin(ph)
    nr, ni = abar_re - 1.0, abar_im
    den = lam_re * lam_re + lam_im * lam_im
    coef_re = (nr * lam_re + ni * lam_im) / den
    coef_im = (ni * lam_re - nr * lam_im) / den
    br, bi = b_re.astype(F32), b_im.astype(F32)
    bbar_re = coef_re[..., None] * br - coef_im[..., None] * bi
    bbar_im = coef_re[..., None] * bi + coef_im[..., None] * br
    return lam_re * dt, ph, abar_re, abar_im, bbar_re, bbar_im


def _abar_power(log_mag, ph, n):
    nf = jnp.asarray(n, F32)[:, None, None]
    mag = jnp.exp(nf * log_mag[None])
    return mag * jnp.cos(nf * ph[None]), mag * jnp.sin(nf * ph[None])


def _s5_prompt_operators(a_re, a_im, log_dt, b_re, b_im, c_re, c_im):
    L = SSM_CHUNK
    log_mag, ph, _, _, bb_re, bb_im = _s5_discretise(a_re, a_im, log_dt, b_re, b_im)
    cr, ci = c_re.astype(F32), c_im.astype(F32)
    pw_re, pw_im = _abar_power(log_mag, ph, np.arange(L + 1))
    eye = jnp.eye(OCTET, dtype=F32)

    ab_re = pw_re[:L, :, :, None] * bb_re[None] - pw_im[:L, :, :, None] * bb_im[None]
    ab_im = pw_re[:L, :, :, None] * bb_im[None] + pw_im[:L, :, :, None] * bb_re[None]
    lag = (jnp.einsum('gop,lgpi->lgoi', cr, ab_re, precision=_HIGHEST)
           - jnp.einsum('gop,lgpi->lgoi', ci, ab_im, precision=_HIGHEST))
    lag = lag.reshape(L, N_OCTETS, OCTET, SSM_GROUP_CH, SSM_GROUP_CH)
    bd = jnp.einsum('logci,gh->olgihc', lag, eye).reshape(N_OCTETS, L, LANES, LANES).astype(BF16)
    stack = bd[:, ::-1].reshape(N_OCTETS, L * LANES, LANES)
    shifted = jnp.concatenate([stack[:, LANES:], jnp.zeros((N_OCTETS, LANES, LANES), BF16)], axis=1)
    t_op = jnp.concatenate([shifted, stack], axis=-1)

    group_of_lane = jnp.arange(LANES) // SSM_GROUP_CH
    parts = []
    for part in (ab_re[::-1], ab_im[::-1]):
        x = part.reshape(L, N_OCTETS, OCTET, SSM_STATE, SSM_GROUP_CH)
        parts.append(jnp.transpose(x, (1, 0, 3, 2, 4)).reshape(N_OCTETS, L, SSM_STATE, LANES))
    x = jnp.stack(parts, axis=2)
    own = (jnp.arange(OCTET)[:, None, None] == group_of_lane[None, None, :])
    mt_op = jnp.where(own, x[:, :, :, None], 0.0)
    mt_op = mt_op.reshape(N_OCTETS, L, 2 * OCT_STATE, LANES)

    p1_re, p1_im = pw_re[1:], pw_im[1:]
    on_re = cr[None] * p1_re[:, :, None, :] - ci[None] * p1_im[:, :, None, :]
    on_im = -cr[None] * p1_im[:, :, None, :] - ci[None] * p1_re[:, :, None, :]
    parts = []
    for part in (on_re, on_im):
        y = part.reshape(L, N_OCTETS, OCTET, SSM_GROUP_CH, SSM_STATE)
        parts.append(jnp.transpose(y, (1, 0, 3, 2, 4)).reshape(N_OCTETS, L, SSM_GROUP_CH, OCT_STATE))
    y = jnp.concatenate(parts, axis=-1)
    group_of_state = (jnp.arange(2 * OCT_STATE) % OCT_STATE) // SSM_STATE
    own = (jnp.arange(OCTET)[:, None, None] == group_of_state[None, None, :])
    pt_op = jnp.where(own, y[:, :, None], 0.0)
    pt_op = pt_op.reshape(N_OCTETS, L * LANES, 2 * OCT_STATE)

    n_steps = 8
    sc_re, sc_im = _abar_power(log_mag, ph, L * (2 ** np.arange(n_steps)))
    sc = jnp.concatenate([sc_re.reshape(n_steps, N_OCTETS, OCT_STATE),
                          sc_im.reshape(n_steps, N_OCTETS, OCT_STATE)], axis=-1)
    sc = jnp.transpose(sc, (1, 0, 2))
    return t_op, mt_op.astype(BF16), pt_op.astype(BF16), sc


def _octet_glu(glu):
    eye = jnp.eye(OCTET, dtype=F32)
    x = glu.astype(F32).reshape(N_OCTETS, OCTET, SSM_GROUP_CH, SSM_GROUP_CH)
    return jnp.einsum('ogce,gh->ogche', x, eye).reshape(N_OCTETS, LANES, LANES).astype(BF16)


def _ssm_prompt_kernel(u_ref, t_ref, mb_ref, pt_ref, sc_ref, d_ref, ga_ref, gb_ref,
                       y_ref, h_ref, uf_ref, ub_ref, st_ref, mt_ref):
    L = SSM_CHUNK
    nt = (((1,), (1,)), ((), ()))

    @pl.when(pl.program_id(1) == 0)
    def _():
        for s in range(L):
            mt_ref[:, s * LANES:(s + 1) * LANES] = mb_ref[0, s]

    n_seq, s_len = u_ref.shape[0], u_ref.shape[1]
    n_chunks = s_len // L
    quarter = s_len // 4
    rows = n_seq * n_chunks

    def staged(sq, step):
        sigma, tau = step % 4, step // 4
        return pl.ds(sq * s_len + sigma * quarter + tau, n_chunks, stride=4)

    for sq in range(n_seq):
        for sigma in range(4):
            st_ref[sq * s_len + sigma * quarter:sq * s_len + (sigma + 1) * quarter, :] = (
                u_ref[sq, pl.ds(sigma, quarter, stride=4), :])
        for step in range(L):
            blk = st_ref[staged(sq, step), :]
            uf_ref[sq * n_chunks:(sq + 1) * n_chunks, step * LANES:(step + 1) * LANES] = blk
            ub_ref[sq * n_chunks:(sq + 1) * n_chunks, step * LANES:(step + 1) * LANES] = blk.astype(BF16)
    ub = ub_ref[...]

    x = lax.dot_general(ub, mt_ref[...], nt, preferred_element_type=F32)
    chunk = lax.broadcasted_iota(jnp.int32, (rows, 1), 0) % n_chunks
    half = OCT_STATE
    k = 1
    step = 0
    while k < n_chunks:
        a_re = sc_ref[0, step:step + 1, :half]
        a_im = sc_ref[0, step:step + 1, half:]
        sh = jnp.where(chunk >= k, pltpu.roll(x, k, axis=0), 0.0)
        s_re, s_im = sh[:, :half], sh[:, half:]
        x = x + jnp.concatenate([a_re * s_re - a_im * s_im, a_re * s_im + a_im * s_re], axis=1)
        k *= 2
        step += 1
    for sq in range(n_seq):
        h_ref[sq, 0] = x[(sq + 1) * n_chunks - 1:(sq + 1) * n_chunks, :]
    h_start = jnp.where(chunk >= 1, pltpu.roll(x, 1, axis=0), 0.0)

    hb = h_start.astype(BF16)
    d = d_ref[0]
    ga = ga_ref[0]
    gb = gb_ref[0]
    for t in range(0, L, 2):
        pair = slice(t * LANES, (t + 2) * LANES)
        y2 = (jnp.dot(ub_ref[:, :(t + 2) * LANES], t_ref[0, (L - 2 - t) * LANES:, :],
                      preferred_element_type=F32)
              + lax.dot_general(hb, pt_ref[0, pair, :], nt, preferred_element_type=F32))
        for j in range(2):
            lanes = slice((t + j) * LANES, (t + j + 1) * LANES)
            g = _gelu_tanh(y2[:, j * LANES:(j + 1) * LANES] + d * uf_ref[:, lanes]).astype(BF16)
            out = (jnp.dot(g, ga, preferred_element_type=F32)
                   * _sigmoid(jnp.dot(g, gb, preferred_element_type=F32)))
            for sq in range(n_seq):
                st_ref[staged(sq, t + j), :] = out[sq * n_chunks:(sq + 1) * n_chunks, :]
    for sq in range(n_seq):
        for sigma in range(4):
            y_ref[sq, pl.ds(sigma, quarter, stride=4), :] = (
                st_ref[sq * s_len + sigma * quarter:sq * s_len + (sigma + 1) * quarter, :])


def _ssm_prompt_call(u, t_op, m_op, p_op, sc, d_oct, ga, gb):
    n, s, _ = u.shape
    L = SSM_CHUNK
    nq = math.gcd(SSM_SEQS, n)
    rows = nq * (s // L)
    wide = L * LANES
    wspec = lambda shape: pl.BlockSpec((1,) + shape, lambda o, n_: (o, 0, 0))
    return pl.pallas_call(
        _ssm_prompt_kernel,
        grid=(N_OCTETS, n // nq),
        in_specs=[pl.BlockSpec((nq, s, LANES), lambda o, n_: (n_, 0, o)),
                  wspec((wide, 2 * LANES)),
                  pl.BlockSpec((1, L, 2 * OCT_STATE, LANES), lambda o, n_: (o, 0, 0, 0)),
                  wspec((wide, 2 * OCT_STATE)),
                  wspec((8, 2 * OCT_STATE)), wspec((1, LANES)),
                  wspec((LANES, LANES)), wspec((LANES, LANES))],
        out_specs=[pl.BlockSpec((nq, s, LANES), lambda o, n_: (n_, 0, o)),
                   pl.BlockSpec((nq, 1, 1, 2 * OCT_STATE), lambda o, n_: (n_, o, 0, 0))],
        out_shape=[jax.ShapeDtypeStruct((n, s, SSM_WIDTH), F32),
                   jax.ShapeDtypeStruct((n, N_OCTETS, 1, 2 * OCT_STATE), F32)],
        scratch_shapes=[pltpu.VMEM((rows, wide), F32),
                        pltpu.VMEM((rows, wide), BF16),
                        pltpu.VMEM((nq * s, LANES), F32),
                        pltpu.VMEM((2 * OCT_STATE, wide), BF16)],
        compiler_params=_cparams(("arbitrary", "arbitrary")),
        name="ssm_prompt",
    )(u, t_op, m_op, p_op, sc, d_oct, ga, gb)


def _attn_decode_kernel(q_ref, kn_ref, vn_ref, kt_ref, vt_ref, b_ref, mult_ref, o_ref):
    nt = (((1,), (1,)), ((), ()))
    mult = mult_ref[...]
    for h in range(ATTN_HEADS):
        q = q_ref[0, h]
        kt = jnp.concatenate([kt_ref[0, h].astype(BF16), kn_ref[0, h]], axis=1)
        vt = jnp.concatenate([vt_ref[0, h].astype(BF16), vn_ref[0, h]], axis=1)
        s = jnp.dot(q, kt, preferred_element_type=F32) + b_ref[h]
        m = jnp.max(s, axis=1, keepdims=True)
        p = jnp.exp(s - m) * mult
        den = jnp.sum(p, axis=1, keepdims=True)
        o = lax.dot_general(p.astype(BF16), vt, nt, preferred_element_type=F32)
        o_ref[0, h] = o / den


def _decode_tables(rel_bias, t_len, w_rows):
    t = np.arange(t_len)[:, None]
    dist = np.concatenate([w_rows + t - np.arange(w_rows)[None, :],
                           t - np.arange(LANES)[None, :]], axis=1)
    mult = np.zeros(dist.shape, np.float32)
    for w, r in zip(WINDOWS, DILATIONS):
        mult += (dist >= 0) & (dist % r == 0) & (dist <= w)
    mult = np.concatenate([mult, np.zeros((Q_ROWS - t_len, dist.shape[1]), np.float32)], axis=0)
    mult[t_len:, 0] = 1.0
    by_dist = _bias_by_distance(rel_bias, np.arange(w_rows + t_len))
    rows = []
    for ti in range(t_len):
        cache_part = by_dist[ti + 1:w_rows + ti + 1][::-1]
        new_part = by_dist[:ti + 1][::-1]
        pad = jnp.zeros((LANES - ti - 1, ATTN_HEADS), F32)
        rows.append(jnp.concatenate([cache_part, new_part, pad], axis=0))
    bias = jnp.stack(rows + [jnp.zeros_like(rows[0])] * (Q_ROWS - t_len), axis=0)
    bias = jnp.transpose(bias, (2, 0, 1))
    bias = jnp.where(jnp.asarray(mult)[None] > 0, bias, _NEG_INF)
    return bias, jnp.asarray(mult)


def _attn_decode_call(q, k_new, v_new, cache_k, cache_v, rel_bias):
    n, t_len, w = q.shape
    w_rows = cache_k.shape[1]
    if t_len > min(DILATIONS[1:]) or t_len > Q_ROWS or w_rows < max(WINDOWS):
        raise ValueError("unsupported decode shape")
    heads = (ATTN_HEADS, HEAD_DIM)

    def head_major(a, pad_to):
        a = jnp.transpose(a.reshape((n, t_len) + heads), (0, 2, 1, 3))
        return jnp.pad(a, ((0, 0), (0, 0), (0, pad_to - t_len), (0, 0)))

    qh = head_major(q * (HEAD_DIM ** -0.5), Q_ROWS).astype(BF16)
    knt = jnp.swapaxes(head_major(k_new, LANES), 2, 3).astype(BF16)
    vnt = jnp.swapaxes(head_major(v_new, LANES), 2, 3).astype(BF16)
    kt = jnp.transpose(cache_k.astype(F32), (0, 2, 3, 1))
    vt = jnp.transpose(cache_v.astype(F32), (0, 2, 3, 1))
    bias, mult = _decode_tables(rel_bias, t_len, w_rows)
    keys = w_rows + LANES
    per_seq = lambda shape: pl.BlockSpec((1,) + shape, lambda i: (i, 0, 0, 0))
    out = pl.pallas_call(
        _attn_decode_kernel,
        grid=(n,),
        in_specs=[per_seq((ATTN_HEADS, Q_ROWS, HEAD_DIM)),
                  per_seq((ATTN_HEADS, HEAD_DIM, LANES)), per_seq((ATTN_HEADS, HEAD_DIM, LANES)),
                  per_seq((ATTN_HEADS, HEAD_DIM, w_rows)), per_seq((ATTN_HEADS, HEAD_DIM, w_rows)),
                  pl.BlockSpec((ATTN_HEADS, Q_ROWS, keys), lambda i: (0, 0, 0)),
                  pl.BlockSpec((Q_ROWS, keys), lambda i: (0, 0))],
        out_specs=per_seq((ATTN_HEADS, Q_ROWS, HEAD_DIM)),
        out_shape=jax.ShapeDtypeStruct((n, ATTN_HEADS, Q_ROWS, HEAD_DIM), F32),
        compiler_params=_cparams(("arbitrary",)),
        name="attn_decode",
    )(qh, knt, vnt, kt, vt, bias, mult)
    return jnp.transpose(out[:, :, :t_len], (0, 2, 1, 3)).reshape(n, t_len, w)


def _ssm_decode_kernel(u_ref, hre_ref, him_ref, are_ref, aim_ref, bre_ref, bim_ref,
                       cre_ref, cim_ref, d_ref, ga_ref, gb_ref, y_ref, ore_ref, oim_ref, *, t_len):
    h_re, h_im = hre_ref[...], him_ref[...]
    a_re, a_im = are_ref[...], aim_ref[...]
    for t in range(t_len):
        u = u_ref[t]
        ub = u.astype(BF16)
        n_re = a_re * h_re - a_im * h_im + jnp.dot(ub, bre_ref[...], preferred_element_type=F32)
        n_im = a_re * h_im + a_im * h_re + jnp.dot(ub, bim_ref[...], preferred_element_type=F32)
        h_re, h_im = n_re, n_im
        y = (jnp.dot(h_re.astype(BF16), cre_ref[...], preferred_element_type=F32)
             - jnp.dot(h_im.astype(BF16), cim_ref[...], preferred_element_type=F32)
             + d_ref[...] * u)
        g = _gelu_tanh(y).astype(BF16)
        y_ref[t] = (jnp.dot(g, ga_ref[...], preferred_element_type=F32)
                    * _sigmoid(jnp.dot(g, gb_ref[...], preferred_element_type=F32)))
    ore_ref[...] = h_re
    oim_ref[...] = h_im


def _group_blockdiag(x):
    g, a, b = x.shape
    return jnp.einsum('gab,gh->gahb', x, jnp.eye(g, dtype=x.dtype)).reshape(g * a, g * b)


def _ssm_decode_call(u_tm, h0_re, h0_im, a_re, a_im, log_dt, b_re, b_im, c_re, c_im,
                     d_skip, glu_a, glu_b):
    n = h0_re.shape[0]
    _, _, abar_re, abar_im, bb_re, bb_im = _s5_discretise(a_re, a_im, log_dt, b_re, b_im)
    state = SSM_GROUPS * SSM_STATE
    t_len = u_tm.shape[0]
    args = (u_tm, h0_re.reshape(n, state).astype(F32), h0_im.reshape(n, state).astype(F32),
            abar_re.reshape(1, state), abar_im.reshape(1, state),
            _group_blockdiag(jnp.transpose(bb_re, (0, 2, 1))).astype(BF16),
            _group_blockdiag(jnp.transpose(bb_im, (0, 2, 1))).astype(BF16),
            _group_blockdiag(jnp.transpose(c_re.astype(F32), (0, 2, 1))).astype(BF16),
            _group_blockdiag(jnp.transpose(c_im.astype(F32), (0, 2, 1))).astype(BF16),
            d_skip.astype(F32).reshape(1, SSM_WIDTH),
            _group_blockdiag(glu_a.astype(F32)).astype(BF16),
            _group_blockdiag(glu_b.astype(F32)).astype(BF16))
    full = lambda a: pl.BlockSpec(a.shape, lambda i: (0,) * a.ndim)
    out_shape = [jax.ShapeDtypeStruct(u_tm.shape, F32),
                 jax.ShapeDtypeStruct((n, state), F32), jax.ShapeDtypeStruct((n, state), F32)]
    return pl.pallas_call(
        functools.partial(_ssm_decode_kernel, t_len=t_len),
        grid=(1,),
        in_specs=[full(a) for a in args],
        out_specs=[full(o) for o in out_shape],
        out_shape=out_shape,
        compiler_params=_cparams(("arbitrary",)),
        name="ssm_decode",
    )(*args)


def _outproj_kernel(att_ref, ssm_ref, x_ref, g1_ref, sh2_ref, sc2_ref, ag_ref, sg_ref, n2_ref,
                    wo_ref, wr_ref, rb_ref, x1_ref, h2_ref, route_ref, ids_ref):
    mixed = jnp.concatenate([_rmsnorm(att_ref[...], ag_ref[...]), _rmsnorm(ssm_ref[...], sg_ref[...])],
                            axis=1).astype(BF16)
    x1 = x_ref[...] + g1_ref[0] * jnp.dot(mixed, wo_ref[...], preferred_element_type=F32)
    x1_ref[...] = x1
    h2 = _rmsnorm(x1, n2_ref[...]) * (1.0 + sc2_ref[0]) + sh2_ref[0]
    hi = h2.astype(BF16)
    h2_ref[...] = hi
    lo = (h2 - hi.astype(F32)).astype(BF16)
    r1 = jnp.dot(hi, wr_ref[...], preferred_element_type=F32)
    r2 = jnp.dot(lo, wr_ref[:, :LANES], preferred_element_type=F32)
    logits = r1[:, :LANES] + r1[:, LANES:] + r2 + rb_ref[...]

    lane = lax.broadcasted_iota(jnp.int32, (1, LANES), 1)
    lane_f = lane.astype(F32)
    big = float(LANES)
    ng, epg = N_EXPERT_GROUPS, EXPERTS_PER_GROUP
    lg = jnp.where(lane < ng, logits, _NEG_INF)
    gmax = jnp.max(lg, axis=1, keepdims=True)
    p_star = 1.0 / jnp.sum(jnp.exp(lg - gmax), axis=1, keepdims=True)
    g_star = jnp.min(jnp.where(lg == gmax, lane_f, big), axis=1, keepdims=True)
    in_group = ((lane >= ng) & (lane < ng + ng * epg)
                & (lax.shift_right_arithmetic(lane - ng, int(math.log2(epg))).astype(F32) == g_star))
    le = jnp.where(in_group, logits, _NEG_INF)
    v1 = jnp.max(le, axis=1, keepdims=True)
    i1 = jnp.min(jnp.where(le == v1, lane_f, big), axis=1, keepdims=True)
    le2 = jnp.where(lane_f == i1, _NEG_INF, le)
    v2 = jnp.max(le2, axis=1, keepdims=True)
    i2 = jnp.min(jnp.where(le2 == v2, lane_f, big), axis=1, keepdims=True)
    e2 = jnp.exp(v2 - v1)
    w1 = p_star / (1.0 + e2)
    w2 = p_star * e2 / (1.0 + e2)
    route = jnp.where(lane == 0, i1 - ng,
                      jnp.where(lane == 1, i2 - ng,
                                jnp.where(lane == 2, w1, jnp.where(lane == 3, w2, 0.0))))
    route_ref[...] = route
    ids_ref[...] = route.T[:ids_ref.shape[0], :]


def _outproj_call(att, ssm_y, x_rows, mod, attn_g, ssm_g, norm2_g, w_out_bf, wr, rb):
    rows, d = x_rows.shape
    ts = min(WIDE_ROW_TILE, rows)
    row = lambda width: pl.BlockSpec((ts, width), lambda i: (i, 0))
    const = lambda a: pl.BlockSpec(a.shape, lambda i: (0,) * a.ndim)
    attn_g = attn_g.reshape(1, ATTN_WIDTH)
    ssm_g = ssm_g.reshape(1, SSM_WIDTH)
    norm2_g = norm2_g.reshape(1, d)
    return pl.pallas_call(
        _outproj_kernel,
        grid=(rows // ts,),
        in_specs=[row(ATTN_WIDTH), row(SSM_WIDTH), row(d),
                  _mod_spec(mod, rows, ts, 2), _mod_spec(mod, rows, ts, 3), _mod_spec(mod, rows, ts, 4),
                  const(attn_g), const(ssm_g), const(norm2_g), const(w_out_bf), const(wr), const(rb)],
        out_specs=[row(d), row(d), row(LANES), pl.BlockSpec((ROUTE_ROWS, ts), lambda i: (0, i))],
        out_shape=[jax.ShapeDtypeStruct((rows, d), F32), jax.ShapeDtypeStruct((rows, d), BF16),
                   jax.ShapeDtypeStruct((rows, LANES), F32),
                   jax.ShapeDtypeStruct((ROUTE_ROWS, rows), F32)],
        compiler_params=_cparams(("arbitrary",)),
        name="outproj_router",
    )(att, ssm_y, x_rows, mod, mod, mod, attn_g, ssm_g, norm2_g, w_out_bf, wr, rb)


def _router_weights(router_g_w, router_g_b, router_e_w, router_e_b):
    d = router_g_w.shape[0]
    ne = N_EXPERT_GROUPS * EXPERTS_PER_GROUP
    w = jnp.concatenate([router_g_w.astype(F32),
                         jnp.transpose(router_e_w.astype(F32), (1, 0, 2)).reshape(d, ne)], axis=1)
    w = jnp.pad(w, ((0, 0), (0, LANES - w.shape[1])))
    hi = w.astype(BF16)
    lo = (w - hi.astype(F32)).astype(BF16)
    b = jnp.concatenate([router_g_b.astype(F32), router_e_b.astype(F32).reshape(ne)])
    b = jnp.pad(b, (0, LANES - b.shape[0])).reshape(1, LANES)
    return jnp.concatenate([hi, lo], axis=1), b


def _moe_kernel(te_ref, x_ref, wg_ref, wu_ref, wd_ref, o_ref, wgb_ref, wub_ref, wdb_ref):
    i = pl.program_id(0)
    changed = jnp.logical_or(i == 0, te_ref[i] != te_ref[jnp.maximum(i - 1, 0)])

    @pl.when(changed)
    def _():
        wgb_ref[...] = wg_ref[0].astype(BF16)
        wub_ref[...] = wu_ref[0].astype(BF16)
        wdb_ref[...] = wd_ref[0].astype(BF16)

    x = x_ref[...]
    gate = jnp.dot(x, wgb_ref[...], preferred_element_type=F32)
    up = jnp.dot(x, wub_ref[...], preferred_element_type=F32)
    a = (gate * _sigmoid(gate)) * up
    o_ref[...] = jnp.dot(a.astype(BF16), wdb_ref[...], preferred_element_type=F32).astype(o_ref.dtype)


def _moe_call(tile_expert, x_sorted, wg, wu, wd, tm):
    n_slots, d = x_sorted.shape
    fe = wg.shape[2]
    grid_spec = pltpu.PrefetchScalarGridSpec(
        num_scalar_prefetch=1,
        grid=(n_slots // tm,),
        in_specs=[pl.BlockSpec((tm, d), lambda i, te: (i, 0)),
                  pl.BlockSpec((1, d, fe), lambda i, te: (te[i], 0, 0)),
                  pl.BlockSpec((1, d, fe), lambda i, te: (te[i], 0, 0)),
                  pl.BlockSpec((1, fe, d), lambda i, te: (te[i], 0, 0))],
        out_specs=pl.BlockSpec((tm, d), lambda i, te: (i, 0)),
        scratch_shapes=[pltpu.VMEM((d, fe), BF16), pltpu.VMEM((d, fe), BF16), pltpu.VMEM((fe, d), BF16)],
    )
    return pl.pallas_call(
        _moe_kernel,
        grid_spec=grid_spec,
        out_shape=jax.ShapeDtypeStruct((n_slots, d), BF16),
        compiler_params=_cparams(("arbitrary",)),
        name="moe_experts",
    )(tile_expert, x_sorted, wg, wu, wd)


def _moe_dispatch(ids_rows, tm):
    n_tok = ids_rows.shape[1]
    ids = jnp.concatenate([ids_rows[0], ids_rows[1]]).astype(jnp.int32)
    n_pairs = ids.shape[0]
    n_slots = (-(-n_pairs // tm) + N_EXPERTS) * tm
    hot = (ids[:, None] == jnp.arange(N_EXPERTS)[None, :]).astype(jnp.int32)
    csum = jnp.cumsum(hot, axis=0)
    rank = jnp.sum((csum - hot) * hot, axis=1)
    counts = csum[-1]
    padded = -(-counts // tm) * tm
    ends = jnp.cumsum(padded)
    starts = ends - padded
    pos = jnp.sum(hot * starts[None, :], axis=1) + rank
    tok_of_slot = (jnp.arange(n_slots, dtype=jnp.int32) % n_tok).at[pos].set(
        jnp.arange(n_pairs, dtype=jnp.int32) % n_tok, unique_indices=True, mode="promise_in_bounds")
    tile_start = jnp.arange(n_slots // tm, dtype=jnp.int32) * tm
    tile_expert = jnp.minimum(jnp.sum((tile_start[:, None] >= ends[None, :]).astype(jnp.int32), axis=1),
                              N_EXPERTS - 1).astype(jnp.int32)
    return pos.reshape(2, n_tok), tok_of_slot, tile_expert


def _take_rows(x, idx):
    return x.at[idx].get(mode="promise_in_bounds")


def _final_kernel(x1_ref, ya_ref, yb_ref, route_ref, g2_ref, fg_ref, o_ref):
    wa = route_ref[:, 2:3]
    wb = route_ref[:, 3:4]
    x = x1_ref[...] + g2_ref[0] * (wa * ya_ref[...].astype(F32) + wb * yb_ref[...].astype(F32))
    o_ref[...] = _rmsnorm(x, fg_ref[...])


def _final_call(x1, ya, yb, route, mod, final_g):
    rows, d = x1.shape
    ts = min(WIDE_ROW_TILE, rows)
    row = pl.BlockSpec((ts, d), lambda i: (i, 0))
    return pl.pallas_call(
        _final_kernel,
        grid=(rows // ts,),
        in_specs=[row, row, row, pl.BlockSpec((ts, LANES), lambda i: (i, 0)), _mod_spec(mod, rows, ts, 5),
                  pl.BlockSpec((1, d), lambda i: (0, 0))],
        out_specs=row,
        out_shape=jax.ShapeDtypeStruct((rows, d), F32),
        compiler_params=_cparams(("arbitrary",)),
        name="final_norm",
    )(x1, ya, yb, route, mod, final_g.reshape(1, d))


def kernel(x_prompt, x_sample, c_prompt, c_sample, cache_k, cache_v, state_ssm_re, state_ssm_im,
           rel_bias, ada_w, ada_b, norm1_g, w_in, ssm_a_re, ssm_a_im, ssm_log_dt, ssm_b_re, ssm_b_im,
           ssm_c_re, ssm_c_im, ssm_d, glu_a, glu_b, attn_out_g, ssm_out_g, w_out, norm2_g,
           router_g_w, router_g_b, router_e_w, router_e_b, w_gate, w_up, w_down, final_norm_g):
    if ada_w.shape[0] != 1:
        raise ValueError("single-layer trunk expected")
    nb, s_len, d = x_prompt.shape
    nd, t_len, _ = x_sample.shape
    if s_len != max(WINDOWS):
        raise ValueError("prompt length must equal the widest window")
    n_p, n_s = nb * s_len, nd * t_len
    if n_p % ROW_TILE or n_s % ROW_TILE:
        raise ValueError("token counts must be multiples of the row tile")

    mod = _mod_call(jnp.concatenate([c_prompt, c_sample], axis=0).astype(F32), ada_w[0], ada_b[0])
    mod_p = mod[:nb].reshape(nb, 1, 6 * d)
    mod_s = jnp.repeat(mod[nb:], t_len, axis=0).reshape(1, n_s, 6 * d)

    w_in_bf = w_in[0].astype(BF16)
    w_out_bf = w_out[0].astype(BF16)
    xp_rows = x_prompt.reshape(n_p, d)
    xs_rows = x_sample.reshape(n_s, d)

    qp, kp, vp, up, kp_t, vp_t = _inproj_call(xp_rows, mod_p, norm1_g[0], w_in_bf, seq_len=s_len)
    qs, ks, vs, us = _inproj_call(xs_rows, mod_s, norm1_g[0], w_in_bf)

    seq = lambda a: a.reshape(nb, s_len, ATTN_WIDTH)
    att_p = _attn_prompt_call(seq(qp), seq(kp), seq(vp), _prompt_bias_tiles(rel_bias))
    dec = lambda a: a.reshape(nd, t_len, ATTN_WIDTH)
    att_s = _attn_decode_call(dec(qs), dec(ks), dec(vs), cache_k[0], cache_v[0], rel_bias)

    s5 = (ssm_a_re[0], ssm_a_im[0], ssm_log_dt[0], ssm_b_re[0], ssm_b_im[0], ssm_c_re[0], ssm_c_im[0])
    t_op, m_op, p_op, sc = _s5_prompt_operators(*s5)
    d_oct = ssm_d[0].astype(F32).reshape(N_OCTETS, 1, LANES)
    ssm_p, hT_p = _ssm_prompt_call(seq(up), t_op, m_op, p_op, sc, d_oct,
                                   _octet_glu(glu_a[0]), _octet_glu(glu_b[0]))
    hT_p = hT_p.reshape(nb, N_OCTETS, 2, OCTET, SSM_STATE)
    ssm_re_p = hT_p[:, :, 0].reshape(nb, SSM_GROUPS, SSM_STATE)
    ssm_im_p = hT_p[:, :, 1].reshape(nb, SSM_GROUPS, SSM_STATE)
    us_tm = jnp.transpose(us.reshape(nd, t_len, SSM_WIDTH), (1, 0, 2))
    ssm_s, hre_s, him_s = _ssm_decode_call(us_tm, state_ssm_re[0], state_ssm_im[0], *s5,
                                           ssm_d[0], glu_a[0], glu_b[0])
    ssm_s = jnp.transpose(ssm_s, (1, 0, 2)).reshape(n_s, SSM_WIDTH)

    wr, rb = _router_weights(router_g_w[0], router_g_b[0], router_e_w[0], router_e_b[0])
    norms = (attn_out_g[0], ssm_out_g[0], norm2_g[0], w_out_bf, wr, rb)
    x1_p, h2_p, route_p, ids_p = _outproj_call(att_p.reshape(n_p, ATTN_WIDTH), ssm_p.reshape(n_p, SSM_WIDTH),
                                        xp_rows, mod_p, *norms)
    ne = N_EXPERTS
    wg = w_gate[0].reshape(ne, d, D_EXPERT)
    wu = w_up[0].reshape(ne, d, D_EXPERT)
    wd = w_down[0].reshape(ne, D_EXPERT, d)

    def experts(x1, x_sorted, pos, tile_expert, route, mod_rows, tm):
        y_slots = _moe_call(tile_expert, x_sorted, wg, wu, wd, tm)
        return _final_call(x1, _take_rows(y_slots, pos[0]), _take_rows(y_slots, pos[1]),
                           route, mod_rows, final_norm_g)

    pos_p, tok_of_slot_p, tile_expert_p = _moe_dispatch(ids_p, MOE_TILE)
    x_sorted_p = _take_rows(h2_p, tok_of_slot_p)
    x_sorted_p, att_s = lax.optimization_barrier((x_sorted_p, att_s))
    y_p = experts(x1_p, x_sorted_p, pos_p, tile_expert_p, route_p, mod_p, MOE_TILE)

    x1_s, h2_s, route_s, ids_s = _outproj_call(att_s.reshape(n_s, ATTN_WIDTH), ssm_s, xs_rows, mod_s, *norms)
    pos_s, tok_of_slot_s, tile_expert_s = _moe_dispatch(ids_s, MOE_TILE_DECODE)
    y_s = experts(x1_s, _take_rows(h2_s, tok_of_slot_s), pos_s, tile_expert_s, route_s, mod_s, MOE_TILE_DECODE)

    heads = (ATTN_HEADS, HEAD_DIM)
    cache_out = lambda a: jnp.transpose(a.reshape((1, nb) + heads + (s_len,)), (0, 1, 4, 2, 3))
    return (y_p.reshape(nb, s_len, d), y_s.reshape(nd, t_len, d),
            cache_out(kp_t), cache_out(vp_t),
            ks.reshape((1, nd, t_len) + heads), vs.reshape((1, nd, t_len) + heads),
            ssm_re_p[None], ssm_im_p[None],
            hre_s.reshape(1, nd, SSM_GROUPS, SSM_STATE), him_s.reshape(1, nd, SSM_GROUPS, SSM_STATE))
```

```python
import functools
import math

import numpy as np

import jax
import jax.numpy as jnp
from jax import lax
from jax.experimental import pallas as pl
from jax.experimental.pallas import tpu as pltpu

F32 = jnp.float32
BF16 = jnp.bfloat16

D_MODEL = 1024
HEAD_DIM = 64
ATTN_WIDTH = 512
ATTN_HEADS = 8
SSM_WIDTH = 512
SSM_GROUP_CH = 16
SSM_GROUPS = 32
SSM_STATE = 64
WINDOWS = (128, 512, 2048)
DILATIONS = (1, 4, 16)
WINDOW_STEPS = 128
N_BUCKETS = 32
MAX_EXACT = 16
BUCKET_MAX_DIST = 2048
N_EXPERT_GROUPS = 4
EXPERTS_PER_GROUP = 4
N_EXPERTS = N_EXPERT_GROUPS * EXPERTS_PER_GROUP
D_EXPERT = 512
NORM_EPS = 1e-6

LANES = 128
Q_ROWS = 16
OCTET = LANES // SSM_GROUP_CH
N_OCTETS = SSM_GROUPS // OCTET
OCT_STATE = OCTET * SSM_STATE
SSM_CHUNK = 16
SSM_SEQS = 2
ROW_TILE = 512
WIDE_ROW_TILE = 1024
MOE_TILE = 512
MOE_TILE_DECODE = 128
ROUTE_ROWS = 8
ATTN_GROUP = 4
ATTN_MXU_UNROLL = 8
VMEM_LIMIT = 56 * 1024 * 1024

_NEG_INF = float("-inf")
_HIGHEST = lax.Precision.HIGHEST


def _cparams(sem):
    return pltpu.CompilerParams(dimension_semantics=sem, vmem_limit_bytes=VMEM_LIMIT)


def _rmsnorm(x, g):
    return x * lax.rsqrt(jnp.mean(x * x, axis=-1, keepdims=True) + NORM_EPS) * g


def _gelu_tanh(x):
    c = math.sqrt(2.0 / math.pi)
    return 0.5 * x * (1.0 + jnp.tanh(c * (x + 0.044715 * (x * x * x))))


def _sigmoid(x):
    return 1.0 / (1.0 + jnp.exp(-x))


def _mod_kernel(c_ref, w_ref, b_ref, o_ref):
    c = c_ref[...]
    a = (c * _sigmoid(c)).astype(BF16)
    o_ref[...] = jnp.dot(a, w_ref[...].astype(BF16), preferred_element_type=F32) + b_ref[...]


def _mod_call(c_all, ada_w, ada_b):
    rows, d = c_all.shape
    n_out = ada_w.shape[1]
    tn = 1024
    return pl.pallas_call(
        _mod_kernel,
        grid=(n_out // tn,),
        in_specs=[pl.BlockSpec((rows, d), lambda j: (0, 0)),
                  pl.BlockSpec((d, tn), lambda j: (0, j)),
                  pl.BlockSpec((1, tn), lambda j: (0, j))],
        out_specs=pl.BlockSpec((rows, tn), lambda j: (0, j)),
        out_shape=jax.ShapeDtypeStruct((rows, n_out), F32),
        compiler_params=_cparams(("arbitrary",)),
        name="adaln_mod",
    )(c_all, ada_w, ada_b.reshape(1, n_out))


def _mod_spec(mod, rows, ts, chunk):
    if mod.shape[1] == 1:
        tiles_per_group = (rows // mod.shape[0]) // ts
        return pl.BlockSpec((1, 1, D_MODEL), lambda i: (i // tiles_per_group, 0, chunk))
    tiles_per_period = mod.shape[1] // ts
    return pl.BlockSpec((1, ts, D_MODEL), lambda i: (0, i % tiles_per_period, chunk))


def _row_tile(tile, rows, mod):
    return min(tile, rows) if mod.shape[1] == 1 else min(tile, rows, mod.shape[1])


def _inproj_kernel(x_ref, sh_ref, sc_ref, g_ref, w_ref, *rest, key_major):
    h = _rmsnorm(x_ref[...], g_ref[...]) * (1.0 + sc_ref[0]) + sh_ref[0]
    hb = h.astype(BF16)
    z = jnp.dot(hb, w_ref[...], preferred_element_type=F32)
    aw = ATTN_WIDTH
    if key_major:
        q_ref, k_ref, v_ref, u_ref, kt_ref, vt_ref = rest
        kt_ref[0] = z[:, aw:2 * aw].T
        vt_ref[0] = z[:, 2 * aw:3 * aw].T
    else:
        q_ref, k_ref, v_ref, u_ref = rest
    q_ref[...] = z[:, :aw]
    k_ref[...] = z[:, aw:2 * aw]
    v_ref[...] = z[:, 2 * aw:3 * aw]
    u_ref[...] = z[:, 3 * aw:]


def _inproj_call(x_rows, mod, norm_g, w_in_bf, seq_len=None):
    rows, d = x_rows.shape
    ts = _row_tile(ROW_TILE, rows, mod)
    proj = w_in_bf.shape[1]
    out = jax.ShapeDtypeStruct((rows, ATTN_WIDTH), F32)
    ospec = pl.BlockSpec((ts, ATTN_WIDTH), lambda i: (i, 0))
    in_specs = [pl.BlockSpec((ts, d), lambda i: (i, 0)),
                _mod_spec(mod, rows, ts, 0),
                _mod_spec(mod, rows, ts, 1),
                pl.BlockSpec((1, d), lambda i: (0, 0)),
                pl.BlockSpec((d, proj), lambda i: (0, 0))]
    args = [x_rows, mod, mod, norm_g.reshape(1, d), w_in_bf]
    out_specs = [ospec, ospec, ospec, ospec]
    out_shape = [out, out, out, out]
    if seq_len is not None:
        tiles = seq_len // ts
        tspec = pl.BlockSpec((1, ATTN_WIDTH, ts), lambda i: (i // tiles, 0, i % tiles))
        out_specs += [tspec, tspec]
        out_shape += [jax.ShapeDtypeStruct((rows // seq_len, ATTN_WIDTH, seq_len), F32)] * 2
    return pl.pallas_call(
        functools.partial(_inproj_kernel, key_major=seq_len is not None),
        grid=(rows // ts,),
        in_specs=in_specs,
        out_specs=out_specs,
        out_shape=out_shape,
        compiler_params=_cparams(("arbitrary",)),
        name="inproj",
    )(*args)


def _t5_bucket(dist):
    d = jnp.maximum(dist, MAX_EXACT).astype(F32)
    log_part = MAX_EXACT + (jnp.log(d / MAX_EXACT) / math.log(BUCKET_MAX_DIST / MAX_EXACT)
                            * (N_BUCKETS - MAX_EXACT)).astype(jnp.int32)
    return jnp.where(dist < MAX_EXACT, dist, jnp.minimum(log_part, N_BUCKETS - 1))


def _bias_by_distance(rel_bias, dists):
    hot = (_t5_bucket(jnp.asarray(dists, jnp.int32))[:, None]
           == jnp.arange(N_BUCKETS, dtype=jnp.int32)[None, :]).astype(F32)
    return jnp.dot(hot, rel_bias.astype(F32), precision=_HIGHEST)


def _prompt_bias_tiles(rel_bias):
    steps = WINDOW_STEPS
    period = 3 * steps
    tiles = []
    for r in DILATIONS:
        vec = _bias_by_distance(rel_bias, r * np.arange(steps + 1))
        fill = jnp.full((steps - 1, ATTN_HEADS), _NEG_INF, F32)
        w = jnp.concatenate([fill, vec[::-1], fill, fill[:1]], axis=0)
        rep = jnp.tile(w.T, (1, steps))[:, :steps * (period - 1)]
        toe = rep.reshape(ATTN_HEADS, steps, period - 1)[:, :, steps - 1:]
        toe = toe.reshape(ATTN_HEADS // 2, 2, steps, 2 * steps)
        own_only = jnp.where(jnp.arange(2 * steps) < steps, _NEG_INF, toe)
        tiles.append(jnp.stack([toe, own_only], axis=2))
    return jnp.stack(tiles)


def _attn_prompt_kernel(q_ref, k_ref, v_ref, bias_ref, o_ref,
                        p4_ref, qh_ref, kb_ref, vb_ref, s_ref, p_ref, res_ref, stage_ref, nat_ref):
    s_len = q_ref.shape[1]
    steps = WINDOW_STEPS
    n_tiles = s_len // steps
    quarter = s_len // 4
    nt = (((1,), (1,)), ((), ()))
    lane = lax.broadcasted_iota(jnp.int32, (1, LANES), 1)
    first_head = lane < HEAD_DIM
    srcs = (q_ref, k_ref, v_ref)

    kb_ref[0:steps, :] = jnp.zeros((steps, LANES), BF16)
    vb_ref[0:steps, 0:LANES] = jnp.zeros((steps, LANES), BF16)
    vb_ref[:, LANES:] = jnp.ones((s_len + steps, LANES), BF16)
    for x in range(3):
        for sigma in range(4):
            p4_ref[x, sigma * quarter:(sigma + 1) * quarter, :] = srcs[x][0, pl.ds(sigma, quarter, stride=4), :]

    def source(branch, x, tile_idx):
        rows = slice(tile_idx * steps, (tile_idx + 1) * steps)
        if branch == 0:
            return srcs[x][0, rows, :]
        if branch == 1:
            return p4_ref[x, rows, :]
        sigma, tau = tile_idx % 4, tile_idx // 4
        return p4_ref[x, pl.ds(sigma * quarter + tau, steps, stride=4), :]

    for branch, r in enumerate(DILATIONS):
        blocks_per_class = (s_len // r) // steps
        width = 2 * steps if blocks_per_class > 1 else steps

        for t in range(n_tiles):
            rows = slice(t * steps, (t + 1) * steps)
            q2 = source(branch, 0, t) * (HEAD_DIM ** -0.5)
            qh_ref[0, rows, :] = jnp.where(first_head, q2, 0.0).astype(BF16)
            qh_ref[1, rows, :] = jnp.where(first_head, 0.0, q2).astype(BF16)
            kb_ref[steps + t * steps:steps + (t + 1) * steps, :] = source(branch, 1, t).astype(BF16)
            vb_ref[steps + t * steps:steps + (t + 1) * steps, 0:LANES] = source(branch, 2, t).astype(BF16)

        def aligned(start):
            return start if isinstance(start, int) else pl.multiple_of(start, steps)

        def tile_rows(t):
            return pl.ds(aligned(t * steps), steps)

        def key_rows(t, width=width):
            start = t * steps if width == 2 * steps else (t + 1) * steps
            return pl.ds(aligned(start), width)

        def scores(t, branch=branch, width=width, blocks_per_class=blocks_per_class, key_rows=key_rows):
            rows = tile_rows(t)
            keys = kb_ref[key_rows(t), :]
            first = jnp.where(t % blocks_per_class == 0, 1, 0)
            for hh in range(2):
                if width == 2 * steps:
                    bias = bias_ref[branch, 0, hh, pl.ds(first, 1), :, :][0]
                else:
                    bias = bias_ref[branch, 0, hh, 0, :, steps:]
                sc = lax.dot_general(qh_ref[hh, rows, :], keys, nt, preferred_element_type=F32)
                s_ref[hh, rows, 0:width] = sc + bias

        def softmax(t, width=width):
            rows = tile_rows(t)
            for hh in range(2):
                sc = s_ref[hh, rows, 0:width]
                m = jnp.max(sc, axis=1, keepdims=True)
                p_ref[hh, rows, 0:width] = jnp.exp(sc - m).astype(BF16)
                res_ref[1, rows, hh * HEAD_DIM:(hh + 1) * HEAD_DIM] = jnp.broadcast_to(m, (steps, HEAD_DIM))

        def weighted(t, width=width, key_rows=key_rows):
            rows = tile_rows(t)
            vals = vb_ref[key_rows(t), :]
            r0 = jnp.dot(p_ref[0, rows, 0:width], vals, preferred_element_type=F32)
            r1 = jnp.dot(p_ref[1, rows, 0:width], vals, preferred_element_type=F32)
            res_ref[0, rows, :] = jnp.where(first_head, r0[:, :LANES], r1[:, :LANES])
            res_ref[2, rows, :] = jnp.where(first_head, r0[:, LANES:], r1[:, LANES:])

        def stage(g_scores, g_softmax, scores=scores, softmax=softmax):
            for fn, g in ((softmax, g_softmax), (scores, g_scores)):
                if g is not None:
                    for j in range(ATTN_GROUP):
                        fn(g * ATTN_GROUP + j)

        n_groups = n_tiles // ATTN_GROUP
        stage(0, None)

        def steady(g, carry, stage=stage):
            stage(g, g - 1)
            return carry

        lax.fori_loop(1, n_groups, steady, 0)
        stage(None, n_groups - 1)

        def weighted_pass(t, carry, weighted=weighted):
            weighted(t)
            return carry

        lax.fori_loop(0, n_tiles, weighted_pass, 0, unroll=ATTN_MXU_UNROLL)

        for kind in range(3):
            if branch == 0:
                nat_ref[0, kind] = res_ref[kind]
                continue
            src = res_ref
            if branch == 2:
                for t in range(n_tiles):
                    sigma, tau = t % 4, t // 4
                    stage_ref[kind, pl.ds(sigma * quarter + tau, steps, stride=4), :] = (
                        res_ref[kind, t * steps:(t + 1) * steps, :])
                src = stage_ref
            for sigma in range(4):
                nat_ref[branch, kind, pl.ds(sigma, quarter, stride=4), :] = (
                    src[kind, sigma * quarter:(sigma + 1) * quarter, :])

    def merge(i, carry):
        rows = pl.ds(pl.multiple_of(i * 256, 256), 256)
        m0, m1, m2 = nat_ref[0, 1, rows, :], nat_ref[1, 1, rows, :], nat_ref[2, 1, rows, :]
        m_all = jnp.maximum(jnp.maximum(m0, m1), m2)
        w0, w1, w2 = jnp.exp(m0 - m_all), jnp.exp(m1 - m_all), jnp.exp(m2 - m_all)
        num = w0 * nat_ref[0, 0, rows, :] + w1 * nat_ref[1, 0, rows, :] + w2 * nat_ref[2, 0, rows, :]
        den = w0 * nat_ref[0, 2, rows, :] + w1 * nat_ref[1, 2, rows, :] + w2 * nat_ref[2, 2, rows, :]
        o_ref[0, rows, :] = num / den
        return carry

    lax.fori_loop(0, s_len // 256, merge, 0)


def _attn_prompt_call(q, k, v, bias_tiles):
    n, s, _ = q.shape
    pairs = ATTN_HEADS // 2
    steps = WINDOW_STEPS
    qspec = pl.BlockSpec((1, s, LANES), lambda g, n_: (n_, 0, g))
    return pl.pallas_call(
        _attn_prompt_kernel,
        grid=(pairs, n),
        in_specs=[qspec, qspec, qspec,
                  pl.BlockSpec((3, 1, 2, 2, steps, 2 * steps), lambda g, n_: (0, g, 0, 0, 0, 0))],
        out_specs=qspec,
        out_shape=jax.ShapeDtypeStruct((n, s, ATTN_WIDTH), F32),
        scratch_shapes=[pltpu.VMEM((3, s, LANES), F32),
                        pltpu.VMEM((2, s, LANES), BF16),
                        pltpu.VMEM((s + steps, LANES), BF16),
                        pltpu.VMEM((s + steps, 2 * LANES), BF16),
                        pltpu.VMEM((2, s, 2 * steps), F32),
                        pltpu.VMEM((2, s, 2 * steps), BF16),
                        pltpu.VMEM((3, s, LANES), F32), pltpu.VMEM((3, s, LANES), F32),
                        pltpu.VMEM((3, 3, s, LANES), F32)],
        compiler_params=_cparams(("arbitrary", "arbitrary")),
        name="attn_prompt",
    )(q, k, v, bias_tiles)


def _s5_discretise(a_re, a_im, log_dt, b_re, b_im):
    lam_re = jnp.minimum(a_re.astype(F32), -1e-4)
    lam_im = a_im.astype(F32)
    dt = jnp.exp(log_dt.astype(F32))[:, None]
    mag = jnp.exp(lam_re * dt)
    ph = lam_im * dt
    abar_re, abar_im = mag * jnp.cos(ph), mag * jnp.sin(ph)
    nr, ni = abar_re - 1.0, abar_im
    den = lam_re * lam_re + lam_im * lam_im
    coef_re = (nr * lam_re + ni * lam_im) / den
    coef_im = (ni * lam_re - nr * lam_im) / den
    br, bi = b_re.astype(F32), b_im.astype(F32)
    bbar_re = coef_re[..., None] * br - coef_im[..., None] * bi
    bbar_im = coef_re[..., None] * bi + coef_im[..., None] * br
    return lam_re * dt, ph, abar_re, abar_im, bbar_re, bbar_im


def _abar_power(log_mag, ph, n):
    nf = jnp.asarray(n, F32)[:, None, None]
    mag = jnp.exp(nf * log_mag[None])
    return mag * jnp.cos(nf * ph[None]), mag * jnp.sin(nf * ph[None])


def _s5_prompt_operators(a_re, a_im, log_dt, b_re, b_im, c_re, c_im):
    L = SSM_CHUNK
    log_mag, ph, _, _, bb_re, bb_im = _s5_discretise(a_re, a_im, log_dt, b_re, b_im)
    cr, ci = c_re.astype(F32), c_im.astype(F32)
    pw_re, pw_im = _abar_power(log_mag, ph, np.arange(L + 1))
    eye = jnp.eye(OCTET, dtype=F32)

    ab_re = pw_re[:L, :, :, None] * bb_re[None] - pw_im[:L, :, :, None] * bb_im[None]
    ab_im = pw_re[:L, :, :, None] * bb_im[None] + pw_im[:L, :, :, None] * bb_re[None]
    lag = (jnp.einsum('gop,lgpi->lgoi', cr, ab_re, precision=_HIGHEST)
           - jnp.einsum('gop,lgpi->lgoi', ci, ab_im, precision=_HIGHEST))
    lag = lag.reshape(L, N_OCTETS, OCTET, SSM_GROUP_CH, SSM_GROUP_CH)
    bd = jnp.einsum('logci,gh->olgihc', lag, eye).reshape(N_OCTETS, L, LANES, LANES).astype(BF16)
    stack = bd[:, ::-1].reshape(N_OCTETS, L * LANES, LANES)
    shifted = jnp.concatenate([stack[:, LANES:], jnp.zeros((N_OCTETS, LANES, LANES), BF16)], axis=1)
    t_op = jnp.concatenate([shifted, stack], axis=-1)

    group_of_lane = jnp.arange(LANES) // SSM_GROUP_CH
    parts = []
    for part in (ab_re[::-1], ab_im[::-1]):
        x = part.reshape(L, N_OCTETS, OCTET, SSM_STATE, SSM_GROUP_CH)
        parts.append(jnp.transpose(x, (1, 0, 3, 2, 4)).reshape(N_OCTETS, L, SSM_STATE, LANES))
    x = jnp.stack(parts, axis=2)
    own = (jnp.arange(OCTET)[:, None, None] == group_of_lane[None, None, :])
    mt_op = jnp.where(own, x[:, :, :, None], 0.0)
    mt_op = mt_op.reshape(N_OCTETS, L, 2 * OCT_STATE, LANES)

    p1_re, p1_im = pw_re[1:], pw_im[1:]
    on_re = cr[None] * p1_re[:, :, None, :] - ci[None] * p1_im[:, :, None, :]
    on_im = -cr[None] * p1_im[:, :, None, :] - ci[None] * p1_re[:, :, None, :]
    parts = []
    for part in (on_re, on_im):
        y = part.reshape(L, N_OCTETS, OCTET, SSM_GROUP_CH, SSM_STATE)
        parts.append(jnp.transpose(y, (1, 0, 3, 2, 4)).reshape(N_OCTETS, L, SSM_GROUP_CH, OCT_STATE))
    y = jnp.concatenate(parts, axis=-1)
    group_of_state = (jnp.arange(2 * OCT_STATE) % OCT_STATE) // SSM_STATE
    own = (jnp.arange(OCTET)[:, None, None] == group_of_state[None, None, :])
    pt_op = jnp.where(own, y[:, :, None], 0.0)
    pt_op = pt_op.reshape(N_OCTETS, L * LANES, 2 * OCT_STATE)

    n_steps = 8
    sc_re, sc_im = _abar_power(log_mag, ph, L * (2 ** np.arange(n_steps)))
    sc = jnp.concatenate([sc_re.reshape(n_steps, N_OCTETS, OCT_STATE),
                          sc_im.reshape(n_steps, N_OCTETS, OCT_STATE)], axis=-1)
    sc = jnp.transpose(sc, (1, 0, 2))
    return t_op, mt_op.astype(BF16), pt_op.astype(BF16), sc


def _octet_glu(glu):
    eye = jnp.eye(OCTET, dtype=F32)
    x = glu.astype(F32).reshape(N_OCTETS, OCTET, SSM_GROUP_CH, SSM_GROUP_CH)
    return jnp.einsum('ogce,gh->ogche', x, eye).reshape(N_OCTETS, LANES, LANES).astype(BF16)


def _ssm_prompt_kernel(u_ref, t_ref, mb_ref, pt_ref, sc_ref, d_ref, ga_ref, gb_ref,
                       y_ref, h_ref, uf_ref, ub_ref, st_ref, mt_ref):
    L = SSM_CHUNK
    nt = (((1,), (1,)), ((), ()))

    @pl.when(pl.program_id(1) == 0)
    def _():
        for s in range(L):
            mt_ref[:, s * LANES:(s + 1) * LANES] = mb_ref[0, s]

    n_seq, s_len = u_ref.shape[0], u_ref.shape[1]
    n_chunks = s_len // L
    quarter = s_len // 4
    rows = n_seq * n_chunks

    def staged(sq, step):
        sigma, tau = step % 4, step // 4
        return pl.ds(sq * s_len + sigma * quarter + tau, n_chunks, stride=4)

    for sq in range(n_seq):
        for sigma in range(4):
            st_ref[sq * s_len + sigma * quarter:sq * s_len + (sigma + 1) * quarter, :] = (
                u_ref[sq, pl.ds(sigma, quarter, stride=4), :])
        for step in range(L):
            blk = st_ref[staged(sq, step), :]
            uf_ref[sq * n_chunks:(sq + 1) * n_chunks, step * LANES:(step + 1) * LANES] = blk
            ub_ref[sq * n_chunks:(sq + 1) * n_chunks, step * LANES:(step + 1) * LANES] = blk.astype(BF16)
    ub = ub_ref[...]

    x = lax.dot_general(ub, mt_ref[...], nt, preferred_element_type=F32)
    chunk = lax.broadcasted_iota(jnp.int32, (rows, 1), 0) % n_chunks
    half = OCT_STATE
    k = 1
    step = 0
    while k < n_chunks:
        a_re = sc_ref[0, step:step + 1, :half]
        a_im = sc_ref[0, step:step + 1, half:]
        sh = jnp.where(chunk >= k, pltpu.roll(x, k, axis=0), 0.0)
        s_re, s_im = sh[:, :half], sh[:, half:]
        x = x + jnp.concatenate([a_re * s_re - a_im * s_im, a_re * s_im + a_im * s_re], axis=1)
        k *= 2
        step += 1
    for sq in range(n_seq):
        h_ref[sq, 0] = x[(sq + 1) * n_chunks - 1:(sq + 1) * n_chunks, :]
    h_start = jnp.where(chunk >= 1, pltpu.roll(x, 1, axis=0), 0.0)

    hb = h_start.astype(BF16)
    d = d_ref[0]
    ga = ga_ref[0]
    gb = gb_ref[0]
    for t in range(0, L, 2):
        pair = slice(t * LANES, (t + 2) * LANES)
        y2 = (jnp.dot(ub_ref[:, :(t + 2) * LANES], t_ref[0, (L - 2 - t) * LANES:, :],
                      preferred_element_type=F32)
              + lax.dot_general(hb, pt_ref[0, pair, :], nt, preferred_element_type=F32))
        for j in range(2):
            lanes = slice((t + j) * LANES, (t + j + 1) * LANES)
            g = _gelu_tanh(y2[:, j * LANES:(j + 1) * LANES] + d * uf_ref[:, lanes]).astype(BF16)
            out = (jnp.dot(g, ga, preferred_element_type=F32)
                   * _sigmoid(jnp.dot(g, gb, preferred_element_type=F32)))
            for sq in range(n_seq):
                st_ref[staged(sq, t + j), :] = out[sq * n_chunks:(sq + 1) * n_chunks, :]
    for sq in range(n_seq):
        for sigma in range(4):
            y_ref[sq, pl.ds(sigma, quarter, stride=4), :] = (
                st_ref[sq * s_len + sigma * quarter:sq * s_len + (sigma + 1) * quarter, :])


def _ssm_prompt_call(u, t_op, m_op, p_op, sc, d_oct, ga, gb):
    n, s, _ = u.shape
    L = SSM_CHUNK
    nq = math.gcd(SSM_SEQS, n)
    rows = nq * (s // L)
    wide = L * LANES
    wspec = lambda shape: pl.BlockSpec((1,) + shape, lambda o, n_: (o, 0, 0))
    return pl.pallas_call(
        _ssm_prompt_kernel,
        grid=(N_OCTETS, n // nq),
        in_specs=[pl.BlockSpec((nq, s, LANES), lambda o, n_: (n_, 0, o)),
                  wspec((wide, 2 * LANES)),
                  pl.BlockSpec((1, L, 2 * OCT_STATE, LANES), lambda o, n_: (o, 0, 0, 0)),
                  wspec((wide, 2 * OCT_STATE)),
                  wspec((8, 2 * OCT_STATE)), wspec((1, LANES)),
                  wspec((LANES, LANES)), wspec((LANES, LANES))],
        out_specs=[pl.BlockSpec((nq, s, LANES), lambda o, n_: (n_, 0, o)),
                   pl.BlockSpec((nq, 1, 1, 2 * OCT_STATE), lambda o, n_: (n_, o, 0, 0))],
        out_shape=[jax.ShapeDtypeStruct((n, s, SSM_WIDTH), F32),
                   jax.ShapeDtypeStruct((n, N_OCTETS, 1, 2 * OCT_STATE), F32)],
        scratch_shapes=[pltpu.VMEM((rows, wide), F32),
                        pltpu.VMEM((rows, wide), BF16),
                        pltpu.VMEM((nq * s, LANES), F32),
                        pltpu.VMEM((2 * OCT_STATE, wide), BF16)],
        compiler_params=_cparams(("arbitrary", "arbitrary")),
        name="ssm_prompt",
    )(u, t_op, m_op, p_op, sc, d_oct, ga, gb)


def _attn_decode_kernel(q_ref, kn_ref, vn_ref, kt_ref, vt_ref, b_ref, mult_ref, o_ref):
    nt = (((1,), (1,)), ((), ()))
    mult = mult_ref[...]
    for h in range(ATTN_HEADS):
        q = q_ref[0, h]
        kt = jnp.concatenate([kt_ref[0, h].astype(BF16), kn_ref[0]], axis=1)
        vt = jnp.concatenate([vt_ref[0, h].astype(BF16), vn_ref[0]], axis=1)
        s = jnp.dot(q, kt, preferred_element_type=F32) + b_ref[h]
        m = jnp.max(s, axis=1, keepdims=True)
        p = jnp.exp(s - m) * mult
        den = jnp.sum(p, axis=1, keepdims=True)
        o = lax.dot_general(p.astype(BF16), vt, nt, preferred_element_type=F32)
        o_ref[0, h] = o / den


def _decode_tables(rel_bias, t_len, w_rows):
    slot = LANES // ATTN_HEADS
    t = np.arange(t_len)[:, None]
    new_t = np.arange(LANES)[None, :] % slot
    dist = np.concatenate([w_rows + t - np.arange(w_rows)[None, :],
                           t - new_t], axis=1)
    mult = np.zeros(dist.shape, np.float32)
    for w, r in zip(WINDOWS, DILATIONS):
        mult += (dist >= 0) & (dist % r == 0) & (dist <= w)
    mult = np.concatenate([mult, np.zeros((Q_ROWS - t_len, dist.shape[1]), np.float32)], axis=0)
    mult[t_len:, 0] = 1.0
    by_dist = _bias_by_distance(rel_bias, np.arange(w_rows + t_len))
    rows = []
    for ti in range(t_len):
        cache_part = by_dist[ti + 1:w_rows + ti + 1][::-1]
        new_part = jnp.concatenate([by_dist[:ti + 1][::-1],
                                    jnp.zeros((slot - ti - 1, ATTN_HEADS), F32)], axis=0)
        rows.append(jnp.concatenate([cache_part] + [new_part] * ATTN_HEADS, axis=0))
    bias = jnp.stack(rows + [jnp.zeros_like(rows[0])] * (Q_ROWS - t_len), axis=0)
    bias = jnp.transpose(bias, (2, 0, 1))
    own_head = np.concatenate([np.ones((ATTN_HEADS, w_rows), bool),
                               (np.arange(LANES) // slot)[None, :] == np.arange(ATTN_HEADS)[:, None]], axis=1)
    bias = jnp.where(jnp.asarray((mult[None] > 0) & own_head[:, None, :]), bias, _NEG_INF)
    return bias, jnp.asarray(mult)


def _attn_decode_call(q, k_new, v_new, cache_k, cache_v, rel_bias):
    n, t_len, w = q.shape
    w_rows = cache_k.shape[1]
    if t_len > min(DILATIONS[1:]) or t_len > Q_ROWS or w_rows < max(WINDOWS):
        raise ValueError("unsupported decode shape")
    heads = (ATTN_HEADS, HEAD_DIM)

    def head_major(a, pad_to):
        a = jnp.transpose(a.reshape((n, t_len) + heads), (0, 2, 1, 3))
        return jnp.pad(a, ((0, 0), (0, 0), (0, pad_to - t_len), (0, 0)))

    def new_tile(a):
        a = jnp.transpose(a.reshape((n, t_len) + heads), (0, 3, 2, 1))
        a = jnp.pad(a, ((0, 0), (0, 0), (0, 0), (0, LANES // ATTN_HEADS - t_len)))
        return a.reshape(n, HEAD_DIM, LANES).astype(BF16)

    qh = head_major(q * (HEAD_DIM ** -0.5), Q_ROWS).astype(BF16)
    knt, vnt = new_tile(k_new), new_tile(v_new)
    kt = jnp.transpose(cache_k.astype(F32), (0, 2, 3, 1))
    vt = jnp.transpose(cache_v.astype(F32), (0, 2, 3, 1))
    bias, mult = _decode_tables(rel_bias, t_len, w_rows)
    keys = w_rows + LANES
    per_seq = lambda shape: pl.BlockSpec((1,) + shape, lambda i: (i, 0, 0, 0))
    out = pl.pallas_call(
        _attn_decode_kernel,
        grid=(n,),
        in_specs=[per_seq((ATTN_HEADS, Q_ROWS, HEAD_DIM)),
                  pl.BlockSpec((1, HEAD_DIM, LANES), lambda i: (i, 0, 0)),
                  pl.BlockSpec((1, HEAD_DIM, LANES), lambda i: (i, 0, 0)),
                  per_seq((ATTN_HEADS, HEAD_DIM, w_rows)), per_seq((ATTN_HEADS, HEAD_DIM, w_rows)),
                  pl.BlockSpec((ATTN_HEADS, Q_ROWS, keys), lambda i: (0, 0, 0)),
                  pl.BlockSpec((Q_ROWS, keys), lambda i: (0, 0))],
        out_specs=per_seq((ATTN_HEADS, Q_ROWS, HEAD_DIM)),
        out_shape=jax.ShapeDtypeStruct((n, ATTN_HEADS, Q_ROWS, HEAD_DIM), F32),
        compiler_params=_cparams(("arbitrary",)),
        name="attn_decode",
    )(qh, knt, vnt, kt, vt, bias, mult)
    return jnp.transpose(out[:, :, :t_len], (0, 2, 1, 3)).reshape(n, t_len, w)


def _ssm_decode_kernel(u_ref, hre_ref, him_ref, are_ref, aim_ref, bre_ref, bim_ref,
                       cre_ref, cim_ref, d_ref, ga_ref, gb_ref, y_ref, ore_ref, oim_ref, *, t_len):
    h_re, h_im = hre_ref[...], him_ref[...]
    a_re, a_im = are_ref[...], aim_ref[...]
    for t in range(t_len):
        u = u_ref[t]
        ub = u.astype(BF16)
        n_re = a_re * h_re - a_im * h_im + jnp.dot(ub, bre_ref[...], preferred_element_type=F32)
        n_im = a_re * h_im + a_im * h_re + jnp.dot(ub, bim_ref[...], preferred_element_type=F32)
        h_re, h_im = n_re, n_im
        y = (jnp.dot(h_re.astype(BF16), cre_ref[...], preferred_element_type=F32)
             - jnp.dot(h_im.astype(BF16), cim_ref[...], preferred_element_type=F32)
             + d_ref[...] * u)
        g = _gelu_tanh(y).astype(BF16)
        y_ref[t] = (jnp.dot(g, ga_ref[...], preferred_element_type=F32)
                    * _sigmoid(jnp.dot(g, gb_ref[...], preferred_element_type=F32)))
    ore_ref[...] = h_re
    oim_ref[...] = h_im


def _group_blockdiag(x):
    g, a, b = x.shape
    panel = jnp.transpose(x, (1, 0, 2)).reshape(a, g * b)
    own = jnp.arange(g)[:, None, None] == (jnp.arange(g * b) // b)[None, None, :]
    return jnp.where(own, panel[None], 0).reshape(g * a, g * b)


def _ssm_decode_call(u_tm, h0_re, h0_im, a_re, a_im, log_dt, b_re, b_im, c_re, c_im,
                     d_skip, glu_a, glu_b):
    n = h0_re.shape[0]
    _, _, abar_re, abar_im, bb_re, bb_im = _s5_discretise(a_re, a_im, log_dt, b_re, b_im)
    state = SSM_GROUPS * SSM_STATE
    t_len = u_tm.shape[0]
    args = (u_tm, h0_re.reshape(n, state).astype(F32), h0_im.reshape(n, state).astype(F32),
            abar_re.reshape(1, state), abar_im.reshape(1, state),
            _group_blockdiag(jnp.transpose(bb_re, (0, 2, 1))).astype(BF16),
            _group_blockdiag(jnp.transpose(bb_im, (0, 2, 1))).astype(BF16),
            _group_blockdiag(jnp.transpose(c_re.astype(F32), (0, 2, 1))).astype(BF16),
            _group_blockdiag(jnp.transpose(c_im.astype(F32), (0, 2, 1))).astype(BF16),
            d_skip.astype(F32).reshape(1, SSM_WIDTH),
            _group_blockdiag(glu_a.astype(F32)).astype(BF16),
            _group_blockdiag(glu_b.astype(F32)).astype(BF16))
    full = lambda a: pl.BlockSpec(a.shape, lambda i: (0,) * a.ndim)
    out_shape = [jax.ShapeDtypeStruct(u_tm.shape, F32),
                 jax.ShapeDtypeStruct((n, state), F32), jax.ShapeDtypeStruct((n, state), F32)]
    return pl.pallas_call(
        functools.partial(_ssm_decode_kernel, t_len=t_len),
        grid=(1,),
        in_specs=[full(a) for a in args],
        out_specs=[full(o) for o in out_shape],
        out_shape=out_shape,
        compiler_params=_cparams(("arbitrary",)),
        name="ssm_decode",
    )(*args)


def _outproj_kernel(att_ref, ssm_ref, x_ref, g1_ref, sh2_ref, sc2_ref, ag_ref, sg_ref, n2_ref,
                    wo_ref, wr_ref, rb_ref, x1_ref, h2_ref, route_ref, ids_ref):
    mixed = jnp.concatenate([_rmsnorm(att_ref[...], ag_ref[...]), _rmsnorm(ssm_ref[...], sg_ref[...])],
                            axis=1).astype(BF16)
    x1 = x_ref[...] + g1_ref[0] * jnp.dot(mixed, wo_ref[...], preferred_element_type=F32)
    x1_ref[...] = x1
    h2 = _rmsnorm(x1, n2_ref[...]) * (1.0 + sc2_ref[0]) + sh2_ref[0]
    hi = h2.astype(BF16)
    h2_ref[...] = hi
    lo = (h2 - hi.astype(F32)).astype(BF16)
    r1 = jnp.dot(hi, wr_ref[...], preferred_element_type=F32)
    r2 = jnp.dot(lo, wr_ref[:, :LANES], preferred_element_type=F32)
    logits = r1[:, :LANES] + r1[:, LANES:] + r2 + rb_ref[...]

    lane = lax.broadcasted_iota(jnp.int32, (1, LANES), 1)
    lane_f = lane.astype(F32)
    big = float(LANES)
    ng, epg = N_EXPERT_GROUPS, EXPERTS_PER_GROUP
    lg = jnp.where(lane < ng, logits, _NEG_INF)
    gmax = jnp.max(lg, axis=1, keepdims=True)
    p_star = 1.0 / jnp.sum(jnp.exp(lg - gmax), axis=1, keepdims=True)
    g_star = jnp.min(jnp.where(lg == gmax, lane_f, big), axis=1, keepdims=True)
    in_group = ((lane >= ng) & (lane < ng + ng * epg)
                & (lax.shift_right_arithmetic(lane - ng, int(math.log2(epg))).astype(F32) == g_star))
    le = jnp.where(in_group, logits, _NEG_INF)
    v1 = jnp.max(le, axis=1, keepdims=True)
    i1 = jnp.min(jnp.where(le == v1, lane_f, big), axis=1, keepdims=True)
    le2 = jnp.where(lane_f == i1, _NEG_INF, le)
    v2 = jnp.max(le2, axis=1, keepdims=True)
    i2 = jnp.min(jnp.where(le2 == v2, lane_f, big), axis=1, keepdims=True)
    e2 = jnp.exp(v2 - v1)
    w1 = p_star / (1.0 + e2)
    w2 = p_star * e2 / (1.0 + e2)
    route = jnp.where(lane == 0, i1 - ng,
                      jnp.where(lane == 1, i2 - ng,
                                jnp.where(lane == 2, w1, jnp.where(lane == 3, w2, 0.0))))
    route_ref[...] = route
    ids_ref[...] = route.T[:ids_ref.shape[0], :]


def _outproj_call(att, ssm_y, x_rows, mod, attn_g, ssm_g, norm2_g, w_out_bf, wr, rb):
    rows, d = x_rows.shape
    ts = _row_tile(WIDE_ROW_TILE, rows, mod)
    row = lambda width: pl.BlockSpec((ts, width), lambda i: (i, 0))
    const = lambda a: pl.BlockSpec(a.shape, lambda i: (0,) * a.ndim)
    attn_g = attn_g.reshape(1, ATTN_WIDTH)
    ssm_g = ssm_g.reshape(1, SSM_WIDTH)
    norm2_g = norm2_g.reshape(1, d)
    return pl.pallas_call(
        _outproj_kernel,
        grid=(rows // ts,),
        in_specs=[row(ATTN_WIDTH), row(SSM_WIDTH), row(d),
                  _mod_spec(mod, rows, ts, 2), _mod_spec(mod, rows, ts, 3), _mod_spec(mod, rows, ts, 4),
                  const(attn_g), const(ssm_g), const(norm2_g), const(w_out_bf), const(wr), const(rb)],
        out_specs=[row(d), row(d), row(LANES), pl.BlockSpec((ROUTE_ROWS, ts), lambda i: (0, i))],
        out_shape=[jax.ShapeDtypeStruct((rows, d), F32), jax.ShapeDtypeStruct((rows, d), BF16),
                   jax.ShapeDtypeStruct((rows, LANES), F32),
                   jax.ShapeDtypeStruct((ROUTE_ROWS, rows), F32)],
        compiler_params=_cparams(("arbitrary",)),
        name="outproj_router",
    )(att, ssm_y, x_rows, mod, mod, mod, attn_g, ssm_g, norm2_g, w_out_bf, wr, rb)


def _router_weights(router_g_w, router_g_b, router_e_w, router_e_b):
    d = router_g_w.shape[0]
    ne = N_EXPERT_GROUPS * EXPERTS_PER_GROUP
    w = jnp.concatenate([router_g_w.astype(F32),
                         jnp.transpose(router_e_w.astype(F32), (1, 0, 2)).reshape(d, ne)], axis=1)
    w = jnp.pad(w, ((0, 0), (0, LANES - w.shape[1])))
    hi = w.astype(BF16)
    lo = (w - hi.astype(F32)).astype(BF16)
    b = jnp.concatenate([router_g_b.astype(F32), router_e_b.astype(F32).reshape(ne)])
    b = jnp.pad(b, (0, LANES - b.shape[0])).reshape(1, LANES)
    return jnp.concatenate([hi, lo], axis=1), b


def _moe_kernel(te_ref, x_ref, wg_ref, wu_ref, wd_ref, o_ref, wgb_ref, wub_ref, wdb_ref):
    i = pl.program_id(0)
    changed = jnp.logical_or(i == 0, te_ref[i] != te_ref[jnp.maximum(i - 1, 0)])

    @pl.when(changed)
    def _():
        wgb_ref[...] = wg_ref[0].astype(BF16)
        wub_ref[...] = wu_ref[0].astype(BF16)
        wdb_ref[...] = wd_ref[0].astype(BF16)

    x = x_ref[...]
    gate = jnp.dot(x, wgb_ref[...], preferred_element_type=F32)
    up = jnp.dot(x, wub_ref[...], preferred_element_type=F32)
    a = (gate * _sigmoid(gate)) * up
    o_ref[...] = jnp.dot(a.astype(BF16), wdb_ref[...], preferred_element_type=F32).astype(o_ref.dtype)


def _moe_call(tile_expert, x_sorted, wg, wu, wd, tm):
    n_slots, d = x_sorted.shape
    fe = wg.shape[2]
    grid_spec = pltpu.PrefetchScalarGridSpec(
        num_scalar_prefetch=1,
        grid=(n_slots // tm,),
        in_specs=[pl.BlockSpec((tm, d), lambda i, te: (i, 0)),
                  pl.BlockSpec((1, d, fe), lambda i, te: (te[i], 0, 0)),
                  pl.BlockSpec((1, d, fe), lambda i, te: (te[i], 0, 0)),
                  pl.BlockSpec((1, fe, d), lambda i, te: (te[i], 0, 0))],
        out_specs=pl.BlockSpec((tm, d), lambda i, te: (i, 0)),
        scratch_shapes=[pltpu.VMEM((d, fe), BF16), pltpu.VMEM((d, fe), BF16), pltpu.VMEM((fe, d), BF16)],
    )
    return pl.pallas_call(
        _moe_kernel,
        grid_spec=grid_spec,
        out_shape=jax.ShapeDtypeStruct((n_slots, d), BF16),
        compiler_params=_cparams(("arbitrary",)),
        name="moe_experts",
    )(tile_expert, x_sorted, wg, wu, wd)


def _moe_dispatch(ids_rows, tm):
    n_tok = ids_rows.shape[1]
    ids = jnp.concatenate([ids_rows[0], ids_rows[1]]).astype(jnp.int32)
    n_pairs = ids.shape[0]
    n_slots = (-(-n_pairs // tm) + N_EXPERTS) * tm
    hot = (ids[:, None] == jnp.arange(N_EXPERTS)[None, :]).astype(jnp.int32)
    csum = jnp.cumsum(hot, axis=0)
    rank = jnp.sum((csum - hot) * hot, axis=1)
    counts = csum[-1]
    padded = -(-counts // tm) * tm
    ends = jnp.cumsum(padded)
    starts = ends - padded
    pos = jnp.sum(hot * starts[None, :], axis=1) + rank
    tok_of_slot = (jnp.arange(n_slots, dtype=jnp.int32) % n_tok).at[pos].set(
        jnp.arange(n_pairs, dtype=jnp.int32) % n_tok, unique_indices=True, mode="promise_in_bounds")
    tile_start = jnp.arange(n_slots // tm, dtype=jnp.int32) * tm
    tile_expert = jnp.minimum(jnp.sum((tile_start[:, None] >= ends[None, :]).astype(jnp.int32), axis=1),
                              N_EXPERTS - 1).astype(jnp.int32)
    return pos.reshape(2, n_tok), tok_of_slot, tile_expert


def _take_rows(x, idx):
    return x.at[idx].get(mode="promise_in_bounds")


def _final_kernel(x1_ref, ya_ref, yb_ref, route_ref, g2_ref, fg_ref, o_ref):
    wa = route_ref[:, 2:3]
    wb = route_ref[:, 3:4]
    x = x1_ref[...] + g2_ref[0] * (wa * ya_ref[...].astype(F32) + wb * yb_ref[...].astype(F32))
    o_ref[...] = _rmsnorm(x, fg_ref[...])


def _final_call(x1, ya, yb, route, mod, final_g):
    rows, d = x1.shape
    ts = _row_tile(WIDE_ROW_TILE, rows, mod)
    row = pl.BlockSpec((ts, d), lambda i: (i, 0))
    return pl.pallas_call(
        _final_kernel,
        grid=(rows // ts,),
        in_specs=[row, row, row, pl.BlockSpec((ts, LANES), lambda i: (i, 0)), _mod_spec(mod, rows, ts, 5),
                  pl.BlockSpec((1, d), lambda i: (0, 0))],
        out_specs=row,
        out_shape=jax.ShapeDtypeStruct((rows, d), F32),
        compiler_params=_cparams(("arbitrary",)),
        name="final_norm",
    )(x1, ya, yb, route, mod, final_g.reshape(1, d))


def kernel(x_prompt, x_sample, c_prompt, c_sample, cache_k, cache_v, state_ssm_re, state_ssm_im,
           rel_bias, ada_w, ada_b, norm1_g, w_in, ssm_a_re, ssm_a_im, ssm_log_dt, ssm_b_re, ssm_b_im,
           ssm_c_re, ssm_c_im, ssm_d, glu_a, glu_b, attn_out_g, ssm_out_g, w_out, norm2_g,
           router_g_w, router_g_b, router_e_w, router_e_b, w_gate, w_up, w_down, final_norm_g):
    if ada_w.shape[0] != 1:
        raise ValueError("single-layer trunk expected")
    nb, s_len, d = x_prompt.shape
    nd, t_len, _ = x_sample.shape
    if s_len != max(WINDOWS):
        raise ValueError("prompt length must equal the widest window")
    n_p, n_s = nb * s_len, nd * t_len
    if n_p % ROW_TILE or n_s % ROW_TILE:
        raise ValueError("token counts must be multiples of the row tile")

    mod = _mod_call(jnp.concatenate([c_prompt, c_sample], axis=0).astype(F32), ada_w[0], ada_b[0])
    mod_p = mod[:nb].reshape(nb, 1, 6 * d)
    mod_s = mod[nb:].reshape(1, nd, 6 * d)

    w_in_bf = w_in[0].astype(BF16)
    w_out_bf = w_out[0].astype(BF16)
    xp_rows = x_prompt.reshape(n_p, d)
    xs_rows = jnp.transpose(x_sample, (1, 0, 2)).reshape(n_s, d)

    qp, kp, vp, up, kp_t, vp_t = _inproj_call(xp_rows, mod_p, norm1_g[0], w_in_bf, seq_len=s_len)
    qs, ks, vs, us, ks_t, vs_t = _inproj_call(xs_rows, mod_s, norm1_g[0], w_in_bf, seq_len=nd)

    seq = lambda a: a.reshape(nb, s_len, ATTN_WIDTH)
    att_p = _attn_prompt_call(seq(qp), seq(kp), seq(vp), _prompt_bias_tiles(rel_bias))
    dec = lambda a: jnp.transpose(a.reshape(t_len, nd, ATTN_WIDTH), (1, 0, 2))
    att_s = _attn_decode_call(dec(qs), dec(ks), dec(vs), cache_k[0], cache_v[0], rel_bias)
    att_s = jnp.transpose(att_s, (1, 0, 2)).reshape(n_s, ATTN_WIDTH)

    s5 = (ssm_a_re[0], ssm_a_im[0], ssm_log_dt[0], ssm_b_re[0], ssm_b_im[0], ssm_c_re[0], ssm_c_im[0])
    t_op, m_op, p_op, sc = _s5_prompt_operators(*s5)
    d_oct = ssm_d[0].astype(F32).reshape(N_OCTETS, 1, LANES)
    ssm_p, hT_p = _ssm_prompt_call(seq(up), t_op, m_op, p_op, sc, d_oct,
                                   _octet_glu(glu_a[0]), _octet_glu(glu_b[0]))
    hT_p = hT_p.reshape(nb, N_OCTETS, 2, OCTET, SSM_STATE)
    ssm_re_p = hT_p[:, :, 0].reshape(nb, SSM_GROUPS, SSM_STATE)
    ssm_im_p = hT_p[:, :, 1].reshape(nb, SSM_GROUPS, SSM_STATE)
    ssm_s, hre_s, him_s = _ssm_decode_call(us.reshape(t_len, nd, SSM_WIDTH), state_ssm_re[0],
                                           state_ssm_im[0], *s5, ssm_d[0], glu_a[0], glu_b[0])
    ssm_s = ssm_s.reshape(n_s, SSM_WIDTH)

    wr, rb = _router_weights(router_g_w[0], router_g_b[0], router_e_w[0], router_e_b[0])
    norms = (attn_out_g[0], ssm_out_g[0], norm2_g[0], w_out_bf, wr, rb)
    x1_p, h2_p, route_p, ids_p = _outproj_call(att_p.reshape(n_p, ATTN_WIDTH), ssm_p.reshape(n_p, SSM_WIDTH),
                                        xp_rows, mod_p, *norms)
    ne = N_EXPERTS
    wg = w_gate[0].reshape(ne, d, D_EXPERT)
    wu = w_up[0].reshape(ne, d, D_EXPERT)
    wd = w_down[0].reshape(ne, D_EXPERT, d)

    def experts(x1, x_sorted, pos, tile_expert, route, mod_rows, tm):
        y_slots = _moe_call(tile_expert, x_sorted, wg, wu, wd, tm)
        return _final_call(x1, _take_rows(y_slots, pos[0]), _take_rows(y_slots, pos[1]),
                           route, mod_rows, final_norm_g)

    pos_p, tok_of_slot_p, tile_expert_p = _moe_dispatch(ids_p, MOE_TILE)
    x_sorted_p = _take_rows(h2_p, tok_of_slot_p)
    x_sorted_p, att_s = lax.optimization_barrier((x_sorted_p, att_s))
    y_p = experts(x1_p, x_sorted_p, pos_p, tile_expert_p, route_p, mod_p, MOE_TILE)

    x1_s, h2_s, route_s, ids_s = _outproj_call(att_s, ssm_s, xs_rows, mod_s, *norms)
    pos_s, tok_of_slot_s, tile_expert_s = _moe_dispatch(ids_s, MOE_TILE_DECODE)
    y_s = experts(x1_s, _take_rows(h2_s, tok_of_slot_s), pos_s, tile_expert_s, route_s, mod_s, MOE_TILE_DECODE)

    heads = (ATTN_HEADS, HEAD_DIM)
    cache_out = lambda a: jnp.transpose(a.reshape((1, nb) + heads + (s_len,)), (0, 1, 4, 2, 3))
    step_out = lambda a: jnp.transpose(a.reshape((1, t_len) + heads + (nd,)), (0, 4, 1, 2, 3))
    return (y_p.reshape(nb, s_len, d), jnp.transpose(y_s.reshape(t_len, nd, d), (1, 0, 2)),
            cache_out(kp_t), cache_out(vp_t), step_out(ks_t), step_out(vs_t),
            ssm_re_p[None], ssm_im_p[None],
            hre_s.reshape(1, nd, SSM_GROUPS, SSM_STATE), him_s.reshape(1, nd, SSM_GROUPS, SSM_STATE))
```

```python
import functools
import math

import numpy as np

import jax
import jax.numpy as jnp
from jax import lax
from jax.experimental import pallas as pl
from jax.experimental.pallas import tpu as pltpu

F32 = jnp.float32
BF16 = jnp.bfloat16

D_MODEL = 1024
HEAD_DIM = 64
ATTN_WIDTH = 512
ATTN_HEADS = 8
SSM_WIDTH = 512
SSM_GROUP_CH = 16
SSM_GROUPS = 32
SSM_STATE = 64
WINDOWS = (128, 512, 2048)
DILATIONS = (1, 4, 16)
WINDOW_STEPS = 128
N_BUCKETS = 32
MAX_EXACT = 16
BUCKET_MAX_DIST = 2048
N_EXPERT_GROUPS = 4
EXPERTS_PER_GROUP = 4
N_EXPERTS = N_EXPERT_GROUPS * EXPERTS_PER_GROUP
D_EXPERT = 512
NORM_EPS = 1e-6

LANES = 128
Q_ROWS = 16
OCTET = LANES // SSM_GROUP_CH
N_OCTETS = SSM_GROUPS // OCTET
OCT_STATE = OCTET * SSM_STATE
SSM_CHUNK = 16
SSM_SEQS = 4
ROW_TILE = 512
WIDE_ROW_TILE = 1024
MOE_TILE = 512
MOE_TILE_DECODE = 128
ROUTE_ROWS = 8
ATTN_GROUP = 4
ATTN_MXU_UNROLL = 8
VMEM_LIMIT = 56 * 1024 * 1024

_NEG_INF = float("-inf")
_HIGHEST = lax.Precision.HIGHEST


def _cparams(sem):
    return pltpu.CompilerParams(dimension_semantics=sem, vmem_limit_bytes=VMEM_LIMIT)


def _rmsnorm(x, g):
    return x * lax.rsqrt(jnp.mean(x * x, axis=-1, keepdims=True) + NORM_EPS) * g


def _gelu_tanh(x):
    c = math.sqrt(2.0 / math.pi)
    return 0.5 * x * (1.0 + jnp.tanh(c * (x + 0.044715 * (x * x * x))))


def _sigmoid(x):
    return 1.0 / (1.0 + jnp.exp(-x))


def _mod_kernel(c_ref, w_ref, b_ref, o_ref):
    c = c_ref[...]
    a = (c * _sigmoid(c)).astype(BF16)
    o_ref[...] = jnp.dot(a, w_ref[...].astype(BF16), preferred_element_type=F32) + b_ref[...]


def _mod_call(c_all, ada_w, ada_b):
    rows, d = c_all.shape
    n_out = ada_w.shape[1]
    tn = 1024
    return pl.pallas_call(
        _mod_kernel,
        grid=(n_out // tn,),
        in_specs=[pl.BlockSpec((rows, d), lambda j: (0, 0)),
                  pl.BlockSpec((d, tn), lambda j: (0, j)),
                  pl.BlockSpec((1, tn), lambda j: (0, j))],
        out_specs=pl.BlockSpec((rows, tn), lambda j: (0, j)),
        out_shape=jax.ShapeDtypeStruct((rows, n_out), F32),
        compiler_params=_cparams(("arbitrary",)),
        name="adaln_mod",
    )(c_all, ada_w, ada_b.reshape(1, n_out))


def _mod_spec(mod, rows, ts, chunk):
    if mod.shape[1] == 1:
        tiles_per_group = (rows // mod.shape[0]) // ts
        return pl.BlockSpec((1, 1, D_MODEL), lambda i: (i // tiles_per_group, 0, chunk))
    tiles_per_period = mod.shape[1] // ts
    return pl.BlockSpec((1, ts, D_MODEL), lambda i: (0, i % tiles_per_period, chunk))


def _row_tile(tile, rows, mod):
    return min(tile, rows) if mod.shape[1] == 1 else min(tile, rows, mod.shape[1])


def _inproj_kernel(x_ref, sh_ref, sc_ref, g_ref, w_ref, *rest, key_major):
    h = _rmsnorm(x_ref[...], g_ref[...]) * (1.0 + sc_ref[0]) + sh_ref[0]
    hb = h.astype(BF16)
    z = jnp.dot(hb, w_ref[...], preferred_element_type=F32)
    aw = ATTN_WIDTH
    if key_major:
        q_ref, k_ref, v_ref, u_ref, kt_ref, vt_ref = rest
        kt_ref[0] = z[:, aw:2 * aw].T
        vt_ref[0] = z[:, 2 * aw:3 * aw].T
    else:
        q_ref, k_ref, v_ref, u_ref = rest
    q_ref[...] = z[:, :aw]
    k_ref[...] = z[:, aw:2 * aw]
    v_ref[...] = z[:, 2 * aw:3 * aw]
    u_ref[...] = z[:, 3 * aw:]


def _inproj_call(x_rows, mod, norm_g, w_in_bf, seq_len=None):
    rows, d = x_rows.shape
    ts = _row_tile(ROW_TILE, rows, mod)
    proj = w_in_bf.shape[1]
    out = jax.ShapeDtypeStruct((rows, ATTN_WIDTH), F32)
    ospec = pl.BlockSpec((ts, ATTN_WIDTH), lambda i: (i, 0))
    in_specs = [pl.BlockSpec((ts, d), lambda i: (i, 0)),
                _mod_spec(mod, rows, ts, 0),
                _mod_spec(mod, rows, ts, 1),
                pl.BlockSpec((1, d), lambda i: (0, 0)),
                pl.BlockSpec((d, proj), lambda i: (0, 0))]
    args = [x_rows, mod, mod, norm_g.reshape(1, d), w_in_bf]
    out_specs = [ospec, ospec, ospec, ospec]
    out_shape = [out, out, out, out]
    if seq_len is not None:
        tiles = seq_len // ts
        tspec = pl.BlockSpec((1, ATTN_WIDTH, ts), lambda i: (i // tiles, 0, i % tiles))
        out_specs += [tspec, tspec]
        out_shape += [jax.ShapeDtypeStruct((rows // seq_len, ATTN_WIDTH, seq_len), F32)] * 2
    return pl.pallas_call(
        functools.partial(_inproj_kernel, key_major=seq_len is not None),
        grid=(rows // ts,),
        in_specs=in_specs,
        out_specs=out_specs,
        out_shape=out_shape,
        compiler_params=_cparams(("arbitrary",)),
        name="inproj",
    )(*args)


def _t5_bucket(dist):
    d = jnp.maximum(dist, MAX_EXACT).astype(F32)
    log_part = MAX_EXACT + (jnp.log(d / MAX_EXACT) / math.log(BUCKET_MAX_DIST / MAX_EXACT)
                            * (N_BUCKETS - MAX_EXACT)).astype(jnp.int32)
    return jnp.where(dist < MAX_EXACT, dist, jnp.minimum(log_part, N_BUCKETS - 1))


def _bias_by_distance(rel_bias, dists):
    hot = (_t5_bucket(jnp.asarray(dists, jnp.int32))[:, None]
           == jnp.arange(N_BUCKETS, dtype=jnp.int32)[None, :]).astype(F32)
    return jnp.dot(hot, rel_bias.astype(F32), precision=_HIGHEST)


def _prompt_bias_tiles(rel_bias):
    steps = WINDOW_STEPS
    period = 3 * steps
    tiles = []
    for r in DILATIONS:
        vec = _bias_by_distance(rel_bias, r * np.arange(steps + 1))
        fill = jnp.full((steps - 1, ATTN_HEADS), _NEG_INF, F32)
        w = jnp.concatenate([fill, vec[::-1], fill, fill[:1]], axis=0)
        rep = jnp.tile(w.T, (1, steps))[:, :steps * (period - 1)]
        toe = rep.reshape(ATTN_HEADS, steps, period - 1)[:, :, steps - 1:]
        toe = toe.reshape(ATTN_HEADS // 2, 2, steps, 2 * steps)
        own_only = jnp.where(jnp.arange(2 * steps) < steps, _NEG_INF, toe)
        tiles.append(jnp.stack([toe, own_only], axis=2))
    return jnp.stack(tiles)


def _attn_prompt_kernel(q_ref, k_ref, v_ref, bias_ref, o_ref,
                        p4_ref, qh_ref, kb_ref, vb_ref, s_ref, p_ref, res_ref, stage_ref, nat_ref):
    s_len = q_ref.shape[1]
    steps = WINDOW_STEPS
    n_tiles = s_len // steps
    quarter = s_len // 4
    nt = (((1,), (1,)), ((), ()))
    lane = lax.broadcasted_iota(jnp.int32, (1, LANES), 1)
    first_head = lane < HEAD_DIM
    srcs = (q_ref, k_ref, v_ref)

    kb_ref[0:steps, :] = jnp.zeros((steps, LANES), BF16)
    vb_ref[0:steps, 0:LANES] = jnp.zeros((steps, LANES), BF16)
    vb_ref[:, LANES:] = jnp.ones((s_len + steps, LANES), BF16)
    for x in range(3):
        for sigma in range(4):
            p4_ref[x, sigma * quarter:(sigma + 1) * quarter, :] = srcs[x][0, pl.ds(sigma, quarter, stride=4), :]

    def source(branch, x, tile_idx):
        rows = slice(tile_idx * steps, (tile_idx + 1) * steps)
        if branch == 0:
            return srcs[x][0, rows, :]
        if branch == 1:
            return p4_ref[x, rows, :]
        sigma, tau = tile_idx % 4, tile_idx // 4
        return p4_ref[x, pl.ds(sigma * quarter + tau, steps, stride=4), :]

    for branch, r in enumerate(DILATIONS):
        blocks_per_class = (s_len // r) // steps
        width = 2 * steps if blocks_per_class > 1 else steps

        for t in range(n_tiles):
            rows = slice(t * steps, (t + 1) * steps)
            q2 = source(branch, 0, t) * (HEAD_DIM ** -0.5)
            qh_ref[0, rows, :] = jnp.where(first_head, q2, 0.0).astype(BF16)
            qh_ref[1, rows, :] = jnp.where(first_head, 0.0, q2).astype(BF16)
            kb_ref[steps + t * steps:steps + (t + 1) * steps, :] = source(branch, 1, t).astype(BF16)
            vb_ref[steps + t * steps:steps + (t + 1) * steps, 0:LANES] = source(branch, 2, t).astype(BF16)

        def aligned(start):
            return start if isinstance(start, int) else pl.multiple_of(start, steps)

        def tile_rows(t):
            return pl.ds(aligned(t * steps), steps)

        def key_rows(t, width=width):
            start = t * steps if width == 2 * steps else (t + 1) * steps
            return pl.ds(aligned(start), width)

        def scores(t, branch=branch, width=width, blocks_per_class=blocks_per_class, key_rows=key_rows):
            rows = tile_rows(t)
            keys = kb_ref[key_rows(t), :]
            first = jnp.where(t % blocks_per_class == 0, 1, 0)
            for hh in range(2):
                if width == 2 * steps:
                    bias = bias_ref[branch, 0, hh, pl.ds(first, 1), :, :][0]
                else:
                    bias = bias_ref[branch, 0, hh, 0, :, steps:]
                sc = lax.dot_general(qh_ref[hh, rows, :], keys, nt, preferred_element_type=F32)
                s_ref[hh, rows, 0:width] = sc + bias

        def softmax(t, width=width):
            rows = tile_rows(t)
            for hh in range(2):
                sc = s_ref[hh, rows, 0:width]
                m = jnp.max(sc, axis=1, keepdims=True)
                p_ref[hh, rows, 0:width] = jnp.exp(sc - m).astype(BF16)
                res_ref[1, rows, hh * HEAD_DIM:(hh + 1) * HEAD_DIM] = jnp.broadcast_to(m, (steps, HEAD_DIM))

        def weighted(t, width=width, key_rows=key_rows):
            rows = tile_rows(t)
            vals = vb_ref[key_rows(t), :]
            r0 = jnp.dot(p_ref[0, rows, 0:width], vals, preferred_element_type=F32)
            r1 = jnp.dot(p_ref[1, rows, 0:width], vals, preferred_element_type=F32)
            res_ref[0, rows, :] = jnp.where(first_head, r0[:, :LANES], r1[:, :LANES])
            res_ref[2, rows, :] = jnp.where(first_head, r0[:, LANES:], r1[:, LANES:])

        def stage(g_scores, g_softmax, scores=scores, softmax=softmax):
            for fn, g in ((softmax, g_softmax), (scores, g_scores)):
                if g is not None:
                    for j in range(ATTN_GROUP):
                        fn(g * ATTN_GROUP + j)

        n_groups = n_tiles // ATTN_GROUP
        stage(0, None)

        def steady(g, carry, stage=stage):
            stage(g, g - 1)
            return carry

        lax.fori_loop(1, n_groups, steady, 0)
        stage(None, n_groups - 1)

        def weighted_pass(t, carry, weighted=weighted):
            weighted(t)
            return carry

        lax.fori_loop(0, n_tiles, weighted_pass, 0, unroll=ATTN_MXU_UNROLL)

        for kind in range(3):
            if branch == 0:
                nat_ref[0, kind] = res_ref[kind]
                continue
            src = res_ref
            if branch == 2:
                for t in range(n_tiles):
                    sigma, tau = t % 4, t // 4
                    stage_ref[kind, pl.ds(sigma * quarter + tau, steps, stride=4), :] = (
                        res_ref[kind, t * steps:(t + 1) * steps, :])
                src = stage_ref
            for sigma in range(4):
                nat_ref[branch, kind, pl.ds(sigma, quarter, stride=4), :] = (
                    src[kind, sigma * quarter:(sigma + 1) * quarter, :])

    def merge(i, carry):
        rows = pl.ds(pl.multiple_of(i * 256, 256), 256)
        m0, m1, m2 = nat_ref[0, 1, rows, :], nat_ref[1, 1, rows, :], nat_ref[2, 1, rows, :]
        m_all = jnp.maximum(jnp.maximum(m0, m1), m2)
        w0, w1, w2 = jnp.exp(m0 - m_all), jnp.exp(m1 - m_all), jnp.exp(m2 - m_all)
        num = w0 * nat_ref[0, 0, rows, :] + w1 * nat_ref[1, 0, rows, :] + w2 * nat_ref[2, 0, rows, :]
        den = w0 * nat_ref[0, 2, rows, :] + w1 * nat_ref[1, 2, rows, :] + w2 * nat_ref[2, 2, rows, :]
        o_ref[0, rows, :] = num / den
        return carry

    lax.fori_loop(0, s_len // 256, merge, 0)


def _attn_prompt_call(q, k, v, bias_tiles):
    n, s, _ = q.shape
    pairs = ATTN_HEADS // 2
    steps = WINDOW_STEPS
    qspec = pl.BlockSpec((1, s, LANES), lambda g, n_: (n_, 0, g))
    return pl.pallas_call(
        _attn_prompt_kernel,
        grid=(pairs, n),
        in_specs=[qspec, qspec, qspec,
                  pl.BlockSpec((3, 1, 2, 2, steps, 2 * steps), lambda g, n_: (0, g, 0, 0, 0, 0))],
        out_specs=qspec,
        out_shape=jax.ShapeDtypeStruct((n, s, ATTN_WIDTH), F32),
        scratch_shapes=[pltpu.VMEM((3, s, LANES), F32),
                        pltpu.VMEM((2, s, LANES), BF16),
                        pltpu.VMEM((s + steps, LANES), BF16),
                        pltpu.VMEM((s + steps, 2 * LANES), BF16),
                        pltpu.VMEM((2, s, 2 * steps), F32),
                        pltpu.VMEM((2, s, 2 * steps), BF16),
                        pltpu.VMEM((3, s, LANES), F32), pltpu.VMEM((3, s, LANES), F32),
                        pltpu.VMEM((3, 3, s, LANES), F32)],
        compiler_params=_cparams(("arbitrary", "arbitrary")),
        name="attn_prompt",
    )(q, k, v, bias_tiles)


def _s5_discretise(a_re, a_im, log_dt, b_re, b_im):
    lam_re = jnp.minimum(a_re.astype(F32), -1e-4)
    lam_im = a_im.astype(F32)
    dt = jnp.exp(log_dt.astype(F32))[:, None]
    mag = jnp.exp(lam_re * dt)
    ph = lam_im * dt
    abar_re, abar_im = mag * jnp.cos(ph), mag * jnp.sin(ph)
    nr, ni = abar_re - 1.0, abar_im
    den = lam_re * lam_re + lam_im * lam_im
    coef_re = (nr * lam_re + ni * lam_im) / den
    coef_im = (ni * lam_re - nr * lam_im) / den
    br, bi = b_re.astype(F32), b_im.astype(F32)
    bbar_re = coef_re[..., None] * br - coef_im[..., None] * bi
    bbar_im = coef_re[..., None] * bi + coef_im[..., None] * br
    return lam_re * dt, ph, abar_re, abar_im, bbar_re, bbar_im


def _abar_power(log_mag, ph, n):
    nf = jnp.asarray(n, F32)[:, None, None]
    mag = jnp.exp(nf * log_mag[None])
    return mag * jnp.cos(nf * ph[None]), mag * jnp.sin(nf * ph[None])


def _s5_prompt_operators(a_re, a_im, log_dt, b_re, b_im, c_re, c_im):
    L = SSM_CHUNK
    log_mag, ph, _, _, bb_re, bb_im = _s5_discretise(a_re, a_im, log_dt, b_re, b_im)
    cr, ci = c_re.astype(F32), c_im.astype(F32)
    pw_re, pw_im = _abar_power(log_mag, ph, np.arange(L + 1))
    eye = jnp.eye(OCTET, dtype=F32)

    ab_re = pw_re[:L, :, :, None] * bb_re[None] - pw_im[:L, :, :, None] * bb_im[None]
    ab_im = pw_re[:L, :, :, None] * bb_im[None] + pw_im[:L, :, :, None] * bb_re[None]
    lag = (jnp.einsum('gop,lgpi->lgoi', cr, ab_re, precision=_HIGHEST)
           - jnp.einsum('gop,lgpi->lgoi', ci, ab_im, precision=_HIGHEST))
    lag = lag.reshape(L, N_OCTETS, OCTET, SSM_GROUP_CH, SSM_GROUP_CH)
    bd = jnp.einsum('logci,gh->olgihc', lag, eye).reshape(N_OCTETS, L, LANES, LANES).astype(BF16)
    stack = bd[:, ::-1].reshape(N_OCTETS, L * LANES, LANES)
    shifted = jnp.concatenate([stack[:, LANES:], jnp.zeros((N_OCTETS, LANES, LANES), BF16)], axis=1)
    t_op = jnp.concatenate([shifted, stack], axis=-1)

    group_of_lane = jnp.arange(LANES) // SSM_GROUP_CH
    parts = []
    for part in (ab_re[::-1], ab_im[::-1]):
        x = part.reshape(L, N_OCTETS, OCTET, SSM_STATE, SSM_GROUP_CH)
        parts.append(jnp.transpose(x, (1, 0, 3, 2, 4)).reshape(N_OCTETS, L, SSM_STATE, LANES))
    x = jnp.stack(parts, axis=2)
    own = (jnp.arange(OCTET)[:, None, None] == group_of_lane[None, None, :])
    mt_op = jnp.where(own, x[:, :, :, None], 0.0)
    mt_op = mt_op.reshape(N_OCTETS, L, 2 * OCT_STATE, LANES)

    p1_re, p1_im = pw_re[1:], pw_im[1:]
    on_re = cr[None] * p1_re[:, :, None, :] - ci[None] * p1_im[:, :, None, :]
    on_im = -cr[None] * p1_im[:, :, None, :] - ci[None] * p1_re[:, :, None, :]
    parts = []
    for part in (on_re, on_im):
        y = part.reshape(L, N_OCTETS, OCTET, SSM_GROUP_CH, SSM_STATE)
        parts.append(jnp.transpose(y, (1, 0, 3, 2, 4)).reshape(N_OCTETS, L, SSM_GROUP_CH, OCT_STATE))
    y = jnp.concatenate(parts, axis=-1)
    group_of_state = (jnp.arange(2 * OCT_STATE) % OCT_STATE) // SSM_STATE
    own = (jnp.arange(OCTET)[:, None, None] == group_of_state[None, None, :])
    pt_op = jnp.where(own, y[:, :, None], 0.0)
    pt_op = pt_op.reshape(N_OCTETS, L * LANES, 2 * OCT_STATE)

    n_steps = 8
    sc_re, sc_im = _abar_power(log_mag, ph, L * (2 ** np.arange(n_steps)))
    sc = jnp.concatenate([sc_re.reshape(n_steps, N_OCTETS, OCT_STATE),
                          sc_im.reshape(n_steps, N_OCTETS, OCT_STATE)], axis=-1)
    sc = jnp.transpose(sc, (1, 0, 2))
    return t_op, mt_op.astype(BF16), pt_op.astype(BF16), sc


def _octet_glu(glu):
    eye = jnp.eye(OCTET, dtype=F32)
    x = glu.astype(F32).reshape(N_OCTETS, OCTET, SSM_GROUP_CH, SSM_GROUP_CH)
    return jnp.einsum('ogce,gh->ogche', x, eye).reshape(N_OCTETS, LANES, LANES).astype(BF16)


def _ssm_prompt_kernel(u_ref, t_ref, mb_ref, pt_ref, sc_ref, d_ref, ga_ref, gb_ref,
                       y_ref, h_ref, uf_ref, ub_ref, st_ref, mt_ref):
    L = SSM_CHUNK
    nt = (((1,), (1,)), ((), ()))

    @pl.when(pl.program_id(1) == 0)
    def _():
        for s in range(L):
            mt_ref[:, s * LANES:(s + 1) * LANES] = mb_ref[0, s]

    n_seq, s_len = u_ref.shape[0], u_ref.shape[1]
    n_chunks = s_len // L
    quarter = s_len // 4
    rows = n_seq * n_chunks

    def staged(sq, step):
        sigma, tau = step % 4, step // 4
        return pl.ds(sq * s_len + sigma * quarter + tau, n_chunks, stride=4)

    for sq in range(n_seq):
        for sigma in range(4):
            st_ref[sq * s_len + sigma * quarter:sq * s_len + (sigma + 1) * quarter, :] = (
                u_ref[sq, pl.ds(sigma, quarter, stride=4), :])
        for step in range(L):
            blk = st_ref[staged(sq, step), :]
            uf_ref[sq * n_chunks:(sq + 1) * n_chunks, step * LANES:(step + 1) * LANES] = blk
            ub_ref[sq * n_chunks:(sq + 1) * n_chunks, step * LANES:(step + 1) * LANES] = blk.astype(BF16)
    ub = ub_ref[...]

    x = lax.dot_general(ub, mt_ref[...], nt, preferred_element_type=F32)
    chunk = lax.broadcasted_iota(jnp.int32, (rows, 1), 0) % n_chunks
    half = OCT_STATE
    k = 1
    step = 0
    while k < n_chunks:
        a_re = sc_ref[0, step:step + 1, :half]
        a_im = sc_ref[0, step:step + 1, half:]
        sh = jnp.where(chunk >= k, pltpu.roll(x, k, axis=0), 0.0)
        s_re, s_im = sh[:, :half], sh[:, half:]
        x = x + jnp.concatenate([a_re * s_re - a_im * s_im, a_re * s_im + a_im * s_re], axis=1)
        k *= 2
        step += 1
    for sq in range(n_seq):
        h_ref[sq, 0] = x[(sq + 1) * n_chunks - 1:(sq + 1) * n_chunks, :]
    h_start = jnp.where(chunk >= 1, pltpu.roll(x, 1, axis=0), 0.0)

    hb = h_start.astype(BF16)
    d = d_ref[0]
    ga = ga_ref[0]
    gb = gb_ref[0]
    for t in range(0, L, 2):
        pair = slice(t * LANES, (t + 2) * LANES)
        y2 = (jnp.dot(ub_ref[:, :(t + 2) * LANES], t_ref[0, (L - 2 - t) * LANES:, :],
                      preferred_element_type=F32)
              + lax.dot_general(hb, pt_ref[0, pair, :], nt, preferred_element_type=F32))
        for j in range(2):
            lanes = slice((t + j) * LANES, (t + j + 1) * LANES)
            g = _gelu_tanh(y2[:, j * LANES:(j + 1) * LANES] + d * uf_ref[:, lanes]).astype(BF16)
            out = (jnp.dot(g, ga, preferred_element_type=F32)
                   * _sigmoid(jnp.dot(g, gb, preferred_element_type=F32)))
            for sq in range(n_seq):
                st_ref[staged(sq, t + j), :] = out[sq * n_chunks:(sq + 1) * n_chunks, :]
    for sq in range(n_seq):
        for sigma in range(4):
            y_ref[sq, pl.ds(sigma, quarter, stride=4), :] = (
                st_ref[sq * s_len + sigma * quarter:sq * s_len + (sigma + 1) * quarter, :])


def _ssm_prompt_call(u, t_op, m_op, p_op, sc, d_oct, ga, gb):
    n, s, _ = u.shape
    L = SSM_CHUNK
    nq = math.gcd(SSM_SEQS, n)
    rows = nq * (s // L)
    wide = L * LANES
    wspec = lambda shape: pl.BlockSpec((1,) + shape, lambda o, n_: (o, 0, 0))
    return pl.pallas_call(
        _ssm_prompt_kernel,
        grid=(N_OCTETS, n // nq),
        in_specs=[pl.BlockSpec((nq, s, LANES), lambda o, n_: (n_, 0, o)),
                  wspec((wide, 2 * LANES)),
                  pl.BlockSpec((1, L, 2 * OCT_STATE, LANES), lambda o, n_: (o, 0, 0, 0)),
                  wspec((wide, 2 * OCT_STATE)),
                  wspec((8, 2 * OCT_STATE)), wspec((1, LANES)),
                  wspec((LANES, LANES)), wspec((LANES, LANES))],
        out_specs=[pl.BlockSpec((nq, s, LANES), lambda o, n_: (n_, 0, o)),
                   pl.BlockSpec((nq, 1, 1, 2 * OCT_STATE), lambda o, n_: (n_, o, 0, 0))],
        out_shape=[jax.ShapeDtypeStruct((n, s, SSM_WIDTH), F32),
                   jax.ShapeDtypeStruct((n, N_OCTETS, 1, 2 * OCT_STATE), F32)],
        scratch_shapes=[pltpu.VMEM((rows, wide), F32),
                        pltpu.VMEM((rows, wide), BF16),
                        pltpu.VMEM((nq * s, LANES), F32),
                        pltpu.VMEM((2 * OCT_STATE, wide), BF16)],
        compiler_params=_cparams(("arbitrary", "arbitrary")),
        name="ssm_prompt",
    )(u, t_op, m_op, p_op, sc, d_oct, ga, gb)


def _attn_decode_kernel(q_ref, kn_ref, vn_ref, kt_ref, vt_ref, b_ref, mult_ref, o_ref):
    nt = (((1,), (1,)), ((), ()))
    mult = mult_ref[...]
    for h in range(ATTN_HEADS):
        q = q_ref[0, h]
        kt = jnp.concatenate([kt_ref[0, h].astype(BF16), kn_ref[0]], axis=1)
        vt = jnp.concatenate([vt_ref[0, h].astype(BF16), vn_ref[0]], axis=1)
        s = jnp.dot(q, kt, preferred_element_type=F32) + b_ref[h]
        m = jnp.max(s, axis=1, keepdims=True)
        p = jnp.exp(s - m) * mult
        den = jnp.sum(p, axis=1, keepdims=True)
        o = lax.dot_general(p.astype(BF16), vt, nt, preferred_element_type=F32)
        o_ref[0, h] = o / den


def _decode_tables(rel_bias, t_len, w_rows):
    slot = LANES // ATTN_HEADS
    t = np.arange(t_len)[:, None]
    new_t = np.arange(LANES)[None, :] % slot
    dist = np.concatenate([w_rows + t - np.arange(w_rows)[None, :],
                           t - new_t], axis=1)
    mult = np.zeros(dist.shape, np.float32)
    for w, r in zip(WINDOWS, DILATIONS):
        mult += (dist >= 0) & (dist % r == 0) & (dist <= w)
    mult = np.concatenate([mult, np.zeros((Q_ROWS - t_len, dist.shape[1]), np.float32)], axis=0)
    mult[t_len:, 0] = 1.0
    by_dist = _bias_by_distance(rel_bias, np.arange(w_rows + t_len))
    rows = []
    for ti in range(t_len):
        cache_part = by_dist[ti + 1:w_rows + ti + 1][::-1]
        new_part = jnp.concatenate([by_dist[:ti + 1][::-1],
                                    jnp.zeros((slot - ti - 1, ATTN_HEADS), F32)], axis=0)
        rows.append(jnp.concatenate([cache_part] + [new_part] * ATTN_HEADS, axis=0))
    bias = jnp.stack(rows + [jnp.zeros_like(rows[0])] * (Q_ROWS - t_len), axis=0)
    bias = jnp.transpose(bias, (2, 0, 1))
    own_head = np.concatenate([np.ones((ATTN_HEADS, w_rows), bool),
                               (np.arange(LANES) // slot)[None, :] == np.arange(ATTN_HEADS)[:, None]], axis=1)
    bias = jnp.where(jnp.asarray((mult[None] > 0) & own_head[:, None, :]), bias, _NEG_INF)
    return bias, jnp.asarray(mult)


def _attn_decode_call(q, k_new, v_new, cache_k, cache_v, rel_bias):
    n, t_len, w = q.shape
    w_rows = cache_k.shape[1]
    if t_len > min(DILATIONS[1:]) or t_len > Q_ROWS or w_rows < max(WINDOWS):
        raise ValueError("unsupported decode shape")
    heads = (ATTN_HEADS, HEAD_DIM)

    def head_major(a, pad_to):
        a = jnp.transpose(a.reshape((n, t_len) + heads), (0, 2, 1, 3))
        return jnp.pad(a, ((0, 0), (0, 0), (0, pad_to - t_len), (0, 0)))

    def new_tile(a):
        a = jnp.transpose(a.reshape((n, t_len) + heads), (0, 3, 2, 1))
        a = jnp.pad(a, ((0, 0), (0, 0), (0, 0), (0, LANES // ATTN_HEADS - t_len)))
        return a.reshape(n, HEAD_DIM, LANES).astype(BF16)

    qh = head_major(q * (HEAD_DIM ** -0.5), Q_ROWS).astype(BF16)
    knt, vnt = new_tile(k_new), new_tile(v_new)
    kt = jnp.transpose(cache_k.astype(F32), (0, 2, 3, 1))
    vt = jnp.transpose(cache_v.astype(F32), (0, 2, 3, 1))
    bias, mult = _decode_tables(rel_bias, t_len, w_rows)
    keys = w_rows + LANES
    per_seq = lambda shape: pl.BlockSpec((1,) + shape, lambda i: (i, 0, 0, 0))
    out = pl.pallas_call(
        _attn_decode_kernel,
        grid=(n,),
        in_specs=[per_seq((ATTN_HEADS, Q_ROWS, HEAD_DIM)),
                  pl.BlockSpec((1, HEAD_DIM, LANES), lambda i: (i, 0, 0)),
                  pl.BlockSpec((1, HEAD_DIM, LANES), lambda i: (i, 0, 0)),
                  per_seq((ATTN_HEADS, HEAD_DIM, w_rows)), per_seq((ATTN_HEADS, HEAD_DIM, w_rows)),
                  pl.BlockSpec((ATTN_HEADS, Q_ROWS, keys), lambda i: (0, 0, 0)),
                  pl.BlockSpec((Q_ROWS, keys), lambda i: (0, 0))],
        out_specs=per_seq((ATTN_HEADS, Q_ROWS, HEAD_DIM)),
        out_shape=jax.ShapeDtypeStruct((n, ATTN_HEADS, Q_ROWS, HEAD_DIM), F32),
        compiler_params=_cparams(("arbitrary",)),
        name="attn_decode",
    )(qh, knt, vnt, kt, vt, bias, mult)
    return jnp.transpose(out[:, :, :t_len], (0, 2, 1, 3)).reshape(n, t_len, w)


def _ssm_decode_kernel(u_ref, hre_ref, him_ref, are_ref, aim_ref, bre_ref, bim_ref,
                       cre_ref, cim_ref, d_ref, ga_ref, gb_ref, y_ref, ore_ref, oim_ref, *, t_len):
    h_re, h_im = hre_ref[...], him_ref[...]
    a_re, a_im = are_ref[...], aim_ref[...]
    for t in range(t_len):
        u = u_ref[t]
        ub = u.astype(BF16)
        n_re = a_re * h_re - a_im * h_im + jnp.dot(ub, bre_ref[...], preferred_element_type=F32)
        n_im = a_re * h_im + a_im * h_re + jnp.dot(ub, bim_ref[...], preferred_element_type=F32)
        h_re, h_im = n_re, n_im
        y = (jnp.dot(h_re.astype(BF16), cre_ref[...], preferred_element_type=F32)
             - jnp.dot(h_im.astype(BF16), cim_ref[...], preferred_element_type=F32)
             + d_ref[...] * u)
        g = _gelu_tanh(y).astype(BF16)
        y_ref[t] = (jnp.dot(g, ga_ref[...], preferred_element_type=F32)
                    * _sigmoid(jnp.dot(g, gb_ref[...], preferred_element_type=F32)))
    ore_ref[...] = h_re
    oim_ref[...] = h_im


def _group_blockdiag(x):
    g, a, b = x.shape
    panel = jnp.transpose(x, (1, 0, 2)).reshape(a, g * b)
    own = jnp.arange(g)[:, None, None] == (jnp.arange(g * b) // b)[None, None, :]
    return jnp.where(own, panel[None], 0).reshape(g * a, g * b)


def _ssm_decode_call(u_tm, h0_re, h0_im, a_re, a_im, log_dt, b_re, b_im, c_re, c_im,
                     d_skip, glu_a, glu_b):
    n = h0_re.shape[0]
    _, _, abar_re, abar_im, bb_re, bb_im = _s5_discretise(a_re, a_im, log_dt, b_re, b_im)
    state = SSM_GROUPS * SSM_STATE
    t_len = u_tm.shape[0]
    args = (u_tm, h0_re.reshape(n, state).astype(F32), h0_im.reshape(n, state).astype(F32),
            abar_re.reshape(1, state), abar_im.reshape(1, state),
            _group_blockdiag(jnp.transpose(bb_re, (0, 2, 1))).astype(BF16),
            _group_blockdiag(jnp.transpose(bb_im, (0, 2, 1))).astype(BF16),
            _group_blockdiag(jnp.transpose(c_re.astype(F32), (0, 2, 1))).astype(BF16),
            _group_blockdiag(jnp.transpose(c_im.astype(F32), (0, 2, 1))).astype(BF16),
            d_skip.astype(F32).reshape(1, SSM_WIDTH),
            _group_blockdiag(glu_a.astype(F32)).astype(BF16),
            _group_blockdiag(glu_b.astype(F32)).astype(BF16))
    full = lambda a: pl.BlockSpec(a.shape, lambda i: (0,) * a.ndim)
    out_shape = [jax.ShapeDtypeStruct(u_tm.shape, F32),
                 jax.ShapeDtypeStruct((n, state), F32), jax.ShapeDtypeStruct((n, state), F32)]
    return pl.pallas_call(
        functools.partial(_ssm_decode_kernel, t_len=t_len),
        grid=(1,),
        in_specs=[full(a) for a in args],
        out_specs=[full(o) for o in out_shape],
        out_shape=out_shape,
        compiler_params=_cparams(("arbitrary",)),
        name="ssm_decode",
    )(*args)


def _outproj_kernel(att_ref, ssm_ref, x_ref, g1_ref, sh2_ref, sc2_ref, ag_ref, sg_ref, n2_ref,
                    wo_ref, wr_ref, rb_ref, x1_ref, h2_ref, route_ref, ids_ref):
    mixed = jnp.concatenate([_rmsnorm(att_ref[...], ag_ref[...]), _rmsnorm(ssm_ref[...], sg_ref[...])],
                            axis=1).astype(BF16)
    x1 = x_ref[...] + g1_ref[0] * jnp.dot(mixed, wo_ref[...], preferred_element_type=F32)
    x1_ref[...] = x1
    h2 = _rmsnorm(x1, n2_ref[...]) * (1.0 + sc2_ref[0]) + sh2_ref[0]
    hi = h2.astype(BF16)
    h2_ref[...] = hi
    lo = (h2 - hi.astype(F32)).astype(BF16)
    r1 = jnp.dot(hi, wr_ref[...], preferred_element_type=F32)
    r2 = jnp.dot(lo, wr_ref[:, :LANES], preferred_element_type=F32)
    logits = r1[:, :LANES] + r1[:, LANES:] + r2 + rb_ref[...]

    lane = lax.broadcasted_iota(jnp.int32, (1, LANES), 1)
    lane_f = lane.astype(F32)
    big = float(LANES)
    ng, epg = N_EXPERT_GROUPS, EXPERTS_PER_GROUP
    lg = jnp.where(lane < ng, logits, _NEG_INF)
    gmax = jnp.max(lg, axis=1, keepdims=True)
    p_star = 1.0 / jnp.sum(jnp.exp(lg - gmax), axis=1, keepdims=True)
    g_star = jnp.min(jnp.where(lg == gmax, lane_f, big), axis=1, keepdims=True)
    in_group = ((lane >= ng) & (lane < ng + ng * epg)
                & (lax.shift_right_arithmetic(lane - ng, int(math.log2(epg))).astype(F32) == g_star))
    le = jnp.where(in_group, logits, _NEG_INF)
    v1 = jnp.max(le, axis=1, keepdims=True)
    i1 = jnp.min(jnp.where(le == v1, lane_f, big), axis=1, keepdims=True)
    le2 = jnp.where(lane_f == i1, _NEG_INF, le)
    v2 = jnp.max(le2, axis=1, keepdims=True)
    i2 = jnp.min(jnp.where(le2 == v2, lane_f, big), axis=1, keepdims=True)
    e2 = jnp.exp(v2 - v1)
    w1 = p_star / (1.0 + e2)
    w2 = p_star * e2 / (1.0 + e2)
    route = jnp.where(lane == 0, i1 - ng,
                      jnp.where(lane == 1, i2 - ng,
                                jnp.where(lane == 2, w1, jnp.where(lane == 3, w2, 0.0))))
    route_ref[...] = route
    ids_ref[...] = route.T[:ids_ref.shape[0], :]


def _outproj_call(att, ssm_y, x_rows, mod, attn_g, ssm_g, norm2_g, w_out_bf, wr, rb):
    rows, d = x_rows.shape
    ts = _row_tile(WIDE_ROW_TILE, rows, mod)
    row = lambda width: pl.BlockSpec((ts, width), lambda i: (i, 0))
    const = lambda a: pl.BlockSpec(a.shape, lambda i: (0,) * a.ndim)
    attn_g = attn_g.reshape(1, ATTN_WIDTH)
    ssm_g = ssm_g.reshape(1, SSM_WIDTH)
    norm2_g = norm2_g.reshape(1, d)
    return pl.pallas_call(
        _outproj_kernel,
        grid=(rows // ts,),
        in_specs=[row(ATTN_WIDTH), row(SSM_WIDTH), row(d),
                  _mod_spec(mod, rows, ts, 2), _mod_spec(mod, rows, ts, 3), _mod_spec(mod, rows, ts, 4),
                  const(attn_g), const(ssm_g), const(norm2_g), const(w_out_bf), const(wr), const(rb)],
        out_specs=[row(d), row(d), row(LANES), pl.BlockSpec((ROUTE_ROWS, ts), lambda i: (0, i))],
        out_shape=[jax.ShapeDtypeStruct((rows, d), F32), jax.ShapeDtypeStruct((rows, d), BF16),
                   jax.ShapeDtypeStruct((rows, LANES), F32),
                   jax.ShapeDtypeStruct((ROUTE_ROWS, rows), F32)],
        compiler_params=_cparams(("arbitrary",)),
        name="outproj_router",
    )(att, ssm_y, x_rows, mod, mod, mod, attn_g, ssm_g, norm2_g, w_out_bf, wr, rb)


def _router_weights(router_g_w, router_g_b, router_e_w, router_e_b):
    d = router_g_w.shape[0]
    ne = N_EXPERT_GROUPS * EXPERTS_PER_GROUP
    w = jnp.concatenate([router_g_w.astype(F32),
                         jnp.transpose(router_e_w.astype(F32), (1, 0, 2)).reshape(d, ne)], axis=1)
    w = jnp.pad(w, ((0, 0), (0, LANES - w.shape[1])))
    hi = w.astype(BF16)
    lo = (w - hi.astype(F32)).astype(BF16)
    b = jnp.concatenate([router_g_b.astype(F32), router_e_b.astype(F32).reshape(ne)])
    b = jnp.pad(b, (0, LANES - b.shape[0])).reshape(1, LANES)
    return jnp.concatenate([hi, lo], axis=1), b


def _moe_kernel(te_ref, x_ref, wg_ref, wu_ref, wd_ref, o_ref, wgb_ref, wub_ref, wdb_ref):
    i = pl.program_id(0)
    changed = jnp.logical_or(i == 0, te_ref[i] != te_ref[jnp.maximum(i - 1, 0)])

    @pl.when(changed)
    def _():
        wgb_ref[...] = wg_ref[0].astype(BF16)
        wub_ref[...] = wu_ref[0].astype(BF16)
        wdb_ref[...] = wd_ref[0].astype(BF16)

    x = x_ref[...]
    gate = jnp.dot(x, wgb_ref[...], preferred_element_type=F32)
    up = jnp.dot(x, wub_ref[...], preferred_element_type=F32)
    a = (gate * _sigmoid(gate)) * up
    o_ref[...] = jnp.dot(a.astype(BF16), wdb_ref[...], preferred_element_type=F32).astype(o_ref.dtype)


def _moe_call(tile_expert, x_sorted, wg, wu, wd, tm):
    n_slots, d = x_sorted.shape
    fe = wg.shape[2]
    grid_spec = pltpu.PrefetchScalarGridSpec(
        num_scalar_prefetch=1,
        grid=(n_slots // tm,),
        in_specs=[pl.BlockSpec((tm, d), lambda i, te: (i, 0)),
                  pl.BlockSpec((1, d, fe), lambda i, te: (te[i], 0, 0)),
                  pl.BlockSpec((1, d, fe), lambda i, te: (te[i], 0, 0)),
                  pl.BlockSpec((1, fe, d), lambda i, te: (te[i], 0, 0))],
        out_specs=pl.BlockSpec((tm, d), lambda i, te: (i, 0)),
        scratch_shapes=[pltpu.VMEM((d, fe), BF16), pltpu.VMEM((d, fe), BF16), pltpu.VMEM((fe, d), BF16)],
    )
    return pl.pallas_call(
        _moe_kernel,
        grid_spec=grid_spec,
        out_shape=jax.ShapeDtypeStruct((n_slots, d), BF16),
        compiler_params=_cparams(("arbitrary",)),
        name="moe_experts",
    )(tile_expert, x_sorted, wg, wu, wd)


def _moe_dispatch(ids_rows, tm):
    n_tok = ids_rows.shape[1]
    ids = jnp.concatenate([ids_rows[0], ids_rows[1]]).astype(jnp.int32)
    n_pairs = ids.shape[0]
    n_slots = (-(-n_pairs // tm) + N_EXPERTS) * tm
    hot = (ids[:, None] == jnp.arange(N_EXPERTS)[None, :]).astype(jnp.int32)
    csum = jnp.cumsum(hot, axis=0)
    rank = jnp.sum((csum - hot) * hot, axis=1)
    counts = csum[-1]
    padded = -(-counts // tm) * tm
    ends = jnp.cumsum(padded)
    starts = ends - padded
    pos = jnp.sum(hot * starts[None, :], axis=1) + rank
    tok_of_slot = (jnp.arange(n_slots, dtype=jnp.int32) % n_tok).at[pos].set(
        jnp.arange(n_pairs, dtype=jnp.int32) % n_tok, unique_indices=True, mode="promise_in_bounds")
    tile_start = jnp.arange(n_slots // tm, dtype=jnp.int32) * tm
    tile_expert = jnp.minimum(jnp.sum((tile_start[:, None] >= ends[None, :]).astype(jnp.int32), axis=1),
                              N_EXPERTS - 1).astype(jnp.int32)
    return pos.reshape(2, n_tok), tok_of_slot, tile_expert


def _take_rows(x, idx):
    return x.at[idx].get(mode="promise_in_bounds")


def _final_kernel(x1_ref, ya_ref, yb_ref, route_ref, g2_ref, fg_ref, o_ref):
    wa = route_ref[:, 2:3]
    wb = route_ref[:, 3:4]
    x = x1_ref[...] + g2_ref[0] * (wa * ya_ref[...].astype(F32) + wb * yb_ref[...].astype(F32))
    o_ref[...] = _rmsnorm(x, fg_ref[...])


def _final_call(x1, ya, yb, route, mod, final_g):
    rows, d = x1.shape
    ts = _row_tile(WIDE_ROW_TILE, rows, mod)
    row = pl.BlockSpec((ts, d), lambda i: (i, 0))
    return pl.pallas_call(
        _final_kernel,
        grid=(rows // ts,),
        in_specs=[row, row, row, pl.BlockSpec((ts, LANES), lambda i: (i, 0)), _mod_spec(mod, rows, ts, 5),
                  pl.BlockSpec((1, d), lambda i: (0, 0))],
        out_specs=row,
        out_shape=jax.ShapeDtypeStruct((rows, d), F32),
        compiler_params=_cparams(("arbitrary",)),
        name="final_norm",
    )(x1, ya, yb, route, mod, final_g.reshape(1, d))


def kernel(x_prompt, x_sample, c_prompt, c_sample, cache_k, cache_v, state_ssm_re, state_ssm_im,
           rel_bias, ada_w, ada_b, norm1_g, w_in, ssm_a_re, ssm_a_im, ssm_log_dt, ssm_b_re, ssm_b_im,
           ssm_c_re, ssm_c_im, ssm_d, glu_a, glu_b, attn_out_g, ssm_out_g, w_out, norm2_g,
           router_g_w, router_g_b, router_e_w, router_e_b, w_gate, w_up, w_down, final_norm_g):
    if ada_w.shape[0] != 1:
        raise ValueError("single-layer trunk expected")
    nb, s_len, d = x_prompt.shape
    nd, t_len, _ = x_sample.shape
    if s_len != max(WINDOWS):
        raise ValueError("prompt length must equal the widest window")
    n_p, n_s = nb * s_len, nd * t_len
    if n_p % ROW_TILE or n_s % ROW_TILE:
        raise ValueError("token counts must be multiples of the row tile")

    mod = _mod_call(jnp.concatenate([c_prompt, c_sample], axis=0).astype(F32), ada_w[0], ada_b[0])
    mod_p = mod[:nb].reshape(nb, 1, 6 * d)
    mod_s = mod[nb:].reshape(1, nd, 6 * d)

    w_in_bf = w_in[0].astype(BF16)
    w_out_bf = w_out[0].astype(BF16)
    xp_rows = x_prompt.reshape(n_p, d)
    xs_rows = jnp.transpose(x_sample, (1, 0, 2)).reshape(n_s, d)

    qp, kp, vp, up, kp_t, vp_t = _inproj_call(xp_rows, mod_p, norm1_g[0], w_in_bf, seq_len=s_len)
    qs, ks, vs, us, ks_t, vs_t = _inproj_call(xs_rows, mod_s, norm1_g[0], w_in_bf, seq_len=nd)

    seq = lambda a: a.reshape(nb, s_len, ATTN_WIDTH)
    att_p = _attn_prompt_call(seq(qp), seq(kp), seq(vp), _prompt_bias_tiles(rel_bias))
    dec = lambda a: jnp.transpose(a.reshape(t_len, nd, ATTN_WIDTH), (1, 0, 2))
    att_s = _attn_decode_call(dec(qs), dec(ks), dec(vs), cache_k[0], cache_v[0], rel_bias)
    att_s = jnp.transpose(att_s, (1, 0, 2)).reshape(n_s, ATTN_WIDTH)

    s5 = (ssm_a_re[0], ssm_a_im[0], ssm_log_dt[0], ssm_b_re[0], ssm_b_im[0], ssm_c_re[0], ssm_c_im[0])
    t_op, m_op, p_op, sc = _s5_prompt_operators(*s5)
    d_oct = ssm_d[0].astype(F32).reshape(N_OCTETS, 1, LANES)
    ssm_p, hT_p = _ssm_prompt_call(seq(up), t_op, m_op, p_op, sc, d_oct,
                                   _octet_glu(glu_a[0]), _octet_glu(glu_b[0]))
    hT_p = hT_p.reshape(nb, N_OCTETS, 2, OCTET, SSM_STATE)
    ssm_re_p = hT_p[:, :, 0].reshape(nb, SSM_GROUPS, SSM_STATE)
    ssm_im_p = hT_p[:, :, 1].reshape(nb, SSM_GROUPS, SSM_STATE)
    ssm_s, hre_s, him_s = _ssm_decode_call(us.reshape(t_len, nd, SSM_WIDTH), state_ssm_re[0],
                                           state_ssm_im[0], *s5, ssm_d[0], glu_a[0], glu_b[0])
    ssm_s = ssm_s.reshape(n_s, SSM_WIDTH)

    wr, rb = _router_weights(router_g_w[0], router_g_b[0], router_e_w[0], router_e_b[0])
    norms = (attn_out_g[0], ssm_out_g[0], norm2_g[0], w_out_bf, wr, rb)
    x1_p, h2_p, route_p, ids_p = _outproj_call(att_p.reshape(n_p, ATTN_WIDTH), ssm_p.reshape(n_p, SSM_WIDTH),
                                        xp_rows, mod_p, *norms)
    ne = N_EXPERTS
    wg = w_gate[0].reshape(ne, d, D_EXPERT)
    wu = w_up[0].reshape(ne, d, D_EXPERT)
    wd = w_down[0].reshape(ne, D_EXPERT, d)

    def experts(x1, x_sorted, pos, tile_expert, route, mod_rows, tm):
        y_slots = _moe_call(tile_expert, x_sorted, wg, wu, wd, tm)
        return _final_call(x1, _take_rows(y_slots, pos[0]), _take_rows(y_slots, pos[1]),
                           route, mod_rows, final_norm_g)

    pos_p, tok_of_slot_p, tile_expert_p = _moe_dispatch(ids_p, MOE_TILE)
    x_sorted_p = _take_rows(h2_p, tok_of_slot_p)
    x_sorted_p, att_s = lax.optimization_barrier((x_sorted_p, att_s))
    y_p = experts(x1_p, x_sorted_p, pos_p, tile_expert_p, route_p, mod_p, MOE_TILE)

    x1_s, h2_s, route_s, ids_s = _outproj_call(att_s, ssm_s, xs_rows, mod_s, *norms)
    pos_s, tok_of_slot_s, tile_expert_s = _moe_dispatch(ids_s, MOE_TILE_DECODE)
    y_s = experts(x1_s, _take_rows(h2_s, tok_of_slot_s), pos_s, tile_expert_s, route_s, mod_s, MOE_TILE_DECODE)

    heads = (ATTN_HEADS, HEAD_DIM)
    cache_out = lambda a: jnp.transpose(a.reshape((1, nb) + heads + (s_len,)), (0, 1, 4, 2, 3))
    step_out = lambda a: jnp.transpose(a.reshape((1, t_len) + heads + (nd,)), (0, 4, 1, 2, 3))
    return (y_p.reshape(nb, s_len, d), jnp.transpose(y_s.reshape(t_len, nd, d), (1, 0, 2)),
            cache_out(kp_t), cache_out(vp_t), step_out(ks_t), step_out(vs_t),
            ssm_re_p[None], ssm_im_p[None],
            hre_s.reshape(1, nd, SSM_GROUPS, SSM_STATE), him_s.reshape(1, nd, SSM_GROUPS, SSM_STATE))
```

```python
import functools
import math

import numpy as np

import jax
import jax.numpy as jnp
from jax import lax
from jax.experimental import pallas as pl
from jax.experimental.pallas import tpu as pltpu

F32 = jnp.float32
BF16 = jnp.bfloat16

D_MODEL = 1024
HEAD_DIM = 64
ATTN_WIDTH = 512
ATTN_HEADS = 8
SSM_WIDTH = 512
SSM_GROUP_CH = 16
SSM_GROUPS = 32
SSM_STATE = 64
WINDOWS = (128, 512, 2048)
DILATIONS = (1, 4, 16)
WINDOW_STEPS = 128
N_BUCKETS = 32
MAX_EXACT = 16
BUCKET_MAX_DIST = 2048
N_EXPERT_GROUPS = 4
EXPERTS_PER_GROUP = 4
N_EXPERTS = N_EXPERT_GROUPS * EXPERTS_PER_GROUP
D_EXPERT = 512
NORM_EPS = 1e-6

LANES = 128
Q_ROWS = 16
OCTET = LANES // SSM_GROUP_CH
N_OCTETS = SSM_GROUPS // OCTET
OCT_STATE = OCTET * SSM_STATE
SSM_CHUNK = 16
SSM_SEQS = 4
ROW_TILE = 512
WIDE_ROW_TILE = 1024
MOE_TILE = 512
MOE_TILE_DECODE = 128
ROUTE_ROWS = 8
ATTN_GROUP = 4
VMEM_LIMIT = 56 * 1024 * 1024

_NEG_INF = float("-inf")
_HIGHEST = lax.Precision.HIGHEST


def _cparams(sem):
    return pltpu.CompilerParams(dimension_semantics=sem, vmem_limit_bytes=VMEM_LIMIT)


def _rmsnorm(x, g):
    return x * lax.rsqrt(jnp.mean(x * x, axis=-1, keepdims=True) + NORM_EPS) * g


def _gelu_tanh(x):
    c = math.sqrt(2.0 / math.pi)
    return 0.5 * x * (1.0 + jnp.tanh(c * (x + 0.044715 * (x * x * x))))


def _sigmoid(x):
    return 1.0 / (1.0 + jnp.exp(-x))


def _mod_kernel(c_ref, w_ref, b_ref, o_ref):
    c = c_ref[...]
    a = (c * _sigmoid(c)).astype(BF16)
    o_ref[...] = jnp.dot(a, w_ref[...].astype(BF16), preferred_element_type=F32) + b_ref[...]


def _mod_call(c_all, ada_w, ada_b):
    rows, d = c_all.shape
    n_out = ada_w.shape[1]
    tn = 1024
    return pl.pallas_call(
        _mod_kernel,
        grid=(n_out // tn,),
        in_specs=[pl.BlockSpec((rows, d), lambda j: (0, 0)),
                  pl.BlockSpec((d, tn), lambda j: (0, j)),
                  pl.BlockSpec((1, tn), lambda j: (0, j))],
        out_specs=pl.BlockSpec((rows, tn), lambda j: (0, j)),
        out_shape=jax.ShapeDtypeStruct((rows, n_out), F32),
        compiler_params=_cparams(("arbitrary",)),
        name="adaln_mod",
    )(c_all, ada_w, ada_b.reshape(1, n_out))


def _mod_spec(mod, rows, ts, chunk):
    if mod.shape[1] == 1:
        tiles_per_group = (rows // mod.shape[0]) // ts
        return pl.BlockSpec((1, 1, D_MODEL), lambda i: (i // tiles_per_group, 0, chunk))
    tiles_per_period = mod.shape[1] // ts
    return pl.BlockSpec((1, ts, D_MODEL), lambda i: (0, i % tiles_per_period, chunk))


def _row_tile(tile, rows, mod):
    return min(tile, rows) if mod.shape[1] == 1 else min(tile, rows, mod.shape[1])


def _inproj_kernel(x_ref, sh_ref, sc_ref, g_ref, w_ref, *rest, key_major):
    h = _rmsnorm(x_ref[...], g_ref[...]) * (1.0 + sc_ref[0]) + sh_ref[0]
    hb = h.astype(BF16)
    z = jnp.dot(hb, w_ref[...], preferred_element_type=F32)
    aw = ATTN_WIDTH
    if key_major:
        q_ref, k_ref, v_ref, u_ref, kt_ref, vt_ref = rest
        kt_ref[0] = z[:, aw:2 * aw].T
        vt_ref[0] = z[:, 2 * aw:3 * aw].T
    else:
        q_ref, k_ref, v_ref, u_ref = rest
    q_ref[...] = z[:, :aw]
    k_ref[...] = z[:, aw:2 * aw]
    v_ref[...] = z[:, 2 * aw:3 * aw]
    u_ref[...] = z[:, 3 * aw:]


def _inproj_call(x_rows, mod, norm_g, w_in_bf, seq_len=None):
    rows, d = x_rows.shape
    ts = _row_tile(ROW_TILE, rows, mod)
    proj = w_in_bf.shape[1]
    out = jax.ShapeDtypeStruct((rows, ATTN_WIDTH), F32)
    ospec = pl.BlockSpec((ts, ATTN_WIDTH), lambda i: (i, 0))
    in_specs = [pl.BlockSpec((ts, d), lambda i: (i, 0)),
                _mod_spec(mod, rows, ts, 0),
                _mod_spec(mod, rows, ts, 1),
                pl.BlockSpec((1, d), lambda i: (0, 0)),
                pl.BlockSpec((d, proj), lambda i: (0, 0))]
    args = [x_rows, mod, mod, norm_g.reshape(1, d), w_in_bf]
    out_specs = [ospec, ospec, ospec, ospec]
    out_shape = [out, out, out, out]
    if seq_len is not None:
        tiles = seq_len // ts
        tspec = pl.BlockSpec((1, ATTN_WIDTH, ts), lambda i: (i // tiles, 0, i % tiles))
        out_specs += [tspec, tspec]
        out_shape += [jax.ShapeDtypeStruct((rows // seq_len, ATTN_WIDTH, seq_len), F32)] * 2
    return pl.pallas_call(
        functools.partial(_inproj_kernel, key_major=seq_len is not None),
        grid=(rows // ts,),
        in_specs=in_specs,
        out_specs=out_specs,
        out_shape=out_shape,
        compiler_params=_cparams(("arbitrary",)),
        name="inproj",
    )(*args)


def _t5_bucket(dist):
    d = jnp.maximum(dist, MAX_EXACT).astype(F32)
    log_part = MAX_EXACT + (jnp.log(d / MAX_EXACT) / math.log(BUCKET_MAX_DIST / MAX_EXACT)
                            * (N_BUCKETS - MAX_EXACT)).astype(jnp.int32)
    return jnp.where(dist < MAX_EXACT, dist, jnp.minimum(log_part, N_BUCKETS - 1))


def _bias_by_distance(rel_bias, dists):
    hot = (_t5_bucket(jnp.asarray(dists, jnp.int32))[:, None]
           == jnp.arange(N_BUCKETS, dtype=jnp.int32)[None, :]).astype(F32)
    return jnp.dot(hot, rel_bias.astype(F32), precision=_HIGHEST)


def _prompt_bias_tiles(rel_bias):
    steps = WINDOW_STEPS
    period = 3 * steps
    tiles = []
    for r in DILATIONS:
        vec = _bias_by_distance(rel_bias, r * np.arange(steps + 1))
        fill = jnp.full((steps - 1, ATTN_HEADS), _NEG_INF, F32)
        w = jnp.concatenate([fill, vec[::-1], fill, fill[:1]], axis=0)
        rep = jnp.tile(w.T, (1, steps))[:, :steps * (period - 1)]
        toe = rep.reshape(ATTN_HEADS, steps, period - 1)[:, :, steps - 1:]
        toe = toe.reshape(ATTN_HEADS // 2, 2, steps, 2 * steps)
        own_only = jnp.where(jnp.arange(2 * steps) < steps, _NEG_INF, toe)
        tiles.append(jnp.stack([toe, own_only], axis=2))
    return jnp.stack(tiles)


def _attn_prompt_kernel(q_ref, k_ref, v_ref, bias_ref, o_ref,
                        p4_ref, qh_ref, kb_ref, vb_ref, s_ref, p_ref, res_ref, stage_ref, nat_ref):
    s_len = q_ref.shape[1]
    steps = WINDOW_STEPS
    n_tiles = s_len // steps
    quarter = s_len // 4
    nt = (((1,), (1,)), ((), ()))
    lane = lax.broadcasted_iota(jnp.int32, (1, LANES), 1)
    first_head = lane < HEAD_DIM
    srcs = (q_ref, k_ref, v_ref)

    kb_ref[0:steps, :] = jnp.zeros((steps, LANES), BF16)
    vb_ref[:, 0:steps, 0:LANES] = jnp.zeros((2, steps, LANES), BF16)
    vb_ref[:, :, LANES:] = jnp.ones((2, s_len + steps, LANES), BF16)
    for x in range(3):
        for sigma in range(4):
            p4_ref[x, sigma * quarter:(sigma + 1) * quarter, :] = srcs[x][0, pl.ds(sigma, quarter, stride=4), :]

    def source(branch, x, tile_idx):
        rows = slice(tile_idx * steps, (tile_idx + 1) * steps)
        if branch == 0:
            return srcs[x][0, rows, :]
        if branch == 1:
            return p4_ref[x, rows, :]
        sigma, tau = tile_idx % 4, tile_idx // 4
        return p4_ref[x, pl.ds(sigma * quarter + tau, steps, stride=4), :]

    def stage_operands(branch):
        for t in range(n_tiles):
            rows = slice(t * steps, (t + 1) * steps)
            q2 = source(branch, 0, t) * (HEAD_DIM ** -0.5)
            qh_ref[0, rows, :] = jnp.where(first_head, q2, 0.0).astype(BF16)
            qh_ref[1, rows, :] = jnp.where(first_head, 0.0, q2).astype(BF16)
            kb_ref[steps + t * steps:steps + (t + 1) * steps, :] = source(branch, 1, t).astype(BF16)
            vb_ref[branch % 2, steps + t * steps:steps + (t + 1) * steps, 0:LANES] = (
                source(branch, 2, t).astype(BF16))

    stage_operands(0)
    for branch, r in enumerate(DILATIONS):
        blocks_per_class = (s_len // r) // steps
        width = 2 * steps if blocks_per_class > 1 else steps

        def aligned(start):
            return start if isinstance(start, int) else pl.multiple_of(start, steps)

        def tile_rows(t):
            return pl.ds(aligned(t * steps), steps)

        def key_rows(t, width=width):
            start = t * steps if width == 2 * steps else (t + 1) * steps
            return pl.ds(aligned(start), width)

        def scores(t, branch=branch, width=width, blocks_per_class=blocks_per_class, key_rows=key_rows):
            rows = tile_rows(t)
            keys = kb_ref[key_rows(t), :]
            first = jnp.where(t % blocks_per_class == 0, 1, 0)
            for hh in range(2):
                if width == 2 * steps:
                    bias = bias_ref[branch, 0, hh, pl.ds(first, 1), :, :][0]
                else:
                    bias = bias_ref[branch, 0, hh, 0, :, steps:]
                sc = lax.dot_general(qh_ref[hh, rows, :], keys, nt, preferred_element_type=F32)
                s_ref[hh, rows, 0:width] = sc + bias

        def softmax(t, width=width):
            rows = tile_rows(t)
            for hh in range(2):
                sc = s_ref[hh, rows, 0:width]
                m = jnp.max(sc, axis=1, keepdims=True)
                p_ref[hh, rows, 0:width] = jnp.exp(sc - m).astype(BF16)
                res_ref[1, rows, hh * HEAD_DIM:(hh + 1) * HEAD_DIM] = jnp.broadcast_to(m, (steps, HEAD_DIM))

        def weighted(t, branch=branch, width=width, key_rows=key_rows):
            rows = tile_rows(t)
            vals = vb_ref[branch % 2, key_rows(t), :]
            r0 = jnp.dot(p_ref[0, rows, 0:width], vals, preferred_element_type=F32)
            r1 = jnp.dot(p_ref[1, rows, 0:width], vals, preferred_element_type=F32)
            res_ref[0, rows, :] = jnp.where(first_head, r0[:, :LANES], r1[:, :LANES])
            res_ref[2, rows, :] = jnp.where(first_head, r0[:, LANES:], r1[:, LANES:])

        def stage(g_scores, g_softmax, scores=scores, softmax=softmax):
            for fn, g in ((softmax, g_softmax), (scores, g_scores)):
                if g is not None:
                    for j in range(ATTN_GROUP):
                        fn(g * ATTN_GROUP + j)

        n_groups = n_tiles // ATTN_GROUP
        stage(0, None)

        def steady(g, carry, stage=stage):
            stage(g, g - 1)
            return carry

        lax.fori_loop(1, n_groups, steady, 0)
        stage(None, n_groups - 1)

        def to_natural_order(branch=branch):
            for kind in range(3):
                if branch == 0:
                    nat_ref[0, kind] = res_ref[kind]
                    continue
                src = res_ref
                if branch == 2:
                    for t in range(n_tiles):
                        sigma, tau = t % 4, t // 4
                        stage_ref[kind, pl.ds(sigma * quarter + tau, steps, stride=4), :] = (
                            res_ref[kind, t * steps:(t + 1) * steps, :])
                    src = stage_ref
                for sigma in range(4):
                    nat_ref[branch, kind, pl.ds(sigma, quarter, stride=4), :] = (
                        src[kind, sigma * quarter:(sigma + 1) * quarter, :])

        @pl.when(pl.program_id(1) >= 0)
        def _(branch=branch, weighted=weighted, to_natural_order=to_natural_order):
            for t in range(n_tiles):
                weighted(t)
            to_natural_order()
            if branch + 1 < len(DILATIONS):
                stage_operands(branch + 1)

    def merge(i, carry):
        rows = pl.ds(pl.multiple_of(i * 256, 256), 256)
        m0, m1, m2 = nat_ref[0, 1, rows, :], nat_ref[1, 1, rows, :], nat_ref[2, 1, rows, :]
        m_all = jnp.maximum(jnp.maximum(m0, m1), m2)
        w0, w1, w2 = jnp.exp(m0 - m_all), jnp.exp(m1 - m_all), jnp.exp(m2 - m_all)
        num = w0 * nat_ref[0, 0, rows, :] + w1 * nat_ref[1, 0, rows, :] + w2 * nat_ref[2, 0, rows, :]
        den = w0 * nat_ref[0, 2, rows, :] + w1 * nat_ref[1, 2, rows, :] + w2 * nat_ref[2, 2, rows, :]
        o_ref[0, rows, :] = num / den
        return carry

    lax.fori_loop(0, s_len // 256, merge, 0)


def _attn_prompt_call(q, k, v, bias_tiles):
    n, s, _ = q.shape
    pairs = ATTN_HEADS // 2
    steps = WINDOW_STEPS
    qspec = pl.BlockSpec((1, s, LANES), lambda g, n_: (n_, 0, g))
    return pl.pallas_call(
        _attn_prompt_kernel,
        grid=(pairs, n),
        in_specs=[qspec, qspec, qspec,
                  pl.BlockSpec((3, 1, 2, 2, steps, 2 * steps), lambda g, n_: (0, g, 0, 0, 0, 0))],
        out_specs=qspec,
        out_shape=jax.ShapeDtypeStruct((n, s, ATTN_WIDTH), F32),
        scratch_shapes=[pltpu.VMEM((3, s, LANES), F32),
                        pltpu.VMEM((2, s, LANES), BF16),
                        pltpu.VMEM((s + steps, LANES), BF16),
                        pltpu.VMEM((2, s + steps, 2 * LANES), BF16),
                        pltpu.VMEM((2, s, 2 * steps), F32),
                        pltpu.VMEM((2, s, 2 * steps), BF16),
                        pltpu.VMEM((3, s, LANES), F32), pltpu.VMEM((3, s, LANES), F32),
                        pltpu.VMEM((3, 3, s, LANES), F32)],
        compiler_params=_cparams(("arbitrary", "arbitrary")),
        name="attn_prompt",
    )(q, k, v, bias_tiles)


def _s5_discretise(a_re, a_im, log_dt, b_re, b_im):
    lam_re = jnp.minimum(a_re.astype(F32), -1e-4)
    lam_im = a_im.astype(F32)
    dt = jnp.exp(log_dt.astype(F32))[:, None]
    mag = jnp.exp(lam_re * dt)
    ph = lam_im * dt
    abar_re, abar_im = mag * jnp.cos(ph), mag * jnp.sin(ph)
    nr, ni = abar_re - 1.0, abar_im
    den = lam_re * lam_re + lam_im * lam_im
    coef_re = (nr * lam_re + ni * lam_im) / den
    coef_im = (ni * lam_re - nr * lam_im) / den
    br, bi = b_re.astype(F32), b_im.astype(F32)
    bbar_re = coef_re[..., None] * br - coef_im[..., None] * bi
    bbar_im = coef_re[..., None] * bi + coef_im[..., None] * br
    return lam_re * dt, ph, abar_re, abar_im, bbar_re, bbar_im


def _abar_power(log_mag, ph, n):
    nf = jnp.asarray(n, F32)[:, None, None]
    mag = jnp.exp(nf * log_mag[None])
    return mag * jnp.cos(nf * ph[None]), mag * jnp.sin(nf * ph[None])


def _s5_prompt_operators(a_re, a_im, log_dt, b_re, b_im, c_re, c_im):
    L = SSM_CHUNK
    log_mag, ph, _, _, bb_re, bb_im = _s5_discretise(a_re, a_im, log_dt, b_re, b_im)
    cr, ci = c_re.astype(F32), c_im.astype(F32)
    pw_re, pw_im = _abar_power(log_mag, ph, np.arange(L + 1))
    eye = jnp.eye(OCTET, dtype=F32)

    ab_re = pw_re[:L, :, :, None] * bb_re[None] - pw_im[:L, :, :, None] * bb_im[None]
    ab_im = pw_re[:L, :, :, None] * bb_im[None] + pw_im[:L, :, :, None] * bb_re[None]
    lag = (jnp.einsum('gop,lgpi->lgoi', cr, ab_re, precision=_HIGHEST)
           - jnp.einsum('gop,lgpi->lgoi', ci, ab_im, precision=_HIGHEST))
    lag = lag.reshape(L, N_OCTETS, OCTET, SSM_GROUP_CH, SSM_GROUP_CH)
    bd = jnp.einsum('logci,gh->olgihc', lag, eye).reshape(N_OCTETS, L, LANES, LANES).astype(BF16)
    stack = bd[:, ::-1].reshape(N_OCTETS, L * LANES, LANES)
    shifted = jnp.concatenate([stack[:, LANES:], jnp.zeros((N_OCTETS, LANES, LANES), BF16)], axis=1)
    t_op = jnp.concatenate([shifted, stack], axis=-1)

    group_of_lane = jnp.arange(LANES) // SSM_GROUP_CH
    parts = []
    for part in (ab_re[::-1], ab_im[::-1]):
        x = part.reshape(L, N_OCTETS, OCTET, SSM_STATE, SSM_GROUP_CH)
        parts.append(jnp.transpose(x, (1, 0, 3, 2, 4)).reshape(N_OCTETS, L, SSM_STATE, LANES))
    x = jnp.stack(parts, axis=2)
    own = (jnp.arange(OCTET)[:, None, None] == group_of_lane[None, None, :])
    mt_op = jnp.where(own, x[:, :, :, None], 0.0)
    mt_op = mt_op.reshape(N_OCTETS, L, 2 * OCT_STATE, LANES)

    p1_re, p1_im = pw_re[1:], pw_im[1:]
    on_re = cr[None] * p1_re[:, :, None, :] - ci[None] * p1_im[:, :, None, :]
    on_im = -cr[None] * p1_im[:, :, None, :] - ci[None] * p1_re[:, :, None, :]
    parts = []
    for part in (on_re, on_im):
        y = part.reshape(L, N_OCTETS, OCTET, SSM_GROUP_CH, SSM_STATE)
        parts.append(jnp.transpose(y, (1, 0, 3, 2, 4)).reshape(N_OCTETS, L, SSM_GROUP_CH, OCT_STATE))
    y = jnp.concatenate(parts, axis=-1)
    group_of_state = (jnp.arange(2 * OCT_STATE) % OCT_STATE) // SSM_STATE
    own = (jnp.arange(OCTET)[:, None, None] == group_of_state[None, None, :])
    pt_op = jnp.where(own, y[:, :, None], 0.0)
    pt_op = pt_op.reshape(N_OCTETS, L * LANES, 2 * OCT_STATE)

    n_steps = 8
    sc_re, sc_im = _abar_power(log_mag, ph, L * (2 ** np.arange(n_steps)))
    sc = jnp.concatenate([sc_re.reshape(n_steps, N_OCTETS, OCT_STATE),
                          sc_im.reshape(n_steps, N_OCTETS, OCT_STATE)], axis=-1)
    sc = jnp.transpose(sc, (1, 0, 2))
    return t_op, mt_op.astype(BF16), pt_op.astype(BF16), sc


def _octet_glu(glu):
    eye = jnp.eye(OCTET, dtype=F32)
    x = glu.astype(F32).reshape(N_OCTETS, OCTET, SSM_GROUP_CH, SSM_GROUP_CH)
    return jnp.einsum('ogce,gh->ogche', x, eye).reshape(N_OCTETS, LANES, LANES).astype(BF16)


def _ssm_prompt_kernel(u_ref, t_ref, mb_ref, pt_ref, sc_ref, d_ref, ga_ref, gb_ref,
                       y_ref, h_ref, uf_ref, ub_ref, st_ref, mt_ref):
    L = SSM_CHUNK
    nt = (((1,), (1,)), ((), ()))

    @pl.when(pl.program_id(1) == 0)
    def _():
        for s in range(L):
            mt_ref[:, s * LANES:(s + 1) * LANES] = mb_ref[0, s]

    n_seq, s_len = u_ref.shape[0], u_ref.shape[1]
    n_chunks = s_len // L
    quarter = s_len // 4
    rows = n_seq * n_chunks

    def staged(sq, step):
        sigma, tau = step % 4, step // 4
        return pl.ds(sq * s_len + sigma * quarter + tau, n_chunks, stride=4)

    for sq in range(n_seq):
        for sigma in range(4):
            st_ref[sq * s_len + sigma * quarter:sq * s_len + (sigma + 1) * quarter, :] = (
                u_ref[sq, pl.ds(sigma, quarter, stride=4), :])
        for step in range(L):
            blk = st_ref[staged(sq, step), :]
            uf_ref[sq * n_chunks:(sq + 1) * n_chunks, step * LANES:(step + 1) * LANES] = blk
            ub_ref[sq * n_chunks:(sq + 1) * n_chunks, step * LANES:(step + 1) * LANES] = blk.astype(BF16)
    ub = ub_ref[...]

    x = lax.dot_general(ub, mt_ref[...], nt, preferred_element_type=F32)
    chunk = lax.broadcasted_iota(jnp.int32, (rows, 1), 0) % n_chunks
    half = OCT_STATE
    k = 1
    step = 0
    while k < n_chunks:
        a_re = sc_ref[0, step:step + 1, :half]
        a_im = sc_ref[0, step:step + 1, half:]
        sh = jnp.where(chunk >= k, pltpu.roll(x, k, axis=0), 0.0)
        s_re, s_im = sh[:, :half], sh[:, half:]
        x = x + jnp.concatenate([a_re * s_re - a_im * s_im, a_re * s_im + a_im * s_re], axis=1)
        k *= 2
        step += 1
    for sq in range(n_seq):
        h_ref[sq, 0] = x[(sq + 1) * n_chunks - 1:(sq + 1) * n_chunks, :]
    h_start = jnp.where(chunk >= 1, pltpu.roll(x, 1, axis=0), 0.0)

    hb = h_start.astype(BF16)
    d = d_ref[0]
    ga = ga_ref[0]
    gb = gb_ref[0]
    for t in range(0, L, 2):
        pair = slice(t * LANES, (t + 2) * LANES)
        y2 = (jnp.dot(ub_ref[:, :(t + 2) * LANES], t_ref[0, (L - 2 - t) * LANES:, :],
                      preferred_element_type=F32)
              + lax.dot_general(hb, pt_ref[0, pair, :], nt, preferred_element_type=F32))
        for j in range(2):
            lanes = slice((t + j) * LANES, (t + j + 1) * LANES)
            g = _gelu_tanh(y2[:, j * LANES:(j + 1) * LANES] + d * uf_ref[:, lanes]).astype(BF16)
            out = (jnp.dot(g, ga, preferred_element_type=F32)
                   * _sigmoid(jnp.dot(g, gb, preferred_element_type=F32)))
            for sq in range(n_seq):
                st_ref[staged(sq, t + j), :] = out[sq * n_chunks:(sq + 1) * n_chunks, :]
    for sq in range(n_seq):
        for sigma in range(4):
            y_ref[sq, pl.ds(sigma, quarter, stride=4), :] = (
                st_ref[sq * s_len + sigma * quarter:sq * s_len + (sigma + 1) * quarter, :])


def _ssm_prompt_call(u, t_op, m_op, p_op, sc, d_oct, ga, gb):
    n, s, _ = u.shape
    L = SSM_CHUNK
    nq = math.gcd(SSM_SEQS, n)
    rows = nq * (s // L)
    wide = L * LANES
    wspec = lambda shape: pl.BlockSpec((1,) + shape, lambda o, n_: (o, 0, 0))
    return pl.pallas_call(
        _ssm_prompt_kernel,
        grid=(N_OCTETS, n // nq),
        in_specs=[pl.BlockSpec((nq, s, LANES), lambda o, n_: (n_, 0, o)),
                  wspec((wide, 2 * LANES)),
                  pl.BlockSpec((1, L, 2 * OCT_STATE, LANES), lambda o, n_: (o, 0, 0, 0)),
                  wspec((wide, 2 * OCT_STATE)),
                  wspec((8, 2 * OCT_STATE)), wspec((1, LANES)),
                  wspec((LANES, LANES)), wspec((LANES, LANES))],
        out_specs=[pl.BlockSpec((nq, s, LANES), lambda o, n_: (n_, 0, o)),
                   pl.BlockSpec((nq, 1, 1, 2 * OCT_STATE), lambda o, n_: (n_, o, 0, 0))],
        out_shape=[jax.ShapeDtypeStruct((n, s, SSM_WIDTH), F32),
                   jax.ShapeDtypeStruct((n, N_OCTETS, 1, 2 * OCT_STATE), F32)],
        scratch_shapes=[pltpu.VMEM((rows, wide), F32),
                        pltpu.VMEM((rows, wide), BF16),
                        pltpu.VMEM((nq * s, LANES), F32),
                        pltpu.VMEM((2 * OCT_STATE, wide), BF16)],
        compiler_params=_cparams(("arbitrary", "arbitrary")),
        name="ssm_prompt",
    )(u, t_op, m_op, p_op, sc, d_oct, ga, gb)


def _attn_decode_kernel(q_ref, kn_ref, vn_ref, kt_ref, vt_ref, b_ref, mult_ref, o_ref):
    nt = (((1,), (1,)), ((), ()))
    mult = mult_ref[...]
    for h in range(ATTN_HEADS):
        q = q_ref[0, h]
        kt = jnp.concatenate([kt_ref[0, h].astype(BF16), kn_ref[0]], axis=1)
        vt = jnp.concatenate([vt_ref[0, h].astype(BF16), vn_ref[0]], axis=1)
        s = jnp.dot(q, kt, preferred_element_type=F32) + b_ref[h]
        m = jnp.max(s, axis=1, keepdims=True)
        p = jnp.exp(s - m) * mult
        den = jnp.sum(p, axis=1, keepdims=True)
        o = lax.dot_general(p.astype(BF16), vt, nt, preferred_element_type=F32)
        o_ref[0, h] = o / den


def _decode_tables(rel_bias, t_len, w_rows):
    slot = LANES // ATTN_HEADS
    t = np.arange(t_len)[:, None]
    new_t = np.arange(LANES)[None, :] % slot
    dist = np.concatenate([w_rows + t - np.arange(w_rows)[None, :],
                           t - new_t], axis=1)
    mult = np.zeros(dist.shape, np.float32)
    for w, r in zip(WINDOWS, DILATIONS):
        mult += (dist >= 0) & (dist % r == 0) & (dist <= w)
    mult = np.concatenate([mult, np.zeros((Q_ROWS - t_len, dist.shape[1]), np.float32)], axis=0)
    mult[t_len:, 0] = 1.0
    by_dist = _bias_by_distance(rel_bias, np.arange(w_rows + t_len))
    rows = []
    for ti in range(t_len):
        cache_part = by_dist[ti + 1:w_rows + ti + 1][::-1]
        new_part = jnp.concatenate([by_dist[:ti + 1][::-1],
                                    jnp.zeros((slot - ti - 1, ATTN_HEADS), F32)], axis=0)
        rows.append(jnp.concatenate([cache_part] + [new_part] * ATTN_HEADS, axis=0))
    bias = jnp.stack(rows + [jnp.zeros_like(rows[0])] * (Q_ROWS - t_len), axis=0)
    bias = jnp.transpose(bias, (2, 0, 1))
    own_head = np.concatenate([np.ones((ATTN_HEADS, w_rows), bool),
                               (np.arange(LANES) // slot)[None, :] == np.arange(ATTN_HEADS)[:, None]], axis=1)
    bias = jnp.where(jnp.asarray((mult[None] > 0) & own_head[:, None, :]), bias, _NEG_INF)
    return bias, jnp.asarray(mult)


def _attn_decode_call(q, k_new, v_new, cache_k, cache_v, rel_bias):
    n, t_len, w = q.shape
    w_rows = cache_k.shape[1]
    if t_len > min(DILATIONS[1:]) or t_len > Q_ROWS or w_rows < max(WINDOWS):
        raise ValueError("unsupported decode shape")
    heads = (ATTN_HEADS, HEAD_DIM)

    def head_major(a, pad_to):
        a = jnp.transpose(a.reshape((n, t_len) + heads), (0, 2, 1, 3))
        return jnp.pad(a, ((0, 0), (0, 0), (0, pad_to - t_len), (0, 0)))

    def new_tile(a):
        a = jnp.transpose(a.reshape((n, t_len) + heads), (0, 3, 2, 1))
        a = jnp.pad(a, ((0, 0), (0, 0), (0, 0), (0, LANES // ATTN_HEADS - t_len)))
        return a.reshape(n, HEAD_DIM, LANES).astype(BF16)

    qh = head_major(q * (HEAD_DIM ** -0.5), Q_ROWS).astype(BF16)
    knt, vnt = new_tile(k_new), new_tile(v_new)
    kt = jnp.transpose(cache_k.astype(F32), (0, 2, 3, 1))
    vt = jnp.transpose(cache_v.astype(F32), (0, 2, 3, 1))
    bias, mult = _decode_tables(rel_bias, t_len, w_rows)
    keys = w_rows + LANES
    per_seq = lambda shape: pl.BlockSpec((1,) + shape, lambda i: (i, 0, 0, 0))
    out = pl.pallas_call(
        _attn_decode_kernel,
        grid=(n,),
        in_specs=[per_seq((ATTN_HEADS, Q_ROWS, HEAD_DIM)),
                  pl.BlockSpec((1, HEAD_DIM, LANES), lambda i: (i, 0, 0)),
                  pl.BlockSpec((1, HEAD_DIM, LANES), lambda i: (i, 0, 0)),
                  per_seq((ATTN_HEADS, HEAD_DIM, w_rows)), per_seq((ATTN_HEADS, HEAD_DIM, w_rows)),
                  pl.BlockSpec((ATTN_HEADS, Q_ROWS, keys), lambda i: (0, 0, 0)),
                  pl.BlockSpec((Q_ROWS, keys), lambda i: (0, 0))],
        out_specs=per_seq((ATTN_HEADS, Q_ROWS, HEAD_DIM)),
        out_shape=jax.ShapeDtypeStruct((n, ATTN_HEADS, Q_ROWS, HEAD_DIM), F32),
        compiler_params=_cparams(("arbitrary",)),
        name="attn_decode",
    )(qh, knt, vnt, kt, vt, bias, mult)
    return jnp.transpose(out[:, :, :t_len], (0, 2, 1, 3)).reshape(n, t_len, w)


def _ssm_decode_kernel(u_ref, hre_ref, him_ref, are_ref, aim_ref, bre_ref, bim_ref,
                       cre_ref, cim_ref, d_ref, ga_ref, gb_ref, y_ref, ore_ref, oim_ref, *, t_len):
    h_re, h_im = hre_ref[...], him_ref[...]
    a_re, a_im = are_ref[...], aim_ref[...]
    for t in range(t_len):
        u = u_ref[t]
        ub = u.astype(BF16)
        n_re = a_re * h_re - a_im * h_im + jnp.dot(ub, bre_ref[...], preferred_element_type=F32)
        n_im = a_re * h_im + a_im * h_re + jnp.dot(ub, bim_ref[...], preferred_element_type=F32)
        h_re, h_im = n_re, n_im
        y = (jnp.dot(h_re.astype(BF16), cre_ref[...], preferred_element_type=F32)
             - jnp.dot(h_im.astype(BF16), cim_ref[...], preferred_element_type=F32)
             + d_ref[...] * u)
        g = _gelu_tanh(y).astype(BF16)
        y_ref[t] = (jnp.dot(g, ga_ref[...], preferred_element_type=F32)
                    * _sigmoid(jnp.dot(g, gb_ref[...], preferred_element_type=F32)))
    ore_ref[...] = h_re
    oim_ref[...] = h_im


def _group_blockdiag(x):
    g, a, b = x.shape
    panel = jnp.transpose(x, (1, 0, 2)).reshape(a, g * b)
    own = jnp.arange(g)[:, None, None] == (jnp.arange(g * b) // b)[None, None, :]
    return jnp.where(own, panel[None], 0).reshape(g * a, g * b)


def _ssm_decode_call(u_tm, h0_re, h0_im, a_re, a_im, log_dt, b_re, b_im, c_re, c_im,
                     d_skip, glu_a, glu_b):
    n = h0_re.shape[0]
    _, _, abar_re, abar_im, bb_re, bb_im = _s5_discretise(a_re, a_im, log_dt, b_re, b_im)
    state = SSM_GROUPS * SSM_STATE
    t_len = u_tm.shape[0]
    args = (u_tm, h0_re.reshape(n, state).astype(F32), h0_im.reshape(n, state).astype(F32),
            abar_re.reshape(1, state), abar_im.reshape(1, state),
            _group_blockdiag(jnp.transpose(bb_re, (0, 2, 1))).astype(BF16),
            _group_blockdiag(jnp.transpose(bb_im, (0, 2, 1))).astype(BF16),
            _group_blockdiag(jnp.transpose(c_re.astype(F32), (0, 2, 1))).astype(BF16),
            _group_blockdiag(jnp.transpose(c_im.astype(F32), (0, 2, 1))).astype(BF16),
            d_skip.astype(F32).reshape(1, SSM_WIDTH),
            _group_blockdiag(glu_a.astype(F32)).astype(BF16),
            _group_blockdiag(glu_b.astype(F32)).astype(BF16))
    full = lambda a: pl.BlockSpec(a.shape, lambda i: (0,) * a.ndim)
    out_shape = [jax.ShapeDtypeStruct(u_tm.shape, F32),
                 jax.ShapeDtypeStruct((n, state), F32), jax.ShapeDtypeStruct((n, state), F32)]
    return pl.pallas_call(
        functools.partial(_ssm_decode_kernel, t_len=t_len),
        grid=(1,),
        in_specs=[full(a) for a in args],
        out_specs=[full(o) for o in out_shape],
        out_shape=out_shape,
        compiler_params=_cparams(("arbitrary",)),
        name="ssm_decode",
    )(*args)


def _outproj_kernel(att_ref, ssm_ref, x_ref, g1_ref, sh2_ref, sc2_ref, ag_ref, sg_ref, n2_ref,
                    wo_ref, wr_ref, rb_ref, x1_ref, h2_ref, route_ref, ids_ref):
    mixed = jnp.concatenate([_rmsnorm(att_ref[...], ag_ref[...]), _rmsnorm(ssm_ref[...], sg_ref[...])],
                            axis=1).astype(BF16)
    x1 = x_ref[...] + g1_ref[0] * jnp.dot(mixed, wo_ref[...], preferred_element_type=F32)
    x1_ref[...] = x1
    h2 = _rmsnorm(x1, n2_ref[...]) * (1.0 + sc2_ref[0]) + sh2_ref[0]
    hi = h2.astype(BF16)
    h2_ref[...] = hi
    lo = (h2 - hi.astype(F32)).astype(BF16)
    r1 = jnp.dot(hi, wr_ref[...], preferred_element_type=F32)
    r2 = jnp.dot(lo, wr_ref[:, :LANES], preferred_element_type=F32)
    logits = r1[:, :LANES] + r1[:, LANES:] + r2 + rb_ref[...]

    lane = lax.broadcasted_iota(jnp.int32, (1, LANES), 1)
    lane_f = lane.astype(F32)
    big = float(LANES)
    ng, epg = N_EXPERT_GROUPS, EXPERTS_PER_GROUP
    lg = jnp.where(lane < ng, logits, _NEG_INF)
    gmax = jnp.max(lg, axis=1, keepdims=True)
    p_star = 1.0 / jnp.sum(jnp.exp(lg - gmax), axis=1, keepdims=True)
    g_star = jnp.min(jnp.where(lg == gmax, lane_f, big), axis=1, keepdims=True)
    in_group = ((lane >= ng) & (lane < ng + ng * epg)
                & (lax.shift_right_arithmetic(lane - ng, int(math.log2(epg))).astype(F32) == g_star))
    le = jnp.where(in_group, logits, _NEG_INF)
    v1 = jnp.max(le, axis=1, keepdims=True)
    i1 = jnp.min(jnp.where(le == v1, lane_f, big), axis=1, keepdims=True)
    le2 = jnp.where(lane_f == i1, _NEG_INF, le)
    v2 = jnp.max(le2, axis=1, keepdims=True)
    i2 = jnp.min(jnp.where(le2 == v2, lane_f, big), axis=1, keepdims=True)
    e2 = jnp.exp(v2 - v1)
    w1 = p_star / (1.0 + e2)
    w2 = p_star * e2 / (1.0 + e2)
    route = jnp.where(lane == 0, i1 - ng,
                      jnp.where(lane == 1, i2 - ng,
                                jnp.where(lane == 2, w1, jnp.where(lane == 3, w2, 0.0))))
    route_ref[...] = route
    ids_ref[...] = route.T[:ids_ref.shape[0], :]


def _outproj_call(att, ssm_y, x_rows, mod, attn_g, ssm_g, norm2_g, w_out_bf, wr, rb):
    rows, d = x_rows.shape
    ts = _row_tile(WIDE_ROW_TILE, rows, mod)
    row = lambda width: pl.BlockSpec((ts, width), lambda i: (i, 0))
    const = lambda a: pl.BlockSpec(a.shape, lambda i: (0,) * a.ndim)
    attn_g = attn_g.reshape(1, ATTN_WIDTH)
    ssm_g = ssm_g.reshape(1, SSM_WIDTH)
    norm2_g = norm2_g.reshape(1, d)
    return pl.pallas_call(
        _outproj_kernel,
        grid=(rows // ts,),
        in_specs=[row(ATTN_WIDTH), row(SSM_WIDTH), row(d),
                  _mod_spec(mod, rows, ts, 2), _mod_spec(mod, rows, ts, 3), _mod_spec(mod, rows, ts, 4),
                  const(attn_g), const(ssm_g), const(norm2_g), const(w_out_bf), const(wr), const(rb)],
        out_specs=[row(d), row(d), row(LANES), pl.BlockSpec((ROUTE_ROWS, ts), lambda i: (0, i))],
        out_shape=[jax.ShapeDtypeStruct((rows, d), F32), jax.ShapeDtypeStruct((rows, d), BF16),
                   jax.ShapeDtypeStruct((rows, LANES), F32),
                   jax.ShapeDtypeStruct((ROUTE_ROWS, rows), F32)],
        compiler_params=_cparams(("arbitrary",)),
        name="outproj_router",
    )(att, ssm_y, x_rows, mod, mod, mod, attn_g, ssm_g, norm2_g, w_out_bf, wr, rb)


def _router_weights(router_g_w, router_g_b, router_e_w, router_e_b):
    d = router_g_w.shape[0]
    ne = N_EXPERT_GROUPS * EXPERTS_PER_GROUP
    w = jnp.concatenate([router_g_w.astype(F32),
                         jnp.transpose(router_e_w.astype(F32), (1, 0, 2)).reshape(d, ne)], axis=1)
    w = jnp.pad(w, ((0, 0), (0, LANES - w.shape[1])))
    hi = w.astype(BF16)
    lo = (w - hi.astype(F32)).astype(BF16)
    b = jnp.concatenate([router_g_b.astype(F32), router_e_b.astype(F32).reshape(ne)])
    b = jnp.pad(b, (0, LANES - b.shape[0])).reshape(1, LANES)
    return jnp.concatenate([hi, lo], axis=1), b


def _moe_kernel(te_ref, x_ref, wg_ref, wu_ref, wd_ref, o_ref, wgb_ref, wub_ref, wdb_ref):
    i = pl.program_id(0)
    changed = jnp.logical_or(i == 0, te_ref[i] != te_ref[jnp.maximum(i - 1, 0)])

    @pl.when(changed)
    def _():
        wgb_ref[...] = wg_ref[0].astype(BF16)
        wub_ref[...] = wu_ref[0].astype(BF16)
        wdb_ref[...] = wd_ref[0].astype(BF16)

    x = x_ref[...]
    gate = jnp.dot(x, wgb_ref[...], preferred_element_type=F32)
    up = jnp.dot(x, wub_ref[...], preferred_element_type=F32)
    a = (gate * _sigmoid(gate)) * up
    o_ref[...] = jnp.dot(a.astype(BF16), wdb_ref[...], preferred_element_type=F32).astype(o_ref.dtype)


def _moe_call(tile_expert, x_sorted, wg, wu, wd, tm):
    n_slots, d = x_sorted.shape
    fe = wg.shape[2]
    grid_spec = pltpu.PrefetchScalarGridSpec(
        num_scalar_prefetch=1,
        grid=(n_slots // tm,),
        in_specs=[pl.BlockSpec((tm, d), lambda i, te: (i, 0)),
                  pl.BlockSpec((1, d, fe), lambda i, te: (te[i], 0, 0)),
                  pl.BlockSpec((1, d, fe), lambda i, te: (te[i], 0, 0)),
                  pl.BlockSpec((1, fe, d), lambda i, te: (te[i], 0, 0))],
        out_specs=pl.BlockSpec((tm, d), lambda i, te: (i, 0)),
        scratch_shapes=[pltpu.VMEM((d, fe), BF16), pltpu.VMEM((d, fe), BF16), pltpu.VMEM((fe, d), BF16)],
    )
    return pl.pallas_call(
        _moe_kernel,
        grid_spec=grid_spec,
        out_shape=jax.ShapeDtypeStruct((n_slots, d), BF16),
        compiler_params=_cparams(("arbitrary",)),
        name="moe_experts",
    )(tile_expert, x_sorted, wg, wu, wd)


def _moe_dispatch(ids_rows, tm):
    n_tok = ids_rows.shape[1]
    ids = jnp.concatenate([ids_rows[0], ids_rows[1]]).astype(jnp.int32)
    n_pairs = ids.shape[0]
    n_slots = (-(-n_pairs // tm) + N_EXPERTS) * tm
    hot = (ids[:, None] == jnp.arange(N_EXPERTS)[None, :]).astype(jnp.int32)
    csum = jnp.cumsum(hot, axis=0)
    rank = jnp.sum((csum - hot) * hot, axis=1)
    counts = csum[-1]
    padded = -(-counts // tm) * tm
    ends = jnp.cumsum(padded)
    starts = ends - padded
    pos = jnp.sum(hot * starts[None, :], axis=1) + rank
    tok_of_slot = (jnp.arange(n_slots, dtype=jnp.int32) % n_tok).at[pos].set(
        jnp.arange(n_pairs, dtype=jnp.int32) % n_tok, unique_indices=True, mode="promise_in_bounds")
    tile_start = jnp.arange(n_slots // tm, dtype=jnp.int32) * tm
    tile_expert = jnp.minimum(jnp.sum((tile_start[:, None] >= ends[None, :]).astype(jnp.int32), axis=1),
                              N_EXPERTS - 1).astype(jnp.int32)
    return pos.reshape(2, n_tok), tok_of_slot, tile_expert


def _take_rows(x, idx):
    return x.at[idx].get(mode="promise_in_bounds")


def _final_kernel(x1_ref, ya_ref, yb_ref, route_ref, g2_ref, fg_ref, o_ref):
    wa = route_ref[:, 2:3]
    wb = route_ref[:, 3:4]
    x = x1_ref[...] + g2_ref[0] * (wa * ya_ref[...].astype(F32) + wb * yb_ref[...].astype(F32))
    o_ref[...] = _rmsnorm(x, fg_ref[...])


def _final_call(x1, ya, yb, route, mod, final_g):
    rows, d = x1.shape
    ts = _row_tile(WIDE_ROW_TILE, rows, mod)
    row = pl.BlockSpec((ts, d), lambda i: (i, 0))
    return pl.pallas_call(
        _final_kernel,
        grid=(rows // ts,),
        in_specs=[row, row, row, pl.BlockSpec((ts, LANES), lambda i: (i, 0)), _mod_spec(mod, rows, ts, 5),
                  pl.BlockSpec((1, d), lambda i: (0, 0))],
        out_specs=row,
        out_shape=jax.ShapeDtypeStruct((rows, d), F32),
        compiler_params=_cparams(("arbitrary",)),
        name="final_norm",
    )(x1, ya, yb, route, mod, final_g.reshape(1, d))


def kernel(x_prompt, x_sample, c_prompt, c_sample, cache_k, cache_v, state_ssm_re, state_ssm_im,
           rel_bias, ada_w, ada_b, norm1_g, w_in, ssm_a_re, ssm_a_im, ssm_log_dt, ssm_b_re, ssm_b_im,
           ssm_c_re, ssm_c_im, ssm_d, glu_a, glu_b, attn_out_g, ssm_out_g, w_out, norm2_g,
           router_g_w, router_g_b, router_e_w, router_e_b, w_gate, w_up, w_down, final_norm_g):
    if ada_w.shape[0] != 1:
        raise ValueError("single-layer trunk expected")
    nb, s_len, d = x_prompt.shape
    nd, t_len, _ = x_sample.shape
    if s_len != max(WINDOWS):
        raise ValueError("prompt length must equal the widest window")
    n_p, n_s = nb * s_len, nd * t_len
    if n_p % ROW_TILE or n_s % ROW_TILE:
        raise ValueError("token counts must be multiples of the row tile")

    mod = _mod_call(jnp.concatenate([c_prompt, c_sample], axis=0).astype(F32), ada_w[0], ada_b[0])
    mod_p = mod[:nb].reshape(nb, 1, 6 * d)
    mod_s = mod[nb:].reshape(1, nd, 6 * d)

    w_in_bf = w_in[0].astype(BF16)
    w_out_bf = w_out[0].astype(BF16)
    xp_rows = x_prompt.reshape(n_p, d)
    xs_rows = jnp.transpose(x_sample, (1, 0, 2)).reshape(n_s, d)

    qp, kp, vp, up, kp_t, vp_t = _inproj_call(xp_rows, mod_p, norm1_g[0], w_in_bf, seq_len=s_len)
    qs, ks, vs, us, ks_t, vs_t = _inproj_call(xs_rows, mod_s, norm1_g[0], w_in_bf, seq_len=nd)

    seq = lambda a: a.reshape(nb, s_len, ATTN_WIDTH)
    att_p = _attn_prompt_call(seq(qp), seq(kp), seq(vp), _prompt_bias_tiles(rel_bias))
    dec = lambda a: jnp.transpose(a.reshape(t_len, nd, ATTN_WIDTH), (1, 0, 2))
    att_s = _attn_decode_call(dec(qs), dec(ks), dec(vs), cache_k[0], cache_v[0], rel_bias)
    att_s = jnp.transpose(att_s, (1, 0, 2)).reshape(n_s, ATTN_WIDTH)

    s5 = (ssm_a_re[0], ssm_a_im[0], ssm_log_dt[0], ssm_b_re[0], ssm_b_im[0], ssm_c_re[0], ssm_c_im[0])
    t_op, m_op, p_op, sc = _s5_prompt_operators(*s5)
    d_oct = ssm_d[0].astype(F32).reshape(N_OCTETS, 1, LANES)
    ssm_p, hT_p = _ssm_prompt_call(seq(up), t_op, m_op, p_op, sc, d_oct,
                                   _octet_glu(glu_a[0]), _octet_glu(glu_b[0]))
    hT_p = hT_p.reshape(nb, N_OCTETS, 2, OCTET, SSM_STATE)
    ssm_re_p = hT_p[:, :, 0].reshape(nb, SSM_GROUPS, SSM_STATE)
    ssm_im_p = hT_p[:, :, 1].reshape(nb, SSM_GROUPS, SSM_STATE)
    ssm_s, hre_s, him_s = _ssm_decode_call(us.reshape(t_len, nd, SSM_WIDTH), state_ssm_re[0],
                                           state_ssm_im[0], *s5, ssm_d[0], glu_a[0], glu_b[0])
    ssm_s = ssm_s.reshape(n_s, SSM_WIDTH)

    wr, rb = _router_weights(router_g_w[0], router_g_b[0], router_e_w[0], router_e_b[0])
    norms = (attn_out_g[0], ssm_out_g[0], norm2_g[0], w_out_bf, wr, rb)
    x1_p, h2_p, route_p, ids_p = _outproj_call(att_p.reshape(n_p, ATTN_WIDTH), ssm_p.reshape(n_p, SSM_WIDTH),
                                        xp_rows, mod_p, *norms)
    ne = N_EXPERTS
    wg = w_gate[0].reshape(ne, d, D_EXPERT)
    wu = w_up[0].reshape(ne, d, D_EXPERT)
    wd = w_down[0].reshape(ne, D_EXPERT, d)

    def experts(x1, x_sorted, pos, tile_expert, route, mod_rows, tm):
        y_slots = _moe_call(tile_expert, x_sorted, wg, wu, wd, tm)
        return _final_call(x1, _take_rows(y_slots, pos[0]), _take_rows(y_slots, pos[1]),
                           route, mod_rows, final_norm_g)

    pos_p, tok_of_slot_p, tile_expert_p = _moe_dispatch(ids_p, MOE_TILE)
    x_sorted_p = _take_rows(h2_p, tok_of_slot_p)
    x_sorted_p, att_s = lax.optimization_barrier((x_sorted_p, att_s))
    y_p = experts(x1_p, x_sorted_p, pos_p, tile_expert_p, route_p, mod_p, MOE_TILE)

    x1_s, h2_s, route_s, ids_s = _outproj_call(att_s, ssm_s, xs_rows, mod_s, *norms)
    pos_s, tok_of_slot_s, tile_expert_s = _moe_dispatch(ids_s, MOE_TILE_DECODE)
    y_s = experts(x1_s, _take_rows(h2_s, tok_of_slot_s), pos_s, tile_expert_s, route_s, mod_s, MOE_TILE_DECODE)

    heads = (ATTN_HEADS, HEAD_DIM)
    cache_out = lambda a: jnp.transpose(a.reshape((1, nb) + heads + (s_len,)), (0, 1, 4, 2, 3))
    step_out = lambda a: jnp.transpose(a.reshape((1, t_len) + heads + (nd,)), (0, 4, 1, 2, 3))
    return (y_p.reshape(nb, s_len, d), jnp.transpose(y_s.reshape(t_len, nd, d), (1, 0, 2)),
            cache_out(kp_t), cache_out(vp_t), step_out(ks_t), step_out(vs_t),
            ssm_re_p[None], ssm_im_p[None],
            hre_s.reshape(1, nd, SSM_GROUPS, SSM_STATE), him_s.reshape(1, nd, SSM_GROUPS, SSM_STATE))
```

```python
import functools
import math

import numpy as np

import jax
import jax.numpy as jnp
from jax import lax
from jax.experimental import pallas as pl
from jax.experimental.pallas import tpu as pltpu

F32 = jnp.float32
BF16 = jnp.bfloat16

D_MODEL = 1024
HEAD_DIM = 64
ATTN_WIDTH = 512
ATTN_HEADS = 8
SSM_WIDTH = 512
SSM_GROUP_CH = 16
SSM_GROUPS = 32
SSM_STATE = 64
WINDOWS = (128, 512, 2048)
DILATIONS = (1, 4, 16)
WINDOW_STEPS = 128
N_BUCKETS = 32
MAX_EXACT = 16
BUCKET_MAX_DIST = 2048
N_EXPERT_GROUPS = 4
EXPERTS_PER_GROUP = 4
N_EXPERTS = N_EXPERT_GROUPS * EXPERTS_PER_GROUP
D_EXPERT = 512
NORM_EPS = 1e-6

LANES = 128
Q_ROWS = 16
OCTET = LANES // SSM_GROUP_CH
N_OCTETS = SSM_GROUPS // OCTET
OCT_STATE = OCTET * SSM_STATE
SSM_CHUNK = 16
SSM_SEQS = 4
ROW_TILE = 512
WIDE_ROW_TILE = 1024
MOE_TILE = 512
MOE_TILE_DECODE = 128
ROUTE_ROWS = 8
ATTN_GROUP = 4
VMEM_LIMIT = 56 * 1024 * 1024

_NEG_INF = float("-inf")
_HIGHEST = lax.Precision.HIGHEST


def _cparams(sem):
    return pltpu.CompilerParams(dimension_semantics=sem, vmem_limit_bytes=VMEM_LIMIT)


def _rmsnorm(x, g):
    return x * lax.rsqrt(jnp.mean(x * x, axis=-1, keepdims=True) + NORM_EPS) * g


def _gelu_tanh(x):
    c = math.sqrt(2.0 / math.pi)
    return 0.5 * x * (1.0 + jnp.tanh(c * (x + 0.044715 * (x * x * x))))


def _sigmoid(x):
    return 1.0 / (1.0 + jnp.exp(-x))


def _mod_kernel(c_ref, w_ref, b_ref, o_ref):
    c = c_ref[...]
    a = (c * _sigmoid(c)).astype(BF16)
    o_ref[...] = jnp.dot(a, w_ref[...].astype(BF16), preferred_element_type=F32) + b_ref[...]


def _mod_call(c_all, ada_w, ada_b):
    rows, d = c_all.shape
    n_out = ada_w.shape[1]
    tn = 1024
    return pl.pallas_call(
        _mod_kernel,
        grid=(n_out // tn,),
        in_specs=[pl.BlockSpec((rows, d), lambda j: (0, 0)),
                  pl.BlockSpec((d, tn), lambda j: (0, j)),
                  pl.BlockSpec((1, tn), lambda j: (0, j))],
        out_specs=pl.BlockSpec((rows, tn), lambda j: (0, j)),
        out_shape=jax.ShapeDtypeStruct((rows, n_out), F32),
        compiler_params=_cparams(("arbitrary",)),
        name="adaln_mod",
    )(c_all, ada_w, ada_b.reshape(1, n_out))


def _mod_spec(mod, rows, ts, chunk):
    if mod.shape[1] == 1:
        tiles_per_group = (rows // mod.shape[0]) // ts
        return pl.BlockSpec((1, 1, D_MODEL), lambda i: (i // tiles_per_group, 0, chunk))
    tiles_per_period = mod.shape[1] // ts
    return pl.BlockSpec((1, ts, D_MODEL), lambda i: (0, i % tiles_per_period, chunk))


def _row_tile(tile, rows, mod):
    return min(tile, rows) if mod.shape[1] == 1 else min(tile, rows, mod.shape[1])


def _inproj_kernel(x_ref, sh_ref, sc_ref, g_ref, w_ref, *rest, key_major):
    h = _rmsnorm(x_ref[...], g_ref[...]) * (1.0 + sc_ref[0]) + sh_ref[0]
    hb = h.astype(BF16)
    z = jnp.dot(hb, w_ref[...], preferred_element_type=F32)
    aw = ATTN_WIDTH
    if key_major:
        q_ref, k_ref, v_ref, u_ref, kt_ref, vt_ref = rest
        kt_ref[0] = z[:, aw:2 * aw].T
        vt_ref[0] = z[:, 2 * aw:3 * aw].T
    else:
        q_ref, k_ref, v_ref, u_ref = rest
    q_ref[...] = z[:, :aw]
    k_ref[...] = z[:, aw:2 * aw]
    v_ref[...] = z[:, 2 * aw:3 * aw]
    u_ref[...] = z[:, 3 * aw:]


def _inproj_call(x_rows, mod, norm_g, w_in_bf, seq_len=None):
    rows, d = x_rows.shape
    ts = _row_tile(ROW_TILE, rows, mod)
    proj = w_in_bf.shape[1]
    out = jax.ShapeDtypeStruct((rows, ATTN_WIDTH), F32)
    ospec = pl.BlockSpec((ts, ATTN_WIDTH), lambda i: (i, 0))
    in_specs = [pl.BlockSpec((ts, d), lambda i: (i, 0)),
                _mod_spec(mod, rows, ts, 0),
                _mod_spec(mod, rows, ts, 1),
                pl.BlockSpec((1, d), lambda i: (0, 0)),
                pl.BlockSpec((d, proj), lambda i: (0, 0))]
    args = [x_rows, mod, mod, norm_g.reshape(1, d), w_in_bf]
    out_specs = [ospec, ospec, ospec, ospec]
    out_shape = [out, out, out, out]
    if seq_len is not None:
        tiles = seq_len // ts
        tspec = pl.BlockSpec((1, ATTN_WIDTH, ts), lambda i: (i // tiles, 0, i % tiles))
        out_specs += [tspec, tspec]
        out_shape += [jax.ShapeDtypeStruct((rows // seq_len, ATTN_WIDTH, seq_len), F32)] * 2
    return pl.pallas_call(
        functools.partial(_inproj_kernel, key_major=seq_len is not None),
        grid=(rows // ts,),
        in_specs=in_specs,
        out_specs=out_specs,
        out_shape=out_shape,
        compiler_params=_cparams(("arbitrary",)),
        name="inproj",
    )(*args)


def _t5_bucket(dist):
    d = jnp.maximum(dist, MAX_EXACT).astype(F32)
    log_part = MAX_EXACT + (jnp.log(d / MAX_EXACT) / math.log(BUCKET_MAX_DIST / MAX_EXACT)
                            * (N_BUCKETS - MAX_EXACT)).astype(jnp.int32)
    return jnp.where(dist < MAX_EXACT, dist, jnp.minimum(log_part, N_BUCKETS - 1))


def _bias_by_distance(rel_bias, dists):
    hot = (_t5_bucket(jnp.asarray(dists, jnp.int32))[:, None]
           == jnp.arange(N_BUCKETS, dtype=jnp.int32)[None, :]).astype(F32)
    return jnp.dot(hot, rel_bias.astype(F32), precision=_HIGHEST)


def _prompt_bias_tiles(rel_bias):
    steps = WINDOW_STEPS
    period = 3 * steps
    tiles = []
    for r in DILATIONS:
        vec = _bias_by_distance(rel_bias, r * np.arange(steps + 1))
        fill = jnp.full((steps - 1, ATTN_HEADS), _NEG_INF, F32)
        w = jnp.concatenate([fill, vec[::-1], fill, fill[:1]], axis=0)
        rep = jnp.tile(w.T, (1, steps))[:, :steps * (period - 1)]
        toe = rep.reshape(ATTN_HEADS, steps, period - 1)[:, :, steps - 1:]
        toe = toe.reshape(ATTN_HEADS // 2, 2, steps, 2 * steps)
        own_only = jnp.where(jnp.arange(2 * steps) < steps, _NEG_INF, toe)
        tiles.append(jnp.stack([toe, own_only], axis=2))
    return jnp.stack(tiles)


def _attn_prompt_kernel(q_ref, k_ref, v_ref, bias_ref, o_ref,
                        p4_ref, kb_ref, vb_ref, s_ref, p_ref, res_ref, stage_ref, nat_ref):
    s_len = q_ref.shape[1]
    steps = WINDOW_STEPS
    n_tiles = s_len // steps
    quarter = s_len // 4
    nt = (((1,), (1,)), ((), ()))
    lane = lax.broadcasted_iota(jnp.int32, (1, LANES), 1)
    first_head = lane < HEAD_DIM
    srcs = (q_ref, k_ref, v_ref)

    kb_ref[0:steps, :] = jnp.zeros((steps, LANES), BF16)
    vb_ref[:, 0:steps, 0:LANES] = jnp.zeros((2, steps, LANES), BF16)
    vb_ref[:, :, LANES:] = jnp.ones((2, s_len + steps, LANES), BF16)
    for x in range(3):
        for sigma in range(4):
            p4_ref[x, sigma * quarter:(sigma + 1) * quarter, :] = srcs[x][0, pl.ds(sigma, quarter, stride=4), :]

    def aligned(start):
        return start if isinstance(start, int) else pl.multiple_of(start, steps)

    def source(branch, x, tile_idx):
        if branch == 2:
            sigma, tau = tile_idx % 4, tile_idx // 4
            return p4_ref[x, pl.ds(sigma * quarter + tau, steps, stride=4), :]
        rows = pl.ds(aligned(tile_idx * steps), steps)
        return srcs[x][0, rows, :] if branch == 0 else p4_ref[x, rows, :]

    def stage_operands(branch):
        for t in range(n_tiles):
            kb_ref[steps + t * steps:steps + (t + 1) * steps, :] = source(branch, 1, t).astype(BF16)
            vb_ref[branch % 2, steps + t * steps:steps + (t + 1) * steps, 0:LANES] = (
                source(branch, 2, t).astype(BF16))

    stage_operands(0)
    for branch, r in enumerate(DILATIONS):
        blocks_per_class = (s_len // r) // steps
        width = 2 * steps if blocks_per_class > 1 else steps

        def tile_rows(t):
            return pl.ds(aligned(t * steps), steps)

        def key_rows(t, width=width):
            start = t * steps if width == 2 * steps else (t + 1) * steps
            return pl.ds(aligned(start), width)

        def scores(t, branch=branch, width=width, blocks_per_class=blocks_per_class, key_rows=key_rows):
            rows = tile_rows(t)
            keys = kb_ref[key_rows(t), :]
            first = jnp.where(t % blocks_per_class == 0, 1, 0)
            q2 = source(branch, 0, t) * (HEAD_DIM ** -0.5)
            for hh in range(2):
                if width == 2 * steps:
                    bias = bias_ref[branch, 0, hh, pl.ds(first, 1), :, :][0]
                else:
                    bias = bias_ref[branch, 0, hh, 0, :, steps:]
                qh = jnp.where(first_head == (hh == 0), q2, 0.0).astype(BF16)
                sc = lax.dot_general(qh, keys, nt, preferred_element_type=F32)
                s_ref[hh, rows, 0:width] = sc + bias

        out_ref = nat_ref.at[0] if branch == 0 else res_ref

        def softmax(t, width=width, out_ref=out_ref):
            rows = tile_rows(t)
            for hh in range(2):
                sc = s_ref[hh, rows, 0:width]
                m = jnp.max(sc, axis=1, keepdims=True)
                p_ref[hh, rows, 0:width] = jnp.exp(sc - m).astype(BF16)
                out_ref[1, rows, hh * HEAD_DIM:(hh + 1) * HEAD_DIM] = jnp.broadcast_to(m, (steps, HEAD_DIM))

        def weighted(t, branch=branch, width=width, key_rows=key_rows, out_ref=out_ref):
            rows = tile_rows(t)
            vals = vb_ref[branch % 2, key_rows(t), :]
            r0 = jnp.dot(p_ref[0, rows, 0:width], vals, preferred_element_type=F32)
            r1 = jnp.dot(p_ref[1, rows, 0:width], vals, preferred_element_type=F32)
            out_ref[0, rows, :] = jnp.where(first_head, r0[:, :LANES], r1[:, :LANES])
            out_ref[2, rows, :] = jnp.where(first_head, r0[:, LANES:], r1[:, LANES:])

        def stage(g_scores, g_softmax, scores=scores, softmax=softmax):
            for fn, g in ((softmax, g_softmax), (scores, g_scores)):
                if g is not None:
                    for j in range(ATTN_GROUP):
                        fn(g * ATTN_GROUP + j)

        n_groups = n_tiles // ATTN_GROUP
        stage(0, None)

        def steady(g, carry, stage=stage):
            stage(g, g - 1)
            return carry

        lax.fori_loop(1, n_groups, steady, 0)
        stage(None, n_groups - 1)

        def to_natural_order(branch=branch):
            for kind in range(3 if branch else 0):
                src = res_ref
                if branch == 2:
                    for t in range(n_tiles):
                        sigma, tau = t % 4, t // 4
                        stage_ref[kind, pl.ds(sigma * quarter + tau, steps, stride=4), :] = (
                            res_ref[kind, t * steps:(t + 1) * steps, :])
                    src = stage_ref
                for sigma in range(4):
                    nat_ref[branch, kind, pl.ds(sigma, quarter, stride=4), :] = (
                        src[kind, sigma * quarter:(sigma + 1) * quarter, :])

        @pl.when(pl.program_id(1) >= 0)
        def _(branch=branch, weighted=weighted, to_natural_order=to_natural_order):
            for t in range(n_tiles):
                weighted(t)
            to_natural_order()
            if branch + 1 < len(DILATIONS):
                stage_operands(branch + 1)

    def merge(i, carry):
        rows = pl.ds(pl.multiple_of(i * 256, 256), 256)
        m0, m1, m2 = nat_ref[0, 1, rows, :], nat_ref[1, 1, rows, :], nat_ref[2, 1, rows, :]
        m_all = jnp.maximum(jnp.maximum(m0, m1), m2)
        w0, w1, w2 = jnp.exp(m0 - m_all), jnp.exp(m1 - m_all), jnp.exp(m2 - m_all)
        num = w0 * nat_ref[0, 0, rows, :] + w1 * nat_ref[1, 0, rows, :] + w2 * nat_ref[2, 0, rows, :]
        den = w0 * nat_ref[0, 2, rows, :] + w1 * nat_ref[1, 2, rows, :] + w2 * nat_ref[2, 2, rows, :]
        o_ref[0, rows, :] = num / den
        return carry

    lax.fori_loop(0, s_len // 256, merge, 0)


def _attn_prompt_call(q, k, v, bias_tiles):
    n, s, _ = q.shape
    pairs = ATTN_HEADS // 2
    steps = WINDOW_STEPS
    qspec = pl.BlockSpec((1, s, LANES), lambda g, n_: (n_, 0, g))
    return pl.pallas_call(
        _attn_prompt_kernel,
        grid=(pairs, n),
        in_specs=[qspec, qspec, qspec,
                  pl.BlockSpec((3, 1, 2, 2, steps, 2 * steps), lambda g, n_: (0, g, 0, 0, 0, 0))],
        out_specs=qspec,
        out_shape=jax.ShapeDtypeStruct((n, s, ATTN_WIDTH), F32),
        scratch_shapes=[pltpu.VMEM((3, s, LANES), F32),
                        pltpu.VMEM((s + steps, LANES), BF16),
                        pltpu.VMEM((2, s + steps, 2 * LANES), BF16),
                        pltpu.VMEM((2, s, 2 * steps), F32),
                        pltpu.VMEM((2, s, 2 * steps), BF16),
                        pltpu.VMEM((3, s, LANES), F32), pltpu.VMEM((3, s, LANES), F32),
                        pltpu.VMEM((3, 3, s, LANES), F32)],
        compiler_params=_cparams(("arbitrary", "arbitrary")),
        name="attn_prompt",
    )(q, k, v, bias_tiles)


def _s5_discretise(a_re, a_im, log_dt, b_re, b_im):
    lam_re = jnp.minimum(a_re.astype(F32), -1e-4)
    lam_im = a_im.astype(F32)
    dt = jnp.exp(log_dt.astype(F32))[:, None]
    mag = jnp.exp(lam_re * dt)
    ph = lam_im * dt
    abar_re, abar_im = mag * jnp.cos(ph), mag * jnp.sin(ph)
    nr, ni = abar_re - 1.0, abar_im
    den = lam_re * lam_re + lam_im * lam_im
    coef_re = (nr * lam_re + ni * lam_im) / den
    coef_im = (ni * lam_re - nr * lam_im) / den
    br, bi = b_re.astype(F32), b_im.astype(F32)
    bbar_re = coef_re[..., None] * br - coef_im[..., None] * bi
    bbar_im = coef_re[..., None] * bi + coef_im[..., None] * br
    return lam_re * dt, ph, abar_re, abar_im, bbar_re, bbar_im


def _abar_power(log_mag, ph, n):
    nf = jnp.asarray(n, F32)[:, None, None]
    mag = jnp.exp(nf * log_mag[None])
    return mag * jnp.cos(nf * ph[None]), mag * jnp.sin(nf * ph[None])


def _s5_prompt_operators(a_re, a_im, log_dt, b_re, b_im, c_re, c_im):
    L = SSM_CHUNK
    log_mag, ph, _, _, bb_re, bb_im = _s5_discretise(a_re, a_im, log_dt, b_re, b_im)
    cr, ci = c_re.astype(F32), c_im.astype(F32)
    pw_re, pw_im = _abar_power(log_mag, ph, np.arange(L + 1))
    eye = jnp.eye(OCTET, dtype=F32)

    ab_re = pw_re[:L, :, :, None] * bb_re[None] - pw_im[:L, :, :, None] * bb_im[None]
    ab_im = pw_re[:L, :, :, None] * bb_im[None] + pw_im[:L, :, :, None] * bb_re[None]
    lag = (jnp.einsum('gop,lgpi->lgoi', cr, ab_re, precision=_HIGHEST)
           - jnp.einsum('gop,lgpi->lgoi', ci, ab_im, precision=_HIGHEST))
    lag = lag.reshape(L, N_OCTETS, OCTET, SSM_GROUP_CH, SSM_GROUP_CH)
    bd = jnp.einsum('logci,gh->olgihc', lag, eye).reshape(N_OCTETS, L, LANES, LANES).astype(BF16)
    stack = bd[:, ::-1].reshape(N_OCTETS, L * LANES, LANES)
    shifted = jnp.concatenate([stack[:, LANES:], jnp.zeros((N_OCTETS, LANES, LANES), BF16)], axis=1)
    t_op = jnp.concatenate([shifted, stack], axis=-1)

    group_of_lane = jnp.arange(LANES) // SSM_GROUP_CH
    parts = []
    for part in (ab_re[::-1], ab_im[::-1]):
        x = part.reshape(L, N_OCTETS, OCTET, SSM_STATE, SSM_GROUP_CH)
        parts.append(jnp.transpose(x, (1, 0, 3, 2, 4)).reshape(N_OCTETS, L, SSM_STATE, LANES))
    x = jnp.stack(parts, axis=2)
    own = (jnp.arange(OCTET)[:, None, None] == group_of_lane[None, None, :])
    mt_op = jnp.where(own, x[:, :, :, None], 0.0)
    mt_op = mt_op.reshape(N_OCTETS, L, 2 * OCT_STATE, LANES)

    p1_re, p1_im = pw_re[1:], pw_im[1:]
    on_re = cr[None] * p1_re[:, :, None, :] - ci[None] * p1_im[:, :, None, :]
    on_im = -cr[None] * p1_im[:, :, None, :] - ci[None] * p1_re[:, :, None, :]
    parts = []
    for part in (on_re, on_im):
        y = part.reshape(L, N_OCTETS, OCTET, SSM_GROUP_CH, SSM_STATE)
        parts.append(jnp.transpose(y, (1, 0, 3, 2, 4)).reshape(N_OCTETS, L, SSM_GROUP_CH, OCT_STATE))
    y = jnp.concatenate(parts, axis=-1)
    group_of_state = (jnp.arange(2 * OCT_STATE) % OCT_STATE) // SSM_STATE
    own = (jnp.arange(OCTET)[:, None, None] == group_of_state[None, None, :])
    pt_op = jnp.where(own, y[:, :, None], 0.0)
    pt_op = pt_op.reshape(N_OCTETS, L * LANES, 2 * OCT_STATE)

    n_steps = 8
    sc_re, sc_im = _abar_power(log_mag, ph, L * (2 ** np.arange(n_steps)))
    sc = jnp.concatenate([sc_re.reshape(n_steps, N_OCTETS, OCT_STATE),
                          sc_im.reshape(n_steps, N_OCTETS, OCT_STATE)], axis=-1)
    sc = jnp.transpose(sc, (1, 0, 2))
    return t_op, mt_op.astype(BF16), pt_op.astype(BF16), sc


def _octet_glu(glu):
    eye = jnp.eye(OCTET, dtype=F32)
    x = glu.astype(F32).reshape(N_OCTETS, OCTET, SSM_GROUP_CH, SSM_GROUP_CH)
    return jnp.einsum('ogce,gh->ogche', x, eye).reshape(N_OCTETS, LANES, LANES).astype(BF16)


def _ssm_prompt_kernel(u_ref, t_ref, mb_ref, pt_ref, sc_ref, d_ref, ga_ref, gb_ref,
                       y_ref, h_ref, uf_ref, ub_ref, st_ref, mt_ref):
    L = SSM_CHUNK
    nt = (((1,), (1,)), ((), ()))

    @pl.when(pl.program_id(1) == 0)
    def _():
        for s in range(L):
            mt_ref[:, s * LANES:(s + 1) * LANES] = mb_ref[0, s]

    n_seq, s_len = u_ref.shape[0], u_ref.shape[1]
    n_chunks = s_len // L
    quarter = s_len // 4
    rows = n_seq * n_chunks

    def staged(sq, step):
        sigma, tau = step % 4, step // 4
        return pl.ds(sq * s_len + sigma * quarter + tau, n_chunks, stride=4)

    for sq in range(n_seq):
        for sigma in range(4):
            st_ref[sq * s_len + sigma * quarter:sq * s_len + (sigma + 1) * quarter, :] = (
                u_ref[sq, pl.ds(sigma, quarter, stride=4), :])
        for step in range(L):
            blk = st_ref[staged(sq, step), :]
            uf_ref[sq * n_chunks:(sq + 1) * n_chunks, step * LANES:(step + 1) * LANES] = blk
            ub_ref[sq * n_chunks:(sq + 1) * n_chunks, step * LANES:(step + 1) * LANES] = blk.astype(BF16)
    ub = ub_ref[...]

    x = lax.dot_general(ub, mt_ref[...], nt, preferred_element_type=F32)
    chunk = lax.broadcasted_iota(jnp.int32, (rows, 1), 0) % n_chunks
    half = OCT_STATE
    k = 1
    step = 0
    while k < n_chunks:
        a_re = sc_ref[0, step:step + 1, :half]
        a_im = sc_ref[0, step:step + 1, half:]
        sh = jnp.where(chunk >= k, pltpu.roll(x, k, axis=0), 0.0)
        s_re, s_im = sh[:, :half], sh[:, half:]
        x = x + jnp.concatenate([a_re * s_re - a_im * s_im, a_re * s_im + a_im * s_re], axis=1)
        k *= 2
        step += 1
    for sq in range(n_seq):
        h_ref[sq, 0] = x[(sq + 1) * n_chunks - 1:(sq + 1) * n_chunks, :]
    h_start = jnp.where(chunk >= 1, pltpu.roll(x, 1, axis=0), 0.0)

    hb = h_start.astype(BF16)
    d = d_ref[0]
    ga = ga_ref[0]
    gb = gb_ref[0]
    for t in range(0, L, 2):
        pair = slice(t * LANES, (t + 2) * LANES)
        y2 = (jnp.dot(ub_ref[:, :(t + 2) * LANES], t_ref[0, (L - 2 - t) * LANES:, :],
                      preferred_element_type=F32)
              + lax.dot_general(hb, pt_ref[0, pair, :], nt, preferred_element_type=F32))
        for j in range(2):
            lanes = slice((t + j) * LANES, (t + j + 1) * LANES)
            g = _gelu_tanh(y2[:, j * LANES:(j + 1) * LANES] + d * uf_ref[:, lanes]).astype(BF16)
            out = (jnp.dot(g, ga, preferred_element_type=F32)
                   * _sigmoid(jnp.dot(g, gb, preferred_element_type=F32)))
            for sq in range(n_seq):
                st_ref[staged(sq, t + j), :] = out[sq * n_chunks:(sq + 1) * n_chunks, :]
    for sq in range(n_seq):
        for sigma in range(4):
            y_ref[sq, pl.ds(sigma, quarter, stride=4), :] = (
                st_ref[sq * s_len + sigma * quarter:sq * s_len + (sigma + 1) * quarter, :])


def _ssm_prompt_call(u, t_op, m_op, p_op, sc, d_oct, ga, gb):
    n, s, _ = u.shape
    L = SSM_CHUNK
    nq = math.gcd(SSM_SEQS, n)
    rows = nq * (s // L)
    wide = L * LANES
    wspec = lambda shape: pl.BlockSpec((1,) + shape, lambda o, n_: (o, 0, 0))
    return pl.pallas_call(
        _ssm_prompt_kernel,
        grid=(N_OCTETS, n // nq),
        in_specs=[pl.BlockSpec((nq, s, LANES), lambda o, n_: (n_, 0, o)),
                  wspec((wide, 2 * LANES)),
                  pl.BlockSpec((1, L, 2 * OCT_STATE, LANES), lambda o, n_: (o, 0, 0, 0)),
                  wspec((wide, 2 * OCT_STATE)),
                  wspec((8, 2 * OCT_STATE)), wspec((1, LANES)),
                  wspec((LANES, LANES)), wspec((LANES, LANES))],
        out_specs=[pl.BlockSpec((nq, s, LANES), lambda o, n_: (n_, 0, o)),
                   pl.BlockSpec((nq, 1, 1, 2 * OCT_STATE), lambda o, n_: (n_, o, 0, 0))],
        out_shape=[jax.ShapeDtypeStruct((n, s, SSM_WIDTH), F32),
                   jax.ShapeDtypeStruct((n, N_OCTETS, 1, 2 * OCT_STATE), F32)],
        scratch_shapes=[pltpu.VMEM((rows, wide), F32),
                        pltpu.VMEM((rows, wide), BF16),
                        pltpu.VMEM((nq * s, LANES), F32),
                        pltpu.VMEM((2 * OCT_STATE, wide), BF16)],
        compiler_params=_cparams(("arbitrary", "arbitrary")),
        name="ssm_prompt",
    )(u, t_op, m_op, p_op, sc, d_oct, ga, gb)


def _attn_decode_kernel(q_ref, kn_ref, vn_ref, kt_ref, vt_ref, b_ref, mult_ref, o_ref):
    nt = (((1,), (1,)), ((), ()))
    mult = mult_ref[...]
    for h in range(ATTN_HEADS):
        q = q_ref[0, h]
        kt = jnp.concatenate([kt_ref[0, h].astype(BF16), kn_ref[0]], axis=1)
        vt = jnp.concatenate([vt_ref[0, h].astype(BF16), vn_ref[0]], axis=1)
        s = jnp.dot(q, kt, preferred_element_type=F32) + b_ref[h]
        m = jnp.max(s, axis=1, keepdims=True)
        p = jnp.exp(s - m) * mult
        den = jnp.sum(p, axis=1, keepdims=True)
        o = lax.dot_general(p.astype(BF16), vt, nt, preferred_element_type=F32)
        o_ref[0, h] = o / den


def _decode_tables(rel_bias, t_len, w_rows):
    slot = LANES // ATTN_HEADS
    t = np.arange(t_len)[:, None]
    new_t = np.arange(LANES)[None, :] % slot
    dist = np.concatenate([w_rows + t - np.arange(w_rows)[None, :],
                           t - new_t], axis=1)
    mult = np.zeros(dist.shape, np.float32)
    for w, r in zip(WINDOWS, DILATIONS):
        mult += (dist >= 0) & (dist % r == 0) & (dist <= w)
    mult = np.concatenate([mult, np.zeros((Q_ROWS - t_len, dist.shape[1]), np.float32)], axis=0)
    mult[t_len:, 0] = 1.0
    by_dist = _bias_by_distance(rel_bias, np.arange(w_rows + t_len))
    rows = []
    for ti in range(t_len):
        cache_part = by_dist[ti + 1:w_rows + ti + 1][::-1]
        new_part = jnp.concatenate([by_dist[:ti + 1][::-1],
                                    jnp.zeros((slot - ti - 1, ATTN_HEADS), F32)], axis=0)
        rows.append(jnp.concatenate([cache_part] + [new_part] * ATTN_HEADS, axis=0))
    bias = jnp.stack(rows + [jnp.zeros_like(rows[0])] * (Q_ROWS - t_len), axis=0)
    bias = jnp.transpose(bias, (2, 0, 1))
    own_head = np.concatenate([np.ones((ATTN_HEADS, w_rows), bool),
                               (np.arange(LANES) // slot)[None, :] == np.arange(ATTN_HEADS)[:, None]], axis=1)
    bias = jnp.where(jnp.asarray((mult[None] > 0) & own_head[:, None, :]), bias, _NEG_INF)
    return bias, jnp.asarray(mult)


def _attn_decode_call(q, k_new, v_new, cache_k, cache_v, rel_bias):
    n, t_len, w = q.shape
    w_rows = cache_k.shape[1]
    if t_len > min(DILATIONS[1:]) or t_len > Q_ROWS or w_rows < max(WINDOWS):
        raise ValueError("unsupported decode shape")
    heads = (ATTN_HEADS, HEAD_DIM)

    def head_major(a, pad_to):
        a = jnp.transpose(a.reshape((n, t_len) + heads), (0, 2, 1, 3))
        return jnp.pad(a, ((0, 0), (0, 0), (0, pad_to - t_len), (0, 0)))

    def new_tile(a):
        a = jnp.transpose(a.reshape((n, t_len) + heads), (0, 3, 2, 1))
        a = jnp.pad(a, ((0, 0), (0, 0), (0, 0), (0, LANES // ATTN_HEADS - t_len)))
        return a.reshape(n, HEAD_DIM, LANES).astype(BF16)

    qh = head_major(q * (HEAD_DIM ** -0.5), Q_ROWS).astype(BF16)
    knt, vnt = new_tile(k_new), new_tile(v_new)
    kt = jnp.transpose(cache_k.astype(F32), (0, 2, 3, 1))
    vt = jnp.transpose(cache_v.astype(F32), (0, 2, 3, 1))
    bias, mult = _decode_tables(rel_bias, t_len, w_rows)
    keys = w_rows + LANES
    per_seq = lambda shape: pl.BlockSpec((1,) + shape, lambda i: (i, 0, 0, 0))
    out = pl.pallas_call(
        _attn_decode_kernel,
        grid=(n,),
        in_specs=[per_seq((ATTN_HEADS, Q_ROWS, HEAD_DIM)),
                  pl.BlockSpec((1, HEAD_DIM, LANES), lambda i: (i, 0, 0)),
                  pl.BlockSpec((1, HEAD_DIM, LANES), lambda i: (i, 0, 0)),
                  per_seq((ATTN_HEADS, HEAD_DIM, w_rows)), per_seq((ATTN_HEADS, HEAD_DIM, w_rows)),
                  pl.BlockSpec((ATTN_HEADS, Q_ROWS, keys), lambda i: (0, 0, 0)),
                  pl.BlockSpec((Q_ROWS, keys), lambda i: (0, 0))],
        out_specs=per_seq((ATTN_HEADS, Q_ROWS, HEAD_DIM)),
        out_shape=jax.ShapeDtypeStruct((n, ATTN_HEADS, Q_ROWS, HEAD_DIM), F32),
        compiler_params=_cparams(("arbitrary",)),
        name="attn_decode",
    )(qh, knt, vnt, kt, vt, bias, mult)
    return jnp.transpose(out[:, :, :t_len], (0, 2, 1, 3)).reshape(n, t_len, w)


def _ssm_decode_kernel(u_ref, hre_ref, him_ref, are_ref, aim_ref, bre_ref, bim_ref,
                       cre_ref, cim_ref, d_ref, ga_ref, gb_ref, y_ref, ore_ref, oim_ref, *, t_len):
    h_re, h_im = hre_ref[...], him_ref[...]
    a_re, a_im = are_ref[...], aim_ref[...]
    for t in range(t_len):
        u = u_ref[t]
        ub = u.astype(BF16)
        n_re = a_re * h_re - a_im * h_im + jnp.dot(ub, bre_ref[...], preferred_element_type=F32)
        n_im = a_re * h_im + a_im * h_re + jnp.dot(ub, bim_ref[...], preferred_element_type=F32)
        h_re, h_im = n_re, n_im
        y = (jnp.dot(h_re.astype(BF16), cre_ref[...], preferred_element_type=F32)
             - jnp.dot(h_im.astype(BF16), cim_ref[...], preferred_element_type=F32)
             + d_ref[...] * u)
        g = _gelu_tanh(y).astype(BF16)
        y_ref[t] = (jnp.dot(g, ga_ref[...], preferred_element_type=F32)
                    * _sigmoid(jnp.dot(g, gb_ref[...], preferred_element_type=F32)))
    ore_ref[...] = h_re
    oim_ref[...] = h_im


def _group_blockdiag(x):
    g, a, b = x.shape
    panel = jnp.transpose(x, (1, 0, 2)).reshape(a, g * b)
    own = jnp.arange(g)[:, None, None] == (jnp.arange(g * b) // b)[None, None, :]
    return jnp.where(own, panel[None], 0).reshape(g * a, g * b)


def _ssm_decode_call(u_tm, h0_re, h0_im, a_re, a_im, log_dt, b_re, b_im, c_re, c_im,
                     d_skip, glu_a, glu_b):
    n = h0_re.shape[0]
    _, _, abar_re, abar_im, bb_re, bb_im = _s5_discretise(a_re, a_im, log_dt, b_re, b_im)
    state = SSM_GROUPS * SSM_STATE
    t_len = u_tm.shape[0]
    args = (u_tm, h0_re.reshape(n, state).astype(F32), h0_im.reshape(n, state).astype(F32),
            abar_re.reshape(1, state), abar_im.reshape(1, state),
            _group_blockdiag(jnp.transpose(bb_re, (0, 2, 1))).astype(BF16),
            _group_blockdiag(jnp.transpose(bb_im, (0, 2, 1))).astype(BF16),
            _group_blockdiag(jnp.transpose(c_re.astype(F32), (0, 2, 1))).astype(BF16),
            _group_blockdiag(jnp.transpose(c_im.astype(F32), (0, 2, 1))).astype(BF16),
            d_skip.astype(F32).reshape(1, SSM_WIDTH),
            _group_blockdiag(glu_a.astype(F32)).astype(BF16),
            _group_blockdiag(glu_b.astype(F32)).astype(BF16))
    full = lambda a: pl.BlockSpec(a.shape, lambda i: (0,) * a.ndim)
    out_shape = [jax.ShapeDtypeStruct(u_tm.shape, F32),
                 jax.ShapeDtypeStruct((n, state), F32), jax.ShapeDtypeStruct((n, state), F32)]
    return pl.pallas_call(
        functools.partial(_ssm_decode_kernel, t_len=t_len),
        grid=(1,),
        in_specs=[full(a) for a in args],
        out_specs=[full(o) for o in out_shape],
        out_shape=out_shape,
        compiler_params=_cparams(("arbitrary",)),
        name="ssm_decode",
    )(*args)


def _outproj_kernel(att_ref, ssm_ref, x_ref, g1_ref, sh2_ref, sc2_ref, ag_ref, sg_ref, n2_ref,
                    wo_ref, wr_ref, rb_ref, x1_ref, h2_ref, route_ref, ids_ref):
    mixed = jnp.concatenate([_rmsnorm(att_ref[...], ag_ref[...]), _rmsnorm(ssm_ref[...], sg_ref[...])],
                            axis=1).astype(BF16)
    x1 = x_ref[...] + g1_ref[0] * jnp.dot(mixed, wo_ref[...], preferred_element_type=F32)
    x1_ref[...] = x1
    h2 = _rmsnorm(x1, n2_ref[...]) * (1.0 + sc2_ref[0]) + sh2_ref[0]
    hi = h2.astype(BF16)
    h2_ref[...] = hi
    lo = (h2 - hi.astype(F32)).astype(BF16)
    r1 = jnp.dot(hi, wr_ref[...], preferred_element_type=F32)
    r2 = jnp.dot(lo, wr_ref[:, :LANES], preferred_element_type=F32)
    logits = r1[:, :LANES] + r1[:, LANES:] + r2 + rb_ref[...]

    lane = lax.broadcasted_iota(jnp.int32, (1, LANES), 1)
    lane_f = lane.astype(F32)
    big = float(LANES)
    ng, epg = N_EXPERT_GROUPS, EXPERTS_PER_GROUP
    lg = jnp.where(lane < ng, logits, _NEG_INF)
    gmax = jnp.max(lg, axis=1, keepdims=True)
    p_star = 1.0 / jnp.sum(jnp.exp(lg - gmax), axis=1, keepdims=True)
    g_star = jnp.min(jnp.where(lg == gmax, lane_f, big), axis=1, keepdims=True)
    in_group = ((lane >= ng) & (lane < ng + ng * epg)
                & (lax.shift_right_arithmetic(lane - ng, int(math.log2(epg))).astype(F32) == g_star))
    le = jnp.where(in_group, logits, _NEG_INF)
    v1 = jnp.max(le, axis=1, keepdims=True)
    i1 = jnp.min(jnp.where(le == v1, lane_f, big), axis=1, keepdims=True)
    le2 = jnp.where(lane_f == i1, _NEG_INF, le)
    v2 = jnp.max(le2, axis=1, keepdims=True)
    i2 = jnp.min(jnp.where(le2 == v2, lane_f, big), axis=1, keepdims=True)
    e2 = jnp.exp(v2 - v1)
    w1 = p_star / (1.0 + e2)
    w2 = p_star * e2 / (1.0 + e2)
    route = jnp.where(lane == 0, i1 - ng,
                      jnp.where(lane == 1, i2 - ng,
                                jnp.where(lane == 2, w1, jnp.where(lane == 3, w2, 0.0))))
    route_ref[...] = route
    ids_ref[...] = route.T[:ids_ref.shape[0], :]


def _outproj_call(att, ssm_y, x_rows, mod, attn_g, ssm_g, norm2_g, w_out_bf, wr, rb):
    rows, d = x_rows.shape
    ts = _row_tile(WIDE_ROW_TILE, rows, mod)
    row = lambda width: pl.BlockSpec((ts, width), lambda i: (i, 0))
    const = lambda a: pl.BlockSpec(a.shape, lambda i: (0,) * a.ndim)
    attn_g = attn_g.reshape(1, ATTN_WIDTH)
    ssm_g = ssm_g.reshape(1, SSM_WIDTH)
    norm2_g = norm2_g.reshape(1, d)
    return pl.pallas_call(
        _outproj_kernel,
        grid=(rows // ts,),
        in_specs=[row(ATTN_WIDTH), row(SSM_WIDTH), row(d),
                  _mod_spec(mod, rows, ts, 2), _mod_spec(mod, rows, ts, 3), _mod_spec(mod, rows, ts, 4),
                  const(attn_g), const(ssm_g), const(norm2_g), const(w_out_bf), const(wr), const(rb)],
        out_specs=[row(d), row(d), row(LANES), pl.BlockSpec((ROUTE_ROWS, ts), lambda i: (0, i))],
        out_shape=[jax.ShapeDtypeStruct((rows, d), F32), jax.ShapeDtypeStruct((rows, d), BF16),
                   jax.ShapeDtypeStruct((rows, LANES), F32),
                   jax.ShapeDtypeStruct((ROUTE_ROWS, rows), F32)],
        compiler_params=_cparams(("arbitrary",)),
        name="outproj_router",
    )(att, ssm_y, x_rows, mod, mod, mod, attn_g, ssm_g, norm2_g, w_out_bf, wr, rb)


def _router_weights(router_g_w, router_g_b, router_e_w, router_e_b):
    d = router_g_w.shape[0]
    ne = N_EXPERT_GROUPS * EXPERTS_PER_GROUP
    w = jnp.concatenate([router_g_w.astype(F32),
                         jnp.transpose(router_e_w.astype(F32), (1, 0, 2)).reshape(d, ne)], axis=1)
    w = jnp.pad(w, ((0, 0), (0, LANES - w.shape[1])))
    hi = w.astype(BF16)
    lo = (w - hi.astype(F32)).astype(BF16)
    b = jnp.concatenate([router_g_b.astype(F32), router_e_b.astype(F32).reshape(ne)])
    b = jnp.pad(b, (0, LANES - b.shape[0])).reshape(1, LANES)
    return jnp.concatenate([hi, lo], axis=1), b


def _moe_kernel(te_ref, x_ref, wg_ref, wu_ref, wd_ref, o_ref, wgb_ref, wub_ref, wdb_ref):
    i = pl.program_id(0)
    changed = jnp.logical_or(i == 0, te_ref[i] != te_ref[jnp.maximum(i - 1, 0)])

    @pl.when(changed)
    def _():
        wgb_ref[...] = wg_ref[0].astype(BF16)
        wub_ref[...] = wu_ref[0].astype(BF16)
        wdb_ref[...] = wd_ref[0].astype(BF16)

    x = x_ref[...]
    gate = jnp.dot(x, wgb_ref[...], preferred_element_type=F32)
    up = jnp.dot(x, wub_ref[...], preferred_element_type=F32)
    a = (gate * _sigmoid(gate)) * up
    o_ref[...] = jnp.dot(a.astype(BF16), wdb_ref[...], preferred_element_type=F32).astype(o_ref.dtype)


def _moe_call(tile_expert, x_sorted, wg, wu, wd, tm):
    n_slots, d = x_sorted.shape
    fe = wg.shape[2]
    grid_spec = pltpu.PrefetchScalarGridSpec(
        num_scalar_prefetch=1,
        grid=(n_slots // tm,),
        in_specs=[pl.BlockSpec((tm, d), lambda i, te: (i, 0)),
                  pl.BlockSpec((1, d, fe), lambda i, te: (te[i], 0, 0)),
                  pl.BlockSpec((1, d, fe), lambda i, te: (te[i], 0, 0)),
                  pl.BlockSpec((1, fe, d), lambda i, te: (te[i], 0, 0))],
        out_specs=pl.BlockSpec((tm, d), lambda i, te: (i, 0)),
        scratch_shapes=[pltpu.VMEM((d, fe), BF16), pltpu.VMEM((d, fe), BF16), pltpu.VMEM((fe, d), BF16)],
    )
    return pl.pallas_call(
        _moe_kernel,
        grid_spec=grid_spec,
        out_shape=jax.ShapeDtypeStruct((n_slots, d), BF16),
        compiler_params=_cparams(("arbitrary",)),
        name="moe_experts",
    )(tile_expert, x_sorted, wg, wu, wd)


def _moe_dispatch(ids_rows, tm):
    n_tok = ids_rows.shape[1]
    ids = jnp.concatenate([ids_rows[0], ids_rows[1]]).astype(jnp.int32)
    n_pairs = ids.shape[0]
    n_slots = (-(-n_pairs // tm) + N_EXPERTS) * tm
    hot = (ids[:, None] == jnp.arange(N_EXPERTS)[None, :]).astype(jnp.int32)
    csum = jnp.cumsum(hot, axis=0)
    rank = jnp.sum((csum - hot) * hot, axis=1)
    counts = csum[-1]
    padded = -(-counts // tm) * tm
    ends = jnp.cumsum(padded)
    starts = ends - padded
    pos = jnp.sum(hot * starts[None, :], axis=1) + rank
    tok_of_slot = (jnp.arange(n_slots, dtype=jnp.int32) % n_tok).at[pos].set(
        jnp.arange(n_pairs, dtype=jnp.int32) % n_tok, unique_indices=True, mode="promise_in_bounds")
    tile_start = jnp.arange(n_slots // tm, dtype=jnp.int32) * tm
    tile_expert = jnp.minimum(jnp.sum((tile_start[:, None] >= ends[None, :]).astype(jnp.int32), axis=1),
                              N_EXPERTS - 1).astype(jnp.int32)
    return pos.reshape(2, n_tok), tok_of_slot, tile_expert


def _take_rows(x, idx):
    return x.at[idx].get(mode="promise_in_bounds")


def _final_kernel(x1_ref, ya_ref, yb_ref, route_ref, g2_ref, fg_ref, o_ref):
    wa = route_ref[:, 2:3]
    wb = route_ref[:, 3:4]
    x = x1_ref[...] + g2_ref[0] * (wa * ya_ref[...].astype(F32) + wb * yb_ref[...].astype(F32))
    o_ref[...] = _rmsnorm(x, fg_ref[...])


def _final_call(x1, ya, yb, route, mod, final_g):
    rows, d = x1.shape
    ts = _row_tile(WIDE_ROW_TILE, rows, mod)
    row = pl.BlockSpec((ts, d), lambda i: (i, 0))
    return pl.pallas_call(
        _final_kernel,
        grid=(rows // ts,),
        in_specs=[row, row, row, pl.BlockSpec((ts, LANES), lambda i: (i, 0)), _mod_spec(mod, rows, ts, 5),
                  pl.BlockSpec((1, d), lambda i: (0, 0))],
        out_specs=row,
        out_shape=jax.ShapeDtypeStruct((rows, d), F32),
        compiler_params=_cparams(("arbitrary",)),
        name="final_norm",
    )(x1, ya, yb, route, mod, final_g.reshape(1, d))


def kernel(x_prompt, x_sample, c_prompt, c_sample, cache_k, cache_v, state_ssm_re, state_ssm_im,
           rel_bias, ada_w, ada_b, norm1_g, w_in, ssm_a_re, ssm_a_im, ssm_log_dt, ssm_b_re, ssm_b_im,
           ssm_c_re, ssm_c_im, ssm_d, glu_a, glu_b, attn_out_g, ssm_out_g, w_out, norm2_g,
           router_g_w, router_g_b, router_e_w, router_e_b, w_gate, w_up, w_down, final_norm_g):
    if ada_w.shape[0] != 1:
        raise ValueError("single-layer trunk expected")
    nb, s_len, d = x_prompt.shape
    nd, t_len, _ = x_sample.shape
    if s_len != max(WINDOWS):
        raise ValueError("prompt length must equal the widest window")
    n_p, n_s = nb * s_len, nd * t_len
    if n_p % ROW_TILE or n_s % ROW_TILE:
        raise ValueError("token counts must be multiples of the row tile")

    mod = _mod_call(jnp.concatenate([c_prompt, c_sample], axis=0).astype(F32), ada_w[0], ada_b[0])
    mod_p = mod[:nb].reshape(nb, 1, 6 * d)
    mod_s = mod[nb:].reshape(1, nd, 6 * d)

    w_in_bf = w_in[0].astype(BF16)
    w_out_bf = w_out[0].astype(BF16)
    xp_rows = x_prompt.reshape(n_p, d)
    xs_rows = jnp.transpose(x_sample, (1, 0, 2)).reshape(n_s, d)

    qp, kp, vp, up, kp_t, vp_t = _inproj_call(xp_rows, mod_p, norm1_g[0], w_in_bf, seq_len=s_len)
    qs, ks, vs, us, ks_t, vs_t = _inproj_call(xs_rows, mod_s, norm1_g[0], w_in_bf, seq_len=nd)

    seq = lambda a: a.reshape(nb, s_len, ATTN_WIDTH)
    att_p = _attn_prompt_call(seq(qp), seq(kp), seq(vp), _prompt_bias_tiles(rel_bias))
    dec = lambda a: jnp.transpose(a.reshape(t_len, nd, ATTN_WIDTH), (1, 0, 2))
    att_s = _attn_decode_call(dec(qs), dec(ks), dec(vs), cache_k[0], cache_v[0], rel_bias)
    att_s = jnp.transpose(att_s, (1, 0, 2)).reshape(n_s, ATTN_WIDTH)

    s5 = (ssm_a_re[0], ssm_a_im[0], ssm_log_dt[0], ssm_b_re[0], ssm_b_im[0], ssm_c_re[0], ssm_c_im[0])
    t_op, m_op, p_op, sc = _s5_prompt_operators(*s5)
    d_oct = ssm_d[0].astype(F32).reshape(N_OCTETS, 1, LANES)
    ssm_p, hT_p = _ssm_prompt_call(seq(up), t_op, m_op, p_op, sc, d_oct,
                                   _octet_glu(glu_a[0]), _octet_glu(glu_b[0]))
    hT_p = hT_p.reshape(nb, N_OCTETS, 2, OCTET, SSM_STATE)
    ssm_re_p = hT_p[:, :, 0].reshape(nb, SSM_GROUPS, SSM_STATE)
    ssm_im_p = hT_p[:, :, 1].reshape(nb, SSM_GROUPS, SSM_STATE)
    ssm_s, hre_s, him_s = _ssm_decode_call(us.reshape(t_len, nd, SSM_WIDTH), state_ssm_re[0],
                                           state_ssm_im[0], *s5, ssm_d[0], glu_a[0], glu_b[0])
    ssm_s = ssm_s.reshape(n_s, SSM_WIDTH)

    wr, rb = _router_weights(router_g_w[0], router_g_b[0], router_e_w[0], router_e_b[0])
    norms = (attn_out_g[0], ssm_out_g[0], norm2_g[0], w_out_bf, wr, rb)
    x1_p, h2_p, route_p, ids_p = _outproj_call(att_p.reshape(n_p, ATTN_WIDTH), ssm_p.reshape(n_p, SSM_WIDTH),
                                        xp_rows, mod_p, *norms)
    ne = N_EXPERTS
    wg = w_gate[0].reshape(ne, d, D_EXPERT)
    wu = w_up[0].reshape(ne, d, D_EXPERT)
    wd = w_down[0].reshape(ne, D_EXPERT, d)

    def experts(x1, x_sorted, pos, tile_expert, route, mod_rows, tm):
        y_slots = _moe_call(tile_expert, x_sorted, wg, wu, wd, tm)
        return _final_call(x1, _take_rows(y_slots, pos[0]), _take_rows(y_slots, pos[1]),
                           route, mod_rows, final_norm_g)

    pos_p, tok_of_slot_p, tile_expert_p = _moe_dispatch(ids_p, MOE_TILE)
    x_sorted_p = _take_rows(h2_p, tok_of_slot_p)
    x_sorted_p, att_s = lax.optimization_barrier((x_sorted_p, att_s))
    y_p = experts(x1_p, x_sorted_p, pos_p, tile_expert_p, route_p, mod_p, MOE_TILE)

    x1_s, h2_s, route_s, ids_s = _outproj_call(att_s, ssm_s, xs_rows, mod_s, *norms)
    pos_s, tok_of_slot_s, tile_expert_s = _moe_dispatch(ids_s, MOE_TILE_DECODE)
    y_s = experts(x1_s, _take_rows(h2_s, tok_of_slot_s), pos_s, tile_expert_s, route_s, mod_s, MOE_TILE_DECODE)

    heads = (ATTN_HEADS, HEAD_DIM)
    cache_out = lambda a: jnp.transpose(a.reshape((1, nb) + heads + (s_len,)), (0, 1, 4, 2, 3))
    step_out = lambda a: jnp.transpose(a.reshape((1, t_len) + heads + (nd,)), (0, 4, 1, 2, 3))
    return (y_p.reshape(nb, s_len, d), jnp.transpose(y_s.reshape(t_len, nd, d), (1, 0, 2)),
            cache_out(kp_t), cache_out(vp_t), step_out(ks_t), step_out(vs_t),
            ssm_re_p[None], ssm_im_p[None],
            hre_s.reshape(1, nd, SSM_GROUPS, SSM_STATE), him_s.reshape(1, nd, SSM_GROUPS, SSM_STATE))
```

```python
import functools
import math

import numpy as np

import jax
import jax.numpy as jnp
from jax import lax
from jax.experimental import pallas as pl
from jax.experimental.pallas import tpu as pltpu

F32 = jnp.float32
BF16 = jnp.bfloat16

D_MODEL = 1024
HEAD_DIM = 64
ATTN_WIDTH = 512
ATTN_HEADS = 8
SSM_WIDTH = 512
SSM_GROUP_CH = 16
SSM_GROUPS = 32
SSM_STATE = 64
WINDOWS = (128, 512, 2048)
DILATIONS = (1, 4, 16)
WINDOW_STEPS = 128
N_BUCKETS = 32
MAX_EXACT = 16
BUCKET_MAX_DIST = 2048
N_EXPERT_GROUPS = 4
EXPERTS_PER_GROUP = 4
N_EXPERTS = N_EXPERT_GROUPS * EXPERTS_PER_GROUP
D_EXPERT = 512
NORM_EPS = 1e-6

LANES = 128
Q_ROWS = 16
OCTET = LANES // SSM_GROUP_CH
N_OCTETS = SSM_GROUPS // OCTET
OCT_STATE = OCTET * SSM_STATE
SSM_CHUNK = 16
SSM_SEQS = 4
ROW_TILE = 512
WIDE_ROW_TILE = 1024
MOE_TILE = 512
MOE_TILE_DECODE = 128
ROUTE_ROWS = 8
ATTN_GROUP = 4
VMEM_LIMIT = 56 * 1024 * 1024

_NEG_INF = float("-inf")
_HIGHEST = lax.Precision.HIGHEST


def _cparams(sem):
    return pltpu.CompilerParams(dimension_semantics=sem, vmem_limit_bytes=VMEM_LIMIT)


def _rmsnorm(x, g):
    return x * lax.rsqrt(jnp.mean(x * x, axis=-1, keepdims=True) + NORM_EPS) * g


def _gelu_tanh(x):
    c = math.sqrt(2.0 / math.pi)
    return 0.5 * x * (1.0 + jnp.tanh(c * (x + 0.044715 * (x * x * x))))


def _sigmoid(x):
    return 1.0 / (1.0 + jnp.exp(-x))


def _mod_kernel(c_ref, w_ref, b_ref, o_ref):
    c = c_ref[...]
    a = (c * _sigmoid(c)).astype(BF16)
    o_ref[...] = jnp.dot(a, w_ref[...].astype(BF16), preferred_element_type=F32) + b_ref[...]


def _mod_call(c_all, ada_w, ada_b):
    rows, d = c_all.shape
    n_out = ada_w.shape[1]
    tn = 1024
    return pl.pallas_call(
        _mod_kernel,
        grid=(n_out // tn,),
        in_specs=[pl.BlockSpec((rows, d), lambda j: (0, 0)),
                  pl.BlockSpec((d, tn), lambda j: (0, j)),
                  pl.BlockSpec((1, tn), lambda j: (0, j))],
        out_specs=pl.BlockSpec((rows, tn), lambda j: (0, j)),
        out_shape=jax.ShapeDtypeStruct((rows, n_out), F32),
        compiler_params=_cparams(("arbitrary",)),
        name="adaln_mod",
    )(c_all, ada_w, ada_b.reshape(1, n_out))


def _mod_spec(mod, rows, ts, chunk):
    if mod.shape[1] == 1:
        tiles_per_group = (rows // mod.shape[0]) // ts
        return pl.BlockSpec((1, 1, D_MODEL), lambda i: (i // tiles_per_group, 0, chunk))
    tiles_per_period = mod.shape[1] // ts
    return pl.BlockSpec((1, ts, D_MODEL), lambda i: (0, i % tiles_per_period, chunk))


def _row_tile(tile, rows, mod):
    return min(tile, rows) if mod.shape[1] == 1 else min(tile, rows, mod.shape[1])


def _inproj_kernel(x_ref, sh_ref, sc_ref, g_ref, w_ref, *rest, key_major):
    h = _rmsnorm(x_ref[...], g_ref[...]) * (1.0 + sc_ref[0]) + sh_ref[0]
    hb = h.astype(BF16)
    z = jnp.dot(hb, w_ref[...], preferred_element_type=F32)
    aw = ATTN_WIDTH
    if key_major:
        q_ref, k_ref, v_ref, u_ref, kt_ref, vt_ref = rest
        kt_ref[0] = z[:, aw:2 * aw].T
        vt_ref[0] = z[:, 2 * aw:3 * aw].T
    else:
        q_ref, k_ref, v_ref, u_ref = rest
    q_ref[...] = z[:, :aw]
    k_ref[...] = z[:, aw:2 * aw]
    v_ref[...] = z[:, 2 * aw:3 * aw]
    u_ref[...] = z[:, 3 * aw:]


def _inproj_call(x_rows, mod, norm_g, w_in_bf, seq_len=None):
    rows, d = x_rows.shape
    ts = _row_tile(ROW_TILE, rows, mod)
    proj = w_in_bf.shape[1]
    out = jax.ShapeDtypeStruct((rows, ATTN_WIDTH), F32)
    ospec = pl.BlockSpec((ts, ATTN_WIDTH), lambda i: (i, 0))
    in_specs = [pl.BlockSpec((ts, d), lambda i: (i, 0)),
                _mod_spec(mod, rows, ts, 0),
                _mod_spec(mod, rows, ts, 1),
                pl.BlockSpec((1, d), lambda i: (0, 0)),
                pl.BlockSpec((d, proj), lambda i: (0, 0))]
    args = [x_rows, mod, mod, norm_g.reshape(1, d), w_in_bf]
    out_specs = [ospec, ospec, ospec, ospec]
    out_shape = [out, out, out, out]
    if seq_len is not None:
        tiles = seq_len // ts
        tspec = pl.BlockSpec((1, ATTN_WIDTH, ts), lambda i: (i // tiles, 0, i % tiles))
        out_specs += [tspec, tspec]
        out_shape += [jax.ShapeDtypeStruct((rows // seq_len, ATTN_WIDTH, seq_len), F32)] * 2
    return pl.pallas_call(
        functools.partial(_inproj_kernel, key_major=seq_len is not None),
        grid=(rows // ts,),
        in_specs=in_specs,
        out_specs=out_specs,
        out_shape=out_shape,
        compiler_params=_cparams(("arbitrary",)),
        name="inproj",
    )(*args)


def _t5_bucket(dist):
    d = jnp.maximum(dist, MAX_EXACT).astype(F32)
    log_part = MAX_EXACT + (jnp.log(d / MAX_EXACT) / math.log(BUCKET_MAX_DIST / MAX_EXACT)
                            * (N_BUCKETS - MAX_EXACT)).astype(jnp.int32)
    return jnp.where(dist < MAX_EXACT, dist, jnp.minimum(log_part, N_BUCKETS - 1))


def _bias_by_distance(rel_bias, dists):
    hot = (_t5_bucket(jnp.asarray(dists, jnp.int32))[:, None]
           == jnp.arange(N_BUCKETS, dtype=jnp.int32)[None, :]).astype(F32)
    return jnp.dot(hot, rel_bias.astype(F32), precision=_HIGHEST)


def _prompt_bias_tiles(rel_bias):
    steps = WINDOW_STEPS
    period = 3 * steps
    tiles = []
    for r in DILATIONS:
        vec = _bias_by_distance(rel_bias, r * np.arange(steps + 1))
        fill = jnp.full((steps - 1, ATTN_HEADS), _NEG_INF, F32)
        w = jnp.concatenate([fill, vec[::-1], fill, fill[:1]], axis=0)
        rep = jnp.tile(w.T, (1, steps))[:, :steps * (period - 1)]
        toe = rep.reshape(ATTN_HEADS, steps, period - 1)[:, :, steps - 1:]
        toe = toe.reshape(ATTN_HEADS // 2, 2, steps, 2 * steps)
        own_only = jnp.where(jnp.arange(2 * steps) < steps, _NEG_INF, toe)
        tiles.append(jnp.stack([toe, own_only], axis=2))
    return jnp.stack(tiles)


def _attn_prompt_kernel(q_ref, k_ref, v_ref, bias_ref, o_ref,
                        p4_ref, kb_ref, vb_ref, s_ref, p_ref, res_ref, stage_ref, nat_ref):
    s_len = q_ref.shape[1]
    steps = WINDOW_STEPS
    n_tiles = s_len // steps
    quarter = s_len // 4
    nt = (((1,), (1,)), ((), ()))
    lane = lax.broadcasted_iota(jnp.int32, (1, LANES), 1)
    first_head = lane < HEAD_DIM
    srcs = (q_ref, k_ref, v_ref)

    kb_ref[0:steps, :] = jnp.zeros((steps, LANES), BF16)
    vb_ref[:, 0:steps, 0:LANES] = jnp.zeros((2, steps, LANES), BF16)
    vb_ref[:, :, LANES:] = jnp.ones((2, s_len + steps, LANES), BF16)
    for x in range(3):
        for sigma in range(4):
            p4_ref[x, sigma * quarter:(sigma + 1) * quarter, :] = srcs[x][0, pl.ds(sigma, quarter, stride=4), :]

    def aligned(start):
        return start if isinstance(start, int) else pl.multiple_of(start, steps)

    def source(branch, x, tile_idx):
        if branch == 2:
            sigma, tau = tile_idx % 4, tile_idx // 4
            return p4_ref[x, pl.ds(sigma * quarter + tau, steps, stride=4), :]
        rows = pl.ds(aligned(tile_idx * steps), steps)
        return srcs[x][0, rows, :] if branch == 0 else p4_ref[x, rows, :]

    def stage_operands(branch):
        for t in range(n_tiles):
            kb_ref[steps + t * steps:steps + (t + 1) * steps, :] = source(branch, 1, t).astype(BF16)
            vb_ref[branch % 2, steps + t * steps:steps + (t + 1) * steps, 0:LANES] = (
                source(branch, 2, t).astype(BF16))

    stage_operands(0)
    for branch, r in enumerate(DILATIONS):
        blocks_per_class = (s_len // r) // steps
        width = 2 * steps if blocks_per_class > 1 else steps

        def tile_rows(t):
            return pl.ds(aligned(t * steps), steps)

        def key_rows(t, width=width):
            start = t * steps if width == 2 * steps else (t + 1) * steps
            return pl.ds(aligned(start), width)

        def scores(t, branch=branch, width=width, blocks_per_class=blocks_per_class, key_rows=key_rows):
            rows = tile_rows(t)
            keys = kb_ref[key_rows(t), :]
            first = jnp.where(t % blocks_per_class == 0, 1, 0)
            q2 = source(branch, 0, t) * (HEAD_DIM ** -0.5)
            for hh in range(2):
                if width == 2 * steps:
                    bias = bias_ref[branch, 0, hh, pl.ds(first, 1), :, :][0]
                else:
                    bias = bias_ref[branch, 0, hh, 0, :, steps:]
                qh = jnp.where(first_head == (hh == 0), q2, 0.0).astype(BF16)
                sc = lax.dot_general(qh, keys, nt, preferred_element_type=F32)
                s_ref[hh, rows, 0:width] = sc + bias

        out_ref = nat_ref.at[0] if branch == 0 else res_ref

        def softmax(t, width=width, out_ref=out_ref):
            rows = tile_rows(t)
            for hh in range(2):
                sc = s_ref[hh, rows, 0:width]
                m = jnp.max(sc, axis=1, keepdims=True)
                p_ref[hh, rows, 0:width] = jnp.exp(sc - m).astype(BF16)
                out_ref[1, rows, hh * HEAD_DIM:(hh + 1) * HEAD_DIM] = jnp.broadcast_to(m, (steps, HEAD_DIM))

        def weighted(t, branch=branch, width=width, key_rows=key_rows, out_ref=out_ref):
            rows = tile_rows(t)
            vals = vb_ref[branch % 2, key_rows(t), :]
            r0 = jnp.dot(p_ref[0, rows, 0:width], vals, preferred_element_type=F32)
            r1 = jnp.dot(p_ref[1, rows, 0:width], vals, preferred_element_type=F32)
            out_ref[0, rows, :] = jnp.where(first_head, r0[:, :LANES], r1[:, :LANES])
            out_ref[2, rows, :] = jnp.where(first_head, r0[:, LANES:], r1[:, LANES:])

        def stage(g_scores, g_softmax, scores=scores, softmax=softmax):
            for fn, g in ((softmax, g_softmax), (scores, g_scores)):
                if g is not None:
                    for j in range(ATTN_GROUP):
                        fn(g * ATTN_GROUP + j)

        n_groups = n_tiles // ATTN_GROUP
        stage(0, None)

        def steady(g, carry, stage=stage):
            stage(g, g - 1)
            return carry

        lax.fori_loop(1, n_groups, steady, 0)
        stage(None, n_groups - 1)

        def to_natural_order(branch=branch):
            for kind in range(3 if branch else 0):
                src = res_ref
                if branch == 2:
                    for t in range(n_tiles):
                        sigma, tau = t % 4, t // 4
                        stage_ref[kind, pl.ds(sigma * quarter + tau, steps, stride=4), :] = (
                            res_ref[kind, t * steps:(t + 1) * steps, :])
                    src = stage_ref
                for sigma in range(4):
                    nat_ref[branch, kind, pl.ds(sigma, quarter, stride=4), :] = (
                        src[kind, sigma * quarter:(sigma + 1) * quarter, :])

        @pl.when(pl.program_id(1) >= 0)
        def _(branch=branch, weighted=weighted, to_natural_order=to_natural_order):
            for t in range(n_tiles):
                weighted(t)
            to_natural_order()
            if branch + 1 < len(DILATIONS):
                stage_operands(branch + 1)

    def merge(i, carry):
        rows = pl.ds(pl.multiple_of(i * 256, 256), 256)
        m0, m1, m2 = nat_ref[0, 1, rows, :], nat_ref[1, 1, rows, :], nat_ref[2, 1, rows, :]
        m_all = jnp.maximum(jnp.maximum(m0, m1), m2)
        w0, w1, w2 = jnp.exp(m0 - m_all), jnp.exp(m1 - m_all), jnp.exp(m2 - m_all)
        num = w0 * nat_ref[0, 0, rows, :] + w1 * nat_ref[1, 0, rows, :] + w2 * nat_ref[2, 0, rows, :]
        den = w0 * nat_ref[0, 2, rows, :] + w1 * nat_ref[1, 2, rows, :] + w2 * nat_ref[2, 2, rows, :]
        o_ref[0, rows, :] = num / den
        return carry

    lax.fori_loop(0, s_len // 256, merge, 0)


def _attn_prompt_call(q, k, v, bias_tiles):
    n, s, _ = q.shape
    pairs = ATTN_HEADS // 2
    steps = WINDOW_STEPS
    qspec = pl.BlockSpec((1, s, LANES), lambda g, n_: (n_, 0, g))
    return pl.pallas_call(
        _attn_prompt_kernel,
        grid=(pairs, n),
        in_specs=[qspec, qspec, qspec,
                  pl.BlockSpec((3, 1, 2, 2, steps, 2 * steps), lambda g, n_: (0, g, 0, 0, 0, 0))],
        out_specs=qspec,
        out_shape=jax.ShapeDtypeStruct((n, s, ATTN_WIDTH), F32),
        scratch_shapes=[pltpu.VMEM((3, s, LANES), F32),
                        pltpu.VMEM((s + steps, LANES), BF16),
                        pltpu.VMEM((2, s + steps, 2 * LANES), BF16),
                        pltpu.VMEM((2, s, 2 * steps), F32),
                        pltpu.VMEM((2, s, 2 * steps), BF16),
                        pltpu.VMEM((3, s, LANES), F32), pltpu.VMEM((3, s, LANES), F32),
                        pltpu.VMEM((3, 3, s, LANES), F32)],
        compiler_params=_cparams(("arbitrary", "arbitrary")),
        name="attn_prompt",
    )(q, k, v, bias_tiles)


def _s5_discretise(a_re, a_im, log_dt, b_re, b_im):
    lam_re = jnp.minimum(a_re.astype(F32), -1e-4)
    lam_im = a_im.astype(F32)
    dt = jnp.exp(log_dt.astype(F32))[:, None]
    mag = jnp.exp(lam_re * dt)
    ph = lam_im * dt
    abar_re, abar_im = mag * jnp.cos(ph), mag * jnp.sin(ph)
    nr, ni = abar_re - 1.0, abar_im
    den = lam_re * lam_re + lam_im * lam_im
    coef_re = (nr * lam_re + ni * lam_im) / den
    coef_im = (ni * lam_re - nr * lam_im) / den
    br, bi = b_re.astype(F32), b_im.astype(F32)
    bbar_re = coef_re[..., None] * br - coef_im[..., None] * bi
    bbar_im = coef_re[..., None] * bi + coef_im[..., None] * br
    return lam_re * dt, ph, abar_re, abar_im, bbar_re, bbar_im


def _abar_power(log_mag, ph, n):
    nf = jnp.asarray(n, F32)[:, None, None]
    mag = jnp.exp(nf * log_mag[None])
    return mag * jnp.cos(nf * ph[None]), mag * jnp.sin(nf * ph[None])


def _s5_prompt_operators(a_re, a_im, log_dt, b_re, b_im, c_re, c_im):
    L = SSM_CHUNK
    log_mag, ph, _, _, bb_re, bb_im = _s5_discretise(a_re, a_im, log_dt, b_re, b_im)
    cr, ci = c_re.astype(F32), c_im.astype(F32)
    pw_re, pw_im = _abar_power(log_mag, ph, np.arange(L + 1))
    eye = jnp.eye(OCTET, dtype=F32)

    ab_re = pw_re[:L, :, :, None] * bb_re[None] - pw_im[:L, :, :, None] * bb_im[None]
    ab_im = pw_re[:L, :, :, None] * bb_im[None] + pw_im[:L, :, :, None] * bb_re[None]
    lag = (jnp.einsum('gop,lgpi->lgoi', cr, ab_re, precision=_HIGHEST)
           - jnp.einsum('gop,lgpi->lgoi', ci, ab_im, precision=_HIGHEST))
    lag = lag.reshape(L, N_OCTETS, OCTET, SSM_GROUP_CH, SSM_GROUP_CH)
    bd = jnp.einsum('logci,gh->olgihc', lag, eye).reshape(N_OCTETS, L, LANES, LANES).astype(BF16)
    stack = bd[:, ::-1].reshape(N_OCTETS, L * LANES, LANES)
    shifted = jnp.concatenate([stack[:, LANES:], jnp.zeros((N_OCTETS, LANES, LANES), BF16)], axis=1)
    t_op = jnp.concatenate([shifted, stack], axis=-1)

    group_of_lane = jnp.arange(LANES) // SSM_GROUP_CH
    parts = []
    for part in (ab_re[::-1], ab_im[::-1]):
        x = part.reshape(L, N_OCTETS, OCTET, SSM_STATE, SSM_GROUP_CH)
        parts.append(jnp.transpose(x, (1, 0, 3, 2, 4)).reshape(N_OCTETS, L, SSM_STATE, LANES))
    x = jnp.stack(parts, axis=2)
    own = (jnp.arange(OCTET)[:, None, None] == group_of_lane[None, None, :])
    mt_op = jnp.where(own, x[:, :, :, None], 0.0)
    mt_op = mt_op.reshape(N_OCTETS, L, 2 * OCT_STATE, LANES)

    p1_re, p1_im = pw_re[1:], pw_im[1:]
    on_re = cr[None] * p1_re[:, :, None, :] - ci[None] * p1_im[:, :, None, :]
    on_im = -cr[None] * p1_im[:, :, None, :] - ci[None] * p1_re[:, :, None, :]
    parts = []
    for part in (on_re, on_im):
        y = part.reshape(L, N_OCTETS, OCTET, SSM_GROUP_CH, SSM_STATE)
        parts.append(jnp.transpose(y, (1, 0, 3, 2, 4)).reshape(N_OCTETS, L, SSM_GROUP_CH, OCT_STATE))
    y = jnp.concatenate(parts, axis=-1)
    group_of_state = (jnp.arange(2 * OCT_STATE) % OCT_STATE) // SSM_STATE
    own = (jnp.arange(OCTET)[:, None, None] == group_of_state[None, None, :])
    pt_op = jnp.where(own, y[:, :, None], 0.0)
    pt_op = pt_op.reshape(N_OCTETS, L * LANES, 2 * OCT_STATE)

    n_steps = 8
    sc_re, sc_im = _abar_power(log_mag, ph, L * (2 ** np.arange(n_steps)))
    sc = jnp.concatenate([sc_re.reshape(n_steps, N_OCTETS, OCT_STATE),
                          sc_im.reshape(n_steps, N_OCTETS, OCT_STATE)], axis=-1)
    sc = jnp.transpose(sc, (1, 0, 2))
    return t_op, mt_op.astype(BF16), pt_op.astype(BF16), sc


def _octet_glu(glu):
    eye = jnp.eye(OCTET, dtype=F32)
    x = glu.astype(F32).reshape(N_OCTETS, OCTET, SSM_GROUP_CH, SSM_GROUP_CH)
    return jnp.einsum('ogce,gh->ogche', x, eye).reshape(N_OCTETS, LANES, LANES).astype(BF16)


def _ssm_prompt_kernel(u_ref, t_ref, mb_ref, pt_ref, sc_ref, d_ref, ga_ref, gb_ref,
                       y_ref, h_ref, uf_ref, ub_ref, st_ref, mt_ref):
    L = SSM_CHUNK
    nt = (((1,), (1,)), ((), ()))

    @pl.when(pl.program_id(1) == 0)
    def _():
        for s in range(L):
            mt_ref[:, s * LANES:(s + 1) * LANES] = mb_ref[0, s]

    n_seq, s_len = u_ref.shape[0], u_ref.shape[1]
    n_chunks = s_len // L
    quarter = s_len // 4
    rows = n_seq * n_chunks

    def staged(sq, step):
        sigma, tau = step % 4, step // 4
        return pl.ds(sq * s_len + sigma * quarter + tau, n_chunks, stride=4)

    for sq in range(n_seq):
        for sigma in range(4):
            st_ref[sq * s_len + sigma * quarter:sq * s_len + (sigma + 1) * quarter, :] = (
                u_ref[sq, pl.ds(sigma, quarter, stride=4), :])
        for step in range(L):
            blk = st_ref[staged(sq, step), :]
            uf_ref[sq * n_chunks:(sq + 1) * n_chunks, step * LANES:(step + 1) * LANES] = blk
            ub_ref[sq * n_chunks:(sq + 1) * n_chunks, step * LANES:(step + 1) * LANES] = blk.astype(BF16)
    ub = ub_ref[...]

    x = lax.dot_general(ub, mt_ref[...], nt, preferred_element_type=F32)
    chunk = lax.broadcasted_iota(jnp.int32, (rows, 1), 0) % n_chunks
    half = OCT_STATE
    k = 1
    step = 0
    while k < n_chunks:
        a_re = sc_ref[0, step:step + 1, :half]
        a_im = sc_ref[0, step:step + 1, half:]
        sh = jnp.where(chunk >= k, pltpu.roll(x, k, axis=0), 0.0)
        s_re, s_im = sh[:, :half], sh[:, half:]
        x = x + jnp.concatenate([a_re * s_re - a_im * s_im, a_re * s_im + a_im * s_re], axis=1)
        k *= 2
        step += 1
    for sq in range(n_seq):
        h_ref[sq, 0] = x[(sq + 1) * n_chunks - 1:(sq + 1) * n_chunks, :]
    h_start = jnp.where(chunk >= 1, pltpu.roll(x, 1, axis=0), 0.0)

    hb = h_start.astype(BF16)
    d = d_ref[0]
    ga = ga_ref[0]
    gb = gb_ref[0]
    for t in range(0, L, 2):
        pair = slice(t * LANES, (t + 2) * LANES)
        y2 = (jnp.dot(ub_ref[:, :(t + 2) * LANES], t_ref[0, (L - 2 - t) * LANES:, :],
                      preferred_element_type=F32)
              + lax.dot_general(hb, pt_ref[0, pair, :], nt, preferred_element_type=F32))
        for j in range(2):
            lanes = slice((t + j) * LANES, (t + j + 1) * LANES)
            g = _gelu_tanh(y2[:, j * LANES:(j + 1) * LANES] + d * uf_ref[:, lanes]).astype(BF16)
            out = (jnp.dot(g, ga, preferred_element_type=F32)
                   * _sigmoid(jnp.dot(g, gb, preferred_element_type=F32)))
            for sq in range(n_seq):
                st_ref[staged(sq, t + j), :] = out[sq * n_chunks:(sq + 1) * n_chunks, :]
    for sq in range(n_seq):
        for sigma in range(4):
            y_ref[sq, pl.ds(sigma, quarter, stride=4), :] = (
                st_ref[sq * s_len + sigma * quarter:sq * s_len + (sigma + 1) * quarter, :])


def _ssm_prompt_call(u, t_op, m_op, p_op, sc, d_oct, ga, gb):
    n, s, _ = u.shape
    L = SSM_CHUNK
    nq = math.gcd(SSM_SEQS, n)
    rows = nq * (s // L)
    wide = L * LANES
    wspec = lambda shape: pl.BlockSpec((1,) + shape, lambda o, n_: (o, 0, 0))
    return pl.pallas_call(
        _ssm_prompt_kernel,
        grid=(N_OCTETS, n // nq),
        in_specs=[pl.BlockSpec((nq, s, LANES), lambda o, n_: (n_, 0, o)),
                  wspec((wide, 2 * LANES)),
                  pl.BlockSpec((1, L, 2 * OCT_STATE, LANES), lambda o, n_: (o, 0, 0, 0)),
                  wspec((wide, 2 * OCT_STATE)),
                  wspec((8, 2 * OCT_STATE)), wspec((1, LANES)),
                  wspec((LANES, LANES)), wspec((LANES, LANES))],
        out_specs=[pl.BlockSpec((nq, s, LANES), lambda o, n_: (n_, 0, o)),
                   pl.BlockSpec((nq, 1, 1, 2 * OCT_STATE), lambda o, n_: (n_, o, 0, 0))],
        out_shape=[jax.ShapeDtypeStruct((n, s, SSM_WIDTH), F32),
                   jax.ShapeDtypeStruct((n, N_OCTETS, 1, 2 * OCT_STATE), F32)],
        scratch_shapes=[pltpu.VMEM((rows, wide), F32),
                        pltpu.VMEM((rows, wide), BF16),
                        pltpu.VMEM((nq * s, LANES), F32),
                        pltpu.VMEM((2 * OCT_STATE, wide), BF16)],
        compiler_params=_cparams(("arbitrary", "arbitrary")),
        name="ssm_prompt",
    )(u, t_op, m_op, p_op, sc, d_oct, ga, gb)


def _attn_decode_kernel(q_ref, kn_ref, vn_ref, kt_ref, vt_ref, b_ref, mult_ref, o_ref):
    nt = (((1,), (1,)), ((), ()))
    mult = mult_ref[...]
    for h in range(ATTN_HEADS):
        q = q_ref[0, h]
        kt = jnp.concatenate([kt_ref[0, h].astype(BF16), kn_ref[0]], axis=1)
        vt = jnp.concatenate([vt_ref[0, h].astype(BF16), vn_ref[0]], axis=1)
        s = jnp.dot(q, kt, preferred_element_type=F32) + b_ref[h]
        m = jnp.max(s, axis=1, keepdims=True)
        p = jnp.exp(s - m) * mult
        den = jnp.sum(p, axis=1, keepdims=True)
        o = lax.dot_general(p.astype(BF16), vt, nt, preferred_element_type=F32)
        o_ref[0, h] = o / den


def _decode_tables(rel_bias, t_len, w_rows):
    slot = LANES // ATTN_HEADS
    t = np.arange(t_len)[:, None]
    new_t = np.arange(LANES)[None, :] % slot
    dist = np.concatenate([w_rows + t - np.arange(w_rows)[None, :],
                           t - new_t], axis=1)
    mult = np.zeros(dist.shape, np.float32)
    for w, r in zip(WINDOWS, DILATIONS):
        mult += (dist >= 0) & (dist % r == 0) & (dist <= w)
    mult = np.concatenate([mult, np.zeros((Q_ROWS - t_len, dist.shape[1]), np.float32)], axis=0)
    mult[t_len:, 0] = 1.0
    by_dist = _bias_by_distance(rel_bias, np.arange(w_rows + t_len))
    rows = []
    for ti in range(t_len):
        cache_part = by_dist[ti + 1:w_rows + ti + 1][::-1]
        new_part = jnp.concatenate([by_dist[:ti + 1][::-1],
                                    jnp.zeros((slot - ti - 1, ATTN_HEADS), F32)], axis=0)
        rows.append(jnp.concatenate([cache_part] + [new_part] * ATTN_HEADS, axis=0))
    bias = jnp.stack(rows + [jnp.zeros_like(rows[0])] * (Q_ROWS - t_len), axis=0)
    bias = jnp.transpose(bias, (2, 0, 1))
    own_head = np.concatenate([np.ones((ATTN_HEADS, w_rows), bool),
                               (np.arange(LANES) // slot)[None, :] == np.arange(ATTN_HEADS)[:, None]], axis=1)
    bias = jnp.where(jnp.asarray((mult[None] > 0) & own_head[:, None, :]), bias, _NEG_INF)
    return bias, jnp.asarray(mult)


def _attn_decode_call(q, k_new, v_new, cache_k, cache_v, rel_bias):
    n, t_len, w = q.shape
    w_rows = cache_k.shape[1]
    if t_len > min(DILATIONS[1:]) or t_len > Q_ROWS or w_rows < max(WINDOWS):
        raise ValueError("unsupported decode shape")
    heads = (ATTN_HEADS, HEAD_DIM)

    def head_major(a, pad_to):
        a = jnp.transpose(a.reshape((n, t_len) + heads), (0, 2, 1, 3))
        return jnp.pad(a, ((0, 0), (0, 0), (0, pad_to - t_len), (0, 0)))

    def new_tile(a):
        a = jnp.transpose(a.reshape((n, t_len) + heads), (0, 3, 2, 1))
        a = jnp.pad(a, ((0, 0), (0, 0), (0, 0), (0, LANES // ATTN_HEADS - t_len)))
        return a.reshape(n, HEAD_DIM, LANES).astype(BF16)

    qh = head_major(q * (HEAD_DIM ** -0.5), Q_ROWS).astype(BF16)
    knt, vnt = new_tile(k_new), new_tile(v_new)
    kt = jnp.transpose(cache_k.astype(F32), (0, 2, 3, 1))
    vt = jnp.transpose(cache_v.astype(F32), (0, 2, 3, 1))
    bias, mult = _decode_tables(rel_bias, t_len, w_rows)
    keys = w_rows + LANES
    per_seq = lambda shape: pl.BlockSpec((1,) + shape, lambda i: (i, 0, 0, 0))
    out = pl.pallas_call(
        _attn_decode_kernel,
        grid=(n,),
        in_specs=[per_seq((ATTN_HEADS, Q_ROWS, HEAD_DIM)),
                  pl.BlockSpec((1, HEAD_DIM, LANES), lambda i: (i, 0, 0)),
                  pl.BlockSpec((1, HEAD_DIM, LANES), lambda i: (i, 0, 0)),
                  per_seq((ATTN_HEADS, HEAD_DIM, w_rows)), per_seq((ATTN_HEADS, HEAD_DIM, w_rows)),
                  pl.BlockSpec((ATTN_HEADS, Q_ROWS, keys), lambda i: (0, 0, 0)),
                  pl.BlockSpec((Q_ROWS, keys), lambda i: (0, 0))],
        out_specs=per_seq((ATTN_HEADS, Q_ROWS, HEAD_DIM)),
        out_shape=jax.ShapeDtypeStruct((n, ATTN_HEADS, Q_ROWS, HEAD_DIM), F32),
        compiler_params=_cparams(("arbitrary",)),
        name="attn_decode",
    )(qh, knt, vnt, kt, vt, bias, mult)
    return jnp.transpose(out[:, :, :t_len], (0, 2, 1, 3)).reshape(n, t_len, w)


def _ssm_decode_kernel(u_ref, hre_ref, him_ref, are_ref, aim_ref, bre_ref, bim_ref,
                       cre_ref, cim_ref, d_ref, ga_ref, gb_ref, y_ref, ore_ref, oim_ref, *, t_len):
    h_re, h_im = hre_ref[...], him_ref[...]
    a_re, a_im = are_ref[...], aim_ref[...]
    for t in range(t_len):
        u = u_ref[t]
        ub = u.astype(BF16)
        n_re = a_re * h_re - a_im * h_im + jnp.dot(ub, bre_ref[...], preferred_element_type=F32)
        n_im = a_re * h_im + a_im * h_re + jnp.dot(ub, bim_ref[...], preferred_element_type=F32)
        h_re, h_im = n_re, n_im
        y = (jnp.dot(h_re.astype(BF16), cre_ref[...], preferred_element_type=F32)
             - jnp.dot(h_im.astype(BF16), cim_ref[...], preferred_element_type=F32)
             + d_ref[...] * u)
        g = _gelu_tanh(y).astype(BF16)
        y_ref[t] = (jnp.dot(g, ga_ref[...], preferred_element_type=F32)
                    * _sigmoid(jnp.dot(g, gb_ref[...], preferred_element_type=F32)))
    ore_ref[...] = h_re
    oim_ref[...] = h_im


def _group_blockdiag(x):
    g, a, b = x.shape
    panel = jnp.transpose(x, (1, 0, 2)).reshape(a, g * b)
    own = jnp.arange(g)[:, None, None] == (jnp.arange(g * b) // b)[None, None, :]
    return jnp.where(own, panel[None], 0).reshape(g * a, g * b)


def _ssm_decode_call(u_tm, h0_re, h0_im, a_re, a_im, log_dt, b_re, b_im, c_re, c_im,
                     d_skip, glu_a, glu_b):
    n = h0_re.shape[0]
    _, _, abar_re, abar_im, bb_re, bb_im = _s5_discretise(a_re, a_im, log_dt, b_re, b_im)
    state = SSM_GROUPS * SSM_STATE
    t_len = u_tm.shape[0]
    args = (u_tm, h0_re.reshape(n, state).astype(F32), h0_im.reshape(n, state).astype(F32),
            abar_re.reshape(1, state), abar_im.reshape(1, state),
            _group_blockdiag(jnp.transpose(bb_re, (0, 2, 1))).astype(BF16),
            _group_blockdiag(jnp.transpose(bb_im, (0, 2, 1))).astype(BF16),
            _group_blockdiag(jnp.transpose(c_re.astype(F32), (0, 2, 1))).astype(BF16),
            _group_blockdiag(jnp.transpose(c_im.astype(F32), (0, 2, 1))).astype(BF16),
            d_skip.astype(F32).reshape(1, SSM_WIDTH),
            _group_blockdiag(glu_a.astype(F32)).astype(BF16),
            _group_blockdiag(glu_b.astype(F32)).astype(BF16))
    full = lambda a: pl.BlockSpec(a.shape, lambda i: (0,) * a.ndim)
    out_shape = [jax.ShapeDtypeStruct(u_tm.shape, F32),
                 jax.ShapeDtypeStruct((n, state), F32), jax.ShapeDtypeStruct((n, state), F32)]
    return pl.pallas_call(
        functools.partial(_ssm_decode_kernel, t_len=t_len),
        grid=(1,),
        in_specs=[full(a) for a in args],
        out_specs=[full(o) for o in out_shape],
        out_shape=out_shape,
        compiler_params=_cparams(("arbitrary",)),
        name="ssm_decode",
    )(*args)


def _outproj_kernel(att_ref, ssm_ref, x_ref, g1_ref, sh2_ref, sc2_ref, ag_ref, sg_ref, n2_ref,
                    wo_ref, wr_ref, rb_ref, x1_ref, h2_ref, route_ref, ids_ref):
    mixed = jnp.concatenate([_rmsnorm(att_ref[...], ag_ref[...]), _rmsnorm(ssm_ref[...], sg_ref[...])],
                            axis=1).astype(BF16)
    x1 = x_ref[...] + g1_ref[0] * jnp.dot(mixed, wo_ref[...], preferred_element_type=F32)
    x1_ref[...] = x1
    h2 = _rmsnorm(x1, n2_ref[...]) * (1.0 + sc2_ref[0]) + sh2_ref[0]
    hi = h2.astype(BF16)
    h2_ref[...] = hi
    lo = (h2 - hi.astype(F32)).astype(BF16)
    r1 = jnp.dot(hi, wr_ref[...], preferred_element_type=F32)
    r2 = jnp.dot(lo, wr_ref[:, :LANES], preferred_element_type=F32)
    logits = r1[:, :LANES] + r1[:, LANES:] + r2 + rb_ref[...]

    lane = lax.broadcasted_iota(jnp.int32, (1, LANES), 1)
    lane_f = lane.astype(F32)
    big = float(LANES)
    ng, epg = N_EXPERT_GROUPS, EXPERTS_PER_GROUP
    lg = jnp.where(lane < ng, logits, _NEG_INF)
    gmax = jnp.max(lg, axis=1, keepdims=True)
    p_star = 1.0 / jnp.sum(jnp.exp(lg - gmax), axis=1, keepdims=True)
    g_star = jnp.min(jnp.where(lg == gmax, lane_f, big), axis=1, keepdims=True)
    in_group = ((lane >= ng) & (lane < ng + ng * epg)
                & (lax.shift_right_arithmetic(lane - ng, int(math.log2(epg))).astype(F32) == g_star))
    le = jnp.where(in_group, logits, _NEG_INF)
    v1 = jnp.max(le, axis=1, keepdims=True)
    i1 = jnp.min(jnp.where(le == v1, lane_f, big), axis=1, keepdims=True)
    le2 = jnp.where(lane_f == i1, _NEG_INF, le)
    v2 = jnp.max(le2, axis=1, keepdims=True)
    i2 = jnp.min(jnp.where(le2 == v2, lane_f, big), axis=1, keepdims=True)
    e2 = jnp.exp(v2 - v1)
    w1 = p_star / (1.0 + e2)
    w2 = p_star * e2 / (1.0 + e2)
    route = jnp.where(lane == 0, i1 - ng,
                      jnp.where(lane == 1, i2 - ng,
                                jnp.where(lane == 2, w1, jnp.where(lane == 3, w2, 0.0))))
    route_ref[...] = route
    ids_ref[...] = route.T[:ids_ref.shape[0], :]


def _outproj_call(att, ssm_y, x_rows, mod, attn_g, ssm_g, norm2_g, w_out_bf, wr, rb):
    rows, d = x_rows.shape
    ts = _row_tile(WIDE_ROW_TILE, rows, mod)
    row = lambda width: pl.BlockSpec((ts, width), lambda i: (i, 0))
    const = lambda a: pl.BlockSpec(a.shape, lambda i: (0,) * a.ndim)
    attn_g = attn_g.reshape(1, ATTN_WIDTH)
    ssm_g = ssm_g.reshape(1, SSM_WIDTH)
    norm2_g = norm2_g.reshape(1, d)
    return pl.pallas_call(
        _outproj_kernel,
        grid=(rows // ts,),
        in_specs=[row(ATTN_WIDTH), row(SSM_WIDTH), row(d),
                  _mod_spec(mod, rows, ts, 2), _mod_spec(mod, rows, ts, 3), _mod_spec(mod, rows, ts, 4),
                  const(attn_g), const(ssm_g), const(norm2_g), const(w_out_bf), const(wr), const(rb)],
        out_specs=[row(d), row(d), row(LANES), pl.BlockSpec((ROUTE_ROWS, ts), lambda i: (0, i))],
        out_shape=[jax.ShapeDtypeStruct((rows, d), F32), jax.ShapeDtypeStruct((rows, d), BF16),
                   jax.ShapeDtypeStruct((rows, LANES), F32),
                   jax.ShapeDtypeStruct((ROUTE_ROWS, rows), F32)],
        compiler_params=_cparams(("arbitrary",)),
        name="outproj_router",
    )(att, ssm_y, x_rows, mod, mod, mod, attn_g, ssm_g, norm2_g, w_out_bf, wr, rb)


def _router_weights(router_g_w, router_g_b, router_e_w, router_e_b):
    d = router_g_w.shape[0]
    ne = N_EXPERT_GROUPS * EXPERTS_PER_GROUP
    w = jnp.concatenate([router_g_w.astype(F32),
                         jnp.transpose(router_e_w.astype(F32), (1, 0, 2)).reshape(d, ne)], axis=1)
    w = jnp.pad(w, ((0, 0), (0, LANES - w.shape[1])))
    hi = w.astype(BF16)
    lo = (w - hi.astype(F32)).astype(BF16)
    b = jnp.concatenate([router_g_b.astype(F32), router_e_b.astype(F32).reshape(ne)])
    b = jnp.pad(b, (0, LANES - b.shape[0])).reshape(1, LANES)
    return jnp.concatenate([hi, lo], axis=1), b


def _swiglu(x, wg, wu, wd, out_dtype):
    gate = jnp.dot(x, wg, preferred_element_type=F32)
    up = jnp.dot(x, wu, preferred_element_type=F32)
    a = (gate * _sigmoid(gate)) * up
    return jnp.dot(a.astype(BF16), wd, preferred_element_type=F32).astype(out_dtype)


def _moe_kernel(te_ref, x_ref, wg_ref, wu_ref, wd_ref, o_ref, wgo_ref, wuo_ref, wdo_ref,
                wgb_ref, wub_ref, wdb_ref):
    i = pl.program_id(0)
    changed = jnp.logical_or(i == 0, te_ref[i] != te_ref[jnp.maximum(i - 1, 0)])

    @pl.when(changed)
    def _():
        for src, scr, dst in ((wg_ref, wgb_ref, wgo_ref), (wu_ref, wub_ref, wuo_ref), (wd_ref, wdb_ref, wdo_ref)):
            w = src[0].astype(BF16)
            scr[...] = w
            dst[0] = w

    o_ref[...] = _swiglu(x_ref[...], wgb_ref[...], wub_ref[...], wdb_ref[...], o_ref.dtype)


def _moe_bf16_kernel(te_ref, x_ref, wg_ref, wu_ref, wd_ref, o_ref):
    del te_ref
    o_ref[...] = _swiglu(x_ref[...], wg_ref[0], wu_ref[0], wd_ref[0], o_ref.dtype)


def _moe_call(tile_expert, x_sorted, wg, wu, wd, tm):
    n_slots, d = x_sorted.shape
    fe = wg.shape[2]
    cast = wg.dtype != BF16
    wspec = lambda shape: pl.BlockSpec((1,) + shape, lambda i, te: (te[i], 0, 0))
    slot_spec = pl.BlockSpec((tm, d), lambda i, te: (i, 0))
    slots = jax.ShapeDtypeStruct((n_slots, d), BF16)
    w_specs = [wspec((d, fe)), wspec((d, fe)), wspec((fe, d))]
    grid_spec = pltpu.PrefetchScalarGridSpec(
        num_scalar_prefetch=1,
        grid=(n_slots // tm,),
        in_specs=[slot_spec] + w_specs,
        out_specs=[slot_spec] + w_specs if cast else slot_spec,
        scratch_shapes=([pltpu.VMEM((d, fe), BF16), pltpu.VMEM((d, fe), BF16), pltpu.VMEM((fe, d), BF16)]
                        if cast else []),
    )
    bf = lambda w: jax.ShapeDtypeStruct(w.shape, BF16)
    return pl.pallas_call(
        _moe_kernel if cast else _moe_bf16_kernel,
        grid_spec=grid_spec,
        out_shape=[slots, bf(wg), bf(wu), bf(wd)] if cast else slots,
        compiler_params=_cparams(("arbitrary",)),
        name="moe_experts",
    )(tile_expert, x_sorted, wg, wu, wd)


def _moe_dispatch(ids_rows, tm):
    n_tok = ids_rows.shape[1]
    ids = jnp.concatenate([ids_rows[0], ids_rows[1]]).astype(jnp.int32)
    n_pairs = ids.shape[0]
    n_slots = (-(-n_pairs // tm) + N_EXPERTS) * tm
    hot = (ids[:, None] == jnp.arange(N_EXPERTS)[None, :]).astype(jnp.int32)
    csum = jnp.cumsum(hot, axis=0)
    rank = jnp.sum((csum - hot) * hot, axis=1)
    counts = csum[-1]
    padded = jnp.maximum(-(-counts // tm), 1) * tm
    ends = jnp.cumsum(padded)
    starts = ends - padded
    pos = jnp.sum(hot * starts[None, :], axis=1) + rank
    tok_of_slot = (jnp.arange(n_slots, dtype=jnp.int32) % n_tok).at[pos].set(
        jnp.arange(n_pairs, dtype=jnp.int32) % n_tok, unique_indices=True, mode="promise_in_bounds")
    tile_start = jnp.arange(n_slots // tm, dtype=jnp.int32) * tm
    tile_expert = jnp.minimum(jnp.sum((tile_start[:, None] >= ends[None, :]).astype(jnp.int32), axis=1),
                              N_EXPERTS - 1).astype(jnp.int32)
    return pos.reshape(2, n_tok), tok_of_slot, tile_expert


def _take_rows(x, idx):
    return x.at[idx].get(mode="promise_in_bounds")


def _final_kernel(x1_ref, ya_ref, yb_ref, route_ref, g2_ref, fg_ref, o_ref):
    wa = route_ref[:, 2:3]
    wb = route_ref[:, 3:4]
    x = x1_ref[...] + g2_ref[0] * (wa * ya_ref[...].astype(F32) + wb * yb_ref[...].astype(F32))
    o_ref[...] = _rmsnorm(x, fg_ref[...])


def _final_call(x1, ya, yb, route, mod, final_g):
    rows, d = x1.shape
    ts = _row_tile(WIDE_ROW_TILE, rows, mod)
    row = pl.BlockSpec((ts, d), lambda i: (i, 0))
    return pl.pallas_call(
        _final_kernel,
        grid=(rows // ts,),
        in_specs=[row, row, row, pl.BlockSpec((ts, LANES), lambda i: (i, 0)), _mod_spec(mod, rows, ts, 5),
                  pl.BlockSpec((1, d), lambda i: (0, 0))],
        out_specs=row,
        out_shape=jax.ShapeDtypeStruct((rows, d), F32),
        compiler_params=_cparams(("arbitrary",)),
        name="final_norm",
    )(x1, ya, yb, route, mod, final_g.reshape(1, d))


def kernel(x_prompt, x_sample, c_prompt, c_sample, cache_k, cache_v, state_ssm_re, state_ssm_im,
           rel_bias, ada_w, ada_b, norm1_g, w_in, ssm_a_re, ssm_a_im, ssm_log_dt, ssm_b_re, ssm_b_im,
           ssm_c_re, ssm_c_im, ssm_d, glu_a, glu_b, attn_out_g, ssm_out_g, w_out, norm2_g,
           router_g_w, router_g_b, router_e_w, router_e_b, w_gate, w_up, w_down, final_norm_g):
    if ada_w.shape[0] != 1:
        raise ValueError("single-layer trunk expected")
    nb, s_len, d = x_prompt.shape
    nd, t_len, _ = x_sample.shape
    if s_len != max(WINDOWS):
        raise ValueError("prompt length must equal the widest window")
    n_p, n_s = nb * s_len, nd * t_len
    if n_p % ROW_TILE or n_s % ROW_TILE:
        raise ValueError("token counts must be multiples of the row tile")

    mod = _mod_call(jnp.concatenate([c_prompt, c_sample], axis=0).astype(F32), ada_w[0], ada_b[0])
    mod_p = mod[:nb].reshape(nb, 1, 6 * d)
    mod_s = mod[nb:].reshape(1, nd, 6 * d)

    w_in_bf = w_in[0].astype(BF16)
    w_out_bf = w_out[0].astype(BF16)
    xp_rows = x_prompt.reshape(n_p, d)
    xs_rows = jnp.transpose(x_sample, (1, 0, 2)).reshape(n_s, d)

    qp, kp, vp, up, kp_t, vp_t = _inproj_call(xp_rows, mod_p, norm1_g[0], w_in_bf, seq_len=s_len)
    qs, ks, vs, us, ks_t, vs_t = _inproj_call(xs_rows, mod_s, norm1_g[0], w_in_bf, seq_len=nd)

    seq = lambda a: a.reshape(nb, s_len, ATTN_WIDTH)
    att_p = _attn_prompt_call(seq(qp), seq(kp), seq(vp), _prompt_bias_tiles(rel_bias))
    dec = lambda a: jnp.transpose(a.reshape(t_len, nd, ATTN_WIDTH), (1, 0, 2))
    att_s = _attn_decode_call(dec(qs), dec(ks), dec(vs), cache_k[0], cache_v[0], rel_bias)
    att_s = jnp.transpose(att_s, (1, 0, 2)).reshape(n_s, ATTN_WIDTH)

    s5 = (ssm_a_re[0], ssm_a_im[0], ssm_log_dt[0], ssm_b_re[0], ssm_b_im[0], ssm_c_re[0], ssm_c_im[0])
    t_op, m_op, p_op, sc = _s5_prompt_operators(*s5)
    d_oct = ssm_d[0].astype(F32).reshape(N_OCTETS, 1, LANES)
    ssm_p, hT_p = _ssm_prompt_call(seq(up), t_op, m_op, p_op, sc, d_oct,
                                   _octet_glu(glu_a[0]), _octet_glu(glu_b[0]))
    hT_p = hT_p.reshape(nb, N_OCTETS, 2, OCTET, SSM_STATE)
    ssm_re_p = hT_p[:, :, 0].reshape(nb, SSM_GROUPS, SSM_STATE)
    ssm_im_p = hT_p[:, :, 1].reshape(nb, SSM_GROUPS, SSM_STATE)
    ssm_s, hre_s, him_s = _ssm_decode_call(us.reshape(t_len, nd, SSM_WIDTH), state_ssm_re[0],
                                           state_ssm_im[0], *s5, ssm_d[0], glu_a[0], glu_b[0])
    ssm_s = ssm_s.reshape(n_s, SSM_WIDTH)

    wr, rb = _router_weights(router_g_w[0], router_g_b[0], router_e_w[0], router_e_b[0])
    norms = (attn_out_g[0], ssm_out_g[0], norm2_g[0], w_out_bf, wr, rb)
    x1_p, h2_p, route_p, ids_p = _outproj_call(att_p.reshape(n_p, ATTN_WIDTH), ssm_p.reshape(n_p, SSM_WIDTH),
                                        xp_rows, mod_p, *norms)
    ne = N_EXPERTS
    wg = w_gate[0].reshape(ne, d, D_EXPERT).astype(F32)
    wu = w_up[0].reshape(ne, d, D_EXPERT).astype(F32)
    wd = w_down[0].reshape(ne, D_EXPERT, d).astype(F32)

    def combine(x1, y_slots, pos, route, mod_rows):
        return _final_call(x1, _take_rows(y_slots, pos[0]), _take_rows(y_slots, pos[1]),
                           route, mod_rows, final_norm_g)

    pos_p, tok_of_slot_p, tile_expert_p = _moe_dispatch(ids_p, MOE_TILE)
    x_sorted_p = _take_rows(h2_p, tok_of_slot_p)
    x_sorted_p, att_s = lax.optimization_barrier((x_sorted_p, att_s))
    y_slots_p, wg_bf, wu_bf, wd_bf = _moe_call(tile_expert_p, x_sorted_p, wg, wu, wd, MOE_TILE)
    y_p = combine(x1_p, y_slots_p, pos_p, route_p, mod_p)

    x1_s, h2_s, route_s, ids_s = _outproj_call(att_s, ssm_s, xs_rows, mod_s, *norms)
    pos_s, tok_of_slot_s, tile_expert_s = _moe_dispatch(ids_s, MOE_TILE_DECODE)
    y_slots_s = _moe_call(tile_expert_s, _take_rows(h2_s, tok_of_slot_s), wg_bf, wu_bf, wd_bf,
                          MOE_TILE_DECODE)
    y_s = combine(x1_s, y_slots_s, pos_s, route_s, mod_s)

    heads = (ATTN_HEADS, HEAD_DIM)
    cache_out = lambda a: jnp.transpose(a.reshape((1, nb) + heads + (s_len,)), (0, 1, 4, 2, 3))
    step_out = lambda a: jnp.transpose(a.reshape((1, t_len) + heads + (nd,)), (0, 4, 1, 2, 3))
    return (y_p.reshape(nb, s_len, d), jnp.transpose(y_s.reshape(t_len, nd, d), (1, 0, 2)),
            cache_out(kp_t), cache_out(vp_t), step_out(ks_t), step_out(vs_t),
            ssm_re_p[None], ssm_im_p[None],
            hre_s.reshape(1, nd, SSM_GROUPS, SSM_STATE), him_s.reshape(1, nd, SSM_GROUPS, SSM_STATE))
```

```python
import functools
import math

import numpy as np

import jax
import jax.numpy as jnp
from jax import lax
from jax.experimental import pallas as pl
from jax.experimental.pallas import tpu as pltpu

F32 = jnp.float32
BF16 = jnp.bfloat16

D_MODEL = 1024
HEAD_DIM = 64
ATTN_WIDTH = 512
ATTN_HEADS = 8
SSM_WIDTH = 512
SSM_GROUP_CH = 16
SSM_GROUPS = 32
SSM_STATE = 64
WINDOWS = (128, 512, 2048)
DILATIONS = (1, 4, 16)
WINDOW_STEPS = 128
N_BUCKETS = 32
MAX_EXACT = 16
BUCKET_MAX_DIST = 2048
N_EXPERT_GROUPS = 4
EXPERTS_PER_GROUP = 4
N_EXPERTS = N_EXPERT_GROUPS * EXPERTS_PER_GROUP
D_EXPERT = 512
NORM_EPS = 1e-6

LANES = 128
Q_ROWS = 16
DECODE_SEQS = 2
OCTET = LANES // SSM_GROUP_CH
N_OCTETS = SSM_GROUPS // OCTET
OCT_STATE = OCTET * SSM_STATE
SSM_CHUNK = 16
SSM_SEQS = 4
ROW_TILE = 512
WIDE_ROW_TILE = 1024
MOE_TILE = 512
MOE_TILE_DECODE = 128
ROUTE_ROWS = 8
ATTN_GROUP = 4
VMEM_LIMIT = 56 * 1024 * 1024

_NEG_INF = float("-inf")
_HIGHEST = lax.Precision.HIGHEST


def _cparams(sem):
    return pltpu.CompilerParams(dimension_semantics=sem, vmem_limit_bytes=VMEM_LIMIT)


def _rmsnorm(x, g):
    return x * lax.rsqrt(jnp.mean(x * x, axis=-1, keepdims=True) + NORM_EPS) * g


def _gelu_tanh(x):
    c = math.sqrt(2.0 / math.pi)
    return 0.5 * x * (1.0 + jnp.tanh(c * (x + 0.044715 * (x * x * x))))


def _sigmoid(x):
    return 1.0 / (1.0 + jnp.exp(-x))


def _mod_kernel(c_ref, w_ref, b_ref, o_ref):
    c = c_ref[...]
    a = (c * _sigmoid(c)).astype(BF16)
    o_ref[...] = jnp.dot(a, w_ref[...].astype(BF16), preferred_element_type=F32) + b_ref[...]


def _mod_call(c_all, ada_w, ada_b):
    rows, d = c_all.shape
    n_out = ada_w.shape[1]
    tn = 1024
    return pl.pallas_call(
        _mod_kernel,
        grid=(n_out // tn,),
        in_specs=[pl.BlockSpec((rows, d), lambda j: (0, 0)),
                  pl.BlockSpec((d, tn), lambda j: (0, j)),
                  pl.BlockSpec((1, tn), lambda j: (0, j))],
        out_specs=pl.BlockSpec((rows, tn), lambda j: (0, j)),
        out_shape=jax.ShapeDtypeStruct((rows, n_out), F32),
        compiler_params=_cparams(("arbitrary",)),
        name="adaln_mod",
    )(c_all, ada_w, ada_b.reshape(1, n_out))


def _mod_spec(mod, rows, ts, chunk):
    if mod.shape[1] == 1:
        tiles_per_group = (rows // mod.shape[0]) // ts
        return pl.BlockSpec((1, 1, D_MODEL), lambda i: (i // tiles_per_group, 0, chunk))
    tiles_per_period = mod.shape[1] // ts
    return pl.BlockSpec((1, ts, D_MODEL), lambda i: (0, i % tiles_per_period, chunk))


def _row_tile(tile, rows, mod):
    return min(tile, rows) if mod.shape[1] == 1 else min(tile, rows, mod.shape[1])


def _inproj_kernel(x_ref, sh_ref, sc_ref, g_ref, w_ref, *rest, key_major):
    h = _rmsnorm(x_ref[...], g_ref[...]) * (1.0 + sc_ref[0]) + sh_ref[0]
    hb = h.astype(BF16)
    z = jnp.dot(hb, w_ref[...], preferred_element_type=F32)
    aw = ATTN_WIDTH
    if key_major:
        q_ref, k_ref, v_ref, u_ref, kt_ref, vt_ref = rest
        kt_ref[0] = z[:, aw:2 * aw].T
        vt_ref[0] = z[:, 2 * aw:3 * aw].T
    else:
        q_ref, k_ref, v_ref, u_ref = rest
    q_ref[...] = z[:, :aw]
    k_ref[...] = z[:, aw:2 * aw]
    v_ref[...] = z[:, 2 * aw:3 * aw]
    u_ref[...] = z[:, 3 * aw:]


def _inproj_call(x_rows, mod, norm_g, w_in_bf, seq_len=None):
    rows, d = x_rows.shape
    ts = _row_tile(ROW_TILE, rows, mod)
    proj = w_in_bf.shape[1]
    out = jax.ShapeDtypeStruct((rows, ATTN_WIDTH), F32)
    ospec = pl.BlockSpec((ts, ATTN_WIDTH), lambda i: (i, 0))
    in_specs = [pl.BlockSpec((ts, d), lambda i: (i, 0)),
                _mod_spec(mod, rows, ts, 0),
                _mod_spec(mod, rows, ts, 1),
                pl.BlockSpec((1, d), lambda i: (0, 0)),
                pl.BlockSpec((d, proj), lambda i: (0, 0))]
    args = [x_rows, mod, mod, norm_g.reshape(1, d), w_in_bf]
    out_specs = [ospec, ospec, ospec, ospec]
    out_shape = [out, out, out, out]
    if seq_len is not None:
        tiles = seq_len // ts
        tspec = pl.BlockSpec((1, ATTN_WIDTH, ts), lambda i: (i // tiles, 0, i % tiles))
        out_specs += [tspec, tspec]
        out_shape += [jax.ShapeDtypeStruct((rows // seq_len, ATTN_WIDTH, seq_len), F32)] * 2
    return pl.pallas_call(
        functools.partial(_inproj_kernel, key_major=seq_len is not None),
        grid=(rows // ts,),
        in_specs=in_specs,
        out_specs=out_specs,
        out_shape=out_shape,
        compiler_params=_cparams(("arbitrary",)),
        name="inproj",
    )(*args)


def _t5_bucket(dist):
    d = jnp.maximum(dist, MAX_EXACT).astype(F32)
    log_part = MAX_EXACT + (jnp.log(d / MAX_EXACT) / math.log(BUCKET_MAX_DIST / MAX_EXACT)
                            * (N_BUCKETS - MAX_EXACT)).astype(jnp.int32)
    return jnp.where(dist < MAX_EXACT, dist, jnp.minimum(log_part, N_BUCKETS - 1))


def _bias_by_distance(rel_bias, dists):
    hot = (_t5_bucket(jnp.asarray(dists, jnp.int32))[:, None]
           == jnp.arange(N_BUCKETS, dtype=jnp.int32)[None, :]).astype(F32)
    return jnp.dot(hot, rel_bias.astype(F32), precision=_HIGHEST)


def _prompt_bias_tiles(rel_bias):
    steps = WINDOW_STEPS
    period = 3 * steps
    tiles = []
    for r in DILATIONS:
        vec = _bias_by_distance(rel_bias, r * np.arange(steps + 1))
        fill = jnp.full((steps - 1, ATTN_HEADS), _NEG_INF, F32)
        w = jnp.concatenate([fill, vec[::-1], fill, fill[:1]], axis=0)
        rep = jnp.tile(w.T, (1, steps))[:, :steps * (period - 1)]
        toe = rep.reshape(ATTN_HEADS, steps, period - 1)[:, :, steps - 1:]
        toe = toe.reshape(ATTN_HEADS // 2, 2, steps, 2 * steps)
        own_only = jnp.where(jnp.arange(2 * steps) < steps, _NEG_INF, toe)
        tiles.append(jnp.stack([toe, own_only], axis=2))
    return jnp.stack(tiles)


def _attn_prompt_kernel(q_ref, k_ref, v_ref, bias_ref, o_ref,
                        p4_ref, kb_ref, vb_ref, s_ref, p_ref, res_ref, stage_ref, nat_ref):
    s_len = q_ref.shape[1]
    steps = WINDOW_STEPS
    n_tiles = s_len // steps
    quarter = s_len // 4
    nt = (((1,), (1,)), ((), ()))
    lane = lax.broadcasted_iota(jnp.int32, (1, LANES), 1)
    first_head = lane < HEAD_DIM
    srcs = (q_ref, k_ref, v_ref)

    kb_ref[0:steps, :] = jnp.zeros((steps, LANES), BF16)
    vb_ref[:, 0:steps, 0:LANES] = jnp.zeros((2, steps, LANES), BF16)
    vb_ref[:, :, LANES:] = jnp.ones((2, s_len + steps, LANES), BF16)
    for x in range(3):
        for sigma in range(4):
            p4_ref[x, sigma * quarter:(sigma + 1) * quarter, :] = srcs[x][0, pl.ds(sigma, quarter, stride=4), :]

    def aligned(start):
        return start if isinstance(start, int) else pl.multiple_of(start, steps)

    def source(branch, x, tile_idx):
        if branch == 2:
            sigma, tau = tile_idx % 4, tile_idx // 4
            return p4_ref[x, pl.ds(sigma * quarter + tau, steps, stride=4), :]
        rows = pl.ds(aligned(tile_idx * steps), steps)
        return srcs[x][0, rows, :] if branch == 0 else p4_ref[x, rows, :]

    def stage_operands(branch):
        for t in range(n_tiles):
            kb_ref[steps + t * steps:steps + (t + 1) * steps, :] = source(branch, 1, t).astype(BF16)
            vb_ref[branch % 2, steps + t * steps:steps + (t + 1) * steps, 0:LANES] = (
                source(branch, 2, t).astype(BF16))

    stage_operands(0)
    for branch, r in enumerate(DILATIONS):
        blocks_per_class = (s_len // r) // steps
        width = 2 * steps if blocks_per_class > 1 else steps

        def tile_rows(t):
            return pl.ds(aligned(t * steps), steps)

        def key_rows(t, width=width):
            start = t * steps if width == 2 * steps else (t + 1) * steps
            return pl.ds(aligned(start), width)

        def scores(t, branch=branch, width=width, blocks_per_class=blocks_per_class, key_rows=key_rows):
            rows = tile_rows(t)
            keys = kb_ref[key_rows(t), :]
            first = jnp.where(t % blocks_per_class == 0, 1, 0)
            q2 = source(branch, 0, t) * (HEAD_DIM ** -0.5)
            for hh in range(2):
                if width == 2 * steps:
                    bias = bias_ref[branch, 0, hh, pl.ds(first, 1), :, :][0]
                else:
                    bias = bias_ref[branch, 0, hh, 0, :, steps:]
                qh = jnp.where(first_head == (hh == 0), q2, 0.0).astype(BF16)
                sc = lax.dot_general(qh, keys, nt, preferred_element_type=F32)
                s_ref[hh, rows, 0:width] = sc + bias

        out_ref = nat_ref.at[0] if branch == 0 else res_ref

        def softmax(t, width=width, out_ref=out_ref):
            rows = tile_rows(t)
            for hh in range(2):
                sc = s_ref[hh, rows, 0:width]
                m = jnp.max(sc, axis=1, keepdims=True)
                p_ref[hh, rows, 0:width] = jnp.exp(sc - m).astype(BF16)
                out_ref[1, rows, hh * HEAD_DIM:(hh + 1) * HEAD_DIM] = jnp.broadcast_to(m, (steps, HEAD_DIM))

        def weighted(t, branch=branch, width=width, key_rows=key_rows, out_ref=out_ref):
            rows = tile_rows(t)
            vals = vb_ref[branch % 2, key_rows(t), :]
            r0 = jnp.dot(p_ref[0, rows, 0:width], vals, preferred_element_type=F32)
            r1 = jnp.dot(p_ref[1, rows, 0:width], vals, preferred_element_type=F32)
            out_ref[0, rows, :] = jnp.where(first_head, r0[:, :LANES], r1[:, :LANES])
            out_ref[2, rows, :] = jnp.where(first_head, r0[:, LANES:], r1[:, LANES:])

        def stage(g_scores, g_softmax, scores=scores, softmax=softmax):
            for fn, g in ((softmax, g_softmax), (scores, g_scores)):
                if g is not None:
                    for j in range(ATTN_GROUP):
                        fn(g * ATTN_GROUP + j)

        n_groups = n_tiles // ATTN_GROUP
        stage(0, None)

        def steady(g, carry, stage=stage):
            stage(g, g - 1)
            return carry

        lax.fori_loop(1, n_groups, steady, 0)
        stage(None, n_groups - 1)

        def to_natural_order(branch=branch):
            for kind in range(3 if branch else 0):
                src = res_ref
                if branch == 2:
                    for t in range(n_tiles):
                        sigma, tau = t % 4, t // 4
                        stage_ref[kind, pl.ds(sigma * quarter + tau, steps, stride=4), :] = (
                            res_ref[kind, t * steps:(t + 1) * steps, :])
                    src = stage_ref
                for sigma in range(4):
                    nat_ref[branch, kind, pl.ds(sigma, quarter, stride=4), :] = (
                        src[kind, sigma * quarter:(sigma + 1) * quarter, :])

        @pl.when(pl.program_id(1) >= 0)
        def _(branch=branch, weighted=weighted, to_natural_order=to_natural_order):
            for t in range(n_tiles):
                weighted(t)
            to_natural_order()
            if branch + 1 < len(DILATIONS):
                stage_operands(branch + 1)

    def merge(i, carry):
        rows = pl.ds(pl.multiple_of(i * 256, 256), 256)
        m0, m1, m2 = nat_ref[0, 1, rows, :], nat_ref[1, 1, rows, :], nat_ref[2, 1, rows, :]
        m_all = jnp.maximum(jnp.maximum(m0, m1), m2)
        w0, w1, w2 = jnp.exp(m0 - m_all), jnp.exp(m1 - m_all), jnp.exp(m2 - m_all)
        num = w0 * nat_ref[0, 0, rows, :] + w1 * nat_ref[1, 0, rows, :] + w2 * nat_ref[2, 0, rows, :]
        den = w0 * nat_ref[0, 2, rows, :] + w1 * nat_ref[1, 2, rows, :] + w2 * nat_ref[2, 2, rows, :]
        o_ref[0, rows, :] = num / den
        return carry

    lax.fori_loop(0, s_len // 256, merge, 0)


def _attn_prompt_call(q, k, v, bias_tiles):
    n, s, _ = q.shape
    pairs = ATTN_HEADS // 2
    steps = WINDOW_STEPS
    qspec = pl.BlockSpec((1, s, LANES), lambda g, n_: (n_, 0, g))
    return pl.pallas_call(
        _attn_prompt_kernel,
        grid=(pairs, n),
        in_specs=[qspec, qspec, qspec,
                  pl.BlockSpec((3, 1, 2, 2, steps, 2 * steps), lambda g, n_: (0, g, 0, 0, 0, 0))],
        out_specs=qspec,
        out_shape=jax.ShapeDtypeStruct((n, s, ATTN_WIDTH), F32),
        scratch_shapes=[pltpu.VMEM((3, s, LANES), F32),
                        pltpu.VMEM((s + steps, LANES), BF16),
                        pltpu.VMEM((2, s + steps, 2 * LANES), BF16),
                        pltpu.VMEM((2, s, 2 * steps), F32),
                        pltpu.VMEM((2, s, 2 * steps), BF16),
                        pltpu.VMEM((3, s, LANES), F32), pltpu.VMEM((3, s, LANES), F32),
                        pltpu.VMEM((3, 3, s, LANES), F32)],
        compiler_params=_cparams(("arbitrary", "arbitrary")),
        name="attn_prompt",
    )(q, k, v, bias_tiles)


def _s5_discretise(a_re, a_im, log_dt, b_re, b_im):
    lam_re = jnp.minimum(a_re.astype(F32), -1e-4)
    lam_im = a_im.astype(F32)
    dt = jnp.exp(log_dt.astype(F32))[:, None]
    mag = jnp.exp(lam_re * dt)
    ph = lam_im * dt
    abar_re, abar_im = mag * jnp.cos(ph), mag * jnp.sin(ph)
    nr, ni = abar_re - 1.0, abar_im
    den = lam_re * lam_re + lam_im * lam_im
    coef_re = (nr * lam_re + ni * lam_im) / den
    coef_im = (ni * lam_re - nr * lam_im) / den
    br, bi = b_re.astype(F32), b_im.astype(F32)
    bbar_re = coef_re[..., None] * br - coef_im[..., None] * bi
    bbar_im = coef_re[..., None] * bi + coef_im[..., None] * br
    return lam_re * dt, ph, abar_re, abar_im, bbar_re, bbar_im


def _abar_power(log_mag, ph, n):
    nf = jnp.asarray(n, F32)[:, None, None]
    mag = jnp.exp(nf * log_mag[None])
    return mag * jnp.cos(nf * ph[None]), mag * jnp.sin(nf * ph[None])


def _s5_prompt_operators(a_re, a_im, log_dt, b_re, b_im, c_re, c_im):
    L = SSM_CHUNK
    log_mag, ph, _, _, bb_re, bb_im = _s5_discretise(a_re, a_im, log_dt, b_re, b_im)
    cr, ci = c_re.astype(F32), c_im.astype(F32)
    pw_re, pw_im = _abar_power(log_mag, ph, np.arange(L + 1))
    eye = jnp.eye(OCTET, dtype=F32)

    ab_re = pw_re[:L, :, :, None] * bb_re[None] - pw_im[:L, :, :, None] * bb_im[None]
    ab_im = pw_re[:L, :, :, None] * bb_im[None] + pw_im[:L, :, :, None] * bb_re[None]
    lag = (jnp.einsum('gop,lgpi->lgoi', cr, ab_re, precision=_HIGHEST)
           - jnp.einsum('gop,lgpi->lgoi', ci, ab_im, precision=_HIGHEST))
    lag = lag.reshape(L, N_OCTETS, OCTET, SSM_GROUP_CH, SSM_GROUP_CH)
    bd = jnp.einsum('logci,gh->olgihc', lag, eye).reshape(N_OCTETS, L, LANES, LANES).astype(BF16)
    stack = bd[:, ::-1].reshape(N_OCTETS, L * LANES, LANES)
    shifted = jnp.concatenate([stack[:, LANES:], jnp.zeros((N_OCTETS, LANES, LANES), BF16)], axis=1)
    t_op = jnp.concatenate([shifted, stack], axis=-1)

    group_of_lane = jnp.arange(LANES) // SSM_GROUP_CH
    parts = []
    for part in (ab_re[::-1], ab_im[::-1]):
        x = part.reshape(L, N_OCTETS, OCTET, SSM_STATE, SSM_GROUP_CH)
        parts.append(jnp.transpose(x, (1, 0, 3, 2, 4)).reshape(N_OCTETS, L, SSM_STATE, LANES))
    x = jnp.stack(parts, axis=2)
    own = (jnp.arange(OCTET)[:, None, None] == group_of_lane[None, None, :])
    mt_op = jnp.where(own, x[:, :, :, None], 0.0)
    mt_op = mt_op.reshape(N_OCTETS, L, 2 * OCT_STATE, LANES)

    p1_re, p1_im = pw_re[1:], pw_im[1:]
    on_re = cr[None] * p1_re[:, :, None, :] - ci[None] * p1_im[:, :, None, :]
    on_im = -cr[None] * p1_im[:, :, None, :] - ci[None] * p1_re[:, :, None, :]
    parts = []
    for part in (on_re, on_im):
        y = part.reshape(L, N_OCTETS, OCTET, SSM_GROUP_CH, SSM_STATE)
        parts.append(jnp.transpose(y, (1, 0, 3, 2, 4)).reshape(N_OCTETS, L, SSM_GROUP_CH, OCT_STATE))
    y = jnp.concatenate(parts, axis=-1)
    group_of_state = (jnp.arange(2 * OCT_STATE) % OCT_STATE) // SSM_STATE
    own = (jnp.arange(OCTET)[:, None, None] == group_of_state[None, None, :])
    pt_op = jnp.where(own, y[:, :, None], 0.0)
    pt_op = pt_op.reshape(N_OCTETS, L * LANES, 2 * OCT_STATE)

    n_steps = 8
    sc_re, sc_im = _abar_power(log_mag, ph, L * (2 ** np.arange(n_steps)))
    sc = jnp.concatenate([sc_re.reshape(n_steps, N_OCTETS, OCT_STATE),
                          sc_im.reshape(n_steps, N_OCTETS, OCT_STATE)], axis=-1)
    sc = jnp.transpose(sc, (1, 0, 2))
    return t_op, mt_op.astype(BF16), pt_op.astype(BF16), sc


def _octet_glu(glu):
    eye = jnp.eye(OCTET, dtype=F32)
    x = glu.astype(F32).reshape(N_OCTETS, OCTET, SSM_GROUP_CH, SSM_GROUP_CH)
    return jnp.einsum('ogce,gh->ogche', x, eye).reshape(N_OCTETS, LANES, LANES).astype(BF16)


def _ssm_prompt_kernel(u_ref, t_ref, mb_ref, pt_ref, sc_ref, d_ref, ga_ref, gb_ref,
                       y_ref, h_ref, uf_ref, ub_ref, st_ref, mt_ref):
    L = SSM_CHUNK
    nt = (((1,), (1,)), ((), ()))

    @pl.when(pl.program_id(1) == 0)
    def _():
        for s in range(L):
            mt_ref[:, s * LANES:(s + 1) * LANES] = mb_ref[0, s]

    n_seq, s_len = u_ref.shape[0], u_ref.shape[1]
    n_chunks = s_len // L
    quarter = s_len // 4
    rows = n_seq * n_chunks

    def staged(sq, step):
        sigma, tau = step % 4, step // 4
        return pl.ds(sq * s_len + sigma * quarter + tau, n_chunks, stride=4)

    for sq in range(n_seq):
        for sigma in range(4):
            st_ref[sq * s_len + sigma * quarter:sq * s_len + (sigma + 1) * quarter, :] = (
                u_ref[sq, pl.ds(sigma, quarter, stride=4), :])
        for step in range(L):
            blk = st_ref[staged(sq, step), :]
            uf_ref[sq * n_chunks:(sq + 1) * n_chunks, step * LANES:(step + 1) * LANES] = blk
            ub_ref[sq * n_chunks:(sq + 1) * n_chunks, step * LANES:(step + 1) * LANES] = blk.astype(BF16)
    ub = ub_ref[...]

    x = lax.dot_general(ub, mt_ref[...], nt, preferred_element_type=F32)
    chunk = lax.broadcasted_iota(jnp.int32, (rows, 1), 0) % n_chunks
    half = OCT_STATE
    k = 1
    step = 0
    while k < n_chunks:
        a_re = sc_ref[0, step:step + 1, :half]
        a_im = sc_ref[0, step:step + 1, half:]
        sh = jnp.where(chunk >= k, pltpu.roll(x, k, axis=0), 0.0)
        s_re, s_im = sh[:, :half], sh[:, half:]
        x = x + jnp.concatenate([a_re * s_re - a_im * s_im, a_re * s_im + a_im * s_re], axis=1)
        k *= 2
        step += 1
    for sq in range(n_seq):
        h_ref[sq, 0] = x[(sq + 1) * n_chunks - 1:(sq + 1) * n_chunks, :]
    h_start = jnp.where(chunk >= 1, pltpu.roll(x, 1, axis=0), 0.0)

    hb = h_start.astype(BF16)
    d = d_ref[0]
    ga = ga_ref[0]
    gb = gb_ref[0]
    for t in range(0, L, 2):
        pair = slice(t * LANES, (t + 2) * LANES)
        y2 = (jnp.dot(ub_ref[:, :(t + 2) * LANES], t_ref[0, (L - 2 - t) * LANES:, :],
                      preferred_element_type=F32)
              + lax.dot_general(hb, pt_ref[0, pair, :], nt, preferred_element_type=F32))
        for j in range(2):
            lanes = slice((t + j) * LANES, (t + j + 1) * LANES)
            g = _gelu_tanh(y2[:, j * LANES:(j + 1) * LANES] + d * uf_ref[:, lanes]).astype(BF16)
            out = (jnp.dot(g, ga, preferred_element_type=F32)
                   * _sigmoid(jnp.dot(g, gb, preferred_element_type=F32)))
            for sq in range(n_seq):
                st_ref[staged(sq, t + j), :] = out[sq * n_chunks:(sq + 1) * n_chunks, :]
    for sq in range(n_seq):
        for sigma in range(4):
            y_ref[sq, pl.ds(sigma, quarter, stride=4), :] = (
                st_ref[sq * s_len + sigma * quarter:sq * s_len + (sigma + 1) * quarter, :])


def _ssm_prompt_call(u, t_op, m_op, p_op, sc, d_oct, ga, gb):
    n, s, _ = u.shape
    L = SSM_CHUNK
    nq = math.gcd(SSM_SEQS, n)
    rows = nq * (s // L)
    wide = L * LANES
    wspec = lambda shape: pl.BlockSpec((1,) + shape, lambda o, n_: (o, 0, 0))
    return pl.pallas_call(
        _ssm_prompt_kernel,
        grid=(N_OCTETS, n // nq),
        in_specs=[pl.BlockSpec((nq, s, LANES), lambda o, n_: (n_, 0, o)),
                  wspec((wide, 2 * LANES)),
                  pl.BlockSpec((1, L, 2 * OCT_STATE, LANES), lambda o, n_: (o, 0, 0, 0)),
                  wspec((wide, 2 * OCT_STATE)),
                  wspec((8, 2 * OCT_STATE)), wspec((1, LANES)),
                  wspec((LANES, LANES)), wspec((LANES, LANES))],
        out_specs=[pl.BlockSpec((nq, s, LANES), lambda o, n_: (n_, 0, o)),
                   pl.BlockSpec((nq, 1, 1, 2 * OCT_STATE), lambda o, n_: (n_, o, 0, 0))],
        out_shape=[jax.ShapeDtypeStruct((n, s, SSM_WIDTH), F32),
                   jax.ShapeDtypeStruct((n, N_OCTETS, 1, 2 * OCT_STATE), F32)],
        scratch_shapes=[pltpu.VMEM((rows, wide), F32),
                        pltpu.VMEM((rows, wide), BF16),
                        pltpu.VMEM((nq * s, LANES), F32),
                        pltpu.VMEM((2 * OCT_STATE, wide), BF16)],
        compiler_params=_cparams(("arbitrary", "arbitrary")),
        name="ssm_prompt",
    )(u, t_op, m_op, p_op, sc, d_oct, ga, gb)


def _attn_decode_kernel(q_ref, kn_ref, vn_ref, kt_ref, vt_ref, b_ref, mult_ref, o_ref):
    nt = (((1,), (1,)), ((), ()))
    mult = mult_ref[...]
    for sq in range(q_ref.shape[0]):
        for h in range(ATTN_HEADS):
            q = q_ref[sq, h]
            kt = jnp.concatenate([kt_ref[sq, h].astype(BF16), kn_ref[sq]], axis=1)
            vt = jnp.concatenate([vt_ref[sq, h].astype(BF16), vn_ref[sq]], axis=1)
            s = jnp.dot(q, kt, preferred_element_type=F32) + b_ref[h]
            m = jnp.max(s, axis=1, keepdims=True)
            p = jnp.exp(s - m) * mult
            den = jnp.sum(p, axis=1, keepdims=True)
            o = lax.dot_general(p.astype(BF16), vt, nt, preferred_element_type=F32)
            o_ref[sq, h] = o / den


def _decode_tables(rel_bias, t_len, w_rows):
    slot = LANES // ATTN_HEADS
    t = np.arange(t_len)[:, None]
    new_t = np.arange(LANES)[None, :] % slot
    dist = np.concatenate([w_rows + t - np.arange(w_rows)[None, :],
                           t - new_t], axis=1)
    mult = np.zeros(dist.shape, np.float32)
    for w, r in zip(WINDOWS, DILATIONS):
        mult += (dist >= 0) & (dist % r == 0) & (dist <= w)
    mult = np.concatenate([mult, np.zeros((Q_ROWS - t_len, dist.shape[1]), np.float32)], axis=0)
    mult[t_len:, 0] = 1.0
    by_dist = _bias_by_distance(rel_bias, np.arange(w_rows + t_len))
    rows = []
    for ti in range(t_len):
        cache_part = by_dist[ti + 1:w_rows + ti + 1][::-1]
        new_part = jnp.concatenate([by_dist[:ti + 1][::-1],
                                    jnp.zeros((slot - ti - 1, ATTN_HEADS), F32)], axis=0)
        rows.append(jnp.concatenate([cache_part] + [new_part] * ATTN_HEADS, axis=0))
    bias = jnp.stack(rows + [jnp.zeros_like(rows[0])] * (Q_ROWS - t_len), axis=0)
    bias = jnp.transpose(bias, (2, 0, 1))
    own_head = np.concatenate([np.ones((ATTN_HEADS, w_rows), bool),
                               (np.arange(LANES) // slot)[None, :] == np.arange(ATTN_HEADS)[:, None]], axis=1)
    bias = jnp.where(jnp.asarray((mult[None] > 0) & own_head[:, None, :]), bias, _NEG_INF)
    return bias, jnp.asarray(mult)


def _attn_decode_call(q, k_new, v_new, cache_k, cache_v, rel_bias):
    n, t_len, w = q.shape
    w_rows = cache_k.shape[1]
    if t_len > min(DILATIONS[1:]) or t_len > Q_ROWS or w_rows < max(WINDOWS):
        raise ValueError("unsupported decode shape")
    heads = (ATTN_HEADS, HEAD_DIM)

    def head_major(a, pad_to):
        a = jnp.transpose(a.reshape((n, t_len) + heads), (0, 2, 1, 3))
        return jnp.pad(a, ((0, 0), (0, 0), (0, pad_to - t_len), (0, 0)))

    def new_tile(a):
        a = jnp.transpose(a.reshape((n, t_len) + heads), (0, 3, 2, 1))
        a = jnp.pad(a, ((0, 0), (0, 0), (0, 0), (0, LANES // ATTN_HEADS - t_len)))
        return a.reshape(n, HEAD_DIM, LANES).astype(BF16)

    qh = head_major(q * (HEAD_DIM ** -0.5), Q_ROWS).astype(BF16)
    knt, vnt = new_tile(k_new), new_tile(v_new)
    kt = jnp.transpose(cache_k.astype(F32), (0, 2, 3, 1))
    vt = jnp.transpose(cache_v.astype(F32), (0, 2, 3, 1))
    bias, mult = _decode_tables(rel_bias, t_len, w_rows)
    keys = w_rows + LANES
    nq = math.gcd(DECODE_SEQS, n)
    per_seq = lambda shape: pl.BlockSpec((nq,) + shape, lambda i: (i, 0, 0, 0))
    out = pl.pallas_call(
        _attn_decode_kernel,
        grid=(n // nq,),
        in_specs=[per_seq((ATTN_HEADS, Q_ROWS, HEAD_DIM)),
                  pl.BlockSpec((nq, HEAD_DIM, LANES), lambda i: (i, 0, 0)),
                  pl.BlockSpec((nq, HEAD_DIM, LANES), lambda i: (i, 0, 0)),
                  per_seq((ATTN_HEADS, HEAD_DIM, w_rows)), per_seq((ATTN_HEADS, HEAD_DIM, w_rows)),
                  pl.BlockSpec((ATTN_HEADS, Q_ROWS, keys), lambda i: (0, 0, 0)),
                  pl.BlockSpec((Q_ROWS, keys), lambda i: (0, 0))],
        out_specs=per_seq((ATTN_HEADS, Q_ROWS, HEAD_DIM)),
        out_shape=jax.ShapeDtypeStruct((n, ATTN_HEADS, Q_ROWS, HEAD_DIM), F32),
        compiler_params=_cparams(("arbitrary",)),
        name="attn_decode",
    )(qh, knt, vnt, kt, vt, bias, mult)
    return jnp.transpose(out[:, :, :t_len], (0, 2, 1, 3)).reshape(n, t_len, w)


def _ssm_decode_kernel(u_ref, hre_ref, him_ref, are_ref, aim_ref, bre_ref, bim_ref,
                       cre_ref, cim_ref, d_ref, ga_ref, gb_ref, y_ref, ore_ref, oim_ref, *, t_len):
    h_re, h_im = hre_ref[...], him_ref[...]
    a_re, a_im = are_ref[...], aim_ref[...]
    for t in range(t_len):
        u = u_ref[t]
        ub = u.astype(BF16)
        n_re = a_re * h_re - a_im * h_im + jnp.dot(ub, bre_ref[...], preferred_element_type=F32)
        n_im = a_re * h_im + a_im * h_re + jnp.dot(ub, bim_ref[...], preferred_element_type=F32)
        h_re, h_im = n_re, n_im
        y = (jnp.dot(h_re.astype(BF16), cre_ref[...], preferred_element_type=F32)
             - jnp.dot(h_im.astype(BF16), cim_ref[...], preferred_element_type=F32)
             + d_ref[...] * u)
        g = _gelu_tanh(y).astype(BF16)
        y_ref[t] = (jnp.dot(g, ga_ref[...], preferred_element_type=F32)
                    * _sigmoid(jnp.dot(g, gb_ref[...], preferred_element_type=F32)))
    ore_ref[...] = h_re
    oim_ref[...] = h_im


def _group_blockdiag(x):
    g, a, b = x.shape
    panel = jnp.transpose(x, (1, 0, 2)).reshape(a, g * b)
    own = jnp.arange(g)[:, None, None] == (jnp.arange(g * b) // b)[None, None, :]
    return jnp.where(own, panel[None], 0).reshape(g * a, g * b)


def _ssm_decode_call(u_tm, h0_re, h0_im, a_re, a_im, log_dt, b_re, b_im, c_re, c_im,
                     d_skip, glu_a, glu_b):
    n = h0_re.shape[0]
    _, _, abar_re, abar_im, bb_re, bb_im = _s5_discretise(a_re, a_im, log_dt, b_re, b_im)
    state = SSM_GROUPS * SSM_STATE
    t_len = u_tm.shape[0]
    args = (u_tm, h0_re.reshape(n, state).astype(F32), h0_im.reshape(n, state).astype(F32),
            abar_re.reshape(1, state), abar_im.reshape(1, state),
            _group_blockdiag(jnp.transpose(bb_re, (0, 2, 1))).astype(BF16),
            _group_blockdiag(jnp.transpose(bb_im, (0, 2, 1))).astype(BF16),
            _group_blockdiag(jnp.transpose(c_re.astype(F32), (0, 2, 1))).astype(BF16),
            _group_blockdiag(jnp.transpose(c_im.astype(F32), (0, 2, 1))).astype(BF16),
            d_skip.astype(F32).reshape(1, SSM_WIDTH),
            _group_blockdiag(glu_a.astype(F32)).astype(BF16),
            _group_blockdiag(glu_b.astype(F32)).astype(BF16))
    full = lambda a: pl.BlockSpec(a.shape, lambda i: (0,) * a.ndim)
    out_shape = [jax.ShapeDtypeStruct(u_tm.shape, F32),
                 jax.ShapeDtypeStruct((n, state), F32), jax.ShapeDtypeStruct((n, state), F32)]
    return pl.pallas_call(
        functools.partial(_ssm_decode_kernel, t_len=t_len),
        grid=(1,),
        in_specs=[full(a) for a in args],
        out_specs=[full(o) for o in out_shape],
        out_shape=out_shape,
        compiler_params=_cparams(("arbitrary",)),
        name="ssm_decode",
    )(*args)


def _outproj_kernel(att_ref, ssm_ref, x_ref, g1_ref, sh2_ref, sc2_ref, ag_ref, sg_ref, n2_ref,
                    wo_ref, wr_ref, rb_ref, x1_ref, h2_ref, route_ref, ids_ref):
    mixed = jnp.concatenate([_rmsnorm(att_ref[...], ag_ref[...]), _rmsnorm(ssm_ref[...], sg_ref[...])],
                            axis=1).astype(BF16)
    x1 = x_ref[...] + g1_ref[0] * jnp.dot(mixed, wo_ref[...], preferred_element_type=F32)
    x1_ref[...] = x1
    h2 = _rmsnorm(x1, n2_ref[...]) * (1.0 + sc2_ref[0]) + sh2_ref[0]
    hi = h2.astype(BF16)
    h2_ref[...] = hi
    lo = (h2 - hi.astype(F32)).astype(BF16)
    r1 = jnp.dot(hi, wr_ref[...], preferred_element_type=F32)
    r2 = jnp.dot(lo, wr_ref[:, :LANES], preferred_element_type=F32)
    logits = r1[:, :LANES] + r1[:, LANES:] + r2 + rb_ref[...]

    lane = lax.broadcasted_iota(jnp.int32, (1, LANES), 1)
    lane_f = lane.astype(F32)
    big = float(LANES)
    ng, epg = N_EXPERT_GROUPS, EXPERTS_PER_GROUP
    lg = jnp.where(lane < ng, logits, _NEG_INF)
    gmax = jnp.max(lg, axis=1, keepdims=True)
    p_star = 1.0 / jnp.sum(jnp.exp(lg - gmax), axis=1, keepdims=True)
    g_star = jnp.min(jnp.where(lg == gmax, lane_f, big), axis=1, keepdims=True)
    in_group = ((lane >= ng) & (lane < ng + ng * epg)
                & (lax.shift_right_arithmetic(lane - ng, int(math.log2(epg))).astype(F32) == g_star))
    le = jnp.where(in_group, logits, _NEG_INF)
    v1 = jnp.max(le, axis=1, keepdims=True)
    i1 = jnp.min(jnp.where(le == v1, lane_f, big), axis=1, keepdims=True)
    le2 = jnp.where(lane_f == i1, _NEG_INF, le)
    v2 = jnp.max(le2, axis=1, keepdims=True)
    i2 = jnp.min(jnp.where(le2 == v2, lane_f, big), axis=1, keepdims=True)
    e2 = jnp.exp(v2 - v1)
    w1 = p_star / (1.0 + e2)
    w2 = p_star * e2 / (1.0 + e2)
    route = jnp.where(lane == 0, i1 - ng,
                      jnp.where(lane == 1, i2 - ng,
                                jnp.where(lane == 2, w1, jnp.where(lane == 3, w2, 0.0))))
    route_ref[...] = route
    ids_ref[...] = route.T[:ids_ref.shape[0], :]


def _outproj_call(att, ssm_y, x_rows, mod, attn_g, ssm_g, norm2_g, w_out_bf, wr, rb):
    rows, d = x_rows.shape
    ts = _row_tile(WIDE_ROW_TILE, rows, mod)
    row = lambda width: pl.BlockSpec((ts, width), lambda i: (i, 0))
    const = lambda a: pl.BlockSpec(a.shape, lambda i: (0,) * a.ndim)
    attn_g = attn_g.reshape(1, ATTN_WIDTH)
    ssm_g = ssm_g.reshape(1, SSM_WIDTH)
    norm2_g = norm2_g.reshape(1, d)
    return pl.pallas_call(
        _outproj_kernel,
        grid=(rows // ts,),
        in_specs=[row(ATTN_WIDTH), row(SSM_WIDTH), row(d),
                  _mod_spec(mod, rows, ts, 2), _mod_spec(mod, rows, ts, 3), _mod_spec(mod, rows, ts, 4),
                  const(attn_g), const(ssm_g), const(norm2_g), const(w_out_bf), const(wr), const(rb)],
        out_specs=[row(d), row(d), row(LANES), pl.BlockSpec((ROUTE_ROWS, ts), lambda i: (0, i))],
        out_shape=[jax.ShapeDtypeStruct((rows, d), F32), jax.ShapeDtypeStruct((rows, d), BF16),
                   jax.ShapeDtypeStruct((rows, LANES), F32),
                   jax.ShapeDtypeStruct((ROUTE_ROWS, rows), F32)],
        compiler_params=_cparams(("arbitrary",)),
        name="outproj_router",
    )(att, ssm_y, x_rows, mod, mod, mod, attn_g, ssm_g, norm2_g, w_out_bf, wr, rb)


def _router_weights(router_g_w, router_g_b, router_e_w, router_e_b):
    d = router_g_w.shape[0]
    ne = N_EXPERT_GROUPS * EXPERTS_PER_GROUP
    w = jnp.concatenate([router_g_w.astype(F32),
                         jnp.transpose(router_e_w.astype(F32), (1, 0, 2)).reshape(d, ne)], axis=1)
    w = jnp.pad(w, ((0, 0), (0, LANES - w.shape[1])))
    hi = w.astype(BF16)
    lo = (w - hi.astype(F32)).astype(BF16)
    b = jnp.concatenate([router_g_b.astype(F32), router_e_b.astype(F32).reshape(ne)])
    b = jnp.pad(b, (0, LANES - b.shape[0])).reshape(1, LANES)
    return jnp.concatenate([hi, lo], axis=1), b


def _moe_kernel(te_ref, x_ref, wg_ref, wu_ref, wd_ref, o_ref, wgb_ref, wub_ref, wdb_ref):
    i = pl.program_id(0)
    changed = jnp.logical_or(i == 0, te_ref[i] != te_ref[jnp.maximum(i - 1, 0)])

    @pl.when(changed)
    def _():
        wgb_ref[...] = wg_ref[0].astype(BF16)
        wub_ref[...] = wu_ref[0].astype(BF16)
        wdb_ref[...] = wd_ref[0].astype(BF16)

    x = x_ref[...]
    gate = jnp.dot(x, wgb_ref[...], preferred_element_type=F32)
    up = jnp.dot(x, wub_ref[...], preferred_element_type=F32)
    a = (gate * _sigmoid(gate)) * up
    o_ref[...] = jnp.dot(a.astype(BF16), wdb_ref[...], preferred_element_type=F32).astype(o_ref.dtype)


def _moe_call(tile_expert, x_sorted, wg, wu, wd, tm):
    n_slots, d = x_sorted.shape
    fe = wg.shape[2]
    grid_spec = pltpu.PrefetchScalarGridSpec(
        num_scalar_prefetch=1,
        grid=(n_slots // tm,),
        in_specs=[pl.BlockSpec((tm, d), lambda i, te: (i, 0)),
                  pl.BlockSpec((1, d, fe), lambda i, te: (te[i], 0, 0)),
                  pl.BlockSpec((1, d, fe), lambda i, te: (te[i], 0, 0)),
                  pl.BlockSpec((1, fe, d), lambda i, te: (te[i], 0, 0))],
        out_specs=pl.BlockSpec((tm, d), lambda i, te: (i, 0)),
        scratch_shapes=[pltpu.VMEM((d, fe), BF16), pltpu.VMEM((d, fe), BF16), pltpu.VMEM((fe, d), BF16)],
    )
    return pl.pallas_call(
        _moe_kernel,
        grid_spec=grid_spec,
        out_shape=jax.ShapeDtypeStruct((n_slots, d), BF16),
        compiler_params=_cparams(("arbitrary",)),
        name="moe_experts",
    )(tile_expert, x_sorted, wg, wu, wd)


def _moe_dispatch(ids_rows, tm):
    n_tok = ids_rows.shape[1]
    ids = jnp.concatenate([ids_rows[0], ids_rows[1]]).astype(jnp.int32)
    n_pairs = ids.shape[0]
    n_slots = (-(-n_pairs // tm) + N_EXPERTS) * tm
    hot = (ids[:, None] == jnp.arange(N_EXPERTS)[None, :]).astype(jnp.int32)
    csum = jnp.cumsum(hot, axis=0)
    rank = jnp.sum((csum - hot) * hot, axis=1)
    counts = csum[-1]
    padded = -(-counts // tm) * tm
    ends = jnp.cumsum(padded)
    starts = ends - padded
    pos = jnp.sum(hot * starts[None, :], axis=1) + rank
    tok_of_slot = (jnp.arange(n_slots, dtype=jnp.int32) % n_tok).at[pos].set(
        jnp.arange(n_pairs, dtype=jnp.int32) % n_tok, unique_indices=True, mode="promise_in_bounds")
    tile_start = jnp.arange(n_slots // tm, dtype=jnp.int32) * tm
    tile_expert = jnp.minimum(jnp.sum((tile_start[:, None] >= ends[None, :]).astype(jnp.int32), axis=1),
                              N_EXPERTS - 1).astype(jnp.int32)
    return pos.reshape(2, n_tok), tok_of_slot, tile_expert


def _take_rows(x, idx):
    return x.at[idx].get(mode="promise_in_bounds")


def _final_kernel(x1_ref, ya_ref, yb_ref, route_ref, g2_ref, fg_ref, o_ref):
    wa = route_ref[:, 2:3]
    wb = route_ref[:, 3:4]
    x = x1_ref[...] + g2_ref[0] * (wa * ya_ref[...].astype(F32) + wb * yb_ref[...].astype(F32))
    o_ref[...] = _rmsnorm(x, fg_ref[...])


def _final_call(x1, ya, yb, route, mod, final_g):
    rows, d = x1.shape
    ts = _row_tile(WIDE_ROW_TILE, rows, mod)
    row = pl.BlockSpec((ts, d), lambda i: (i, 0))
    return pl.pallas_call(
        _final_kernel,
        grid=(rows // ts,),
        in_specs=[row, row, row, pl.BlockSpec((ts, LANES), lambda i: (i, 0)), _mod_spec(mod, rows, ts, 5),
                  pl.BlockSpec((1, d), lambda i: (0, 0))],
        out_specs=row,
        out_shape=jax.ShapeDtypeStruct((rows, d), F32),
        compiler_params=_cparams(("arbitrary",)),
        name="final_norm",
    )(x1, ya, yb, route, mod, final_g.reshape(1, d))


def kernel(x_prompt, x_sample, c_prompt, c_sample, cache_k, cache_v, state_ssm_re, state_ssm_im,
           rel_bias, ada_w, ada_b, norm1_g, w_in, ssm_a_re, ssm_a_im, ssm_log_dt, ssm_b_re, ssm_b_im,
           ssm_c_re, ssm_c_im, ssm_d, glu_a, glu_b, attn_out_g, ssm_out_g, w_out, norm2_g,
           router_g_w, router_g_b, router_e_w, router_e_b, w_gate, w_up, w_down, final_norm_g):
    if ada_w.shape[0] != 1:
        raise ValueError("single-layer trunk expected")
    nb, s_len, d = x_prompt.shape
    nd, t_len, _ = x_sample.shape
    if s_len != max(WINDOWS):
        raise ValueError("prompt length must equal the widest window")
    n_p, n_s = nb * s_len, nd * t_len
    if n_p % ROW_TILE or n_s % ROW_TILE:
        raise ValueError("token counts must be multiples of the row tile")

    mod = _mod_call(jnp.concatenate([c_prompt, c_sample], axis=0).astype(F32), ada_w[0], ada_b[0])
    mod_p = mod[:nb].reshape(nb, 1, 6 * d)
    mod_s = mod[nb:].reshape(1, nd, 6 * d)

    w_in_bf = w_in[0].astype(BF16)
    w_out_bf = w_out[0].astype(BF16)
    xp_rows = x_prompt.reshape(n_p, d)
    xs_rows = jnp.transpose(x_sample, (1, 0, 2)).reshape(n_s, d)

    qp, kp, vp, up, kp_t, vp_t = _inproj_call(xp_rows, mod_p, norm1_g[0], w_in_bf, seq_len=s_len)
    qs, ks, vs, us, ks_t, vs_t = _inproj_call(xs_rows, mod_s, norm1_g[0], w_in_bf, seq_len=nd)

    seq = lambda a: a.reshape(nb, s_len, ATTN_WIDTH)
    att_p = _attn_prompt_call(seq(qp), seq(kp), seq(vp), _prompt_bias_tiles(rel_bias))
    dec = lambda a: jnp.transpose(a.reshape(t_len, nd, ATTN_WIDTH), (1, 0, 2))
    att_s = _attn_decode_call(dec(qs), dec(ks), dec(vs), cache_k[0], cache_v[0], rel_bias)
    att_s = jnp.transpose(att_s, (1, 0, 2)).reshape(n_s, ATTN_WIDTH)

    s5 = (ssm_a_re[0], ssm_a_im[0], ssm_log_dt[0], ssm_b_re[0], ssm_b_im[0], ssm_c_re[0], ssm_c_im[0])
    t_op, m_op, p_op, sc = _s5_prompt_operators(*s5)
    d_oct = ssm_d[0].astype(F32).reshape(N_OCTETS, 1, LANES)
    ssm_p, hT_p = _ssm_prompt_call(seq(up), t_op, m_op, p_op, sc, d_oct,
                                   _octet_glu(glu_a[0]), _octet_glu(glu_b[0]))
    hT_p = hT_p.reshape(nb, N_OCTETS, 2, OCTET, SSM_STATE)
    ssm_re_p = hT_p[:, :, 0].reshape(nb, SSM_GROUPS, SSM_STATE)
    ssm_im_p = hT_p[:, :, 1].reshape(nb, SSM_GROUPS, SSM_STATE)
    ssm_s, hre_s, him_s = _ssm_decode_call(us.reshape(t_len, nd, SSM_WIDTH), state_ssm_re[0],
                                           state_ssm_im[0], *s5, ssm_d[0], glu_a[0], glu_b[0])
    ssm_s = ssm_s.reshape(n_s, SSM_WIDTH)

    wr, rb = _router_weights(router_g_w[0], router_g_b[0], router_e_w[0], router_e_b[0])
    norms = (attn_out_g[0], ssm_out_g[0], norm2_g[0], w_out_bf, wr, rb)
    x1_p, h2_p, route_p, ids_p = _outproj_call(att_p.reshape(n_p, ATTN_WIDTH), ssm_p.reshape(n_p, SSM_WIDTH),
                                        xp_rows, mod_p, *norms)
    ne = N_EXPERTS
    wg = w_gate[0].reshape(ne, d, D_EXPERT)
    wu = w_up[0].reshape(ne, d, D_EXPERT)
    wd = w_down[0].reshape(ne, D_EXPERT, d)

    def experts(x1, x_sorted, pos, tile_expert, route, mod_rows, tm):
        y_slots = _moe_call(tile_expert, x_sorted, wg, wu, wd, tm)
        return _final_call(x1, _take_rows(y_slots, pos[0]), _take_rows(y_slots, pos[1]),
                           route, mod_rows, final_norm_g)

    pos_p, tok_of_slot_p, tile_expert_p = _moe_dispatch(ids_p, MOE_TILE)
    x_sorted_p = _take_rows(h2_p, tok_of_slot_p)
    x_sorted_p, att_s = lax.optimization_barrier((x_sorted_p, att_s))
    y_p = experts(x1_p, x_sorted_p, pos_p, tile_expert_p, route_p, mod_p, MOE_TILE)

    x1_s, h2_s, route_s, ids_s = _outproj_call(att_s, ssm_s, xs_rows, mod_s, *norms)
    pos_s, tok_of_slot_s, tile_expert_s = _moe_dispatch(ids_s, MOE_TILE_DECODE)
    y_s = experts(x1_s, _take_rows(h2_s, tok_of_slot_s), pos_s, tile_expert_s, route_s, mod_s, MOE_TILE_DECODE)

    heads = (ATTN_HEADS, HEAD_DIM)
    cache_out = lambda a: jnp.transpose(a.reshape((1, nb) + heads + (s_len,)), (0, 1, 4, 2, 3))
    step_out = lambda a: jnp.transpose(a.reshape((1, t_len) + heads + (nd,)), (0, 4, 1, 2, 3))
    return (y_p.reshape(nb, s_len, d), jnp.transpose(y_s.reshape(t_len, nd, d), (1, 0, 2)),
            cache_out(kp_t), cache_out(vp_t), step_out(ks_t), step_out(vs_t),
            ssm_re_p[None], ssm_im_p[None],
            hre_s.reshape(1, nd, SSM_GROUPS, SSM_STATE), him_s.reshape(1, nd, SSM_GROUPS, SSM_STATE))
```

```python
import functools
import math

import numpy as np

import jax
import jax.numpy as jnp
from jax import lax
from jax.experimental import pallas as pl
from jax.experimental.pallas import tpu as pltpu

F32 = jnp.float32
BF16 = jnp.bfloat16

D_MODEL = 1024
HEAD_DIM = 64
ATTN_WIDTH = 512
ATTN_HEADS = 8
SSM_WIDTH = 512
SSM_GROUP_CH = 16
SSM_GROUPS = 32
SSM_STATE = 64
WINDOWS = (128, 512, 2048)
DILATIONS = (1, 4, 16)
WINDOW_STEPS = 128
N_BUCKETS = 32
MAX_EXACT = 16
BUCKET_MAX_DIST = 2048
N_EXPERT_GROUPS = 4
EXPERTS_PER_GROUP = 4
N_EXPERTS = N_EXPERT_GROUPS * EXPERTS_PER_GROUP
D_EXPERT = 512
NORM_EPS = 1e-6

LANES = 128
Q_ROWS = 16
DECODE_SEQS = 2
OCTET = LANES // SSM_GROUP_CH
N_OCTETS = SSM_GROUPS // OCTET
OCT_STATE = OCTET * SSM_STATE
SSM_CHUNK = 16
SSM_SEQS = 4
ROW_TILE = 512
WIDE_ROW_TILE = 1024
MOE_TILE = 512
MOE_TILE_DECODE = 128
ROUTE_ROWS = 8
ATTN_GROUP = 4
VMEM_LIMIT = 56 * 1024 * 1024

_NEG_INF = float("-inf")
_HIGHEST = lax.Precision.HIGHEST


def _cparams(sem):
    return pltpu.CompilerParams(dimension_semantics=sem, vmem_limit_bytes=VMEM_LIMIT)


def _rmsnorm(x, g):
    return x * lax.rsqrt(jnp.mean(x * x, axis=-1, keepdims=True) + NORM_EPS) * g


def _gelu_tanh(x):
    c = math.sqrt(2.0 / math.pi)
    return 0.5 * x * (1.0 + jnp.tanh(c * (x + 0.044715 * (x * x * x))))


def _sigmoid(x):
    return 1.0 / (1.0 + jnp.exp(-x))


def _mod_kernel(c_ref, w_ref, b_ref, o_ref):
    c = c_ref[...]
    a = (c * _sigmoid(c)).astype(BF16)
    o_ref[...] = jnp.dot(a, w_ref[...].astype(BF16), preferred_element_type=F32) + b_ref[...]


def _mod_call(c_all, ada_w, ada_b):
    rows, d = c_all.shape
    n_out = ada_w.shape[1]
    tn = 1024
    return pl.pallas_call(
        _mod_kernel,
        grid=(n_out // tn,),
        in_specs=[pl.BlockSpec((rows, d), lambda j: (0, 0)),
                  pl.BlockSpec((d, tn), lambda j: (0, j)),
                  pl.BlockSpec((1, tn), lambda j: (0, j))],
        out_specs=pl.BlockSpec((rows, tn), lambda j: (0, j)),
        out_shape=jax.ShapeDtypeStruct((rows, n_out), F32),
        compiler_params=_cparams(("arbitrary",)),
        name="adaln_mod",
    )(c_all, ada_w, ada_b.reshape(1, n_out))


def _mod_spec(mod, rows, ts, chunk):
    if mod.shape[1] == 1:
        tiles_per_group = (rows // mod.shape[0]) // ts
        return pl.BlockSpec((1, 1, D_MODEL), lambda i: (i // tiles_per_group, 0, chunk))
    tiles_per_period = mod.shape[1] // ts
    return pl.BlockSpec((1, ts, D_MODEL), lambda i: (0, i % tiles_per_period, chunk))


def _row_tile(tile, rows, mod):
    return min(tile, rows) if mod.shape[1] == 1 else min(tile, rows, mod.shape[1])


def _inproj_kernel(x_ref, sh_ref, sc_ref, g_ref, w_ref, *rest, key_major):
    h = _rmsnorm(x_ref[...], g_ref[...]) * (1.0 + sc_ref[0]) + sh_ref[0]
    hb = h.astype(BF16)
    z = jnp.dot(hb, w_ref[...], preferred_element_type=F32)
    aw = ATTN_WIDTH
    if key_major:
        q_ref, k_ref, v_ref, u_ref, kt_ref, vt_ref = rest
        kt_ref[0] = z[:, aw:2 * aw].T
        vt_ref[0] = z[:, 2 * aw:3 * aw].T
    else:
        q_ref, k_ref, v_ref, u_ref = rest
    q_ref[...] = z[:, :aw]
    k_ref[...] = z[:, aw:2 * aw]
    v_ref[...] = z[:, 2 * aw:3 * aw]
    u_ref[...] = z[:, 3 * aw:]


def _inproj_call(x_rows, mod, norm_g, w_in_bf, seq_len=None):
    rows, d = x_rows.shape
    ts = _row_tile(ROW_TILE, rows, mod)
    proj = w_in_bf.shape[1]
    out = jax.ShapeDtypeStruct((rows, ATTN_WIDTH), F32)
    ospec = pl.BlockSpec((ts, ATTN_WIDTH), lambda i: (i, 0))
    in_specs = [pl.BlockSpec((ts, d), lambda i: (i, 0)),
                _mod_spec(mod, rows, ts, 0),
                _mod_spec(mod, rows, ts, 1),
                pl.BlockSpec((1, d), lambda i: (0, 0)),
                pl.BlockSpec((d, proj), lambda i: (0, 0))]
    args = [x_rows, mod, mod, norm_g.reshape(1, d), w_in_bf]
    out_specs = [ospec, ospec, ospec, ospec]
    out_shape = [out, out, out, out]
    if seq_len is not None:
        tiles = seq_len // ts
        tspec = pl.BlockSpec((1, ATTN_WIDTH, ts), lambda i: (i // tiles, 0, i % tiles))
        out_specs += [tspec, tspec]
        out_shape += [jax.ShapeDtypeStruct((rows // seq_len, ATTN_WIDTH, seq_len), F32)] * 2
    return pl.pallas_call(
        functools.partial(_inproj_kernel, key_major=seq_len is not None),
        grid=(rows // ts,),
        in_specs=in_specs,
        out_specs=out_specs,
        out_shape=out_shape,
        compiler_params=_cparams(("arbitrary",)),
        name="inproj",
    )(*args)


def _t5_bucket(dist):
    d = jnp.maximum(dist, MAX_EXACT).astype(F32)
    log_part = MAX_EXACT + (jnp.log(d / MAX_EXACT) / math.log(BUCKET_MAX_DIST / MAX_EXACT)
                            * (N_BUCKETS - MAX_EXACT)).astype(jnp.int32)
    return jnp.where(dist < MAX_EXACT, dist, jnp.minimum(log_part, N_BUCKETS - 1))


def _bias_by_distance(rel_bias, dists):
    hot = (_t5_bucket(jnp.asarray(dists, jnp.int32))[:, None]
           == jnp.arange(N_BUCKETS, dtype=jnp.int32)[None, :]).astype(F32)
    return jnp.dot(hot, rel_bias.astype(F32), precision=_HIGHEST)


def _prompt_bias_tiles(rel_bias):
    steps = WINDOW_STEPS
    period = 3 * steps
    tiles = []
    for r in DILATIONS:
        vec = _bias_by_distance(rel_bias, r * np.arange(steps + 1))
        fill = jnp.full((steps - 1, ATTN_HEADS), _NEG_INF, F32)
        w = jnp.concatenate([fill, vec[::-1], fill, fill[:1]], axis=0)
        rep = jnp.tile(w.T, (1, steps))[:, :steps * (period - 1)]
        toe = rep.reshape(ATTN_HEADS, steps, period - 1)[:, :, steps - 1:]
        toe = toe.reshape(ATTN_HEADS // 2, 2, steps, 2 * steps)
        own_only = jnp.where(jnp.arange(2 * steps) < steps, _NEG_INF, toe)
        tiles.append(jnp.stack([toe, own_only], axis=2))
    return jnp.stack(tiles)


def _attn_prompt_kernel(q_ref, k_ref, v_ref, bias_ref, o_ref,
                        p4_ref, kb_ref, vb_ref, s_ref, p_ref, res_ref, stage_ref, nat_ref):
    s_len = q_ref.shape[1]
    steps = WINDOW_STEPS
    n_tiles = s_len // steps
    quarter = s_len // 4
    nt = (((1,), (1,)), ((), ()))
    lane = lax.broadcasted_iota(jnp.int32, (1, LANES), 1)
    first_head = lane < HEAD_DIM
    srcs = (q_ref, k_ref, v_ref)

    kb_ref[0:steps, :] = jnp.zeros((steps, LANES), BF16)
    vb_ref[:, 0:steps, 0:LANES] = jnp.zeros((2, steps, LANES), BF16)
    vb_ref[:, :, LANES:] = jnp.ones((2, s_len + steps, LANES), BF16)
    for x in range(3):
        for sigma in range(4):
            p4_ref[x, sigma * quarter:(sigma + 1) * quarter, :] = srcs[x][0, pl.ds(sigma, quarter, stride=4), :]

    def aligned(start):
        return start if isinstance(start, int) else pl.multiple_of(start, steps)

    def source(branch, x, tile_idx):
        if branch == 2:
            sigma, tau = tile_idx % 4, tile_idx // 4
            return p4_ref[x, pl.ds(sigma * quarter + tau, steps, stride=4), :]
        rows = pl.ds(aligned(tile_idx * steps), steps)
        return srcs[x][0, rows, :] if branch == 0 else p4_ref[x, rows, :]

    def stage_operands(branch):
        for t in range(n_tiles):
            kb_ref[steps + t * steps:steps + (t + 1) * steps, :] = source(branch, 1, t).astype(BF16)
            vb_ref[branch % 2, steps + t * steps:steps + (t + 1) * steps, 0:LANES] = (
                source(branch, 2, t).astype(BF16))

    stage_operands(0)
    for branch, r in enumerate(DILATIONS):
        blocks_per_class = (s_len // r) // steps
        width = 2 * steps if blocks_per_class > 1 else steps

        def tile_rows(t):
            return pl.ds(aligned(t * steps), steps)

        def key_rows(t, width=width):
            start = t * steps if width == 2 * steps else (t + 1) * steps
            return pl.ds(aligned(start), width)

        def scores(t, branch=branch, width=width, blocks_per_class=blocks_per_class, key_rows=key_rows):
            rows = tile_rows(t)
            keys = kb_ref[key_rows(t), :]
            first = jnp.where(t % blocks_per_class == 0, 1, 0)
            q2 = source(branch, 0, t) * (HEAD_DIM ** -0.5)
            for hh in range(2):
                if width == 2 * steps:
                    bias = bias_ref[branch, 0, hh, pl.ds(first, 1), :, :][0]
                else:
                    bias = bias_ref[branch, 0, hh, 0, :, steps:]
                qh = jnp.where(first_head == (hh == 0), q2, 0.0).astype(BF16)
                sc = lax.dot_general(qh, keys, nt, preferred_element_type=F32)
                s_ref[hh, rows, 0:width] = sc + bias

        out_ref = nat_ref.at[0] if branch == 0 else res_ref

        def softmax(t, width=width, out_ref=out_ref):
            rows = tile_rows(t)
            for hh in range(2):
                sc = s_ref[hh, rows, 0:width]
                m = jnp.max(sc, axis=1, keepdims=True)
                p_ref[hh, rows, 0:width] = jnp.exp(sc - m).astype(BF16)
                out_ref[1, rows, hh * HEAD_DIM:(hh + 1) * HEAD_DIM] = jnp.broadcast_to(m, (steps, HEAD_DIM))

        def weighted(t, branch=branch, width=width, key_rows=key_rows, out_ref=out_ref):
            rows = tile_rows(t)
            vals = vb_ref[branch % 2, key_rows(t), :]
            r0 = jnp.dot(p_ref[0, rows, 0:width], vals, preferred_element_type=F32)
            r1 = jnp.dot(p_ref[1, rows, 0:width], vals, preferred_element_type=F32)
            out_ref[0, rows, :] = jnp.where(first_head, r0[:, :LANES], r1[:, :LANES])
            out_ref[2, rows, :] = jnp.where(first_head, r0[:, LANES:], r1[:, LANES:])

        def stage(g_scores, g_softmax, scores=scores, softmax=softmax):
            for fn, g in ((softmax, g_softmax), (scores, g_scores)):
                if g is not None:
                    for j in range(ATTN_GROUP):
                        fn(g * ATTN_GROUP + j)

        n_groups = n_tiles // ATTN_GROUP
        stage(0, None)

        def steady(g, carry, stage=stage):
            stage(g, g - 1)
            return carry

        lax.fori_loop(1, n_groups, steady, 0)
        stage(None, n_groups - 1)

        def to_natural_order(branch=branch):
            for kind in range(3 if branch else 0):
                src = res_ref
                if branch == 2:
                    for t in range(n_tiles):
                        sigma, tau = t % 4, t // 4
                        stage_ref[kind, pl.ds(sigma * quarter + tau, steps, stride=4), :] = (
                            res_ref[kind, t * steps:(t + 1) * steps, :])
                    src = stage_ref
                for sigma in range(4):
                    nat_ref[branch, kind, pl.ds(sigma, quarter, stride=4), :] = (
                        src[kind, sigma * quarter:(sigma + 1) * quarter, :])

        @pl.when(pl.program_id(1) >= 0)
        def _(branch=branch, weighted=weighted, to_natural_order=to_natural_order):
            for t in range(n_tiles):
                weighted(t)
            to_natural_order()
            if branch + 1 < len(DILATIONS):
                stage_operands(branch + 1)

    def merge(i, carry):
        rows = pl.ds(pl.multiple_of(i * 256, 256), 256)
        m0, m1, m2 = nat_ref[0, 1, rows, :], nat_ref[1, 1, rows, :], nat_ref[2, 1, rows, :]
        m_all = jnp.maximum(jnp.maximum(m0, m1), m2)
        w0, w1, w2 = jnp.exp(m0 - m_all), jnp.exp(m1 - m_all), jnp.exp(m2 - m_all)
        num = w0 * nat_ref[0, 0, rows, :] + w1 * nat_ref[1, 0, rows, :] + w2 * nat_ref[2, 0, rows, :]
        den = w0 * nat_ref[0, 2, rows, :] + w1 * nat_ref[1, 2, rows, :] + w2 * nat_ref[2, 2, rows, :]
        o_ref[0, rows, :] = num / den
        return carry

    lax.fori_loop(0, s_len // 256, merge, 0)


def _attn_prompt_call(q, k, v, bias_tiles):
    n, s, _ = q.shape
    pairs = ATTN_HEADS // 2
    steps = WINDOW_STEPS
    qspec = pl.BlockSpec((1, s, LANES), lambda g, n_: (n_, 0, g))
    return pl.pallas_call(
        _attn_prompt_kernel,
        grid=(pairs, n),
        in_specs=[qspec, qspec, qspec,
                  pl.BlockSpec((3, 1, 2, 2, steps, 2 * steps), lambda g, n_: (0, g, 0, 0, 0, 0))],
        out_specs=qspec,
        out_shape=jax.ShapeDtypeStruct((n, s, ATTN_WIDTH), F32),
        scratch_shapes=[pltpu.VMEM((3, s, LANES), F32),
                        pltpu.VMEM((s + steps, LANES), BF16),
                        pltpu.VMEM((2, s + steps, 2 * LANES), BF16),
                        pltpu.VMEM((2, s, 2 * steps), F32),
                        pltpu.VMEM((2, s, 2 * steps), BF16),
                        pltpu.VMEM((3, s, LANES), F32), pltpu.VMEM((3, s, LANES), F32),
                        pltpu.VMEM((3, 3, s, LANES), F32)],
        compiler_params=_cparams(("arbitrary", "arbitrary")),
        name="attn_prompt",
    )(q, k, v, bias_tiles)


def _s5_discretise(a_re, a_im, log_dt, b_re, b_im):
    lam_re = jnp.minimum(a_re.astype(F32), -1e-4)
    lam_im = a_im.astype(F32)
    dt = jnp.exp(log_dt.astype(F32))[:, None]
    mag = jnp.exp(lam_re * dt)
    ph = lam_im * dt
    abar_re, abar_im = mag * jnp.cos(ph), mag * jnp.sin(ph)
    nr, ni = abar_re - 1.0, abar_im
    den = lam_re * lam_re + lam_im * lam_im
    coef_re = (nr * lam_re + ni * lam_im) / den
    coef_im = (ni * lam_re - nr * lam_im) / den
    br, bi = b_re.astype(F32), b_im.astype(F32)
    bbar_re = coef_re[..., None] * br - coef_im[..., None] * bi
    bbar_im = coef_re[..., None] * bi + coef_im[..., None] * br
    return lam_re * dt, ph, abar_re, abar_im, bbar_re, bbar_im


def _abar_power(log_mag, ph, n):
    nf = jnp.asarray(n, F32)[:, None, None]
    mag = jnp.exp(nf * log_mag[None])
    return mag * jnp.cos(nf * ph[None]), mag * jnp.sin(nf * ph[None])


def _s5_prompt_operators(a_re, a_im, log_dt, b_re, b_im, c_re, c_im):
    L = SSM_CHUNK
    log_mag, ph, _, _, bb_re, bb_im = _s5_discretise(a_re, a_im, log_dt, b_re, b_im)
    cr, ci = c_re.astype(F32), c_im.astype(F32)
    pw_re, pw_im = _abar_power(log_mag, ph, np.arange(L + 1))
    eye = jnp.eye(OCTET, dtype=F32)

    ab_re = pw_re[:L, :, :, None] * bb_re[None] - pw_im[:L, :, :, None] * bb_im[None]
    ab_im = pw_re[:L, :, :, None] * bb_im[None] + pw_im[:L, :, :, None] * bb_re[None]
    lag = (jnp.einsum('gop,lgpi->lgoi', cr, ab_re, precision=_HIGHEST)
           - jnp.einsum('gop,lgpi->lgoi', ci, ab_im, precision=_HIGHEST))
    lag = lag.reshape(L, N_OCTETS, OCTET, SSM_GROUP_CH, SSM_GROUP_CH)
    bd = jnp.einsum('logci,gh->olgihc', lag, eye).reshape(N_OCTETS, L, LANES, LANES).astype(BF16)
    stack = bd[:, ::-1].reshape(N_OCTETS, L * LANES, LANES)
    shifted = jnp.concatenate([stack[:, LANES:], jnp.zeros((N_OCTETS, LANES, LANES), BF16)], axis=1)
    t_op = jnp.concatenate([shifted, stack], axis=-1)

    group_of_lane = jnp.arange(LANES) // SSM_GROUP_CH
    parts = []
    for part in (ab_re[::-1], ab_im[::-1]):
        x = part.reshape(L, N_OCTETS, OCTET, SSM_STATE, SSM_GROUP_CH)
        parts.append(jnp.transpose(x, (1, 0, 3, 2, 4)).reshape(N_OCTETS, L, SSM_STATE, LANES))
    x = jnp.stack(parts, axis=2)
    own = (jnp.arange(OCTET)[:, None, None] == group_of_lane[None, None, :])
    mt_op = jnp.where(own, x[:, :, :, None], 0.0)
    mt_op = mt_op.reshape(N_OCTETS, L, 2 * OCT_STATE, LANES)

    p1_re, p1_im = pw_re[1:], pw_im[1:]
    on_re = cr[None] * p1_re[:, :, None, :] - ci[None] * p1_im[:, :, None, :]
    on_im = -cr[None] * p1_im[:, :, None, :] - ci[None] * p1_re[:, :, None, :]
    parts = []
    for part in (on_re, on_im):
        y = part.reshape(L, N_OCTETS, OCTET, SSM_GROUP_CH, SSM_STATE)
        parts.append(jnp.transpose(y, (1, 0, 3, 2, 4)).reshape(N_OCTETS, L, SSM_GROUP_CH, OCT_STATE))
    y = jnp.concatenate(parts, axis=-1)
    group_of_state = (jnp.arange(2 * OCT_STATE) % OCT_STATE) // SSM_STATE
    own = (jnp.arange(OCTET)[:, None, None] == group_of_state[None, None, :])
    pt_op = jnp.where(own, y[:, :, None], 0.0)
    pt_op = pt_op.reshape(N_OCTETS, L * LANES, 2 * OCT_STATE)

    n_steps = 8
    sc_re, sc_im = _abar_power(log_mag, ph, L * (2 ** np.arange(n_steps)))
    sc = jnp.concatenate([sc_re.reshape(n_steps, N_OCTETS, OCT_STATE),
                          sc_im.reshape(n_steps, N_OCTETS, OCT_STATE)], axis=-1)
    sc = jnp.transpose(sc, (1, 0, 2))
    return t_op, mt_op.astype(BF16), pt_op.astype(BF16), sc


def _octet_glu(glu):
    eye = jnp.eye(OCTET, dtype=F32)
    x = glu.astype(F32).reshape(N_OCTETS, OCTET, SSM_GROUP_CH, SSM_GROUP_CH)
    return jnp.einsum('ogce,gh->ogche', x, eye).reshape(N_OCTETS, LANES, LANES).astype(BF16)


def _ssm_prompt_kernel(u_ref, t_ref, mb_ref, pt_ref, sc_ref, d_ref, ga_ref, gb_ref,
                       y_ref, h_ref, uf_ref, ub_ref, st_ref, mt_ref):
    L = SSM_CHUNK
    nt = (((1,), (1,)), ((), ()))

    @pl.when(pl.program_id(1) == 0)
    def _():
        for s in range(L):
            mt_ref[:, s * LANES:(s + 1) * LANES] = mb_ref[0, s]

    n_seq, s_len = u_ref.shape[0], u_ref.shape[1]
    n_chunks = s_len // L
    quarter = s_len // 4
    rows = n_seq * n_chunks

    def staged(sq, step):
        sigma, tau = step % 4, step // 4
        return pl.ds(sq * s_len + sigma * quarter + tau, n_chunks, stride=4)

    for sq in range(n_seq):
        for sigma in range(4):
            st_ref[sq * s_len + sigma * quarter:sq * s_len + (sigma + 1) * quarter, :] = (
                u_ref[sq, pl.ds(sigma, quarter, stride=4), :])
        for step in range(L):
            blk = st_ref[staged(sq, step), :]
            uf_ref[sq * n_chunks:(sq + 1) * n_chunks, step * LANES:(step + 1) * LANES] = blk
            ub_ref[sq * n_chunks:(sq + 1) * n_chunks, step * LANES:(step + 1) * LANES] = blk.astype(BF16)
    ub = ub_ref[...]

    x = lax.dot_general(ub, mt_ref[...], nt, preferred_element_type=F32)
    chunk = lax.broadcasted_iota(jnp.int32, (rows, 1), 0) % n_chunks
    half = OCT_STATE
    k = 1
    step = 0
    while k < n_chunks:
        a_re = sc_ref[0, step:step + 1, :half]
        a_im = sc_ref[0, step:step + 1, half:]
        sh = jnp.where(chunk >= k, pltpu.roll(x, k, axis=0), 0.0)
        s_re, s_im = sh[:, :half], sh[:, half:]
        x = x + jnp.concatenate([a_re * s_re - a_im * s_im, a_re * s_im + a_im * s_re], axis=1)
        k *= 2
        step += 1
    for sq in range(n_seq):
        h_ref[sq, 0] = x[(sq + 1) * n_chunks - 1:(sq + 1) * n_chunks, :]
    h_start = jnp.where(chunk >= 1, pltpu.roll(x, 1, axis=0), 0.0)

    hb = h_start.astype(BF16)
    d = d_ref[0]
    ga = ga_ref[0]
    gb = gb_ref[0]
    for t in range(0, L, 2):
        pair = slice(t * LANES, (t + 2) * LANES)
        y2 = (jnp.dot(ub_ref[:, :(t + 2) * LANES], t_ref[0, (L - 2 - t) * LANES:, :],
                      preferred_element_type=F32)
              + lax.dot_general(hb, pt_ref[0, pair, :], nt, preferred_element_type=F32))
        for j in range(2):
            lanes = slice((t + j) * LANES, (t + j + 1) * LANES)
            g = _gelu_tanh(y2[:, j * LANES:(j + 1) * LANES] + d * uf_ref[:, lanes]).astype(BF16)
            out = (jnp.dot(g, ga, preferred_element_type=F32)
                   * _sigmoid(jnp.dot(g, gb, preferred_element_type=F32)))
            for sq in range(n_seq):
                st_ref[staged(sq, t + j), :] = out[sq * n_chunks:(sq + 1) * n_chunks, :]
    for sq in range(n_seq):
        for sigma in range(4):
            y_ref[sq, pl.ds(sigma, quarter, stride=4), :] = (
                st_ref[sq * s_len + sigma * quarter:sq * s_len + (sigma + 1) * quarter, :])


def _ssm_prompt_call(u, t_op, m_op, p_op, sc, d_oct, ga, gb):
    n, s, _ = u.shape
    L = SSM_CHUNK
    nq = math.gcd(SSM_SEQS, n)
    rows = nq * (s // L)
    wide = L * LANES
    wspec = lambda shape: pl.BlockSpec((1,) + shape, lambda o, n_: (o, 0, 0))
    return pl.pallas_call(
        _ssm_prompt_kernel,
        grid=(N_OCTETS, n // nq),
        in_specs=[pl.BlockSpec((nq, s, LANES), lambda o, n_: (n_, 0, o)),
                  wspec((wide, 2 * LANES)),
                  pl.BlockSpec((1, L, 2 * OCT_STATE, LANES), lambda o, n_: (o, 0, 0, 0)),
                  wspec((wide, 2 * OCT_STATE)),
                  wspec((8, 2 * OCT_STATE)), wspec((1, LANES)),
                  wspec((LANES, LANES)), wspec((LANES, LANES))],
        out_specs=[pl.BlockSpec((nq, s, LANES), lambda o, n_: (n_, 0, o)),
                   pl.BlockSpec((nq, 1, 1, 2 * OCT_STATE), lambda o, n_: (n_, o, 0, 0))],
        out_shape=[jax.ShapeDtypeStruct((n, s, SSM_WIDTH), F32),
                   jax.ShapeDtypeStruct((n, N_OCTETS, 1, 2 * OCT_STATE), F32)],
        scratch_shapes=[pltpu.VMEM((rows, wide), F32),
                        pltpu.VMEM((rows, wide), BF16),
                        pltpu.VMEM((nq * s, LANES), F32),
                        pltpu.VMEM((2 * OCT_STATE, wide), BF16)],
        compiler_params=_cparams(("arbitrary", "arbitrary")),
        name="ssm_prompt",
    )(u, t_op, m_op, p_op, sc, d_oct, ga, gb)


def _attn_decode_kernel(q_ref, kn_ref, vn_ref, kt_ref, vt_ref, b_ref, mult_ref, o_ref):
    nt = (((1,), (1,)), ((), ()))
    mult = mult_ref[...]
    for sq in range(q_ref.shape[0]):
        for h in range(ATTN_HEADS):
            q = q_ref[sq, h]
            kt = jnp.concatenate([kt_ref[sq, h].astype(BF16), kn_ref[sq]], axis=1)
            vt = jnp.concatenate([vt_ref[sq, h].astype(BF16), vn_ref[sq]], axis=1)
            s = jnp.dot(q, kt, preferred_element_type=F32) + b_ref[h]
            m = jnp.max(s, axis=1, keepdims=True)
            p = jnp.exp(s - m) * mult
            den = jnp.sum(p, axis=1, keepdims=True)
            o = lax.dot_general(p.astype(BF16), vt, nt, preferred_element_type=F32)
            o_ref[sq, h] = o / den


def _decode_tables(rel_bias, t_len, w_rows):
    slot = LANES // ATTN_HEADS
    t = np.arange(t_len)[:, None]
    new_t = np.arange(LANES)[None, :] % slot
    dist = np.concatenate([w_rows + t - np.arange(w_rows)[None, :],
                           t - new_t], axis=1)
    mult = np.zeros(dist.shape, np.float32)
    for w, r in zip(WINDOWS, DILATIONS):
        mult += (dist >= 0) & (dist % r == 0) & (dist <= w)
    mult = np.concatenate([mult, np.zeros((Q_ROWS - t_len, dist.shape[1]), np.float32)], axis=0)
    mult[t_len:, 0] = 1.0
    by_dist = _bias_by_distance(rel_bias, np.arange(w_rows + t_len))
    rows = []
    for ti in range(t_len):
        cache_part = by_dist[ti + 1:w_rows + ti + 1][::-1]
        new_part = jnp.concatenate([by_dist[:ti + 1][::-1],
                                    jnp.zeros((slot - ti - 1, ATTN_HEADS), F32)], axis=0)
        rows.append(jnp.concatenate([cache_part] + [new_part] * ATTN_HEADS, axis=0))
    bias = jnp.stack(rows + [jnp.zeros_like(rows[0])] * (Q_ROWS - t_len), axis=0)
    bias = jnp.transpose(bias, (2, 0, 1))
    own_head = np.concatenate([np.ones((ATTN_HEADS, w_rows), bool),
                               (np.arange(LANES) // slot)[None, :] == np.arange(ATTN_HEADS)[:, None]], axis=1)
    bias = jnp.where(jnp.asarray((mult[None] > 0) & own_head[:, None, :]), bias, _NEG_INF)
    return bias, jnp.asarray(mult)


def _attn_decode_call(q, k_new, v_new, cache_k, cache_v, rel_bias):
    n, t_len, w = q.shape
    w_rows = cache_k.shape[1]
    if t_len > min(DILATIONS[1:]) or t_len > Q_ROWS or w_rows < max(WINDOWS):
        raise ValueError("unsupported decode shape")
    heads = (ATTN_HEADS, HEAD_DIM)

    def head_major(a, pad_to):
        a = jnp.transpose(a.reshape((n, t_len) + heads), (0, 2, 1, 3))
        return jnp.pad(a, ((0, 0), (0, 0), (0, pad_to - t_len), (0, 0)))

    def new_tile(a):
        a = jnp.transpose(a.reshape((n, t_len) + heads), (0, 3, 2, 1))
        a = jnp.pad(a, ((0, 0), (0, 0), (0, 0), (0, LANES // ATTN_HEADS - t_len)))
        return a.reshape(n, HEAD_DIM, LANES).astype(BF16)

    qh = head_major(q * (HEAD_DIM ** -0.5), Q_ROWS).astype(BF16)
    knt, vnt = new_tile(k_new), new_tile(v_new)
    kt = jnp.transpose(cache_k.astype(F32), (0, 2, 3, 1))
    vt = jnp.transpose(cache_v.astype(F32), (0, 2, 3, 1))
    bias, mult = _decode_tables(rel_bias, t_len, w_rows)
    keys = w_rows + LANES
    nq = math.gcd(DECODE_SEQS, n)
    per_seq = lambda shape: pl.BlockSpec((nq,) + shape, lambda i: (i, 0, 0, 0))
    out = pl.pallas_call(
        _attn_decode_kernel,
        grid=(n // nq,),
        in_specs=[per_seq((ATTN_HEADS, Q_ROWS, HEAD_DIM)),
                  pl.BlockSpec((nq, HEAD_DIM, LANES), lambda i: (i, 0, 0)),
                  pl.BlockSpec((nq, HEAD_DIM, LANES), lambda i: (i, 0, 0)),
                  per_seq((ATTN_HEADS, HEAD_DIM, w_rows)), per_seq((ATTN_HEADS, HEAD_DIM, w_rows)),
                  pl.BlockSpec((ATTN_HEADS, Q_ROWS, keys), lambda i: (0, 0, 0)),
                  pl.BlockSpec((Q_ROWS, keys), lambda i: (0, 0))],
        out_specs=per_seq((ATTN_HEADS, Q_ROWS, HEAD_DIM)),
        out_shape=jax.ShapeDtypeStruct((n, ATTN_HEADS, Q_ROWS, HEAD_DIM), F32),
        compiler_params=_cparams(("arbitrary",)),
        name="attn_decode",
    )(qh, knt, vnt, kt, vt, bias, mult)
    return jnp.transpose(out[:, :, :t_len], (0, 2, 1, 3)).reshape(n, t_len, w)


def _ssm_decode_kernel(u_ref, hre_ref, him_ref, are_ref, aim_ref, bre_ref, bim_ref,
                       cre_ref, cim_ref, d_ref, ga_ref, gb_ref, y_ref, ore_ref, oim_ref, *, t_len):
    h_re, h_im = hre_ref[...], him_ref[...]
    a_re, a_im = are_ref[...], aim_ref[...]
    for t in range(t_len):
        u = u_ref[t]
        ub = u.astype(BF16)
        n_re = a_re * h_re - a_im * h_im + jnp.dot(ub, bre_ref[...], preferred_element_type=F32)
        n_im = a_re * h_im + a_im * h_re + jnp.dot(ub, bim_ref[...], preferred_element_type=F32)
        h_re, h_im = n_re, n_im
        y = (jnp.dot(h_re.astype(BF16), cre_ref[...], preferred_element_type=F32)
             - jnp.dot(h_im.astype(BF16), cim_ref[...], preferred_element_type=F32)
             + d_ref[...] * u)
        g = _gelu_tanh(y).astype(BF16)
        y_ref[t] = (jnp.dot(g, ga_ref[...], preferred_element_type=F32)
                    * _sigmoid(jnp.dot(g, gb_ref[...], preferred_element_type=F32)))
    ore_ref[...] = h_re
    oim_ref[...] = h_im


def _group_blockdiag(x):
    g, a, b = x.shape
    panel = jnp.transpose(x, (1, 0, 2)).reshape(a, g * b)
    own = jnp.arange(g)[:, None, None] == (jnp.arange(g * b) // b)[None, None, :]
    return jnp.where(own, panel[None], 0).reshape(g * a, g * b)


def _ssm_decode_call(u_tm, h0_re, h0_im, a_re, a_im, log_dt, b_re, b_im, c_re, c_im,
                     d_skip, glu_a, glu_b):
    n = h0_re.shape[0]
    _, _, abar_re, abar_im, bb_re, bb_im = _s5_discretise(a_re, a_im, log_dt, b_re, b_im)
    state = SSM_GROUPS * SSM_STATE
    t_len = u_tm.shape[0]
    args = (u_tm, h0_re.reshape(n, state).astype(F32), h0_im.reshape(n, state).astype(F32),
            abar_re.reshape(1, state), abar_im.reshape(1, state),
            _group_blockdiag(jnp.transpose(bb_re, (0, 2, 1))).astype(BF16),
            _group_blockdiag(jnp.transpose(bb_im, (0, 2, 1))).astype(BF16),
            _group_blockdiag(jnp.transpose(c_re.astype(F32), (0, 2, 1))).astype(BF16),
            _group_blockdiag(jnp.transpose(c_im.astype(F32), (0, 2, 1))).astype(BF16),
            d_skip.astype(F32).reshape(1, SSM_WIDTH),
            _group_blockdiag(glu_a.astype(F32)).astype(BF16),
            _group_blockdiag(glu_b.astype(F32)).astype(BF16))
    full = lambda a: pl.BlockSpec(a.shape, lambda i: (0,) * a.ndim)
    out_shape = [jax.ShapeDtypeStruct(u_tm.shape, F32),
                 jax.ShapeDtypeStruct((n, state), F32), jax.ShapeDtypeStruct((n, state), F32)]
    return pl.pallas_call(
        functools.partial(_ssm_decode_kernel, t_len=t_len),
        grid=(1,),
        in_specs=[full(a) for a in args],
        out_specs=[full(o) for o in out_shape],
        out_shape=out_shape,
        compiler_params=_cparams(("arbitrary",)),
        name="ssm_decode",
    )(*args)


def _outproj_kernel(att_ref, ssm_ref, x_ref, g1_ref, sh2_ref, sc2_ref, ag_ref, sg_ref, n2_ref,
                    wo_ref, wr_ref, rb_ref, x1_ref, h2_ref, route_ref, ids_ref):
    mixed = jnp.concatenate([_rmsnorm(att_ref[...], ag_ref[...]), _rmsnorm(ssm_ref[...], sg_ref[...])],
                            axis=1).astype(BF16)
    x1 = x_ref[...] + g1_ref[0] * jnp.dot(mixed, wo_ref[...], preferred_element_type=F32)
    x1_ref[...] = x1
    h2 = _rmsnorm(x1, n2_ref[...]) * (1.0 + sc2_ref[0]) + sh2_ref[0]
    hi = h2.astype(BF16)
    h2_ref[...] = hi
    lo = (h2 - hi.astype(F32)).astype(BF16)
    r1 = jnp.dot(hi, wr_ref[...], preferred_element_type=F32)
    r2 = jnp.dot(lo, wr_ref[:, :LANES], preferred_element_type=F32)
    logits = r1[:, :LANES] + r1[:, LANES:] + r2 + rb_ref[...]

    lane = lax.broadcasted_iota(jnp.int32, (1, LANES), 1)
    lane_f = lane.astype(F32)
    big = float(LANES)
    ng, epg = N_EXPERT_GROUPS, EXPERTS_PER_GROUP
    lg = jnp.where(lane < ng, logits, _NEG_INF)
    gmax = jnp.max(lg, axis=1, keepdims=True)
    p_star = 1.0 / jnp.sum(jnp.exp(lg - gmax), axis=1, keepdims=True)
    g_star = jnp.min(jnp.where(lg == gmax, lane_f, big), axis=1, keepdims=True)
    in_group = ((lane >= ng) & (lane < ng + ng * epg)
                & (lax.shift_right_arithmetic(lane - ng, int(math.log2(epg))).astype(F32) == g_star))
    le = jnp.where(in_group, logits, _NEG_INF)
    v1 = jnp.max(le, axis=1, keepdims=True)
    i1 = jnp.min(jnp.where(le == v1, lane_f, big), axis=1, keepdims=True)
    le2 = jnp.where(lane_f == i1, _NEG_INF, le)
    v2 = jnp.max(le2, axis=1, keepdims=True)
    i2 = jnp.min(jnp.where(le2 == v2, lane_f, big), axis=1, keepdims=True)
    e2 = jnp.exp(v2 - v1)
    w1 = p_star / (1.0 + e2)
    w2 = p_star * e2 / (1.0 + e2)
    route = jnp.where(lane == 0, i1 - ng,
                      jnp.where(lane == 1, i2 - ng,
                                jnp.where(lane == 2, w1, jnp.where(lane == 3, w2, 0.0))))
    route_ref[...] = route
    ids_ref[...] = route.T[:ids_ref.shape[0], :]


def _outproj_call(att, ssm_y, x_rows, mod, attn_g, ssm_g, norm2_g, w_out_bf, wr, rb):
    rows, d = x_rows.shape
    ts = _row_tile(WIDE_ROW_TILE, rows, mod)
    row = lambda width: pl.BlockSpec((ts, width), lambda i: (i, 0))
    const = lambda a: pl.BlockSpec(a.shape, lambda i: (0,) * a.ndim)
    attn_g = attn_g.reshape(1, ATTN_WIDTH)
    ssm_g = ssm_g.reshape(1, SSM_WIDTH)
    norm2_g = norm2_g.reshape(1, d)
    return pl.pallas_call(
        _outproj_kernel,
        grid=(rows // ts,),
        in_specs=[row(ATTN_WIDTH), row(SSM_WIDTH), row(d),
                  _mod_spec(mod, rows, ts, 2), _mod_spec(mod, rows, ts, 3), _mod_spec(mod, rows, ts, 4),
                  const(attn_g), const(ssm_g), const(norm2_g), const(w_out_bf), const(wr), const(rb)],
        out_specs=[row(d), row(d), row(LANES), pl.BlockSpec((ROUTE_ROWS, ts), lambda i: (0, i))],
        out_shape=[jax.ShapeDtypeStruct((rows, d), F32), jax.ShapeDtypeStruct((rows, d), BF16),
                   jax.ShapeDtypeStruct((rows, LANES), F32),
                   jax.ShapeDtypeStruct((ROUTE_ROWS, rows), F32)],
        compiler_params=_cparams(("arbitrary",)),
        name="outproj_router",
    )(att, ssm_y, x_rows, mod, mod, mod, attn_g, ssm_g, norm2_g, w_out_bf, wr, rb)


def _router_weights(router_g_w, router_g_b, router_e_w, router_e_b):
    d = router_g_w.shape[0]
    ne = N_EXPERT_GROUPS * EXPERTS_PER_GROUP
    w = jnp.concatenate([router_g_w.astype(F32),
                         jnp.transpose(router_e_w.astype(F32), (1, 0, 2)).reshape(d, ne)], axis=1)
    w = jnp.pad(w, ((0, 0), (0, LANES - w.shape[1])))
    hi = w.astype(BF16)
    lo = (w - hi.astype(F32)).astype(BF16)
    b = jnp.concatenate([router_g_b.astype(F32), router_e_b.astype(F32).reshape(ne)])
    b = jnp.pad(b, (0, LANES - b.shape[0])).reshape(1, LANES)
    return jnp.concatenate([hi, lo], axis=1), b


def _moe_kernel(te_ref, x_ref, wg_ref, wu_ref, wd_ref, o_ref, wgb_ref, wub_ref, wdb_ref):
    i = pl.program_id(0)
    changed = jnp.logical_or(i == 0, te_ref[i] != te_ref[jnp.maximum(i - 1, 0)])

    @pl.when(changed)
    def _():
        wgb_ref[...] = wg_ref[0].astype(BF16)
        wub_ref[...] = wu_ref[0].astype(BF16)
        wdb_ref[...] = wd_ref[0].astype(BF16)

    @pl.when(i < te_ref[pl.num_programs(0)])
    def _():
        x = x_ref[...]
        gate = jnp.dot(x, wgb_ref[...], preferred_element_type=F32)
        up = jnp.dot(x, wub_ref[...], preferred_element_type=F32)
        a = (gate * _sigmoid(gate)) * up
        o_ref[...] = jnp.dot(a.astype(BF16), wdb_ref[...], preferred_element_type=F32).astype(o_ref.dtype)


def _moe_call(tile_expert, x_sorted, wg, wu, wd, tm):
    n_slots, d = x_sorted.shape
    fe = wg.shape[2]
    grid_spec = pltpu.PrefetchScalarGridSpec(
        num_scalar_prefetch=1,
        grid=(n_slots // tm,),
        in_specs=[pl.BlockSpec((tm, d), lambda i, te: (i, 0)),
                  pl.BlockSpec((1, d, fe), lambda i, te: (te[i], 0, 0)),
                  pl.BlockSpec((1, d, fe), lambda i, te: (te[i], 0, 0)),
                  pl.BlockSpec((1, fe, d), lambda i, te: (te[i], 0, 0))],
        out_specs=pl.BlockSpec((tm, d), lambda i, te: (i, 0)),
        scratch_shapes=[pltpu.VMEM((d, fe), BF16), pltpu.VMEM((d, fe), BF16), pltpu.VMEM((fe, d), BF16)],
    )
    return pl.pallas_call(
        _moe_kernel,
        grid_spec=grid_spec,
        out_shape=jax.ShapeDtypeStruct((n_slots, d), BF16),
        compiler_params=_cparams(("arbitrary",)),
        name="moe_experts",
    )(tile_expert, x_sorted, wg, wu, wd)


def _moe_dispatch(ids_rows, tm):
    n_tok = ids_rows.shape[1]
    ids = jnp.concatenate([ids_rows[0], ids_rows[1]]).astype(jnp.int32)
    n_pairs = ids.shape[0]
    n_slots = (-(-n_pairs // tm) + N_EXPERTS) * tm
    hot = (ids[:, None] == jnp.arange(N_EXPERTS)[None, :]).astype(jnp.int32)
    csum = jnp.cumsum(hot, axis=0)
    rank = jnp.sum((csum - hot) * hot, axis=1)
    counts = csum[-1]
    padded = -(-counts // tm) * tm
    ends = jnp.cumsum(padded)
    starts = ends - padded
    pos = jnp.sum(hot * starts[None, :], axis=1) + rank
    tok_of_slot = (jnp.arange(n_slots, dtype=jnp.int32) % n_tok).at[pos].set(
        jnp.arange(n_pairs, dtype=jnp.int32) % n_tok, unique_indices=True, mode="promise_in_bounds")
    tile_start = jnp.arange(n_slots // tm, dtype=jnp.int32) * tm
    tile_expert = jnp.minimum(jnp.sum((tile_start[:, None] >= ends[None, :]).astype(jnp.int32), axis=1),
                              N_EXPERTS - 1).astype(jnp.int32)
    tile_expert = jnp.concatenate([tile_expert, (ends[-1:] // tm).astype(jnp.int32)])
    return pos.reshape(2, n_tok), tok_of_slot, tile_expert


def _take_rows(x, idx):
    return x.at[idx].get(mode="promise_in_bounds")


def _final_kernel(x1_ref, ya_ref, yb_ref, route_ref, g2_ref, fg_ref, o_ref):
    wa = route_ref[:, 2:3]
    wb = route_ref[:, 3:4]
    x = x1_ref[...] + g2_ref[0] * (wa * ya_ref[...].astype(F32) + wb * yb_ref[...].astype(F32))
    o_ref[...] = _rmsnorm(x, fg_ref[...])


def _final_call(x1, ya, yb, route, mod, final_g):
    rows, d = x1.shape
    ts = _row_tile(WIDE_ROW_TILE, rows, mod)
    row = pl.BlockSpec((ts, d), lambda i: (i, 0))
    return pl.pallas_call(
        _final_kernel,
        grid=(rows // ts,),
        in_specs=[row, row, row, pl.BlockSpec((ts, LANES), lambda i: (i, 0)), _mod_spec(mod, rows, ts, 5),
                  pl.BlockSpec((1, d), lambda i: (0, 0))],
        out_specs=row,
        out_shape=jax.ShapeDtypeStruct((rows, d), F32),
        compiler_params=_cparams(("arbitrary",)),
        name="final_norm",
    )(x1, ya, yb, route, mod, final_g.reshape(1, d))


def kernel(x_prompt, x_sample, c_prompt, c_sample, cache_k, cache_v, state_ssm_re, state_ssm_im,
           rel_bias, ada_w, ada_b, norm1_g, w_in, ssm_a_re, ssm_a_im, ssm_log_dt, ssm_b_re, ssm_b_im,
           ssm_c_re, ssm_c_im, ssm_d, glu_a, glu_b, attn_out_g, ssm_out_g, w_out, norm2_g,
           router_g_w, router_g_b, router_e_w, router_e_b, w_gate, w_up, w_down, final_norm_g):
    if ada_w.shape[0] != 1:
        raise ValueError("single-layer trunk expected")
    nb, s_len, d = x_prompt.shape
    nd, t_len, _ = x_sample.shape
    if s_len != max(WINDOWS):
        raise ValueError("prompt length must equal the widest window")
    n_p, n_s = nb * s_len, nd * t_len
    if n_p % ROW_TILE or n_s % ROW_TILE:
        raise ValueError("token counts must be multiples of the row tile")

    mod = _mod_call(jnp.concatenate([c_prompt, c_sample], axis=0).astype(F32), ada_w[0], ada_b[0])
    mod_p = mod[:nb].reshape(nb, 1, 6 * d)
    mod_s = mod[nb:].reshape(1, nd, 6 * d)

    w_in_bf = w_in[0].astype(BF16)
    w_out_bf = w_out[0].astype(BF16)
    xp_rows = x_prompt.reshape(n_p, d)
    xs_rows = jnp.transpose(x_sample, (1, 0, 2)).reshape(n_s, d)

    qp, kp, vp, up, kp_t, vp_t = _inproj_call(xp_rows, mod_p, norm1_g[0], w_in_bf, seq_len=s_len)
    qs, ks, vs, us, ks_t, vs_t = _inproj_call(xs_rows, mod_s, norm1_g[0], w_in_bf, seq_len=nd)

    seq = lambda a: a.reshape(nb, s_len, ATTN_WIDTH)
    att_p = _attn_prompt_call(seq(qp), seq(kp), seq(vp), _prompt_bias_tiles(rel_bias))
    dec = lambda a: jnp.transpose(a.reshape(t_len, nd, ATTN_WIDTH), (1, 0, 2))
    att_s = _attn_decode_call(dec(qs), dec(ks), dec(vs), cache_k[0], cache_v[0], rel_bias)
    att_s = jnp.transpose(att_s, (1, 0, 2)).reshape(n_s, ATTN_WIDTH)

    s5 = (ssm_a_re[0], ssm_a_im[0], ssm_log_dt[0], ssm_b_re[0], ssm_b_im[0], ssm_c_re[0], ssm_c_im[0])
    t_op, m_op, p_op, sc = _s5_prompt_operators(*s5)
    d_oct = ssm_d[0].astype(F32).reshape(N_OCTETS, 1, LANES)
    ssm_p, hT_p = _ssm_prompt_call(seq(up), t_op, m_op, p_op, sc, d_oct,
                                   _octet_glu(glu_a[0]), _octet_glu(glu_b[0]))
    hT_p = hT_p.reshape(nb, N_OCTETS, 2, OCTET, SSM_STATE)
    ssm_re_p = hT_p[:, :, 0].reshape(nb, SSM_GROUPS, SSM_STATE)
    ssm_im_p = hT_p[:, :, 1].reshape(nb, SSM_GROUPS, SSM_STATE)
    ssm_s, hre_s, him_s = _ssm_decode_call(us.reshape(t_len, nd, SSM_WIDTH), state_ssm_re[0],
                                           state_ssm_im[0], *s5, ssm_d[0], glu_a[0], glu_b[0])
    ssm_s = ssm_s.reshape(n_s, SSM_WIDTH)

    wr, rb = _router_weights(router_g_w[0], router_g_b[0], router_e_w[0], router_e_b[0])
    norms = (attn_out_g[0], ssm_out_g[0], norm2_g[0], w_out_bf, wr, rb)
    x1_p, h2_p, route_p, ids_p = _outproj_call(att_p.reshape(n_p, ATTN_WIDTH), ssm_p.reshape(n_p, SSM_WIDTH),
                                        xp_rows, mod_p, *norms)
    ne = N_EXPERTS
    wg = w_gate[0].reshape(ne, d, D_EXPERT)
    wu = w_up[0].reshape(ne, d, D_EXPERT)
    wd = w_down[0].reshape(ne, D_EXPERT, d)

    def experts(x1, x_sorted, pos, tile_expert, route, mod_rows, tm):
        y_slots = _moe_call(tile_expert, x_sorted, wg, wu, wd, tm)
        return _final_call(x1, _take_rows(y_slots, pos[0]), _take_rows(y_slots, pos[1]),
                           route, mod_rows, final_norm_g)

    pos_p, tok_of_slot_p, tile_expert_p = _moe_dispatch(ids_p, MOE_TILE)
    x_sorted_p = _take_rows(h2_p, tok_of_slot_p)
    x_sorted_p, att_s = lax.optimization_barrier((x_sorted_p, att_s))
    y_p = experts(x1_p, x_sorted_p, pos_p, tile_expert_p, route_p, mod_p, MOE_TILE)

    x1_s, h2_s, route_s, ids_s = _outproj_call(att_s, ssm_s, xs_rows, mod_s, *norms)
    pos_s, tok_of_slot_s, tile_expert_s = _moe_dispatch(ids_s, MOE_TILE_DECODE)
    y_s = experts(x1_s, _take_rows(h2_s, tok_of_slot_s), pos_s, tile_expert_s, route_s, mod_s, MOE_TILE_DECODE)

    heads = (ATTN_HEADS, HEAD_DIM)
    cache_out = lambda a: jnp.transpose(a.reshape((1, nb) + heads + (s_len,)), (0, 1, 4, 2, 3))
    step_out = lambda a: jnp.transpose(a.reshape((1, t_len) + heads + (nd,)), (0, 4, 1, 2, 3))
    return (y_p.reshape(nb, s_len, d), jnp.transpose(y_s.reshape(t_len, nd, d), (1, 0, 2)),
            cache_out(kp_t), cache_out(vp_t), step_out(ks_t), step_out(vs_t),
            ssm_re_p[None], ssm_im_p[None],
            hre_s.reshape(1, nd, SSM_GROUPS, SSM_STATE), him_s.reshape(1, nd, SSM_GROUPS, SSM_STATE))
```
